```python
import math
import jax, jax.numpy as jnp
from jax import lax
import numpy as np

D_MODEL = 1024
BATCH = 16
SEQ = 2048
DEPTH = 1

MEM_LEN = 256
A_HEADS = 8
A_HEAD_DIM = 64
A_WIDTH = A_HEADS * A_HEAD_DIM
MOBA_BLOCK = 256
MOBA_TOPK = 3
Q_BLOCK = 128
G_HEADS = 4
G_WIDTH = D_MODEL - A_WIDTH
G_HEAD_V = G_WIDTH // G_HEADS
G_KEY_WIDTH = G_WIDTH // 2
G_HEAD_K = G_KEY_WIDTH // G_HEADS
G_GATE_RANK = 16
G_GATE_NORM = 16.0
G_CHUNK = 64
MIX_WIDTH = A_WIDTH + G_WIDTH
IN_SPLITS = (A_WIDTH, A_WIDTH, A_WIDTH, G_KEY_WIDTH, G_KEY_WIDTH, G_WIDTH, G_GATE_RANK, G_WIDTH)
IN_COLS = 3 * A_WIDTH + 2 * G_KEY_WIDTH + G_WIDTH + G_GATE_RANK + G_WIDTH
X_HEADS = 4
X_HEAD_DIM = D_MODEL // X_HEADS
D_FF = 4 * D_MODEL
RP_BUCKETS = 32
RP_MAX_DIST = 128
EPS = 1e-6

kernel_name = "hybrid_moba_gla_xattn_block"


def rms_norm(x, g):
    xf = x.astype(jnp.float32)
    y = xf * lax.rsqrt(jnp.mean(xf * xf, axis=-1, keepdims=True) + EPS)
    return (y * g.astype(jnp.float32)).astype(x.dtype)


def rel_bucket(dist):
    max_exact = RP_BUCKETS // 2
    d = jnp.maximum(dist, 0)
    large = max_exact + (jnp.log(jnp.maximum(d, 1).astype(jnp.float32) / max_exact)
                         / math.log(RP_MAX_DIST / max_exact) * (RP_BUCKETS - max_exact)).astype(jnp.int32)
    large = jnp.minimum(large, RP_BUCKETS - 1)
    return jnp.where(d < max_exact, d, large)


def moba_attention(q, k, v, rp_table):
    B, T, H, dh = q.shape
    nb = -(-T // MOBA_BLOCK)
    pad = nb * MOBA_BLOCK - T
    kp = jnp.pad(k, ((0, 0), (0, pad), (0, 0), (0, 0)))
    vp = jnp.pad(v, ((0, 0), (0, pad), (0, 0), (0, 0)))
    kb = kp.reshape(B, nb, MOBA_BLOCK, H, dh).transpose(0, 3, 1, 2, 4)
    vb = vp.reshape(B, nb, MOBA_BLOCK, H, dh).transpose(0, 3, 1, 2, 4)
    nq = T // Q_BLOCK
    qc = q.reshape(B, nq, Q_BLOCK, H, dh).transpose(0, 1, 3, 2, 4).reshape(B * nq, H, Q_BLOCK, dh)
    scale = dh ** -0.5
    n_sel = min(MOBA_TOPK, nb - 1)
    table_t = rp_table.T.astype(jnp.float32)
    key_off = jnp.arange(MOBA_BLOCK, dtype=jnp.int32)
    hidx = jnp.arange(H)[:, None, None]
    b_ids = jnp.repeat(jnp.arange(B, dtype=jnp.int32), nq)
    qb_ids = jnp.tile(jnp.arange(nq, dtype=jnp.int32), B)
    xs = (qc, b_ids, qb_ids)
    if n_sel > 0:
        pos = jnp.arange(T, dtype=jnp.int32)
        q_blk = pos // MOBA_BLOCK
        kmean = jnp.mean(kb.astype(jnp.float32), axis=3)
        gate = jnp.einsum('bthd,bhnd->bhtn', q.astype(jnp.float32), kmean)
        past = jnp.arange(nb)[None, :] < q_blk[:, None]
        gate = jnp.where(past[None, None], gate, -jnp.inf)
        _, sel = lax.top_k(gate, n_sel)
        valid = sel < q_blk[None, None, :, None]
        sel_c = sel.reshape(B, H, nq, Q_BLOCK, n_sel).transpose(0, 2, 1, 3, 4).reshape(B * nq, H, Q_BLOCK, n_sel)
        valid_c = valid.reshape(B, H, nq, Q_BLOCK, n_sel).transpose(0, 2, 1, 3, 4).reshape(B * nq, H, Q_BLOCK, n_sel)
        xs = xs + (sel_c, valid_c)

    def step(args):
        qi, b, qb = args[0], args[1], args[2]
        kb_b = kb[b]
        vb_b = vb[b]
        q_pos = qb * Q_BLOCK + jnp.arange(Q_BLOCK, dtype=jnp.int32)
        own = (qb * Q_BLOCK) // MOBA_BLOCK
        k_own = lax.dynamic_index_in_dim(kb_b, own, axis=1, keepdims=False)
        v_own = lax.dynamic_index_in_dim(vb_b, own, axis=1, keepdims=False)
        d_own = q_pos[:, None] - (own * MOBA_BLOCK + key_off)[None, :]
        s_own = jnp.einsum('hqd,hld->hql', qi, k_own).astype(jnp.float32) * scale + table_t[:, rel_bucket(d_own)]
        s_own = jnp.where(d_own[None] >= 0, s_own, -jnp.inf)
        if n_sel > 0:
            sel_i, valid_i = args[3], args[4]
            k_sel = kb_b[hidx, sel_i]
            v_sel = vb_b[hidx, sel_i]
            d_sel = q_pos[None, :, None, None] - (sel_i[..., None] * MOBA_BLOCK + key_off)
            s_sel = (jnp.einsum('hqd,hqcld->hqcl', qi, k_sel).astype(jnp.float32) * scale
                     + table_t[hidx[..., None], rel_bucket(d_sel)])
            s_sel = jnp.where(valid_i[..., None], s_sel, -jnp.inf)
            s = jnp.concatenate([s_sel.reshape(H, Q_BLOCK, n_sel * MOBA_BLOCK), s_own], axis=-1)
            p = jax.nn.softmax(s, axis=-1).astype(v.dtype)
            p_sel = p[..., :n_sel * MOBA_BLOCK].reshape(H, Q_BLOCK, n_sel, MOBA_BLOCK)
            p_own = p[..., n_sel * MOBA_BLOCK:]
            return jnp.einsum('hqcl,hqcld->hqd', p_sel, v_sel) + jnp.einsum('hql,hld->hqd', p_own, v_own)
        p_own = jax.nn.softmax(s_own, axis=-1).astype(v.dtype)
        return jnp.einsum('hql,hld->hqd', p_own, v_own)

    o = lax.map(step, xs)
    return o.reshape(B, nq, H, Q_BLOCK, dh).transpose(0, 1, 3, 2, 4).reshape(B, T, H * dh)


def gla_chunked(q, k, v, log_a):
    B, T, H, dk = q.shape
    dv = v.shape[-1]
    nc = T // G_CHUNK
    scale = dk ** -0.5

    def to_chunks(t):
        return t.reshape(B, nc, G_CHUNK, H, t.shape[-1]).transpose(1, 0, 3, 2, 4)

    causal = jnp.tril(jnp.ones((G_CHUNK, G_CHUNK), dtype=bool))

    def body(S, inp):
        qi, ki, vi, gi = inp
        qf = qi.astype(jnp.float32) * scale
        kf = ki.astype(jnp.float32)
        vf = vi.astype(jnp.float32)
        b = jnp.cumsum(gi.astype(jnp.float32), axis=2)
        o_inter = jnp.einsum('bhcd,bhde->bhce', qf * jnp.exp(b), S)
        diff = b[:, :, :, None, :] - b[:, :, None, :, :]
        decay = jnp.exp(jnp.where(causal[None, None, :, :, None], diff, -jnp.inf))
        A = jnp.einsum('bhid,bhjd,bhijd->bhij', qf, kf, decay)
        o_intra = jnp.einsum('bhij,bhje->bhie', A, vf)
        b_last = b[:, :, -1:, :]
        S_new = jnp.exp(b_last[:, :, 0, :])[..., None] * S + jnp.einsum('bhcd,bhce->bhde', kf * jnp.exp(b_last - b), vf)
        return S_new, o_inter + o_intra

    S0 = jnp.zeros((B, H, dk, dv), jnp.float32)
    _, o = lax.scan(body, S0, (to_chunks(q), to_chunks(k), to_chunks(v), to_chunks(log_a)))
    return o.transpose(1, 0, 3, 2, 4).reshape(B, T, H, dv).astype(v.dtype)


def setup_inputs(seed: int = 0) -> dict:
    key = jax.random.key(seed)
    ks = jax.random.split(key, 20)

    def nrm(k, shape, scale):
        return jax.random.normal(k, shape, jnp.float32) * scale

    def gain(k, shape):
        return 1.0 + 0.05 * jax.random.normal(k, shape, jnp.float32)

    L = DEPTH
    return {
        "x": nrm(ks[0], (BATCH, SEQ, D_MODEL), 1.0),
        "mem": nrm(ks[1], (BATCH, MEM_LEN, D_MODEL), 1.0),
        "rp_table": nrm(ks[2], (RP_BUCKETS, A_HEADS), 0.5),
        "norm_mix": gain(ks[3], (L, D_MODEL)),
        "w_in": nrm(ks[4], (L, D_MODEL, IN_COLS), D_MODEL ** -0.5),
        "w_gate_up": nrm(ks[5], (L, G_GATE_RANK, G_KEY_WIDTH), G_GATE_RANK ** -0.5),
        "b_gate": nrm(ks[6], (L, G_KEY_WIDTH), 0.1),
        "g_norm": gain(ks[7], (L, G_HEAD_V)),
        "w_out": nrm(ks[8], (L, MIX_WIDTH, D_MODEL), MIX_WIDTH ** -0.5),
        "norm_xattn": gain(ks[9], (L, D_MODEL)),
        "norm_mem": gain(ks[10], (L, D_MODEL)),
        "w_xq": nrm(ks[11], (L, D_MODEL, D_MODEL), D_MODEL ** -0.5),
        "w_xkv": nrm(ks[12], (L, D_MODEL, 2 * D_MODEL), D_MODEL ** -0.5),
        "w_xo": nrm(ks[13], (L, D_MODEL, D_MODEL), D_MODEL ** -0.5),
        "norm_mlp": gain(ks[14], (L, D_MODEL)),
        "w_up": nrm(ks[15], (L, D_MODEL, D_FF), D_MODEL ** -0.5),
        "w_down": nrm(ks[16], (L, D_FF, D_MODEL), D_FF ** -0.5),
        "norm_final": gain(ks[17], (D_MODEL,)),
    }


def reference(x, mem, rp_table, norm_mix, w_in, w_gate_up, b_gate, g_norm, w_out,
              norm_xattn, norm_mem, w_xq, w_xkv, w_xo, norm_mlp, w_up, w_down, norm_final):
    B, T, _ = x.shape
    M = mem.shape[1]
    offsets = [int(o) for o in np.cumsum(IN_SPLITS)[:-1]]
    for l in range(DEPTH):
        h = rms_norm(x, norm_mix[l])
        proj = h @ w_in[l]
        qa, ka, va, qg, kg, vg, glr, rg = jnp.split(proj, offsets, axis=-1)
        o_a = moba_attention(qa.reshape(B, T, A_HEADS, A_HEAD_DIM),
                             ka.reshape(B, T, A_HEADS, A_HEAD_DIM),
                             va.reshape(B, T, A_HEADS, A_HEAD_DIM), rp_table)
        log_a = jax.nn.log_sigmoid((glr @ w_gate_up[l] + b_gate[l]).astype(jnp.float32)) / G_GATE_NORM
        o_g = gla_chunked(qg.reshape(B, T, G_HEADS, G_HEAD_K),
                          kg.reshape(B, T, G_HEADS, G_HEAD_K),
                          vg.reshape(B, T, G_HEADS, G_HEAD_V),
                          log_a.reshape(B, T, G_HEADS, G_HEAD_K))
        o_g = rms_norm(o_g, g_norm[l]).reshape(B, T, G_WIDTH) * jax.nn.silu(rg)
        x = x + jnp.concatenate([o_a, o_g], axis=-1) @ w_out[l]
        h = rms_norm(x, norm_xattn[l])
        m = rms_norm(mem, norm_mem[l])
        qx = (h @ w_xq[l]).reshape(B, T, X_HEADS, X_HEAD_DIM)
        kx, vx = jnp.split(m @ w_xkv[l], 2, axis=-1)
        kx = kx.reshape(B, M, X_HEADS, X_HEAD_DIM)
        vx = vx.reshape(B, M, X_HEADS, X_HEAD_DIM)
        s = jnp.einsum('bthd,bmhd->bhtm', qx, kx).astype(jnp.float32) * (X_HEAD_DIM ** -0.5)
        p = jax.nn.softmax(s, axis=-1).astype(vx.dtype)
        ox = jnp.einsum('bhtm,bmhd->bthd', p, vx).reshape(B, T, D_MODEL)
        x = x + ox @ w_xo[l]
        h = rms_norm(x, norm_mlp[l])
        x = x + jnp.square(jax.nn.relu(h @ w_up[l])) @ w_down[l]
    return rms_norm(x, norm_final)
```

```python
import functools
import math

import numpy as np
import jax
import jax.numpy as jnp
from jax import lax
from jax.experimental import pallas as pl
from jax.experimental.pallas import tpu as pltpu

F32 = jnp.float32
BF16 = jnp.bfloat16

D_MODEL = 1024
A_HEADS = 8
A_HEAD_DIM = 64
A_WIDTH = A_HEADS * A_HEAD_DIM
MOBA_BLOCK = 256
MOBA_TOPK = 3
G_HEADS = 4
G_WIDTH = D_MODEL - A_WIDTH
G_HEAD_V = G_WIDTH // G_HEADS
G_KEY_WIDTH = G_WIDTH // 2
G_HEAD_K = G_KEY_WIDTH // G_HEADS
G_GATE_RANK = 16
G_GATE_NORM = 16.0
X_HEADS = 4
X_HEAD_DIM = D_MODEL // X_HEADS
D_FF = 4 * D_MODEL
RP_BUCKETS = 32
RP_MAX_DIST = 128
EPS = 1e-6

LANES = 128
NEG = -1e30
GLA_CHUNK = 128
ROW_TILE = 512
VMEM_LIMIT = 56 * 1024 * 1024

_C_QKV = 3 * A_WIDTH
_C_QG = _C_QKV
_C_KG = _C_QG + G_KEY_WIDTH
_C_VG = _C_KG + G_KEY_WIDTH
_C_RG = _C_VG + G_WIDTH
_C_GLR = _C_RG + G_WIDTH
_C_END = _C_GLR + LANES


def _nt(a, b):
    return lax.dot_general(a, b, (((1,), (1,)), ((), ())), preferred_element_type=F32)


def _tn(a, b):
    return lax.dot_general(a, b, (((0,), (0,)), ((), ())), preferred_element_type=F32)


def _dot(a, b):
    return jnp.dot(a, b, preferred_element_type=F32)


def _rms(x, g):
    return x * lax.rsqrt(jnp.mean(x * x, axis=-1, keepdims=True) + EPS) * g


def _bucket_thresholds():
    max_exact = RP_BUCKETS // 2
    d = np.arange(1, 4 * RP_MAX_DIST)
    val = (np.log(d.astype(np.float32) / np.float32(max_exact)) / np.float32(math.log(RP_MAX_DIST / max_exact))
           * np.float32(RP_BUCKETS - max_exact))
    bucket = np.minimum(max_exact + val.astype(np.int32), RP_BUCKETS - 1)
    return [int(d[(d >= max_exact) & (bucket >= max_exact + k)][0]) for k in range(1, RP_BUCKETS - max_exact)]


_THRESHOLDS = _bucket_thresholds()


def _bias_kernel(tab_ref, out_ref):
    h = pl.program_id(0)
    shape = (MOBA_BLOCK, 2 * MOBA_BLOCK)
    d = lax.broadcasted_iota(jnp.int32, shape, 0) - lax.broadcasted_iota(jnp.int32, shape, 1) + MOBA_BLOCK
    max_exact = RP_BUCKETS // 2
    bucket = jnp.where(d < max_exact, d, max_exact)
    for t in _THRESHOLDS:
        bucket = bucket + jnp.where(d >= t, 1, 0)
    m = jnp.full(shape, NEG, F32)
    for b in range(RP_BUCKETS):
        m = jnp.where(bucket == b, tab_ref[b, h], m)
    out_ref[0] = m


def _bias_tiles(rp_table):
    return pl.pallas_call(
        _bias_kernel,
        grid=(A_HEADS,),
        in_specs=[pl.BlockSpec(memory_space=pltpu.SMEM)],
        out_specs=pl.BlockSpec((1, MOBA_BLOCK, 2 * MOBA_BLOCK), lambda h: (h, 0, 0)),
        out_shape=jax.ShapeDtypeStruct((A_HEADS, MOBA_BLOCK, 2 * MOBA_BLOCK), F32),
        name="bias",
    )(rp_table)


def _inproj_kernel(x_ref, g_ref, w_ref, wgu_ref, bg_ref, qkv_ref, qg_ref, kg_ref, la_ref, vg_ref, rg_ref):
    h = _rms(x_ref[...], g_ref[...]).astype(BF16)

    def proj(lo, hi):
        return _dot(h, w_ref[:, lo:hi])

    qkv_ref[:, 0:A_WIDTH] = (proj(0, A_WIDTH) * (A_HEAD_DIM ** -0.5)).astype(BF16)
    qkv_ref[:, A_WIDTH:_C_QKV] = proj(A_WIDTH, _C_QKV).astype(BF16)
    qg_ref[...] = proj(_C_QG, _C_KG) * (G_HEAD_K ** -0.5)
    kg_ref[...] = proj(_C_KG, _C_VG)
    vg_ref[...] = proj(_C_VG, _C_RG).astype(BF16)
    rg_ref[...] = proj(_C_RG, _C_GLR)
    glr = proj(_C_GLR, _C_END).astype(BF16)
    z = _dot(glr, wgu_ref[...]) + bg_ref[...]
    log_sig = jnp.minimum(z, 0.0) - jnp.log(1.0 + jnp.exp(-jnp.abs(z)))
    la_ref[...] = log_sig * (1.0 / G_GATE_NORM)


def _inproj(x2, g, w, wgu, bg):
    n = x2.shape[0]
    tm = ROW_TILE
    const = functools.partial(pl.BlockSpec, pipeline_mode=pl.Buffered(1))
    row = lambda width: pl.BlockSpec((tm, width), lambda i: (i, 0))
    return pl.pallas_call(
        _inproj_kernel,
        grid=(n // tm,),
        in_specs=[row(D_MODEL),
                  const((1, D_MODEL), lambda i: (0, 0)),
                  const((D_MODEL, _C_END), lambda i: (0, 0)),
                  const((LANES, G_KEY_WIDTH), lambda i: (0, 0)),
                  const((1, G_KEY_WIDTH), lambda i: (0, 0))],
        out_specs=[row(_C_QKV), row(G_KEY_WIDTH), row(G_KEY_WIDTH), row(G_KEY_WIDTH), row(G_WIDTH), row(G_WIDTH)],
        out_shape=[jax.ShapeDtypeStruct((n, _C_QKV), BF16),
                   jax.ShapeDtypeStruct((n, G_KEY_WIDTH), F32),
                   jax.ShapeDtypeStruct((n, G_KEY_WIDTH), F32),
                   jax.ShapeDtypeStruct((n, G_KEY_WIDTH), F32),
                   jax.ShapeDtypeStruct((n, G_WIDTH), BF16),
                   jax.ShapeDtypeStruct((n, G_WIDTH), F32)],
        compiler_params=pltpu.CompilerParams(dimension_semantics=("parallel",), vmem_limit_bytes=VMEM_LIMIT),
        name="inproj",
    )(x2, g, w, wgu, bg)


def _moba_kernel(c31_ref, q_ref, k_ref, v_ref, bias_ref, o_ref, kext, vext, kmh, kml, ident, m_scr, acc_scr, *, seq):
    p = pl.program_id(1)
    qi = pl.program_id(2)
    nblk = seq // MOBA_BLOCK
    half = LANES // 2
    lane = lax.broadcasted_iota(jnp.int32, (1, LANES), 1)

    @pl.when(qi == 0)
    def _build():
        k = k_ref[...]
        v = v_ref[...]
        rowblk = lax.broadcasted_iota(jnp.int32, (seq, LANES), 0) // MOBA_BLOCK
        lane2 = lax.broadcasted_iota(jnp.int32, (seq, LANES), 1)
        lm0 = jnp.where(lane2 < half, 1.0, 0.0).astype(BF16)
        lm1 = jnp.where(lane2 >= half, 1.0, 0.0).astype(BF16)
        kext[0] = k * lm0 + jnp.where(lane2 - half == rowblk, 1.0, 0.0).astype(BF16)
        kext[1] = k * lm1 + jnp.where(lane2 == rowblk, 1.0, 0.0).astype(BF16)
        vext[0] = v * lm0 + lm1
        vext[1] = v * lm1 + lm0
        n_i = lax.broadcasted_iota(jnp.int32, (16, seq), 0)
        t_i = lax.broadcasted_iota(jnp.int32, (16, seq), 1)
        avg = jnp.where(t_i // MOBA_BLOCK == n_i, 1.0 / MOBA_BLOCK, 0.0).astype(BF16)
        km = _dot(avg, k)
        hi = km.astype(BF16)
        kmh[...] = hi
        kml[...] = (km - hi.astype(F32)).astype(BF16)
        r = lax.broadcasted_iota(jnp.int32, (MOBA_BLOCK, MOBA_BLOCK), 0)
        c = lax.broadcasted_iota(jnp.int32, (MOBA_BLOCK, MOBA_BLOCK), 1)
        ident[...] = jnp.where(r == c, 1.0, 0.0).astype(BF16)

    q = q_ref[...]
    lane_q = lax.broadcasted_iota(jnp.int32, (MOBA_BLOCK, LANES), 1)
    rowi = lax.broadcasted_iota(jnp.int32, (16, MOBA_BLOCK), 0)
    outs = []
    for h in range(2):
        qm = q * jnp.where((lane_q < half) if h == 0 else (lane_q >= half), 1.0, 0.0).astype(BF16)
        g = _nt(kmh[...], qm) + _nt(kml[...], qm)
        cnt = jnp.zeros((16, MOBA_BLOCK), F32)
        for m in range(nblk):
            gm = g[m:m + 1, :]
            beats = (gm > g) | ((gm == g) & (m < rowi))
            cnt = cnt + jnp.where(beats, jnp.where(m < qi, 1.0, 0.0), 0.0)
        keep = ((rowi < qi) & (cnt < MOBA_TOPK)) | (rowi == qi) | (rowi >= nblk)
        pen_t = jnp.where(keep, 0.0, NEG).astype(BF16)
        off = half if h == 0 else 0
        parts = ([jnp.zeros((off, MOBA_BLOCK), BF16)] if off else []) + [
            pen_t, jnp.zeros((LANES - 16 - off, MOBA_BLOCK), BF16)]
        pen_q = _nt(ident[...], jnp.concatenate(parts, axis=0))
        qext = qm + pen_q.astype(BF16)

        m_scr[h] = jnp.full((MOBA_BLOCK, LANES), NEG, F32)
        acc_scr[h] = jnp.zeros((MOBA_BLOCK, LANES), F32)

        def step(n, bias, h=h, qext=qext):
            rows = pl.ds(pl.multiple_of(n * MOBA_BLOCK, MOBA_BLOCK), MOBA_BLOCK)
            s = _nt(qext, kext[h, rows, :]) + bias
            m_prev = m_scr[h]
            m_next = jnp.maximum(m_prev, jnp.max(s, axis=1, keepdims=True))
            alpha = jnp.exp(m_prev - m_next)
            pexp = jnp.exp(s - jnp.concatenate([m_next, m_next], axis=1))
            acc_scr[h] = alpha * acc_scr[h] + _dot(pexp.astype(BF16), vext[h, rows, :])
            m_scr[h] = m_next

        step(qi, bias_ref[h, :, MOBA_BLOCK:2 * MOBA_BLOCK])

        @pl.when(qi >= 1)
        def _prev(h=h, step=step):
            step(qi - 1, bias_ref[h, :, 0:MOBA_BLOCK])

        c31 = c31_ref[2 * p + h]

        def far(n, carry, step=step, c31=c31):
            step(n, c31)
            return carry

        lax.fori_loop(0, jnp.maximum(qi - 1, 0), far, 0)
        acc = acc_scr[h]
        outs.append(acc / pltpu.roll(acc, half, 1))
    o_ref[...] = jnp.where(lane < half, outs[0], outs[1]).astype(BF16)


def _moba(qkv, bias, c31, batch, seq):
    n = qkv.shape[0]
    nblk = seq // MOBA_BLOCK
    npair = A_HEADS // 2
    kern = functools.partial(_moba_kernel, seq=seq)
    return pl.pallas_call(
        kern,
        grid=(batch, npair, nblk),
        in_specs=[pl.BlockSpec(memory_space=pltpu.SMEM),
                  pl.BlockSpec((MOBA_BLOCK, LANES), lambda b, p, i: (b * nblk + i, p)),
                  pl.BlockSpec((seq, LANES), lambda b, p, i: (b, npair + p)),
                  pl.BlockSpec((seq, LANES), lambda b, p, i: (b, 2 * npair + p)),
                  pl.BlockSpec((2, MOBA_BLOCK, 2 * MOBA_BLOCK), lambda b, p, i: (p, 0, 0))],
        out_specs=pl.BlockSpec((MOBA_BLOCK, LANES), lambda b, p, i: (b * nblk + i, p)),
        out_shape=jax.ShapeDtypeStruct((n, A_WIDTH), BF16),
        scratch_shapes=[pltpu.VMEM((2, seq, LANES), BF16),
                        pltpu.VMEM((2, seq, LANES), BF16),
                        pltpu.VMEM((16, LANES), BF16),
                        pltpu.VMEM((16, LANES), BF16),
                        pltpu.VMEM((MOBA_BLOCK, MOBA_BLOCK), BF16),
                        pltpu.VMEM((2, MOBA_BLOCK, LANES), F32),
                        pltpu.VMEM((2, MOBA_BLOCK, LANES), F32)],
        compiler_params=pltpu.CompilerParams(dimension_semantics=("parallel", "arbitrary", "arbitrary"),
                                             vmem_limit_bytes=VMEM_LIMIT),
        name="moba",
    )(c31, qkv, qkv, qkv, bias)


_GLA_LEVELS = int(math.log2(GLA_CHUNK))


def _gla_weights():
    c = GLA_CHUNK
    i = np.arange(c)[:, None]
    j = np.arange(c)[None, :]
    mats = [(j <= i)]
    for lvl in range(_GLA_LEVELS):
        s = (c // 2) >> lvl
        ref = (i // (2 * s)) * (2 * s) + s - 1
        mats.append(np.where(i > ref, (j > ref) & (j <= i), (j > i) & (j <= ref)))
    return np.concatenate(mats, axis=0).astype(np.float32)


def _gla_kernel(q_ref, k_ref, la_ref, v_ref, rg_ref, w_ref, gn_ref, o_ref, st_ref):
    c = GLA_CHUNK
    half = LANES // 2

    @pl.when(pl.program_id(1) == 0)
    def _init():
        st_ref[...] = jnp.zeros(st_ref.shape, F32)

    la = la_ref[...]
    hi = la.astype(BF16)
    r1 = la - hi.astype(F32)
    mid = r1.astype(BF16)
    lo = (r1 - mid.astype(F32)).astype(BF16)
    w = w_ref[...]
    d_all = _dot(w, hi) + _dot(w, mid) + _dot(w, lo)
    q = q_ref[...]
    k = k_ref[...]
    b = d_all[0:c]
    blast = b[c - 1:c, :]
    row = lax.broadcasted_iota(jnp.int32, (c, G_KEY_WIDTH), 0)
    ri = lax.broadcasted_iota(jnp.int32, (c, c), 0)
    ci = lax.broadcasted_iota(jnp.int32, (c, c), 1)
    lane = lax.broadcasted_iota(jnp.int32, (1, LANES), 1)
    qts, kts, masks = [], [], []
    for lvl in range(_GLA_LEVELS):
        s = (c // 2) >> lvl
        e = jnp.exp(d_all[c * (lvl + 1):c * (lvl + 2)])
        odd = (row // s) % 2 == 1
        qts.append(jnp.where(odd, q * e, 0.0))
        kts.append(jnp.where(odd, 0.0, k * e).astype(BF16))
        masks.append(None if 2 * s == c else (ri // (2 * s)) == (ci // (2 * s)))
    qts.append(q)
    kts.append(k.astype(BF16))
    masks.append(ri == ci)
    qe = q * jnp.exp(b)
    khat = (k * jnp.exp(blast - b)).astype(BF16)
    dec = jnp.exp(blast)
    for p in range(G_HEADS // 2):
        sl = slice(LANES * p, LANES * (p + 1))
        st = st_ref[p]
        stb = st.astype(BF16)
        ups = []
        for hh in range(2):
            h = 2 * p + hh
            hm = (lane < half) if hh == 0 else (lane >= half)
            vh = v_ref[:, LANES * h:LANES * (h + 1)]
            a = jnp.zeros((c, c), F32)
            for ql, kl, mk in zip(qts, kts, masks):
                pm = _nt(jnp.where(hm, ql[:, sl], 0.0).astype(BF16), kl[:, sl])
                a = a + (pm if mk is None else jnp.where(mk, pm, 0.0))
            o = _dot(a.astype(BF16), vh) + _nt(jnp.where(hm, qe[:, sl], 0.0).astype(BF16), stb)
            y = _rms(o, gn_ref[...])
            rg = rg_ref[:, LANES * h:LANES * (h + 1)]
            y = y * (rg / (1.0 + jnp.exp(-rg)))
            o_ref[:, LANES * h:LANES * (h + 1)] = y.astype(BF16)
            ups.append(_tn(vh, khat[:, sl]))
        st_ref[p] = st * dec[:, sl] + jnp.where(lane < half, ups[0], ups[1])


def _gla(qg, kg, la, vg, rg, gn, batch, seq):
    n = qg.shape[0]
    nchunk = seq // GLA_CHUNK
    w = jnp.asarray(_gla_weights(), BF16)
    row = lambda width: pl.BlockSpec((GLA_CHUNK, width), lambda b, c: (b * nchunk + c, 0))
    const = functools.partial(pl.BlockSpec, pipeline_mode=pl.Buffered(1))
    return pl.pallas_call(
        _gla_kernel,
        grid=(batch, nchunk),
        in_specs=[row(G_KEY_WIDTH), row(G_KEY_WIDTH), row(G_KEY_WIDTH), row(G_WIDTH), row(G_WIDTH),
                  const(w.shape, lambda b, c: (0, 0)),
                  const((1, G_HEAD_V), lambda b, c: (0, 0))],
        out_specs=row(G_WIDTH),
        out_shape=jax.ShapeDtypeStruct((n, G_WIDTH), BF16),
        scratch_shapes=[pltpu.VMEM((G_HEADS // 2, G_HEAD_V, LANES), F32)],
        compiler_params=pltpu.CompilerParams(dimension_semantics=("parallel", "arbitrary"),
                                             vmem_limit_bytes=VMEM_LIMIT),
        name="gla",
    )(qg, kg, la, vg, rg, w, gn)


def _memkv_kernel(m_ref, g_ref, w_ref, k_ref, v_ref):
    h = _rms(m_ref[0], g_ref[...]).astype(BF16)
    k_ref[0] = _dot(h, w_ref[:, 0:D_MODEL]).astype(BF16)
    v_ref[0] = _dot(h, w_ref[:, D_MODEL:2 * D_MODEL]).astype(BF16)


def _memkv(mem, g, w):
    batch, mlen, _ = mem.shape
    const = functools.partial(pl.BlockSpec, pipeline_mode=pl.Buffered(1))
    blk = pl.BlockSpec((1, mlen, D_MODEL), lambda b: (b, 0, 0))
    return pl.pallas_call(
        _memkv_kernel,
        grid=(batch,),
        in_specs=[blk, const((1, D_MODEL), lambda b: (0, 0)), const((D_MODEL, 2 * D_MODEL), lambda b: (0, 0))],
        out_specs=[blk, blk],
        out_shape=[jax.ShapeDtypeStruct(mem.shape, BF16)] * 2,
        compiler_params=pltpu.CompilerParams(dimension_semantics=("parallel",), vmem_limit_bytes=VMEM_LIMIT),
        name="memkv",
    )(mem, g, w)


def _post_kernel(x_ref, oa_ref, og_ref, wo_ref, gx_ref, wq_ref, kx_ref, vx_ref, wxo_ref, o_ref):
    x1 = x_ref[...] + _dot(oa_ref[...], wo_ref[0:A_WIDTH, :]) + _dot(og_ref[...], wo_ref[A_WIDTH:D_MODEL, :])
    h = _rms(x1, gx_ref[...]).astype(BF16)
    qx = (_dot(h, wq_ref[...]) * (X_HEAD_DIM ** -0.5)).astype(BF16)
    outs = []
    for hd in range(X_HEADS):
        sl = slice(X_HEAD_DIM * hd, X_HEAD_DIM * (hd + 1))
        s = _nt(qx[:, sl], kx_ref[0, :, sl])
        pexp = jnp.exp(s - jnp.max(s, axis=1, keepdims=True))
        pn = (pexp / jnp.sum(pexp, axis=1, keepdims=True)).astype(BF16)
        outs.append(_dot(pn, vx_ref[0, :, sl]).astype(BF16))
    ox = jnp.concatenate(outs, axis=1)
    o_ref[...] = x1 + _dot(ox, wxo_ref[...])


def _post(x2, oa, og, wo, gx, wq, kx, vx, wxo, seq):
    n = x2.shape[0]
    tm = ROW_TILE
    per_b = seq // tm
    mlen = kx.shape[1]
    const = functools.partial(pl.BlockSpec, pipeline_mode=pl.Buffered(1))
    row = lambda width: pl.BlockSpec((tm, width), lambda i: (i, 0))
    mem = pl.BlockSpec((1, mlen, D_MODEL), lambda i: (i // per_b, 0, 0))
    sq = const((D_MODEL, D_MODEL), lambda i: (0, 0))
    return pl.pallas_call(
        _post_kernel,
        grid=(n // tm,),
        in_specs=[row(D_MODEL), row(A_WIDTH), row(G_WIDTH), sq, const((1, D_MODEL), lambda i: (0, 0)), sq, mem, mem, sq],
        out_specs=row(D_MODEL),
        out_shape=jax.ShapeDtypeStruct((n, D_MODEL), F32),
        compiler_params=pltpu.CompilerParams(dimension_semantics=("parallel",), vmem_limit_bytes=VMEM_LIMIT),
        name="post",
    )(x2, oa, og, wo, gx, wq, kx, vx, wxo)


def _mlp_kernel(x_ref, g_ref, wu_ref, wd_ref, gf_ref, o_ref, *, final_norm):
    x = x_ref[...]
    h = _rms(x, g_ref[...]).astype(BF16)
    acc = x
    for c in range(D_FF // D_MODEL):
        sl = slice(D_MODEL * c, D_MODEL * (c + 1))
        u = jnp.square(jnp.maximum(_dot(h, wu_ref[:, sl]), 0.0)).astype(BF16)
        acc = acc + _dot(u, wd_ref[sl, :])
    o_ref[...] = _rms(acc, gf_ref[...]) if final_norm else acc


def _mlp(x2, g, wu, wd, gf, final_norm):
    n = x2.shape[0]
    tm = ROW_TILE
    const = functools.partial(pl.BlockSpec, pipeline_mode=pl.Buffered(1))
    row = pl.BlockSpec((tm, D_MODEL), lambda i: (i, 0))
    vec = const((1, D_MODEL), lambda i: (0, 0))
    return pl.pallas_call(
        functools.partial(_mlp_kernel, final_norm=final_norm),
        grid=(n // tm,),
        in_specs=[row, vec, const((D_MODEL, D_FF), lambda i: (0, 0)), const((D_FF, D_MODEL), lambda i: (0, 0)), vec],
        out_specs=row,
        out_shape=jax.ShapeDtypeStruct((n, D_MODEL), F32),
        compiler_params=pltpu.CompilerParams(dimension_semantics=("parallel",), vmem_limit_bytes=VMEM_LIMIT),
        name="mlp",
    )(x2, g, wu, wd, gf)


def kernel(x, mem, rp_table, norm_mix, w_in, w_gate_up, b_gate, g_norm, w_out, norm_xattn, norm_mem, w_xq, w_xkv,
           w_xo, norm_mlp, w_up, w_down, norm_final):
    batch, seq, _ = x.shape
    depth = w_in.shape[0]
    assert seq % ROW_TILE == 0 and seq % MOBA_BLOCK == 0 and seq // MOBA_BLOCK <= 8
    x2 = x.reshape(batch * seq, D_MODEL)
    bias = _bias_tiles(rp_table)
    c31 = rp_table[RP_BUCKETS - 1]
    glr_lo = 3 * A_WIDTH + 2 * G_KEY_WIDTH + G_WIDTH
    for l in range(depth):
        wl = w_in[l]
        w = jnp.concatenate([wl[:, :glr_lo], wl[:, glr_lo + G_GATE_RANK:], wl[:, glr_lo:glr_lo + G_GATE_RANK],
                             jnp.zeros((D_MODEL, LANES - G_GATE_RANK), F32)], axis=1).astype(BF16)
        wgu = jnp.concatenate([w_gate_up[l], jnp.zeros((LANES - G_GATE_RANK, G_KEY_WIDTH), F32)], axis=0).astype(BF16)
        qkv, qg, kg, la, vg, rg = _inproj(x2, norm_mix[l][None], w, wgu, b_gate[l][None])
        oa = _moba(qkv, bias, c31, batch, seq)
        og = _gla(qg, kg, la, vg, rg, g_norm[l][None], batch, seq)
        kx, vx = _memkv(mem, norm_mem[l][None], w_xkv[l].astype(BF16))
        x2 = _post(x2, oa, og, w_out[l].astype(BF16), norm_xattn[l][None], w_xq[l].astype(BF16), kx, vx,
                   w_xo[l].astype(BF16), seq)
        last = l == depth - 1
        x2 = _mlp(x2, norm_mlp[l][None], w_up[l].astype(BF16), w_down[l].astype(BF16), norm_final[None], last)
    return x2.reshape(batch, seq, D_MODEL)
```

```python
import functools
import math

import numpy as np
import jax
import jax.numpy as jnp
from jax import lax
from jax.experimental import pallas as pl
from jax.experimental.pallas import tpu as pltpu

F32 = jnp.float32
BF16 = jnp.bfloat16

D_MODEL = 1024
A_HEADS = 8
A_HEAD_DIM = 64
A_WIDTH = A_HEADS * A_HEAD_DIM
MOBA_BLOCK = 256
MOBA_TOPK = 3
G_HEADS = 4
G_WIDTH = D_MODEL - A_WIDTH
G_HEAD_V = G_WIDTH // G_HEADS
G_KEY_WIDTH = G_WIDTH // 2
G_HEAD_K = G_KEY_WIDTH // G_HEADS
G_GATE_RANK = 16
G_GATE_NORM = 16.0
X_HEADS = 4
X_HEAD_DIM = D_MODEL // X_HEADS
D_FF = 4 * D_MODEL
RP_BUCKETS = 32
RP_MAX_DIST = 128
EPS = 1e-6

LANES = 128
NEG = -1e30
GLA_CHUNK = 128
ROW_TILE = 512
VMEM_LIMIT = 56 * 1024 * 1024

_C_QKV = 3 * A_WIDTH
_C_QG = _C_QKV
_C_KG = _C_QG + G_KEY_WIDTH
_C_VG = _C_KG + G_KEY_WIDTH
_C_RG = _C_VG + G_WIDTH
_C_GLR = _C_RG + G_WIDTH
_C_END = _C_GLR + LANES


def _nt(a, b):
    return lax.dot_general(a, b, (((1,), (1,)), ((), ())), preferred_element_type=F32)


def _tn(a, b):
    return lax.dot_general(a, b, (((0,), (0,)), ((), ())), preferred_element_type=F32)


def _dot(a, b):
    return jnp.dot(a, b, preferred_element_type=F32)


def _rms(x, g):
    return x * lax.rsqrt(jnp.mean(x * x, axis=-1, keepdims=True) + EPS) * g


def _bucket_thresholds():
    max_exact = RP_BUCKETS // 2
    d = np.arange(1, 4 * RP_MAX_DIST)
    val = (np.log(d.astype(np.float32) / np.float32(max_exact)) / np.float32(math.log(RP_MAX_DIST / max_exact))
           * np.float32(RP_BUCKETS - max_exact))
    bucket = np.minimum(max_exact + val.astype(np.int32), RP_BUCKETS - 1)
    return [int(d[(d >= max_exact) & (bucket >= max_exact + k)][0]) for k in range(1, RP_BUCKETS - max_exact)]


_THRESHOLDS = _bucket_thresholds()


def _bias_kernel(tab_ref, out_ref):
    h = pl.program_id(0)
    shape = (MOBA_BLOCK, 2 * MOBA_BLOCK)
    d = lax.broadcasted_iota(jnp.int32, shape, 0) - lax.broadcasted_iota(jnp.int32, shape, 1) + MOBA_BLOCK
    max_exact = RP_BUCKETS // 2
    bucket = jnp.where(d < max_exact, d, max_exact)
    for t in _THRESHOLDS:
        bucket = bucket + jnp.where(d >= t, 1, 0)
    m = jnp.full(shape, NEG, F32)
    for b in range(RP_BUCKETS):
        m = jnp.where(bucket == b, tab_ref[b, h], m)
    out_ref[0] = m


def _bias_tiles(rp_table):
    return pl.pallas_call(
        _bias_kernel,
        grid=(A_HEADS,),
        in_specs=[pl.BlockSpec(memory_space=pltpu.SMEM)],
        out_specs=pl.BlockSpec((1, MOBA_BLOCK, 2 * MOBA_BLOCK), lambda h: (h, 0, 0)),
        out_shape=jax.ShapeDtypeStruct((A_HEADS, MOBA_BLOCK, 2 * MOBA_BLOCK), F32),
        name="bias",
    )(rp_table)


def _inproj_kernel(x_ref, g_ref, w_ref, wgu_ref, bg_ref, qkv_ref, qg_ref, kg_ref, la_ref, vg_ref, rg_ref):
    h = _rms(x_ref[...], g_ref[...]).astype(BF16)

    def proj(lo, hi):
        return _dot(h, w_ref[:, lo:hi])

    qkv_ref[:, 0:A_WIDTH] = (proj(0, A_WIDTH) * (A_HEAD_DIM ** -0.5)).astype(BF16)
    qkv_ref[:, A_WIDTH:_C_QKV] = proj(A_WIDTH, _C_QKV).astype(BF16)
    qg_ref[...] = proj(_C_QG, _C_KG) * (G_HEAD_K ** -0.5)
    kg_ref[...] = proj(_C_KG, _C_VG)
    vg_ref[...] = proj(_C_VG, _C_RG).astype(BF16)
    rg_ref[...] = proj(_C_RG, _C_GLR)
    glr = proj(_C_GLR, _C_END).astype(BF16)
    z = _dot(glr, wgu_ref[...]) + bg_ref[...]
    log_sig = jnp.minimum(z, 0.0) - jnp.log(1.0 + jnp.exp(-jnp.abs(z)))
    la_ref[...] = log_sig * (1.0 / G_GATE_NORM)


def _inproj(x2, g, w, wgu, bg):
    n = x2.shape[0]
    tm = ROW_TILE
    const = functools.partial(pl.BlockSpec, pipeline_mode=pl.Buffered(1))
    row = lambda width: pl.BlockSpec((tm, width), lambda i: (i, 0))
    return pl.pallas_call(
        _inproj_kernel,
        grid=(n // tm,),
        in_specs=[row(D_MODEL),
                  const((1, D_MODEL), lambda i: (0, 0)),
                  const((D_MODEL, _C_END), lambda i: (0, 0)),
                  const((LANES, G_KEY_WIDTH), lambda i: (0, 0)),
                  const((1, G_KEY_WIDTH), lambda i: (0, 0))],
        out_specs=[row(_C_QKV), row(G_KEY_WIDTH), row(G_KEY_WIDTH), row(G_KEY_WIDTH), row(G_WIDTH), row(G_WIDTH)],
        out_shape=[jax.ShapeDtypeStruct((n, _C_QKV), BF16),
                   jax.ShapeDtypeStruct((n, G_KEY_WIDTH), F32),
                   jax.ShapeDtypeStruct((n, G_KEY_WIDTH), F32),
                   jax.ShapeDtypeStruct((n, G_KEY_WIDTH), F32),
                   jax.ShapeDtypeStruct((n, G_WIDTH), BF16),
                   jax.ShapeDtypeStruct((n, G_WIDTH), F32)],
        compiler_params=pltpu.CompilerParams(dimension_semantics=("parallel",), vmem_limit_bytes=VMEM_LIMIT),
        name="inproj",
    )(x2, g, w, wgu, bg)


def _moba_kernel(c31_ref, q_ref, k_ref, v_ref, bias_ref, o_ref, kext, vext, s_scr, p_scr, *, seq):
    p = pl.program_id(1)
    nblk = seq // MOBA_BLOCK
    half = LANES // 2
    lane = lax.broadcasted_iota(jnp.int32, (1, LANES), 1)

    k = k_ref[...]
    v = v_ref[...]
    rowblk = lax.broadcasted_iota(jnp.int32, (seq, LANES), 0) // MOBA_BLOCK
    lane2 = lax.broadcasted_iota(jnp.int32, (seq, LANES), 1)
    lm0 = jnp.where(lane2 < half, 1.0, 0.0).astype(BF16)
    lm1 = jnp.where(lane2 >= half, 1.0, 0.0).astype(BF16)
    kext[0] = k * lm0 + jnp.where(lane2 - half == rowblk, 1.0, 0.0).astype(BF16)
    kext[1] = k * lm1 + jnp.where(lane2 == rowblk, 1.0, 0.0).astype(BF16)
    vext[0] = v * lm0 + lm1
    vext[1] = v * lm1 + lm0
    n_i = lax.broadcasted_iota(jnp.int32, (16, seq), 0)
    t_i = lax.broadcasted_iota(jnp.int32, (16, seq), 1)
    avg = jnp.where(t_i // MOBA_BLOCK == n_i, 1.0 / MOBA_BLOCK, 0.0).astype(BF16)
    km = _dot(avg, k)
    kmh = km.astype(BF16)
    kml = (km - kmh.astype(F32)).astype(BF16)
    r_i = lax.broadcasted_iota(jnp.int32, (MOBA_BLOCK, MOBA_BLOCK), 0)
    c_i = lax.broadcasted_iota(jnp.int32, (MOBA_BLOCK, MOBA_BLOCK), 1)
    ident = jnp.where(r_i == c_i, 1.0, 0.0).astype(BF16)

    lane_q = lax.broadcasted_iota(jnp.int32, (MOBA_BLOCK, LANES), 1)
    rowi = lax.broadcasted_iota(jnp.int32, (16, MOBA_BLOCK), 0)
    blk = lambda n: slice(n * MOBA_BLOCK, (n + 1) * MOBA_BLOCK)
    for qi in range(nblk):
        q = q_ref[blk(qi), :]
        outs = []
        for h in range(2):
            qm = q * jnp.where((lane_q < half) if h == 0 else (lane_q >= half), 1.0, 0.0).astype(BF16)
            qext = qm
            if qi > MOBA_TOPK:
                g = _nt(kmh, qm) + _nt(kml, qm)
                cnt = jnp.zeros((16, MOBA_BLOCK), F32)
                for m in range(qi):
                    gm = g[m:m + 1, :]
                    cnt = cnt + jnp.where((gm > g) | ((gm == g) & (m < rowi)), 1.0, 0.0)
                keep = ((rowi < qi) & (cnt < MOBA_TOPK)) | (rowi >= qi)
                pen_t = jnp.where(keep, 0.0, NEG).astype(BF16)
                off = half if h == 0 else 0
                parts = ([jnp.zeros((off, MOBA_BLOCK), BF16)] if off else []) + [
                    pen_t, jnp.zeros((LANES - 16 - off, MOBA_BLOCK), BF16)]
                pen_q = _nt(ident, jnp.concatenate(parts, axis=0))
                qext = qm + pen_q.astype(BF16)
            c31 = c31_ref[2 * p + h]
            sbuf = s_scr.at[qi % 2, h]
            pbuf = p_scr.at[qi % 2, h]
            near_mx = far_mx = None
            for n in range(qi + 1):
                s = _nt(qext, kext[h, blk(n), :])
                if n == qi:
                    s = s + bias_ref[h, :, MOBA_BLOCK:2 * MOBA_BLOCK]
                elif n == qi - 1:
                    s = s + bias_ref[h, :, 0:MOBA_BLOCK]
                sbuf[:, blk(n)] = s
                t = jnp.maximum(s[:, :LANES], s[:, LANES:])
                if n < qi - 1:
                    far_mx = t if far_mx is None else jnp.maximum(far_mx, t)
                else:
                    near_mx = t if near_mx is None else jnp.maximum(near_mx, t)
            mx = near_mx if far_mx is None else jnp.maximum(near_mx, far_mx + c31)
            m_near = jnp.max(mx, axis=1, keepdims=True)
            m_far = m_near - c31
            for n in range(qi + 1):
                pbuf[:, blk(n)] = jnp.exp(sbuf[:, blk(n)] - (m_far if n < qi - 1 else m_near)).astype(BF16)
            width = (qi + 1) * MOBA_BLOCK
            acc = _dot(pbuf[:, 0:width], vext[h, 0:width, :])
            outs.append(acc / pltpu.roll(acc, half, 1))
        o_ref[blk(qi), :] = jnp.where(lane < half, outs[0], outs[1]).astype(BF16)


def _moba(qkv, bias, c31, batch, seq):
    n = qkv.shape[0]
    nblk = seq // MOBA_BLOCK
    npair = A_HEADS // 2
    kern = functools.partial(_moba_kernel, seq=seq)
    return pl.pallas_call(
        kern,
        grid=(batch, npair),
        in_specs=[pl.BlockSpec(memory_space=pltpu.SMEM),
                  pl.BlockSpec((seq, LANES), lambda b, p: (b, p)),
                  pl.BlockSpec((seq, LANES), lambda b, p: (b, npair + p)),
                  pl.BlockSpec((seq, LANES), lambda b, p: (b, 2 * npair + p)),
                  pl.BlockSpec((2, MOBA_BLOCK, 2 * MOBA_BLOCK), lambda b, p: (p, 0, 0))],
        out_specs=pl.BlockSpec((seq, LANES), lambda b, p: (b, p)),
        out_shape=jax.ShapeDtypeStruct((n, A_WIDTH), BF16),
        scratch_shapes=[pltpu.VMEM((2, seq, LANES), BF16),
                        pltpu.VMEM((2, seq, LANES), BF16),
                        pltpu.VMEM((2, 2, MOBA_BLOCK, seq), F32),
                        pltpu.VMEM((2, 2, MOBA_BLOCK, seq), BF16)],
        compiler_params=pltpu.CompilerParams(dimension_semantics=("parallel", "arbitrary"),
                                             vmem_limit_bytes=VMEM_LIMIT),
        name="moba",
    )(c31, qkv, qkv, qkv, bias)


_GLA_LEVELS = int(math.log2(GLA_CHUNK))


def _gla_weights():
    c = GLA_CHUNK
    i = np.arange(c)[:, None]
    j = np.arange(c)[None, :]
    mats = [(j <= i)]
    for lvl in range(_GLA_LEVELS):
        s = (c // 2) >> lvl
        ref = (i // (2 * s)) * (2 * s) + s - 1
        mats.append(np.where(i > ref, (j > ref) & (j <= i), (j > i) & (j <= ref)))
    return np.concatenate(mats, axis=0).astype(np.float32)


def _gla_kernel(q_ref, k_ref, la_ref, v_ref, rg_ref, w_ref, gn_ref, o_ref, st_ref):
    c = GLA_CHUNK
    half = LANES // 2

    @pl.when(pl.program_id(1) == 0)
    def _init():
        st_ref[...] = jnp.zeros(st_ref.shape, F32)

    la = la_ref[...]
    hi = la.astype(BF16)
    r1 = la - hi.astype(F32)
    mid = r1.astype(BF16)
    lo = (r1 - mid.astype(F32)).astype(BF16)
    w = w_ref[...]
    d_all = _dot(w, hi) + _dot(w, mid) + _dot(w, lo)
    q = q_ref[...]
    k = k_ref[...]
    b = d_all[0:c]
    blast = b[c - 1:c, :]
    row = lax.broadcasted_iota(jnp.int32, (c, G_KEY_WIDTH), 0)
    ri = lax.broadcasted_iota(jnp.int32, (c, c), 0)
    ci = lax.broadcasted_iota(jnp.int32, (c, c), 1)
    lane = lax.broadcasted_iota(jnp.int32, (1, LANES), 1)
    qts, kts, masks = [], [], []
    for lvl in range(_GLA_LEVELS):
        s = (c // 2) >> lvl
        e = jnp.exp(d_all[c * (lvl + 1):c * (lvl + 2)])
        odd = (row // s) % 2 == 1
        qts.append(jnp.where(odd, q * e, 0.0))
        kts.append(jnp.where(odd, 0.0, k * e).astype(BF16))
        masks.append(None if 2 * s == c else (ri // (2 * s)) == (ci // (2 * s)))
    qts.append(q)
    kts.append(k.astype(BF16))
    masks.append(ri == ci)
    qe = q * jnp.exp(b)
    khat = (k * jnp.exp(blast - b)).astype(BF16)
    dec = jnp.exp(blast)
    for p in range(G_HEADS // 2):
        sl = slice(LANES * p, LANES * (p + 1))
        st = st_ref[p]
        stb = st.astype(BF16)
        ups = []
        for hh in range(2):
            h = 2 * p + hh
            hm = (lane < half) if hh == 0 else (lane >= half)
            vh = v_ref[:, LANES * h:LANES * (h + 1)]
            a = jnp.zeros((c, c), F32)
            for ql, kl, mk in zip(qts, kts, masks):
                pm = _nt(jnp.where(hm, ql[:, sl], 0.0).astype(BF16), kl[:, sl])
                a = a + (pm if mk is None else jnp.where(mk, pm, 0.0))
            o = _dot(a.astype(BF16), vh) + _nt(jnp.where(hm, qe[:, sl], 0.0).astype(BF16), stb)
            y = _rms(o, gn_ref[...])
            rg = rg_ref[:, LANES * h:LANES * (h + 1)]
            y = y * (rg / (1.0 + jnp.exp(-rg)))
            o_ref[:, LANES * h:LANES * (h + 1)] = y.astype(BF16)
            ups.append(_tn(vh, khat[:, sl]))
        st_ref[p] = st * dec[:, sl] + jnp.where(lane < half, ups[0], ups[1])


def _gla(qg, kg, la, vg, rg, gn, batch, seq):
    n = qg.shape[0]
    nchunk = seq // GLA_CHUNK
    w = jnp.asarray(_gla_weights(), BF16)
    row = lambda width: pl.BlockSpec((GLA_CHUNK, width), lambda b, c: (b * nchunk + c, 0))
    const = functools.partial(pl.BlockSpec, pipeline_mode=pl.Buffered(1))
    return pl.pallas_call(
        _gla_kernel,
        grid=(batch, nchunk),
        in_specs=[row(G_KEY_WIDTH), row(G_KEY_WIDTH), row(G_KEY_WIDTH), row(G_WIDTH), row(G_WIDTH),
                  const(w.shape, lambda b, c: (0, 0)),
                  const((1, G_HEAD_V), lambda b, c: (0, 0))],
        out_specs=row(G_WIDTH),
        out_shape=jax.ShapeDtypeStruct((n, G_WIDTH), BF16),
        scratch_shapes=[pltpu.VMEM((G_HEADS // 2, G_HEAD_V, LANES), F32)],
        compiler_params=pltpu.CompilerParams(dimension_semantics=("parallel", "arbitrary"),
                                             vmem_limit_bytes=VMEM_LIMIT),
        name="gla",
    )(qg, kg, la, vg, rg, w, gn)


def _memkv_kernel(m_ref, g_ref, w_ref, k_ref, v_ref):
    h = _rms(m_ref[0], g_ref[...]).astype(BF16)
    k_ref[0] = _dot(h, w_ref[:, 0:D_MODEL]).astype(BF16)
    v_ref[0] = _dot(h, w_ref[:, D_MODEL:2 * D_MODEL]).astype(BF16)


def _memkv(mem, g, w):
    batch, mlen, _ = mem.shape
    const = functools.partial(pl.BlockSpec, pipeline_mode=pl.Buffered(1))
    blk = pl.BlockSpec((1, mlen, D_MODEL), lambda b: (b, 0, 0))
    return pl.pallas_call(
        _memkv_kernel,
        grid=(batch,),
        in_specs=[blk, const((1, D_MODEL), lambda b: (0, 0)), const((D_MODEL, 2 * D_MODEL), lambda b: (0, 0))],
        out_specs=[blk, blk],
        out_shape=[jax.ShapeDtypeStruct(mem.shape, BF16)] * 2,
        compiler_params=pltpu.CompilerParams(dimension_semantics=("parallel",), vmem_limit_bytes=VMEM_LIMIT),
        name="memkv",
    )(mem, g, w)


def _post_kernel(x_ref, oa_ref, og_ref, wo_ref, gx_ref, wq_ref, kx_ref, vx_ref, wxo_ref, o_ref):
    x1 = x_ref[...] + _dot(oa_ref[...], wo_ref[0:A_WIDTH, :]) + _dot(og_ref[...], wo_ref[A_WIDTH:D_MODEL, :])
    h = _rms(x1, gx_ref[...]).astype(BF16)
    qx = (_dot(h, wq_ref[...]) * (X_HEAD_DIM ** -0.5)).astype(BF16)
    outs = []
    for hd in range(X_HEADS):
        sl = slice(X_HEAD_DIM * hd, X_HEAD_DIM * (hd + 1))
        s = _nt(qx[:, sl], kx_ref[0, :, sl])
        pexp = jnp.exp(s - jnp.max(s, axis=1, keepdims=True))
        pn = (pexp / jnp.sum(pexp, axis=1, keepdims=True)).astype(BF16)
        outs.append(_dot(pn, vx_ref[0, :, sl]).astype(BF16))
    ox = jnp.concatenate(outs, axis=1)
    o_ref[...] = x1 + _dot(ox, wxo_ref[...])


def _post(x2, oa, og, wo, gx, wq, kx, vx, wxo, seq):
    n = x2.shape[0]
    tm = ROW_TILE
    per_b = seq // tm
    mlen = kx.shape[1]
    const = functools.partial(pl.BlockSpec, pipeline_mode=pl.Buffered(1))
    row = lambda width: pl.BlockSpec((tm, width), lambda i: (i, 0))
    mem = pl.BlockSpec((1, mlen, D_MODEL), lambda i: (i // per_b, 0, 0))
    sq = const((D_MODEL, D_MODEL), lambda i: (0, 0))
    return pl.pallas_call(
        _post_kernel,
        grid=(n // tm,),
        in_specs=[row(D_MODEL), row(A_WIDTH), row(G_WIDTH), sq, const((1, D_MODEL), lambda i: (0, 0)), sq, mem, mem, sq],
        out_specs=row(D_MODEL),
        out_shape=jax.ShapeDtypeStruct((n, D_MODEL), F32),
        compiler_params=pltpu.CompilerParams(dimension_semantics=("parallel",), vmem_limit_bytes=VMEM_LIMIT),
        name="post",
    )(x2, oa, og, wo, gx, wq, kx, vx, wxo)


def _mlp_kernel(x_ref, g_ref, wu_ref, wd_ref, gf_ref, o_ref, *, final_norm):
    x = x_ref[...]
    h = _rms(x, g_ref[...]).astype(BF16)
    acc = x
    for c in range(D_FF // D_MODEL):
        sl = slice(D_MODEL * c, D_MODEL * (c + 1))
        u = jnp.square(jnp.maximum(_dot(h, wu_ref[:, sl]), 0.0)).astype(BF16)
        acc = acc + _dot(u, wd_ref[sl, :])
    o_ref[...] = _rms(acc, gf_ref[...]) if final_norm else acc


def _mlp(x2, g, wu, wd, gf, final_norm):
    n = x2.shape[0]
    tm = ROW_TILE
    const = functools.partial(pl.BlockSpec, pipeline_mode=pl.Buffered(1))
    row = pl.BlockSpec((tm, D_MODEL), lambda i: (i, 0))
    vec = const((1, D_MODEL), lambda i: (0, 0))
    return pl.pallas_call(
        functools.partial(_mlp_kernel, final_norm=final_norm),
        grid=(n // tm,),
        in_specs=[row, vec, const((D_MODEL, D_FF), lambda i: (0, 0)), const((D_FF, D_MODEL), lambda i: (0, 0)), vec],
        out_specs=row,
        out_shape=jax.ShapeDtypeStruct((n, D_MODEL), F32),
        compiler_params=pltpu.CompilerParams(dimension_semantics=("parallel",), vmem_limit_bytes=VMEM_LIMIT),
        name="mlp",
    )(x2, g, wu, wd, gf)


def kernel(x, mem, rp_table, norm_mix, w_in, w_gate_up, b_gate, g_norm, w_out, norm_xattn, norm_mem, w_xq, w_xkv,
           w_xo, norm_mlp, w_up, w_down, norm_final):
    batch, seq, _ = x.shape
    depth = w_in.shape[0]
    assert seq % ROW_TILE == 0 and seq % MOBA_BLOCK == 0 and seq // MOBA_BLOCK <= 8
    x2 = x.reshape(batch * seq, D_MODEL)
    bias = _bias_tiles(rp_table)
    c31 = rp_table[RP_BUCKETS - 1]
    glr_lo = 3 * A_WIDTH + 2 * G_KEY_WIDTH + G_WIDTH
    for l in range(depth):
        wl = w_in[l]
        w = jnp.concatenate([wl[:, :glr_lo], wl[:, glr_lo + G_GATE_RANK:], wl[:, glr_lo:glr_lo + G_GATE_RANK],
                             jnp.zeros((D_MODEL, LANES - G_GATE_RANK), F32)], axis=1).astype(BF16)
        wgu = jnp.concatenate([w_gate_up[l], jnp.zeros((LANES - G_GATE_RANK, G_KEY_WIDTH), F32)], axis=0).astype(BF16)
        qkv, qg, kg, la, vg, rg = _inproj(x2, norm_mix[l][None], w, wgu, b_gate[l][None])
        oa = _moba(qkv, bias, c31, batch, seq)
        og = _gla(qg, kg, la, vg, rg, g_norm[l][None], batch, seq)
        kx, vx = _memkv(mem, norm_mem[l][None], w_xkv[l].astype(BF16))
        x2 = _post(x2, oa, og, w_out[l].astype(BF16), norm_xattn[l][None], w_xq[l].astype(BF16), kx, vx,
                   w_xo[l].astype(BF16), seq)
        last = l == depth - 1
        x2 = _mlp(x2, norm_mlp[l][None], w_up[l].astype(BF16), w_down[l].astype(BF16), norm_final[None], last)
    return x2.reshape(batch, seq, D_MODEL)
```

```python
import functools
import math

import numpy as np
import jax
import jax.numpy as jnp
from jax import lax
from jax.experimental import pallas as pl
from jax.experimental.pallas import tpu as pltpu

F32 = jnp.float32
BF16 = jnp.bfloat16

D_MODEL = 1024
A_HEADS = 8
A_HEAD_DIM = 64
A_WIDTH = A_HEADS * A_HEAD_DIM
MOBA_BLOCK = 256
MOBA_TOPK = 3
G_HEADS = 4
G_WIDTH = D_MODEL - A_WIDTH
G_HEAD_V = G_WIDTH // G_HEADS
G_KEY_WIDTH = G_WIDTH // 2
G_HEAD_K = G_KEY_WIDTH // G_HEADS
G_GATE_RANK = 16
G_GATE_NORM = 16.0
X_HEADS = 4
X_HEAD_DIM = D_MODEL // X_HEADS
D_FF = 4 * D_MODEL
RP_BUCKETS = 32
RP_MAX_DIST = 128
EPS = 1e-6

LANES = 128
NEG = -1e30
GLA_CHUNK = 128
GLA_STEP = 512
LOG2E = float(np.log2(np.e))
ROW_TILE = 512
VMEM_LIMIT = 56 * 1024 * 1024

_C_QKV = 3 * A_WIDTH
_C_QG = _C_QKV
_C_KG = _C_QG + G_KEY_WIDTH
_C_VG = _C_KG + G_KEY_WIDTH
_C_RG = _C_VG + G_WIDTH
_C_GLR = _C_RG + G_WIDTH
_C_END = _C_GLR + LANES


def _nt(a, b):
    return lax.dot_general(a, b, (((1,), (1,)), ((), ())), preferred_element_type=F32)


def _tn(a, b):
    return lax.dot_general(a, b, (((0,), (0,)), ((), ())), preferred_element_type=F32)


def _dot(a, b):
    return jnp.dot(a, b, preferred_element_type=F32)


def _rms(x, g):
    return x * lax.rsqrt(jnp.mean(x * x, axis=-1, keepdims=True) + EPS) * g


def _bucket_thresholds():
    max_exact = RP_BUCKETS // 2
    d = np.arange(1, 4 * RP_MAX_DIST)
    val = (np.log(d.astype(np.float32) / np.float32(max_exact)) / np.float32(math.log(RP_MAX_DIST / max_exact))
           * np.float32(RP_BUCKETS - max_exact))
    bucket = np.minimum(max_exact + val.astype(np.int32), RP_BUCKETS - 1)
    return [int(d[(d >= max_exact) & (bucket >= max_exact + k)][0]) for k in range(1, RP_BUCKETS - max_exact)]


_THRESHOLDS = _bucket_thresholds()


def _bias_kernel(tab_ref, out_ref):
    h = pl.program_id(0)
    shape = (MOBA_BLOCK, 2 * MOBA_BLOCK)
    d = lax.broadcasted_iota(jnp.int32, shape, 0) - lax.broadcasted_iota(jnp.int32, shape, 1) + MOBA_BLOCK
    max_exact = RP_BUCKETS // 2
    bucket = jnp.where(d < max_exact, d, max_exact)
    for t in _THRESHOLDS:
        bucket = bucket + jnp.where(d >= t, 1, 0)
    m = jnp.full(shape, NEG, F32)
    for b in range(RP_BUCKETS):
        m = jnp.where(bucket == b, tab_ref[b, h], m)
    out_ref[0] = m


def _bias_tiles(rp_table):
    return pl.pallas_call(
        _bias_kernel,
        grid=(A_HEADS,),
        in_specs=[pl.BlockSpec(memory_space=pltpu.SMEM)],
        out_specs=pl.BlockSpec((1, MOBA_BLOCK, 2 * MOBA_BLOCK), lambda h: (h, 0, 0)),
        out_shape=jax.ShapeDtypeStruct((A_HEADS, MOBA_BLOCK, 2 * MOBA_BLOCK), F32),
        name="bias",
    )(rp_table)


def _inproj_kernel(x_ref, g_ref, w_ref, wgu_ref, bg_ref, qkv_ref, qg_ref, kg_ref, la_ref, vg_ref, rg_ref):
    h = _rms(x_ref[...], g_ref[...]).astype(BF16)

    def proj(lo, hi):
        return _dot(h, w_ref[:, lo:hi])

    qkv_ref[:, 0:A_WIDTH] = (proj(0, A_WIDTH) * (A_HEAD_DIM ** -0.5)).astype(BF16)
    qkv_ref[:, A_WIDTH:_C_QKV] = proj(A_WIDTH, _C_QKV).astype(BF16)
    qg_ref[...] = proj(_C_QG, _C_KG) * (G_HEAD_K ** -0.5)
    kg_ref[...] = proj(_C_KG, _C_VG)
    vg_ref[...] = proj(_C_VG, _C_RG).astype(BF16)
    rg_ref[...] = proj(_C_RG, _C_GLR)
    glr = proj(_C_GLR, _C_END).astype(BF16)
    z = _dot(glr, wgu_ref[...]) + bg_ref[...]
    log_sig = jnp.minimum(z, 0.0) - jnp.log(1.0 + jnp.exp(-jnp.abs(z)))
    la_ref[...] = log_sig * (1.0 / G_GATE_NORM)


def _inproj(x2, g, w, wgu, bg):
    n = x2.shape[0]
    tm = ROW_TILE
    const = functools.partial(pl.BlockSpec, pipeline_mode=pl.Buffered(1))
    row = lambda width: pl.BlockSpec((tm, width), lambda i: (i, 0))
    return pl.pallas_call(
        _inproj_kernel,
        grid=(n // tm,),
        in_specs=[row(D_MODEL),
                  const((1, D_MODEL), lambda i: (0, 0)),
                  const((D_MODEL, _C_END), lambda i: (0, 0)),
                  const((LANES, G_KEY_WIDTH), lambda i: (0, 0)),
                  const((1, G_KEY_WIDTH), lambda i: (0, 0))],
        out_specs=[row(_C_QKV), row(G_KEY_WIDTH), row(G_KEY_WIDTH), row(G_KEY_WIDTH), row(G_WIDTH), row(G_WIDTH)],
        out_shape=[jax.ShapeDtypeStruct((n, _C_QKV), BF16),
                   jax.ShapeDtypeStruct((n, G_KEY_WIDTH), F32),
                   jax.ShapeDtypeStruct((n, G_KEY_WIDTH), F32),
                   jax.ShapeDtypeStruct((n, G_KEY_WIDTH), F32),
                   jax.ShapeDtypeStruct((n, G_WIDTH), BF16),
                   jax.ShapeDtypeStruct((n, G_WIDTH), F32)],
        compiler_params=pltpu.CompilerParams(dimension_semantics=("parallel",), vmem_limit_bytes=VMEM_LIMIT),
        name="inproj",
    )(x2, g, w, wgu, bg)


def _moba_kernel(c31_ref, q_ref, k_ref, v_ref, bias_ref, o_ref, kext, vext, s_scr, p_scr, *, seq):
    p = pl.program_id(1)
    nblk = seq // MOBA_BLOCK
    half = LANES // 2
    lane = lax.broadcasted_iota(jnp.int32, (1, LANES), 1)

    k = k_ref[...]
    v = v_ref[...]
    rowblk = lax.broadcasted_iota(jnp.int32, (seq, LANES), 0) // MOBA_BLOCK
    lane2 = lax.broadcasted_iota(jnp.int32, (seq, LANES), 1)
    lm0 = jnp.where(lane2 < half, 1.0, 0.0).astype(BF16)
    lm1 = jnp.where(lane2 >= half, 1.0, 0.0).astype(BF16)
    kext[0] = k * lm0 + jnp.where(lane2 - half == rowblk, 1.0, 0.0).astype(BF16)
    kext[1] = k * lm1 + jnp.where(lane2 == rowblk, 1.0, 0.0).astype(BF16)
    vext[0] = v * lm0 + lm1
    vext[1] = v * lm1 + lm0
    n_i = lax.broadcasted_iota(jnp.int32, (16, seq), 0)
    t_i = lax.broadcasted_iota(jnp.int32, (16, seq), 1)
    avg = jnp.where(t_i // MOBA_BLOCK == n_i, 1.0 / MOBA_BLOCK, 0.0).astype(BF16)
    km = _dot(avg, k)
    kmh = km.astype(BF16)
    kml = (km - kmh.astype(F32)).astype(BF16)
    r_i = lax.broadcasted_iota(jnp.int32, (MOBA_BLOCK, MOBA_BLOCK), 0)
    c_i = lax.broadcasted_iota(jnp.int32, (MOBA_BLOCK, MOBA_BLOCK), 1)
    ident = jnp.where(r_i == c_i, 1.0, 0.0).astype(BF16)

    lane_q = lax.broadcasted_iota(jnp.int32, (MOBA_BLOCK, LANES), 1)
    rowi = lax.broadcasted_iota(jnp.int32, (16, MOBA_BLOCK), 0)
    blk = lambda n: slice(n * MOBA_BLOCK, (n + 1) * MOBA_BLOCK)
    for qi in range(nblk):
        q = q_ref[blk(qi), :]
        outs = []
        for h in range(2):
            qm = q * jnp.where((lane_q < half) if h == 0 else (lane_q >= half), 1.0, 0.0).astype(BF16)
            qext = qm
            if qi > MOBA_TOPK:
                g = _nt(kmh, qm) + _nt(kml, qm)
                cnt = jnp.zeros((16, MOBA_BLOCK), F32)
                for m in range(qi):
                    gm = g[m:m + 1, :]
                    cnt = cnt + jnp.where((gm > g) | ((gm == g) & (m < rowi)), 1.0, 0.0)
                keep = ((rowi < qi) & (cnt < MOBA_TOPK)) | (rowi >= qi)
                pen_t = jnp.where(keep, 0.0, NEG).astype(BF16)
                off = half if h == 0 else 0
                parts = ([jnp.zeros((off, MOBA_BLOCK), BF16)] if off else []) + [
                    pen_t, jnp.zeros((LANES - 16 - off, MOBA_BLOCK), BF16)]
                pen_q = _nt(ident, jnp.concatenate(parts, axis=0))
                qext = qm + pen_q.astype(BF16)
            c31 = c31_ref[2 * p + h]
            sbuf = s_scr.at[qi % 2, h]
            pbuf = p_scr.at[qi % 2, h]
            near_mx = far_mx = None
            for n in range(qi + 1):
                s = _nt(qext, kext[h, blk(n), :])
                if n == qi:
                    s = s + bias_ref[h, :, MOBA_BLOCK:2 * MOBA_BLOCK]
                elif n == qi - 1:
                    s = s + bias_ref[h, :, 0:MOBA_BLOCK]
                sbuf[:, blk(n)] = s
                t = jnp.maximum(s[:, :LANES], s[:, LANES:])
                if n < qi - 1:
                    far_mx = t if far_mx is None else jnp.maximum(far_mx, t)
                else:
                    near_mx = t if near_mx is None else jnp.maximum(near_mx, t)
            mx = near_mx if far_mx is None else jnp.maximum(near_mx, far_mx + c31)
            m_near = jnp.max(mx, axis=1, keepdims=True)
            m_far = m_near - c31
            for n in range(qi + 1):
                pbuf[:, blk(n)] = jnp.exp(sbuf[:, blk(n)] - (m_far if n < qi - 1 else m_near)).astype(BF16)
            width = (qi + 1) * MOBA_BLOCK
            acc = _dot(pbuf[:, 0:width], vext[h, 0:width, :])
            outs.append(acc / pltpu.roll(acc, half, 1))
        o_ref[blk(qi), :] = jnp.where(lane < half, outs[0], outs[1]).astype(BF16)


def _moba(qkv, bias, c31, batch, seq):
    n = qkv.shape[0]
    nblk = seq // MOBA_BLOCK
    npair = A_HEADS // 2
    kern = functools.partial(_moba_kernel, seq=seq)
    return pl.pallas_call(
        kern,
        grid=(batch, npair),
        in_specs=[pl.BlockSpec(memory_space=pltpu.SMEM),
                  pl.BlockSpec((seq, LANES), lambda b, p: (b, p)),
                  pl.BlockSpec((seq, LANES), lambda b, p: (b, npair + p)),
                  pl.BlockSpec((seq, LANES), lambda b, p: (b, 2 * npair + p)),
                  pl.BlockSpec((2, MOBA_BLOCK, 2 * MOBA_BLOCK), lambda b, p: (p, 0, 0))],
        out_specs=pl.BlockSpec((seq, LANES), lambda b, p: (b, p)),
        out_shape=jax.ShapeDtypeStruct((n, A_WIDTH), BF16),
        scratch_shapes=[pltpu.VMEM((2, seq, LANES), BF16),
                        pltpu.VMEM((2, seq, LANES), BF16),
                        pltpu.VMEM((2, 2, MOBA_BLOCK, seq), F32),
                        pltpu.VMEM((2, 2, MOBA_BLOCK, seq), BF16)],
        compiler_params=pltpu.CompilerParams(dimension_semantics=("parallel", "arbitrary"),
                                             vmem_limit_bytes=VMEM_LIMIT),
        name="moba",
    )(c31, qkv, qkv, qkv, bias)


_GLA_LEVELS = int(math.log2(GLA_CHUNK))


def _gla_weights():
    c = GLA_CHUNK
    i = np.arange(c)[:, None]
    j = np.arange(c)[None, :]
    mats = [(j <= i)]
    for lvl in range(_GLA_LEVELS):
        s = (c // 2) >> lvl
        ref = (i // (2 * s)) * (2 * s) + s - 1
        mats.append(np.where(i > ref, (j > ref) & (j <= i), (j > i) & (j <= ref)))
    return np.concatenate(mats, axis=0).astype(np.float32)


def _gla_level_map():
    c = GLA_CHUNK
    i = np.arange(c)[:, None]
    j = np.arange(c)[None, :]
    top_bit = np.floor(np.log2(np.maximum(i ^ j, 1))).astype(np.int64)
    lvl = np.where(j < i, _GLA_LEVELS - 1 - top_bit, np.where(j == i, _GLA_LEVELS, _GLA_LEVELS + 1))
    return np.concatenate([lvl, lvl], axis=1).astype(np.int32)


def _gla_kernel(q_ref, k_ref, la_ref, v_ref, rg_ref, w_ref, lmap_ref, gn_ref, o_ref, st_ref):
    c = GLA_CHUNK
    half = LANES // 2
    npair = G_HEADS // 2

    @pl.when(pl.program_id(1) == 0)
    def _init():
        st_ref[...] = jnp.zeros(st_ref.shape, F32)

    w = w_ref[...]
    lmap = lmap_ref[...]
    lane = lax.broadcasted_iota(jnp.int32, (1, LANES), 1)
    lane_c = lax.broadcasted_iota(jnp.int32, (c, LANES), 1)
    lm = [jnp.where(lane_c < half, 1.0, 0.0).astype(BF16), jnp.where(lane_c >= half, 1.0, 0.0).astype(BF16)]
    row_small = lax.broadcasted_iota(jnp.int32, (c, G_KEY_WIDTH), 0)

    def split_rows(x, e, s, want_odd):
        if s < 8:
            odd = (row_small // s) % 2 == 1
            return jnp.where(odd if want_odd else ~odd, x * e, 0.0)
        pieces = []
        for blk in range(c // s):
            rows = slice(blk * s, (blk + 1) * s)
            pieces.append(x[rows] * e[rows] if (blk % 2 == 1) == want_odd else jnp.zeros((s, x.shape[1]), F32))
        return jnp.concatenate(pieces, axis=0)

    states = [st_ref[p] for p in range(npair)]
    for ch in range(GLA_STEP // c):
        rows = slice(ch * c, (ch + 1) * c)
        la = la_ref[rows, :] * LOG2E
        hi = la.astype(BF16)
        lo = (la - hi.astype(F32)).astype(BF16)
        d_all = _dot(w, hi) + _dot(w, lo)
        q = q_ref[rows, :]
        k = k_ref[rows, :]
        b = d_all[0:c]
        blast = b[c - 1:c, :]
        qts, kts = [], []
        for lvl in range(_GLA_LEVELS):
            s = (c // 2) >> lvl
            e = jnp.exp2(d_all[c * (lvl + 1):c * (lvl + 2)])
            qts.append(split_rows(q, e, s, True).astype(BF16))
            kts.append(split_rows(k, e, s, False).astype(BF16))
        qts.append(q.astype(BF16))
        kts.append(k.astype(BF16))
        qe = (q * jnp.exp2(b)).astype(BF16)
        khat = (k * jnp.exp2(blast - b)).astype(BF16)
        dec = jnp.exp2(blast)
        for p in range(npair):
            sl = slice(LANES * p, LANES * (p + 1))
            a = jnp.zeros((c, 2 * c), F32)
            for lvl, (ql, kl) in enumerate(zip(qts, kts)):
                kp = kl[:, sl]
                pm = _nt(ql[:, sl], jnp.concatenate([kp * lm[0], kp * lm[1]], axis=0))
                a = jnp.where(lmap == lvl, pm, a)
            stb = states[p].astype(BF16)
            ups = []
            for hh in range(2):
                h = 2 * p + hh
                vh = v_ref[rows, LANES * h:LANES * (h + 1)]
                o = _dot(a[:, hh * c:(hh + 1) * c].astype(BF16), vh) + _nt(qe[:, sl] * lm[hh], stb)
                y = _rms(o, gn_ref[...])
                rg = rg_ref[rows, LANES * h:LANES * (h + 1)]
                y = y * (rg / (1.0 + jnp.exp(-rg)))
                o_ref[rows, LANES * h:LANES * (h + 1)] = y.astype(BF16)
                ups.append(_tn(vh, khat[:, sl]))
            states[p] = states[p] * dec[:, sl] + jnp.where(lane < half, ups[0], ups[1])
    for p in range(npair):
        st_ref[p] = states[p]


def _gla(qg, kg, la, vg, rg, gn, batch, seq):
    n = qg.shape[0]
    nchunk = seq // GLA_STEP
    w = jnp.asarray(_gla_weights(), BF16)
    lmap = jnp.asarray(_gla_level_map())
    row = lambda width: pl.BlockSpec((GLA_STEP, width), lambda b, c: (b * nchunk + c, 0))
    const = functools.partial(pl.BlockSpec, pipeline_mode=pl.Buffered(1))
    return pl.pallas_call(
        _gla_kernel,
        grid=(batch, nchunk),
        in_specs=[row(G_KEY_WIDTH), row(G_KEY_WIDTH), row(G_KEY_WIDTH), row(G_WIDTH), row(G_WIDTH),
                  const(w.shape, lambda b, c: (0, 0)),
                  const(lmap.shape, lambda b, c: (0, 0)),
                  const((1, G_HEAD_V), lambda b, c: (0, 0))],
        out_specs=row(G_WIDTH),
        out_shape=jax.ShapeDtypeStruct((n, G_WIDTH), BF16),
        scratch_shapes=[pltpu.VMEM((G_HEADS // 2, G_HEAD_V, LANES), F32)],
        compiler_params=pltpu.CompilerParams(dimension_semantics=("parallel", "arbitrary"),
                                             vmem_limit_bytes=VMEM_LIMIT),
        name="gla",
    )(qg, kg, la, vg, rg, w, lmap, gn)


def _memkv_kernel(m_ref, g_ref, w_ref, k_ref, v_ref):
    h = _rms(m_ref[0], g_ref[...]).astype(BF16)
    k_ref[0] = _dot(h, w_ref[:, 0:D_MODEL]).astype(BF16)
    v_ref[0] = _dot(h, w_ref[:, D_MODEL:2 * D_MODEL]).astype(BF16)


def _memkv(mem, g, w):
    batch, mlen, _ = mem.shape
    const = functools.partial(pl.BlockSpec, pipeline_mode=pl.Buffered(1))
    blk = pl.BlockSpec((1, mlen, D_MODEL), lambda b: (b, 0, 0))
    return pl.pallas_call(
        _memkv_kernel,
        grid=(batch,),
        in_specs=[blk, const((1, D_MODEL), lambda b: (0, 0)), const((D_MODEL, 2 * D_MODEL), lambda b: (0, 0))],
        out_specs=[blk, blk],
        out_shape=[jax.ShapeDtypeStruct(mem.shape, BF16)] * 2,
        compiler_params=pltpu.CompilerParams(dimension_semantics=("parallel",), vmem_limit_bytes=VMEM_LIMIT),
        name="memkv",
    )(mem, g, w)


def _post_kernel(x_ref, oa_ref, og_ref, wo_ref, gx_ref, wq_ref, kx_ref, vx_ref, wxo_ref, o_ref):
    x1 = x_ref[...] + _dot(oa_ref[...], wo_ref[0:A_WIDTH, :]) + _dot(og_ref[...], wo_ref[A_WIDTH:D_MODEL, :])
    h = _rms(x1, gx_ref[...]).astype(BF16)
    qx = (_dot(h, wq_ref[...]) * (X_HEAD_DIM ** -0.5)).astype(BF16)
    outs = []
    for hd in range(X_HEADS):
        sl = slice(X_HEAD_DIM * hd, X_HEAD_DIM * (hd + 1))
        s = _nt(qx[:, sl], kx_ref[0, :, sl])
        pexp = jnp.exp(s - jnp.max(s, axis=1, keepdims=True))
        pn = (pexp / jnp.sum(pexp, axis=1, keepdims=True)).astype(BF16)
        outs.append(_dot(pn, vx_ref[0, :, sl]).astype(BF16))
    ox = jnp.concatenate(outs, axis=1)
    o_ref[...] = x1 + _dot(ox, wxo_ref[...])


def _post(x2, oa, og, wo, gx, wq, kx, vx, wxo, seq):
    n = x2.shape[0]
    tm = ROW_TILE
    per_b = seq // tm
    mlen = kx.shape[1]
    const = functools.partial(pl.BlockSpec, pipeline_mode=pl.Buffered(1))
    row = lambda width: pl.BlockSpec((tm, width), lambda i: (i, 0))
    mem = pl.BlockSpec((1, mlen, D_MODEL), lambda i: (i // per_b, 0, 0))
    sq = const((D_MODEL, D_MODEL), lambda i: (0, 0))
    return pl.pallas_call(
        _post_kernel,
        grid=(n // tm,),
        in_specs=[row(D_MODEL), row(A_WIDTH), row(G_WIDTH), sq, const((1, D_MODEL), lambda i: (0, 0)), sq, mem, mem, sq],
        out_specs=row(D_MODEL),
        out_shape=jax.ShapeDtypeStruct((n, D_MODEL), F32),
        compiler_params=pltpu.CompilerParams(dimension_semantics=("parallel",), vmem_limit_bytes=VMEM_LIMIT),
        name="post",
    )(x2, oa, og, wo, gx, wq, kx, vx, wxo)


def _mlp_kernel(x_ref, g_ref, wu_ref, wd_ref, gf_ref, o_ref, *, final_norm):
    x = x_ref[...]
    h = _rms(x, g_ref[...]).astype(BF16)
    acc = x
    for c in range(D_FF // D_MODEL):
        sl = slice(D_MODEL * c, D_MODEL * (c + 1))
        u = jnp.square(jnp.maximum(_dot(h, wu_ref[:, sl]), 0.0)).astype(BF16)
        acc = acc + _dot(u, wd_ref[sl, :])
    o_ref[...] = _rms(acc, gf_ref[...]) if final_norm else acc


def _mlp(x2, g, wu, wd, gf, final_norm):
    n = x2.shape[0]
    tm = ROW_TILE
    const = functools.partial(pl.BlockSpec, pipeline_mode=pl.Buffered(1))
    row = pl.BlockSpec((tm, D_MODEL), lambda i: (i, 0))
    vec = const((1, D_MODEL), lambda i: (0, 0))
    return pl.pallas_call(
        functools.partial(_mlp_kernel, final_norm=final_norm),
        grid=(n // tm,),
        in_specs=[row, vec, const((D_MODEL, D_FF), lambda i: (0, 0)), const((D_FF, D_MODEL), lambda i: (0, 0)), vec],
        out_specs=row,
        out_shape=jax.ShapeDtypeStruct((n, D_MODEL), F32),
        compiler_params=pltpu.CompilerParams(dimension_semantics=("parallel",), vmem_limit_bytes=VMEM_LIMIT),
        name="mlp",
    )(x2, g, wu, wd, gf)


def kernel(x, mem, rp_table, norm_mix, w_in, w_gate_up, b_gate, g_norm, w_out, norm_xattn, norm_mem, w_xq, w_xkv,
           w_xo, norm_mlp, w_up, w_down, norm_final):
    batch, seq, _ = x.shape
    depth = w_in.shape[0]
    assert seq % ROW_TILE == 0 and seq % MOBA_BLOCK == 0 and seq // MOBA_BLOCK <= 8
    x2 = x.reshape(batch * seq, D_MODEL)
    bias = _bias_tiles(rp_table)
    c31 = rp_table[RP_BUCKETS - 1]
    glr_lo = 3 * A_WIDTH + 2 * G_KEY_WIDTH + G_WIDTH
    for l in range(depth):
        wl = w_in[l]
        w = jnp.concatenate([wl[:, :glr_lo], wl[:, glr_lo + G_GATE_RANK:], wl[:, glr_lo:glr_lo + G_GATE_RANK],
                             jnp.zeros((D_MODEL, LANES - G_GATE_RANK), F32)], axis=1).astype(BF16)
        wgu = jnp.concatenate([w_gate_up[l], jnp.zeros((LANES - G_GATE_RANK, G_KEY_WIDTH), F32)], axis=0).astype(BF16)
        qkv, qg, kg, la, vg, rg = _inproj(x2, norm_mix[l][None], w, wgu, b_gate[l][None])
        oa = _moba(qkv, bias, c31, batch, seq)
        og = _gla(qg, kg, la, vg, rg, g_norm[l][None], batch, seq)
        kx, vx = _memkv(mem, norm_mem[l][None], w_xkv[l].astype(BF16))
        x2 = _post(x2, oa, og, w_out[l].astype(BF16), norm_xattn[l][None], w_xq[l].astype(BF16), kx, vx,
                   w_xo[l].astype(BF16), seq)
        last = l == depth - 1
        x2 = _mlp(x2, norm_mlp[l][None], w_up[l].astype(BF16), w_down[l].astype(BF16), norm_final[None], last)
    return x2.reshape(batch, seq, D_MODEL)
```

```python
import functools
import math

import numpy as np
import jax
import jax.numpy as jnp
from jax import lax
from jax.experimental import pallas as pl
from jax.experimental.pallas import tpu as pltpu

F32 = jnp.float32
BF16 = jnp.bfloat16

D_MODEL = 1024
A_HEADS = 8
A_HEAD_DIM = 64
A_WIDTH = A_HEADS * A_HEAD_DIM
MOBA_BLOCK = 256
MOBA_TOPK = 3
G_HEADS = 4
G_WIDTH = D_MODEL - A_WIDTH
G_HEAD_V = G_WIDTH // G_HEADS
G_KEY_WIDTH = G_WIDTH // 2
G_HEAD_K = G_KEY_WIDTH // G_HEADS
G_GATE_RANK = 16
G_GATE_NORM = 16.0
X_HEADS = 4
X_HEAD_DIM = D_MODEL // X_HEADS
D_FF = 4 * D_MODEL
RP_BUCKETS = 32
RP_MAX_DIST = 128
EPS = 1e-6

LANES = 128
NEG = -1e30
GLA_CHUNK = 128
GLA_STEP = 512
LOG2E = float(np.log2(np.e))
ROW_TILE = 512
VMEM_LIMIT = 56 * 1024 * 1024

_C_QKV = 3 * A_WIDTH
_C_QG = _C_QKV
_C_KG = _C_QG + G_KEY_WIDTH
_C_VG = _C_KG + G_KEY_WIDTH
_C_RG = _C_VG + G_WIDTH
_C_GLR = _C_RG + G_WIDTH
_C_END = _C_GLR + LANES


def _nt(a, b):
    return lax.dot_general(a, b, (((1,), (1,)), ((), ())), preferred_element_type=F32)


def _tn(a, b):
    return lax.dot_general(a, b, (((0,), (0,)), ((), ())), preferred_element_type=F32)


def _dot(a, b):
    return jnp.dot(a, b, preferred_element_type=F32)


def _rms(x, g):
    return x * lax.rsqrt(jnp.mean(x * x, axis=-1, keepdims=True) + EPS) * g


def _bucket_thresholds():
    max_exact = RP_BUCKETS // 2
    d = np.arange(1, 4 * RP_MAX_DIST)
    val = (np.log(d.astype(np.float32) / np.float32(max_exact)) / np.float32(math.log(RP_MAX_DIST / max_exact))
           * np.float32(RP_BUCKETS - max_exact))
    bucket = np.minimum(max_exact + val.astype(np.int32), RP_BUCKETS - 1)
    return [int(d[(d >= max_exact) & (bucket >= max_exact + k)][0]) for k in range(1, RP_BUCKETS - max_exact)]


_THRESHOLDS = _bucket_thresholds()


def _bias_kernel(tab_ref, out_ref):
    h = pl.program_id(0)
    shape = (MOBA_BLOCK, 2 * MOBA_BLOCK)
    d = lax.broadcasted_iota(jnp.int32, shape, 0) - lax.broadcasted_iota(jnp.int32, shape, 1) + MOBA_BLOCK
    max_exact = RP_BUCKETS // 2
    bucket = jnp.where(d < max_exact, d, max_exact)
    for t in _THRESHOLDS:
        bucket = bucket + jnp.where(d >= t, 1, 0)
    m = jnp.full(shape, NEG, F32)
    for b in range(RP_BUCKETS):
        m = jnp.where(bucket == b, tab_ref[b, h], m)
    out_ref[0] = m


def _bias_tiles(rp_table):
    return pl.pallas_call(
        _bias_kernel,
        grid=(A_HEADS,),
        in_specs=[pl.BlockSpec(memory_space=pltpu.SMEM)],
        out_specs=pl.BlockSpec((1, MOBA_BLOCK, 2 * MOBA_BLOCK), lambda h: (h, 0, 0)),
        out_shape=jax.ShapeDtypeStruct((A_HEADS, MOBA_BLOCK, 2 * MOBA_BLOCK), F32),
        name="bias",
    )(rp_table)


def _inproj_kernel(x_ref, g_ref, w_ref, wgu_ref, bg_ref, qkv_ref, qg_ref, kg_ref, la_ref, vg_ref, rg_ref):
    h = _rms(x_ref[...], g_ref[...]).astype(BF16)

    def proj(lo, hi):
        return _dot(h, w_ref[:, lo:hi])

    qkv_ref[:, 0:A_WIDTH] = (proj(0, A_WIDTH) * (A_HEAD_DIM ** -0.5)).astype(BF16)
    qkv_ref[:, A_WIDTH:_C_QKV] = proj(A_WIDTH, _C_QKV).astype(BF16)
    qg_ref[...] = proj(_C_QG, _C_KG) * (G_HEAD_K ** -0.5)
    kg_ref[...] = proj(_C_KG, _C_VG)
    vg_ref[...] = proj(_C_VG, _C_RG).astype(BF16)
    rg_ref[...] = proj(_C_RG, _C_GLR)
    glr = proj(_C_GLR, _C_END).astype(BF16)
    z = _dot(glr, wgu_ref[...]) + bg_ref[...]
    log_sig = jnp.minimum(z, 0.0) - jnp.log(1.0 + jnp.exp(-jnp.abs(z)))
    la_ref[...] = log_sig * (1.0 / G_GATE_NORM)


def _inproj(x2, g, w, wgu, bg):
    n = x2.shape[0]
    tm = ROW_TILE
    const = functools.partial(pl.BlockSpec, pipeline_mode=pl.Buffered(1))
    row = lambda width: pl.BlockSpec((tm, width), lambda i: (i, 0))
    return pl.pallas_call(
        _inproj_kernel,
        grid=(n // tm,),
        in_specs=[row(D_MODEL),
                  const((1, D_MODEL), lambda i: (0, 0)),
                  const((D_MODEL, _C_END), lambda i: (0, 0)),
                  const((LANES, G_KEY_WIDTH), lambda i: (0, 0)),
                  const((1, G_KEY_WIDTH), lambda i: (0, 0))],
        out_specs=[row(_C_QKV), row(G_KEY_WIDTH), row(G_KEY_WIDTH), row(G_KEY_WIDTH), row(G_WIDTH), row(G_WIDTH)],
        out_shape=[jax.ShapeDtypeStruct((n, _C_QKV), BF16),
                   jax.ShapeDtypeStruct((n, G_KEY_WIDTH), F32),
                   jax.ShapeDtypeStruct((n, G_KEY_WIDTH), F32),
                   jax.ShapeDtypeStruct((n, G_KEY_WIDTH), F32),
                   jax.ShapeDtypeStruct((n, G_WIDTH), BF16),
                   jax.ShapeDtypeStruct((n, G_WIDTH), F32)],
        compiler_params=pltpu.CompilerParams(dimension_semantics=("parallel",), vmem_limit_bytes=VMEM_LIMIT),
        name="inproj",
    )(x2, g, w, wgu, bg)


def _moba_kernel(c31_ref, q_ref, k_ref, v_ref, bias_ref, o_ref, kext, vext, s_scr, *, seq):
    p = pl.program_id(1)
    nblk = seq // MOBA_BLOCK
    half = LANES // 2
    lane = lax.broadcasted_iota(jnp.int32, (1, LANES), 1)

    k = k_ref[...]
    v = v_ref[...]
    rowblk = lax.broadcasted_iota(jnp.int32, (seq, LANES), 0) // MOBA_BLOCK
    lane2 = lax.broadcasted_iota(jnp.int32, (seq, LANES), 1)
    lm0 = jnp.where(lane2 < half, 1.0, 0.0).astype(BF16)
    lm1 = jnp.where(lane2 >= half, 1.0, 0.0).astype(BF16)
    kext[0] = k * lm0 + jnp.where(lane2 - half == rowblk, 1.0, 0.0).astype(BF16)
    kext[1] = k * lm1 + jnp.where(lane2 == rowblk, 1.0, 0.0).astype(BF16)
    vext[0] = v * lm0 + lm1
    vext[1] = v * lm1 + lm0
    n_i = lax.broadcasted_iota(jnp.int32, (16, seq), 0)
    t_i = lax.broadcasted_iota(jnp.int32, (16, seq), 1)
    avg = jnp.where(t_i // MOBA_BLOCK == n_i, 1.0 / MOBA_BLOCK, 0.0).astype(BF16)
    km = _dot(avg, k)
    kmh = km.astype(BF16)
    kml = (km - kmh.astype(F32)).astype(BF16)
    r_i = lax.broadcasted_iota(jnp.int32, (MOBA_BLOCK, MOBA_BLOCK), 0)
    c_i = lax.broadcasted_iota(jnp.int32, (MOBA_BLOCK, MOBA_BLOCK), 1)
    ident = jnp.where(r_i == c_i, 1.0, 0.0).astype(BF16)

    lane_q = lax.broadcasted_iota(jnp.int32, (MOBA_BLOCK, LANES), 1)
    rowi = lax.broadcasted_iota(jnp.int32, (16, MOBA_BLOCK), 0)
    blk = lambda n: slice(n * MOBA_BLOCK, (n + 1) * MOBA_BLOCK)

    def prepare(qi, h, slot):
        qm = q_ref[blk(qi), :] * jnp.where((lane_q < half) if h == 0 else (lane_q >= half), 1.0, 0.0).astype(BF16)
        qext = qm
        if qi > MOBA_TOPK:
            g = _nt(kmh, qm) + _nt(kml, qm)
            cnt = jnp.zeros((16, MOBA_BLOCK), F32)
            for m in range(qi):
                gm = g[m:m + 1, :]
                cnt = cnt + jnp.where((gm > g) | ((gm == g) & (m < rowi)), 1.0, 0.0)
            keep = ((rowi < qi) & (cnt < MOBA_TOPK)) | (rowi >= qi)
            pen_t = jnp.where(keep, 0.0, NEG).astype(BF16)
            off = half if h == 0 else 0
            parts = ([jnp.zeros((off, MOBA_BLOCK), BF16)] if off else []) + [
                pen_t, jnp.zeros((LANES - 16 - off, MOBA_BLOCK), BF16)]
            pen_q = _nt(ident, jnp.concatenate(parts, axis=0))
            qext = qm + pen_q.astype(BF16)
        return dict(qi=qi, h=h, qext=qext, c31=c31_ref[2 * p + h], sbuf=s_scr.at[slot], near=None, far=None,
                    acc=None)

    def score_block(u, n):
        qi, h = u["qi"], u["h"]
        s = _nt(u["qext"], kext[h, blk(n), :])
        if n == qi:
            s = s + bias_ref[h, :, MOBA_BLOCK:2 * MOBA_BLOCK]
        elif n == qi - 1:
            s = s + bias_ref[h, :, 0:MOBA_BLOCK]
        u["sbuf"][:, blk(n)] = s
        t = jnp.maximum(s[:, :LANES], s[:, LANES:])
        key = "far" if n < qi - 1 else "near"
        u[key] = t if u[key] is None else jnp.maximum(u[key], t)

    def finish_scores(u):
        mx = u["near"] if u["far"] is None else jnp.maximum(u["near"], u["far"] + u["c31"])
        u["m_near"] = jnp.max(mx, axis=1, keepdims=True)
        u["m_far"] = u["m_near"] - u["c31"]

    def prob_block(u, n):
        m = u["m_far"] if n < u["qi"] - 1 else u["m_near"]
        pv = _dot(jnp.exp(u["sbuf"][:, blk(n)] - m).astype(BF16), vext[u["h"], blk(n), :])
        u["acc"] = pv if u["acc"] is None else u["acc"] + pv

    outs = {}

    def finish(u):
        acc = u["acc"]
        outs[u["h"]] = acc / pltpu.roll(acc, half, 1)
        if u["h"] == 1:
            o_ref[blk(u["qi"]), :] = jnp.where(lane < half, outs[0], outs[1]).astype(BF16)

    prev = None
    for idx, (qi, h) in enumerate([(qi, h) for qi in range(nblk) for h in range(2)]):
        cur = prepare(qi, h, idx % 2)
        for n in range(max(qi + 1, prev["qi"] + 1 if prev else 0)):
            if n <= qi:
                score_block(cur, n)
            if prev is not None and n <= prev["qi"]:
                prob_block(prev, n)
        finish_scores(cur)
        if prev is not None:
            finish(prev)
        prev = cur
    for n in range(prev["qi"] + 1):
        prob_block(prev, n)
    finish(prev)


def _moba(qkv, bias, c31, batch, seq):
    n = qkv.shape[0]
    nblk = seq // MOBA_BLOCK
    npair = A_HEADS // 2
    kern = functools.partial(_moba_kernel, seq=seq)
    return pl.pallas_call(
        kern,
        grid=(batch, npair),
        in_specs=[pl.BlockSpec(memory_space=pltpu.SMEM),
                  pl.BlockSpec((seq, LANES), lambda b, p: (b, p)),
                  pl.BlockSpec((seq, LANES), lambda b, p: (b, npair + p)),
                  pl.BlockSpec((seq, LANES), lambda b, p: (b, 2 * npair + p)),
                  pl.BlockSpec((2, MOBA_BLOCK, 2 * MOBA_BLOCK), lambda b, p: (p, 0, 0))],
        out_specs=pl.BlockSpec((seq, LANES), lambda b, p: (b, p)),
        out_shape=jax.ShapeDtypeStruct((n, A_WIDTH), BF16),
        scratch_shapes=[pltpu.VMEM((2, seq, LANES), BF16),
                        pltpu.VMEM((2, seq, LANES), BF16),
                        pltpu.VMEM((2, MOBA_BLOCK, seq), F32)],
        compiler_params=pltpu.CompilerParams(dimension_semantics=("parallel", "arbitrary"),
                                             vmem_limit_bytes=VMEM_LIMIT),
        name="moba",
    )(c31, qkv, qkv, qkv, bias)


_GLA_LEVELS = int(math.log2(GLA_CHUNK))


def _gla_weights():
    c = GLA_CHUNK
    i = np.arange(c)[:, None]
    j = np.arange(c)[None, :]
    mats = [(j <= i)]
    for lvl in range(_GLA_LEVELS):
        s = (c // 2) >> lvl
        ref = (i // (2 * s)) * (2 * s) + s - 1
        mats.append(np.where(i > ref, (j > ref) & (j <= i), (j > i) & (j <= ref)))
    return np.concatenate(mats, axis=0).astype(np.float32)


def _gla_level_map():
    c = GLA_CHUNK
    i = np.arange(c)[:, None]
    j = np.arange(c)[None, :]
    top_bit = np.floor(np.log2(np.maximum(i ^ j, 1))).astype(np.int64)
    lvl = np.where(j < i, _GLA_LEVELS - 1 - top_bit, np.where(j == i, _GLA_LEVELS, _GLA_LEVELS + 1))
    return np.concatenate([lvl, lvl], axis=1).astype(np.int32)


def _gla_kernel(q_ref, k_ref, la_ref, v_ref, rg_ref, w_ref, lmap_ref, gn_ref, o_ref, st_ref):
    c = GLA_CHUNK
    half = LANES // 2
    npair = G_HEADS // 2

    @pl.when(pl.program_id(1) == 0)
    def _init():
        st_ref[...] = jnp.zeros(st_ref.shape, F32)

    w = w_ref[...]
    lmap = lmap_ref[...]
    lane = lax.broadcasted_iota(jnp.int32, (1, LANES), 1)
    lane_c = lax.broadcasted_iota(jnp.int32, (c, LANES), 1)
    lm = [jnp.where(lane_c < half, 1.0, 0.0).astype(BF16), jnp.where(lane_c >= half, 1.0, 0.0).astype(BF16)]
    row_small = lax.broadcasted_iota(jnp.int32, (c, G_KEY_WIDTH), 0)

    def split_rows(x, e, s, want_odd):
        if s < 8:
            odd = (row_small // s) % 2 == 1
            return jnp.where(odd if want_odd else ~odd, x * e, 0.0)
        pieces = []
        for blk in range(c // s):
            rows = slice(blk * s, (blk + 1) * s)
            pieces.append(x[rows] * e[rows] if (blk % 2 == 1) == want_odd else jnp.zeros((s, x.shape[1]), F32))
        return jnp.concatenate(pieces, axis=0)

    states = [st_ref[p] for p in range(npair)]
    for ch in range(GLA_STEP // c):
        rows = slice(ch * c, (ch + 1) * c)
        la = la_ref[rows, :] * LOG2E
        hi = la.astype(BF16)
        lo = (la - hi.astype(F32)).astype(BF16)
        d_all = _dot(w, hi) + _dot(w, lo)
        q = q_ref[rows, :]
        k = k_ref[rows, :]
        b = d_all[0:c]
        blast = b[c - 1:c, :]
        qts, kts = [], []
        for lvl in range(_GLA_LEVELS):
            s = (c // 2) >> lvl
            e = jnp.exp2(d_all[c * (lvl + 1):c * (lvl + 2)])
            qts.append(split_rows(q, e, s, True).astype(BF16))
            kts.append(split_rows(k, e, s, False).astype(BF16))
        qts.append(q.astype(BF16))
        kts.append(k.astype(BF16))
        qe = (q * jnp.exp2(b)).astype(BF16)
        khat = (k * jnp.exp2(blast - b)).astype(BF16)
        dec = jnp.exp2(blast)
        for p in range(npair):
            sl = slice(LANES * p, LANES * (p + 1))
            a = jnp.zeros((c, 2 * c), F32)
            for lvl, (ql, kl) in enumerate(zip(qts, kts)):
                kp = kl[:, sl]
                pm = _nt(ql[:, sl], jnp.concatenate([kp * lm[0], kp * lm[1]], axis=0))
                a = jnp.where(lmap == lvl, pm, a)
            stb = states[p].astype(BF16)
            ups = []
            for hh in range(2):
                h = 2 * p + hh
                vh = v_ref[rows, LANES * h:LANES * (h + 1)]
                o = _dot(a[:, hh * c:(hh + 1) * c].astype(BF16), vh) + _nt(qe[:, sl] * lm[hh], stb)
                y = _rms(o, gn_ref[...])
                rg = rg_ref[rows, LANES * h:LANES * (h + 1)]
                y = y * (rg / (1.0 + jnp.exp(-rg)))
                o_ref[rows, LANES * h:LANES * (h + 1)] = y.astype(BF16)
                ups.append(_tn(vh, khat[:, sl]))
            states[p] = states[p] * dec[:, sl] + jnp.where(lane < half, ups[0], ups[1])
    for p in range(npair):
        st_ref[p] = states[p]


def _gla(qg, kg, la, vg, rg, gn, batch, seq):
    n = qg.shape[0]
    nchunk = seq // GLA_STEP
    w = jnp.asarray(_gla_weights(), BF16)
    lmap = jnp.asarray(_gla_level_map())
    row = lambda width: pl.BlockSpec((GLA_STEP, width), lambda b, c: (b * nchunk + c, 0))
    const = functools.partial(pl.BlockSpec, pipeline_mode=pl.Buffered(1))
    return pl.pallas_call(
        _gla_kernel,
        grid=(batch, nchunk),
        in_specs=[row(G_KEY_WIDTH), row(G_KEY_WIDTH), row(G_KEY_WIDTH), row(G_WIDTH), row(G_WIDTH),
                  const(w.shape, lambda b, c: (0, 0)),
                  const(lmap.shape, lambda b, c: (0, 0)),
                  const((1, G_HEAD_V), lambda b, c: (0, 0))],
        out_specs=row(G_WIDTH),
        out_shape=jax.ShapeDtypeStruct((n, G_WIDTH), BF16),
        scratch_shapes=[pltpu.VMEM((G_HEADS // 2, G_HEAD_V, LANES), F32)],
        compiler_params=pltpu.CompilerParams(dimension_semantics=("parallel", "arbitrary"),
                                             vmem_limit_bytes=VMEM_LIMIT),
        name="gla",
    )(qg, kg, la, vg, rg, w, lmap, gn)


def _memkv_kernel(m_ref, g_ref, w_ref, k_ref, v_ref):
    h = _rms(m_ref[0], g_ref[...]).astype(BF16)
    k_ref[0] = _dot(h, w_ref[:, 0:D_MODEL]).astype(BF16)
    v_ref[0] = _dot(h, w_ref[:, D_MODEL:2 * D_MODEL]).astype(BF16)


def _memkv(mem, g, w):
    batch, mlen, _ = mem.shape
    const = functools.partial(pl.BlockSpec, pipeline_mode=pl.Buffered(1))
    blk = pl.BlockSpec((1, mlen, D_MODEL), lambda b: (b, 0, 0))
    return pl.pallas_call(
        _memkv_kernel,
        grid=(batch,),
        in_specs=[blk, const((1, D_MODEL), lambda b: (0, 0)), const((D_MODEL, 2 * D_MODEL), lambda b: (0, 0))],
        out_specs=[blk, blk],
        out_shape=[jax.ShapeDtypeStruct(mem.shape, BF16)] * 2,
        compiler_params=pltpu.CompilerParams(dimension_semantics=("parallel",), vmem_limit_bytes=VMEM_LIMIT),
        name="memkv",
    )(mem, g, w)


def _post_kernel(x_ref, oa_ref, og_ref, wo_ref, gx_ref, wq_ref, kx_ref, vx_ref, wxo_ref, o_ref):
    x1 = x_ref[...] + _dot(oa_ref[...], wo_ref[0:A_WIDTH, :]) + _dot(og_ref[...], wo_ref[A_WIDTH:D_MODEL, :])
    h = _rms(x1, gx_ref[...]).astype(BF16)
    qx = (_dot(h, wq_ref[...]) * (X_HEAD_DIM ** -0.5)).astype(BF16)
    outs = []
    for hd in range(X_HEADS):
        sl = slice(X_HEAD_DIM * hd, X_HEAD_DIM * (hd + 1))
        s = _nt(qx[:, sl], kx_ref[0, :, sl])
        pexp = jnp.exp(s - jnp.max(s, axis=1, keepdims=True))
        pn = (pexp / jnp.sum(pexp, axis=1, keepdims=True)).astype(BF16)
        outs.append(_dot(pn, vx_ref[0, :, sl]).astype(BF16))
    ox = jnp.concatenate(outs, axis=1)
    o_ref[...] = x1 + _dot(ox, wxo_ref[...])


def _post(x2, oa, og, wo, gx, wq, kx, vx, wxo, seq):
    n = x2.shape[0]
    tm = ROW_TILE
    per_b = seq // tm
    mlen = kx.shape[1]
    const = functools.partial(pl.BlockSpec, pipeline_mode=pl.Buffered(1))
    row = lambda width: pl.BlockSpec((tm, width), lambda i: (i, 0))
    mem = pl.BlockSpec((1, mlen, D_MODEL), lambda i: (i // per_b, 0, 0))
    sq = const((D_MODEL, D_MODEL), lambda i: (0, 0))
    return pl.pallas_call(
        _post_kernel,
        grid=(n // tm,),
        in_specs=[row(D_MODEL), row(A_WIDTH), row(G_WIDTH), sq, const((1, D_MODEL), lambda i: (0, 0)), sq, mem, mem, sq],
        out_specs=row(D_MODEL),
        out_shape=jax.ShapeDtypeStruct((n, D_MODEL), F32),
        compiler_params=pltpu.CompilerParams(dimension_semantics=("parallel",), vmem_limit_bytes=VMEM_LIMIT),
        name="post",
    )(x2, oa, og, wo, gx, wq, kx, vx, wxo)


def _mlp_kernel(x_ref, g_ref, wu_ref, wd_ref, gf_ref, o_ref, *, final_norm):
    x = x_ref[...]
    h = _rms(x, g_ref[...]).astype(BF16)
    acc = x
    for c in range(D_FF // D_MODEL):
        sl = slice(D_MODEL * c, D_MODEL * (c + 1))
        u = jnp.square(jnp.maximum(_dot(h, wu_ref[:, sl]), 0.0)).astype(BF16)
        acc = acc + _dot(u, wd_ref[sl, :])
    o_ref[...] = _rms(acc, gf_ref[...]) if final_norm else acc


def _mlp(x2, g, wu, wd, gf, final_norm):
    n = x2.shape[0]
    tm = ROW_TILE
    const = functools.partial(pl.BlockSpec, pipeline_mode=pl.Buffered(1))
    row = pl.BlockSpec((tm, D_MODEL), lambda i: (i, 0))
    vec = const((1, D_MODEL), lambda i: (0, 0))
    return pl.pallas_call(
        functools.partial(_mlp_kernel, final_norm=final_norm),
        grid=(n // tm,),
        in_specs=[row, vec, const((D_MODEL, D_FF), lambda i: (0, 0)), const((D_FF, D_MODEL), lambda i: (0, 0)), vec],
        out_specs=row,
        out_shape=jax.ShapeDtypeStruct((n, D_MODEL), F32),
        compiler_params=pltpu.CompilerParams(dimension_semantics=("parallel",), vmem_limit_bytes=VMEM_LIMIT),
        name="mlp",
    )(x2, g, wu, wd, gf)


def kernel(x, mem, rp_table, norm_mix, w_in, w_gate_up, b_gate, g_norm, w_out, norm_xattn, norm_mem, w_xq, w_xkv,
           w_xo, norm_mlp, w_up, w_down, norm_final):
    batch, seq, _ = x.shape
    depth = w_in.shape[0]
    assert seq % ROW_TILE == 0 and seq % MOBA_BLOCK == 0 and seq // MOBA_BLOCK <= 8
    x2 = x.reshape(batch * seq, D_MODEL)
    bias = _bias_tiles(rp_table)
    c31 = rp_table[RP_BUCKETS - 1]
    glr_lo = 3 * A_WIDTH + 2 * G_KEY_WIDTH + G_WIDTH
    for l in range(depth):
        wl = w_in[l]
        w = jnp.concatenate([wl[:, :glr_lo], wl[:, glr_lo + G_GATE_RANK:], wl[:, glr_lo:glr_lo + G_GATE_RANK],
                             jnp.zeros((D_MODEL, LANES - G_GATE_RANK), F32)], axis=1).astype(BF16)
        wgu = jnp.concatenate([w_gate_up[l], jnp.zeros((LANES - G_GATE_RANK, G_KEY_WIDTH), F32)], axis=0).astype(BF16)
        qkv, qg, kg, la, vg, rg = _inproj(x2, norm_mix[l][None], w, wgu, b_gate[l][None])
        oa = _moba(qkv, bias, c31, batch, seq)
        og = _gla(qg, kg, la, vg, rg, g_norm[l][None], batch, seq)
        kx, vx = _memkv(mem, norm_mem[l][None], w_xkv[l].astype(BF16))
        x2 = _post(x2, oa, og, w_out[l].astype(BF16), norm_xattn[l][None], w_xq[l].astype(BF16), kx, vx,
                   w_xo[l].astype(BF16), seq)
        last = l == depth - 1
        x2 = _mlp(x2, norm_mlp[l][None], w_up[l].astype(BF16), w_down[l].astype(BF16), norm_final[None], last)
    return x2.reshape(batch, seq, D_MODEL)
```

```python
import functools
import math

import numpy as np
import jax
import jax.numpy as jnp
from jax import lax
from jax.experimental import pallas as pl
from jax.experimental.pallas import tpu as pltpu

F32 = jnp.float32
BF16 = jnp.bfloat16

D_MODEL = 1024
A_HEADS = 8
A_HEAD_DIM = 64
A_WIDTH = A_HEADS * A_HEAD_DIM
MOBA_BLOCK = 256
MOBA_TOPK = 3
G_HEADS = 4
G_WIDTH = D_MODEL - A_WIDTH
G_HEAD_V = G_WIDTH // G_HEADS
G_KEY_WIDTH = G_WIDTH // 2
G_HEAD_K = G_KEY_WIDTH // G_HEADS
G_GATE_RANK = 16
G_GATE_NORM = 16.0
X_HEADS = 4
X_HEAD_DIM = D_MODEL // X_HEADS
D_FF = 4 * D_MODEL
RP_BUCKETS = 32
RP_MAX_DIST = 128
EPS = 1e-6

LANES = 128
NEG = -1e30
GLA_CHUNK = 128
GLA_STEP = 512
LOG2E = float(np.log2(np.e))
ROW_TILE = 512
VMEM_LIMIT = 56 * 1024 * 1024

_C_QK = 2 * A_WIDTH
_C_QG = _C_QK
_C_KG = _C_QG + G_KEY_WIDTH
_C_VG = _C_KG + G_KEY_WIDTH
_C_RG = _C_VG + G_WIDTH
_C_GLR = _C_RG + G_WIDTH
_C_END = _C_GLR + LANES


def _nt(a, b):
    return lax.dot_general(a, b, (((1,), (1,)), ((), ())), preferred_element_type=F32)


def _tn(a, b):
    return lax.dot_general(a, b, (((0,), (0,)), ((), ())), preferred_element_type=F32)


def _dot(a, b):
    return jnp.dot(a, b, preferred_element_type=F32)


def _rms(x, g):
    return x * lax.rsqrt(jnp.mean(x * x, axis=-1, keepdims=True) + EPS) * g


def _bucket_thresholds():
    max_exact = RP_BUCKETS // 2
    d = np.arange(1, 4 * RP_MAX_DIST)
    val = (np.log(d.astype(np.float32) / np.float32(max_exact)) / np.float32(math.log(RP_MAX_DIST / max_exact))
           * np.float32(RP_BUCKETS - max_exact))
    bucket = np.minimum(max_exact + val.astype(np.int32), RP_BUCKETS - 1)
    return [int(d[(d >= max_exact) & (bucket >= max_exact + k)][0]) for k in range(1, RP_BUCKETS - max_exact)]


_THRESHOLDS = _bucket_thresholds()


def _bias_kernel(tab_ref, out_ref):
    h = pl.program_id(0)
    shape = (2 * MOBA_BLOCK, MOBA_BLOCK)
    d = lax.broadcasted_iota(jnp.int32, shape, 1) - lax.broadcasted_iota(jnp.int32, shape, 0) + MOBA_BLOCK
    max_exact = RP_BUCKETS // 2
    bucket = jnp.where(d < max_exact, d, max_exact)
    for t in _THRESHOLDS:
        bucket = bucket + jnp.where(d >= t, 1, 0)
    m = jnp.full(shape, NEG, F32)
    for b in range(RP_BUCKETS):
        m = jnp.where(bucket == b, tab_ref[b, h] * LOG2E, m)
    out_ref[0] = m


def _bias_tiles(rp_table):
    return pl.pallas_call(
        _bias_kernel,
        grid=(A_HEADS,),
        in_specs=[pl.BlockSpec(memory_space=pltpu.SMEM)],
        out_specs=pl.BlockSpec((1, 2 * MOBA_BLOCK, MOBA_BLOCK), lambda h: (h, 0, 0)),
        out_shape=jax.ShapeDtypeStruct((A_HEADS, 2 * MOBA_BLOCK, MOBA_BLOCK), F32),
        name="bias",
    )(rp_table)


def _inproj_kernel(x_ref, g_ref, w_ref, wvt_ref, wgu_ref, bg_ref, qk_ref, vt_ref, qg_ref, kg_ref, la_ref, vg_ref,
                   rg_ref):
    h = _rms(x_ref[...], g_ref[...]).astype(BF16)

    def proj(lo, hi):
        return _dot(h, w_ref[:, lo:hi])

    qk_ref[:, 0:A_WIDTH] = (proj(0, A_WIDTH) * (LOG2E * A_HEAD_DIM ** -0.5)).astype(BF16)
    qk_ref[:, A_WIDTH:_C_QK] = proj(A_WIDTH, _C_QK).astype(BF16)
    vt_ref[...] = _nt(wvt_ref[...], h).astype(BF16)
    qg_ref[...] = proj(_C_QG, _C_KG) * (G_HEAD_K ** -0.5)
    kg_ref[...] = proj(_C_KG, _C_VG)
    vg_ref[...] = proj(_C_VG, _C_RG).astype(BF16)
    rg_ref[...] = proj(_C_RG, _C_GLR)
    glr = proj(_C_GLR, _C_END).astype(BF16)
    z = _dot(glr, wgu_ref[...]) + bg_ref[...]
    log_sig = jnp.minimum(z, 0.0) - jnp.log(1.0 + jnp.exp(-jnp.abs(z)))
    la_ref[...] = log_sig * (1.0 / G_GATE_NORM)


def _inproj(x2, g, w, wvt, wgu, bg):
    n = x2.shape[0]
    tm = ROW_TILE
    const = functools.partial(pl.BlockSpec, pipeline_mode=pl.Buffered(1))
    row = lambda width: pl.BlockSpec((tm, width), lambda i: (i, 0))
    return pl.pallas_call(
        _inproj_kernel,
        grid=(n // tm,),
        in_specs=[row(D_MODEL),
                  const((1, D_MODEL), lambda i: (0, 0)),
                  const((D_MODEL, _C_END), lambda i: (0, 0)),
                  const((A_WIDTH, D_MODEL), lambda i: (0, 0)),
                  const((LANES, G_KEY_WIDTH), lambda i: (0, 0)),
                  const((1, G_KEY_WIDTH), lambda i: (0, 0))],
        out_specs=[row(_C_QK), pl.BlockSpec((A_WIDTH, tm), lambda i: (0, i)),
                   row(G_KEY_WIDTH), row(G_KEY_WIDTH), row(G_KEY_WIDTH), row(G_WIDTH), row(G_WIDTH)],
        out_shape=[jax.ShapeDtypeStruct((n, _C_QK), BF16),
                   jax.ShapeDtypeStruct((A_WIDTH, n), BF16),
                   jax.ShapeDtypeStruct((n, G_KEY_WIDTH), F32),
                   jax.ShapeDtypeStruct((n, G_KEY_WIDTH), F32),
                   jax.ShapeDtypeStruct((n, G_KEY_WIDTH), F32),
                   jax.ShapeDtypeStruct((n, G_WIDTH), BF16),
                   jax.ShapeDtypeStruct((n, G_WIDTH), F32)],
        compiler_params=pltpu.CompilerParams(dimension_semantics=("parallel",), vmem_limit_bytes=VMEM_LIMIT),
        name="inproj",
    )(x2, g, w, wvt, wgu, bg)


def _moba_kernel(c31_ref, q_ref, k_ref, vt_ref, bias_ref, o_ref, kext, s_scr, *, seq):
    p = pl.program_id(1)
    nblk = seq // MOBA_BLOCK
    half = LANES // 2

    k = k_ref[...]
    rowblk = lax.broadcasted_iota(jnp.int32, (seq, LANES), 0) // MOBA_BLOCK
    lane2 = lax.broadcasted_iota(jnp.int32, (seq, LANES), 1)
    lm0 = jnp.where(lane2 < half, 1.0, 0.0).astype(BF16)
    lm1 = jnp.where(lane2 >= half, 1.0, 0.0).astype(BF16)
    kext[0] = k * lm0 + jnp.where(lane2 - half == rowblk, 1.0, 0.0).astype(BF16)
    kext[1] = k * lm1 + jnp.where(lane2 == rowblk, 1.0, 0.0).astype(BF16)
    ones_rows = jnp.ones((half, MOBA_BLOCK), BF16)
    n_i = lax.broadcasted_iota(jnp.int32, (16, seq), 0)
    t_i = lax.broadcasted_iota(jnp.int32, (16, seq), 1)
    avg = jnp.where(t_i // MOBA_BLOCK == n_i, 1.0 / MOBA_BLOCK, 0.0).astype(BF16)
    km = _dot(avg, k)
    kmh = km.astype(BF16)
    kml = (km - kmh.astype(F32)).astype(BF16)
    r_i = lax.broadcasted_iota(jnp.int32, (MOBA_BLOCK, MOBA_BLOCK), 0)
    c_i = lax.broadcasted_iota(jnp.int32, (MOBA_BLOCK, MOBA_BLOCK), 1)
    ident = jnp.where(r_i == c_i, 1.0, 0.0).astype(BF16)

    lane_q = lax.broadcasted_iota(jnp.int32, (MOBA_BLOCK, LANES), 1)
    rowi = lax.broadcasted_iota(jnp.int32, (16, MOBA_BLOCK), 0)
    blk = lambda n: slice(n * MOBA_BLOCK, (n + 1) * MOBA_BLOCK)

    def prepare(qi, h, slot):
        qm = q_ref[blk(qi), :] * jnp.where((lane_q < half) if h == 0 else (lane_q >= half), 1.0, 0.0).astype(BF16)
        qext = qm
        if qi > MOBA_TOPK:
            g = _nt(kmh, qm) + _nt(kml, qm)
            cnt = jnp.zeros((16, MOBA_BLOCK), F32)
            for m in range(qi):
                gm = g[m:m + 1, :]
                cnt = cnt + jnp.where((gm > g) | ((gm == g) & (m < rowi)), 1.0, 0.0)
            keep = ((rowi < qi) & (cnt < MOBA_TOPK)) | (rowi >= qi)
            pen_t = jnp.where(keep, 0.0, NEG).astype(BF16)
            off = half if h == 0 else 0
            parts = ([jnp.zeros((off, MOBA_BLOCK), BF16)] if off else []) + [
                pen_t, jnp.zeros((LANES - 16 - off, MOBA_BLOCK), BF16)]
            pen_q = _nt(ident, jnp.concatenate(parts, axis=0))
            qext = qm + pen_q.astype(BF16)
        return dict(qi=qi, h=h, qext=qext, c31=c31_ref[2 * p + h] * LOG2E, sbuf=s_scr.at[slot], near=None,
                    far=None, acc=None)

    def scores(u):
        qi, h = u["qi"], u["h"]
        s_all = _nt(kext[h, 0:(qi + 1) * MOBA_BLOCK, :], u["qext"])
        for n in range(qi + 1):
            s = s_all[blk(n)]
            if n == qi:
                s = s + bias_ref[h, MOBA_BLOCK:2 * MOBA_BLOCK, :]
            elif n == qi - 1:
                s = s + bias_ref[h, 0:MOBA_BLOCK, :]
            u["sbuf"][blk(n), :] = s
            t = jnp.max(s.reshape(MOBA_BLOCK // 8, 8, MOBA_BLOCK), axis=0)
            key = "far" if n < qi - 1 else "near"
            u[key] = t if u[key] is None else jnp.maximum(u[key], t)
        mx = u["near"] if u["far"] is None else jnp.maximum(u["near"], u["far"] + u["c31"])
        u["m_near"] = jnp.max(mx, axis=0, keepdims=True)
        u["m_far"] = u["m_near"] - u["c31"]

    def probs(u):
        qi, h = u["qi"], u["h"]
        width = (qi + 1) * MOBA_BLOCK
        pt = jnp.concatenate(
            [jnp.exp2(u["sbuf"][blk(n), :] - (u["m_far"] if n < qi - 1 else u["m_near"])).astype(BF16)
             for n in range(qi + 1)], axis=0)
        vrows = vt_ref[half * h:half * (h + 1), 0:width]
        ones_rows = jnp.ones((half, width), BF16)
        vt = jnp.concatenate([vrows, ones_rows] if h == 0 else [ones_rows, vrows], axis=0)
        u["acc"] = _dot(vt, pt)

    outs = {}

    def finish(u):
        acc, h = u["acc"], u["h"]
        num = acc[half * h:half * (h + 1)]
        den = acc[half * (1 - h):half * (1 - h) + 1]
        outs[h] = num / den
        if h == 1:
            o_ref[blk(u["qi"]), :] = jnp.concatenate([outs[0], outs[1]], axis=0).T.astype(BF16)

    prev = []
    group = 2
    for g in range(nblk // group):
        cur = [prepare(qi, h, (g % 2) * 2 * group + 2 * (qi % group) + h)
               for qi in range(g * group, (g + 1) * group) for h in range(2)]
        for u in cur:
            scores(u)
        for u in prev:
            probs(u)
        for u in prev:
            finish(u)
        prev = cur
    for u in prev:
        probs(u)
    for u in prev:
        finish(u)


def _moba(qk, vt, bias, c31, batch, seq):
    n = qk.shape[0]
    npair = A_HEADS // 2
    kern = functools.partial(_moba_kernel, seq=seq)
    return pl.pallas_call(
        kern,
        grid=(batch, npair),
        in_specs=[pl.BlockSpec(memory_space=pltpu.SMEM),
                  pl.BlockSpec((seq, LANES), lambda b, p: (b, p)),
                  pl.BlockSpec((seq, LANES), lambda b, p: (b, npair + p)),
                  pl.BlockSpec((LANES, seq), lambda b, p: (p, b)),
                  pl.BlockSpec((2, 2 * MOBA_BLOCK, MOBA_BLOCK), lambda b, p: (p, 0, 0))],
        out_specs=pl.BlockSpec((seq, LANES), lambda b, p: (b, p)),
        out_shape=jax.ShapeDtypeStruct((n, A_WIDTH), BF16),
        scratch_shapes=[pltpu.VMEM((2, seq, LANES), BF16),
                        pltpu.VMEM((8, seq, MOBA_BLOCK), F32)],
        compiler_params=pltpu.CompilerParams(dimension_semantics=("parallel", "arbitrary"),
                                             vmem_limit_bytes=VMEM_LIMIT),
        name="moba",
    )(c31, qk, qk, vt, bias)


_GLA_LEVELS = int(math.log2(GLA_CHUNK))


def _gla_weights():
    c = GLA_CHUNK
    i = np.arange(c)[:, None]
    j = np.arange(c)[None, :]
    mats = [(j <= i)]
    for lvl in range(_GLA_LEVELS):
        s = (c // 2) >> lvl
        ref = (i // (2 * s)) * (2 * s) + s - 1
        mats.append(np.where(i > ref, (j > ref) & (j <= i), (j > i) & (j <= ref)))
    return np.concatenate(mats, axis=0).astype(np.float32)


def _gla_level_map():
    c = GLA_CHUNK
    i = np.arange(c)[:, None]
    j = np.arange(c)[None, :]
    top_bit = np.floor(np.log2(np.maximum(i ^ j, 1))).astype(np.int64)
    lvl = np.where(j < i, _GLA_LEVELS - 1 - top_bit, np.where(j == i, _GLA_LEVELS, _GLA_LEVELS + 1))
    return np.concatenate([lvl, lvl], axis=1).astype(np.int32)


def _gla_kernel(q_ref, k_ref, la_ref, v_ref, rg_ref, w_ref, lmap_ref, gn_ref, o_ref, st_ref):
    c = GLA_CHUNK
    half = LANES // 2
    npair = G_HEADS // 2

    @pl.when(pl.program_id(1) == 0)
    def _init():
        st_ref[...] = jnp.zeros(st_ref.shape, F32)

    w = w_ref[...]
    lmap = lmap_ref[...]
    lane = lax.broadcasted_iota(jnp.int32, (1, LANES), 1)
    lane_c = lax.broadcasted_iota(jnp.int32, (c, LANES), 1)
    lm = [jnp.where(lane_c < half, 1.0, 0.0).astype(BF16), jnp.where(lane_c >= half, 1.0, 0.0).astype(BF16)]
    row_small = lax.broadcasted_iota(jnp.int32, (c, G_KEY_WIDTH), 0)

    def split_rows(x, e, s, want_odd):
        if s < 8:
            odd = (row_small // s) % 2 == 1
            return jnp.where(odd if want_odd else ~odd, x * e, 0.0)
        pieces = []
        for blk in range(c // s):
            rows = slice(blk * s, (blk + 1) * s)
            pieces.append(x[rows] * e[rows] if (blk % 2 == 1) == want_odd else jnp.zeros((s, x.shape[1]), F32))
        return jnp.concatenate(pieces, axis=0)

    states = [st_ref[p] for p in range(npair)]
    for ch in range(GLA_STEP // c):
        rows = slice(ch * c, (ch + 1) * c)
        la = la_ref[rows, :] * LOG2E
        hi = la.astype(BF16)
        lo = (la - hi.astype(F32)).astype(BF16)
        d_all = _dot(w, hi) + _dot(w, lo)
        q = q_ref[rows, :]
        k = k_ref[rows, :]
        b = d_all[0:c]
        blast = b[c - 1:c, :]
        qts, kts = [], []
        for lvl in range(_GLA_LEVELS):
            s = (c // 2) >> lvl
            e = jnp.exp2(d_all[c * (lvl + 1):c * (lvl + 2)])
            qts.append(split_rows(q, e, s, True).astype(BF16))
            kts.append(split_rows(k, e, s, False).astype(BF16))
        qts.append(q.astype(BF16))
        kts.append(k.astype(BF16))
        qe = (q * jnp.exp2(b)).astype(BF16)
        khat = (k * jnp.exp2(blast - b)).astype(BF16)
        dec = jnp.exp2(blast)
        for p in range(npair):
            sl = slice(LANES * p, LANES * (p + 1))
            a = jnp.zeros((c, 2 * c), F32)
            for lvl, (ql, kl) in enumerate(zip(qts, kts)):
                kp = kl[:, sl]
                pm = _nt(ql[:, sl], jnp.concatenate([kp * lm[0], kp * lm[1]], axis=0))
                a = jnp.where(lmap == lvl, pm, a)
            stb = states[p].astype(BF16)
            ups = []
            for hh in range(2):
                h = 2 * p + hh
                vh = v_ref[rows, LANES * h:LANES * (h + 1)]
                o = _dot(a[:, hh * c:(hh + 1) * c].astype(BF16), vh) + _nt(qe[:, sl] * lm[hh], stb)
                y = _rms(o, gn_ref[...])
                rg = rg_ref[rows, LANES * h:LANES * (h + 1)]
                y = y * (rg / (1.0 + jnp.exp(-rg)))
                o_ref[rows, LANES * h:LANES * (h + 1)] = y.astype(BF16)
                ups.append(_tn(vh, khat[:, sl]))
            states[p] = states[p] * dec[:, sl] + jnp.where(lane < half, ups[0], ups[1])
    for p in range(npair):
        st_ref[p] = states[p]


def _gla(qg, kg, la, vg, rg, gn, batch, seq):
    n = qg.shape[0]
    nchunk = seq // GLA_STEP
    w = jnp.asarray(_gla_weights(), BF16)
    lmap = jnp.asarray(_gla_level_map())
    row = lambda width: pl.BlockSpec((GLA_STEP, width), lambda b, c: (b * nchunk + c, 0))
    const = functools.partial(pl.BlockSpec, pipeline_mode=pl.Buffered(1))
    return pl.pallas_call(
        _gla_kernel,
        grid=(batch, nchunk),
        in_specs=[row(G_KEY_WIDTH), row(G_KEY_WIDTH), row(G_KEY_WIDTH), row(G_WIDTH), row(G_WIDTH),
                  const(w.shape, lambda b, c: (0, 0)),
                  const(lmap.shape, lambda b, c: (0, 0)),
                  const((1, G_HEAD_V), lambda b, c: (0, 0))],
        out_specs=row(G_WIDTH),
        out_shape=jax.ShapeDtypeStruct((n, G_WIDTH), BF16),
        scratch_shapes=[pltpu.VMEM((G_HEADS // 2, G_HEAD_V, LANES), F32)],
        compiler_params=pltpu.CompilerParams(dimension_semantics=("parallel", "arbitrary"),
                                             vmem_limit_bytes=VMEM_LIMIT),
        name="gla",
    )(qg, kg, la, vg, rg, w, lmap, gn)


def _memkv_kernel(m_ref, g_ref, w_ref, k_ref, v_ref):
    h = _rms(m_ref[0], g_ref[...]).astype(BF16)
    k_ref[0] = _dot(h, w_ref[:, 0:D_MODEL]).astype(BF16)
    v_ref[0] = _dot(h, w_ref[:, D_MODEL:2 * D_MODEL]).astype(BF16)


def _memkv(mem, g, w):
    batch, mlen, _ = mem.shape
    const = functools.partial(pl.BlockSpec, pipeline_mode=pl.Buffered(1))
    blk = pl.BlockSpec((1, mlen, D_MODEL), lambda b: (b, 0, 0))
    return pl.pallas_call(
        _memkv_kernel,
        grid=(batch,),
        in_specs=[blk, const((1, D_MODEL), lambda b: (0, 0)), const((D_MODEL, 2 * D_MODEL), lambda b: (0, 0))],
        out_specs=[blk, blk],
        out_shape=[jax.ShapeDtypeStruct(mem.shape, BF16)] * 2,
        compiler_params=pltpu.CompilerParams(dimension_semantics=("parallel",), vmem_limit_bytes=VMEM_LIMIT),
        name="memkv",
    )(mem, g, w)


def _post_kernel(x_ref, oa_ref, og_ref, wo_ref, gx_ref, wq_ref, kx_ref, vx_ref, wxo_ref, o_ref):
    x1 = x_ref[...] + _dot(oa_ref[...], wo_ref[0:A_WIDTH, :]) + _dot(og_ref[...], wo_ref[A_WIDTH:D_MODEL, :])
    h = _rms(x1, gx_ref[...]).astype(BF16)
    qx = (_dot(h, wq_ref[...]) * (X_HEAD_DIM ** -0.5)).astype(BF16)
    outs = []
    for hd in range(X_HEADS):
        sl = slice(X_HEAD_DIM * hd, X_HEAD_DIM * (hd + 1))
        s = _nt(qx[:, sl], kx_ref[0, :, sl])
        pexp = jnp.exp(s - jnp.max(s, axis=1, keepdims=True))
        pn = (pexp / jnp.sum(pexp, axis=1, keepdims=True)).astype(BF16)
        outs.append(_dot(pn, vx_ref[0, :, sl]).astype(BF16))
    ox = jnp.concatenate(outs, axis=1)
    o_ref[...] = x1 + _dot(ox, wxo_ref[...])


def _post(x2, oa, og, wo, gx, wq, kx, vx, wxo, seq):
    n = x2.shape[0]
    tm = ROW_TILE
    per_b = seq // tm
    mlen = kx.shape[1]
    const = functools.partial(pl.BlockSpec, pipeline_mode=pl.Buffered(1))
    row = lambda width: pl.BlockSpec((tm, width), lambda i: (i, 0))
    mem = pl.BlockSpec((1, mlen, D_MODEL), lambda i: (i // per_b, 0, 0))
    sq = const((D_MODEL, D_MODEL), lambda i: (0, 0))
    return pl.pallas_call(
        _post_kernel,
        grid=(n // tm,),
        in_specs=[row(D_MODEL), row(A_WIDTH), row(G_WIDTH), sq, const((1, D_MODEL), lambda i: (0, 0)), sq, mem, mem, sq],
        out_specs=row(D_MODEL),
        out_shape=jax.ShapeDtypeStruct((n, D_MODEL), F32),
        compiler_params=pltpu.CompilerParams(dimension_semantics=("parallel",), vmem_limit_bytes=VMEM_LIMIT),
        name="post",
    )(x2, oa, og, wo, gx, wq, kx, vx, wxo)


def _mlp_kernel(x_ref, g_ref, wu_ref, wd_ref, gf_ref, o_ref, *, final_norm):
    x = x_ref[...]
    h = _rms(x, g_ref[...]).astype(BF16)
    acc = x
    for c in range(D_FF // D_MODEL):
        sl = slice(D_MODEL * c, D_MODEL * (c + 1))
        u = jnp.square(jnp.maximum(_dot(h, wu_ref[:, sl]), 0.0)).astype(BF16)
        acc = acc + _dot(u, wd_ref[sl, :])
    o_ref[...] = _rms(acc, gf_ref[...]) if final_norm else acc


def _mlp(x2, g, wu, wd, gf, final_norm):
    n = x2.shape[0]
    tm = ROW_TILE
    const = functools.partial(pl.BlockSpec, pipeline_mode=pl.Buffered(1))
    row = pl.BlockSpec((tm, D_MODEL), lambda i: (i, 0))
    vec = const((1, D_MODEL), lambda i: (0, 0))
    return pl.pallas_call(
        functools.partial(_mlp_kernel, final_norm=final_norm),
        grid=(n // tm,),
        in_specs=[row, vec, const((D_MODEL, D_FF), lambda i: (0, 0)), const((D_FF, D_MODEL), lambda i: (0, 0)), vec],
        out_specs=row,
        out_shape=jax.ShapeDtypeStruct((n, D_MODEL), F32),
        compiler_params=pltpu.CompilerParams(dimension_semantics=("parallel",), vmem_limit_bytes=VMEM_LIMIT),
        name="mlp",
    )(x2, g, wu, wd, gf)


def kernel(x, mem, rp_table, norm_mix, w_in, w_gate_up, b_gate, g_norm, w_out, norm_xattn, norm_mem, w_xq, w_xkv,
           w_xo, norm_mlp, w_up, w_down, norm_final):
    batch, seq, _ = x.shape
    depth = w_in.shape[0]
    assert seq % ROW_TILE == 0 and seq % MOBA_BLOCK == 0 and seq // MOBA_BLOCK <= 8
    x2 = x.reshape(batch * seq, D_MODEL)
    bias = _bias_tiles(rp_table)
    c31 = rp_table[RP_BUCKETS - 1]
    va_lo = 2 * A_WIDTH
    glr_lo = 3 * A_WIDTH + 2 * G_KEY_WIDTH + G_WIDTH
    for l in range(depth):
        wl = w_in[l]
        w = jnp.concatenate([wl[:, :va_lo], wl[:, va_lo + A_WIDTH:glr_lo], wl[:, glr_lo + G_GATE_RANK:],
                             wl[:, glr_lo:glr_lo + G_GATE_RANK],
                             jnp.zeros((D_MODEL, LANES - G_GATE_RANK), F32)], axis=1).astype(BF16)
        wvt = wl[:, va_lo:va_lo + A_WIDTH].T.astype(BF16)
        wgu = jnp.concatenate([w_gate_up[l], jnp.zeros((LANES - G_GATE_RANK, G_KEY_WIDTH), F32)], axis=0).astype(BF16)
        qk, vt, qg, kg, la, vg, rg = _inproj(x2, norm_mix[l][None], w, wvt, wgu, b_gate[l][None])
        oa = _moba(qk, vt, bias, c31, batch, seq)
        og = _gla(qg, kg, la, vg, rg, g_norm[l][None], batch, seq)
        kx, vx = _memkv(mem, norm_mem[l][None], w_xkv[l].astype(BF16))
        x2 = _post(x2, oa, og, w_out[l].astype(BF16), norm_xattn[l][None], w_xq[l].astype(BF16), kx, vx,
                   w_xo[l].astype(BF16), seq)
        last = l == depth - 1
        x2 = _mlp(x2, norm_mlp[l][None], w_up[l].astype(BF16), w_down[l].astype(BF16), norm_final[None], last)
    return x2.reshape(batch, seq, D_MODEL)
```

```python
import functools
import math

import numpy as np
import jax
import jax.numpy as jnp
from jax import lax
from jax.experimental import pallas as pl
from jax.experimental.pallas import tpu as pltpu

F32 = jnp.float32
BF16 = jnp.bfloat16

D_MODEL = 1024
A_HEADS = 8
A_HEAD_DIM = 64
A_WIDTH = A_HEADS * A_HEAD_DIM
MOBA_BLOCK = 256
MOBA_TOPK = 3
G_HEADS = 4
G_WIDTH = D_MODEL - A_WIDTH
G_HEAD_V = G_WIDTH // G_HEADS
G_KEY_WIDTH = G_WIDTH // 2
G_HEAD_K = G_KEY_WIDTH // G_HEADS
G_GATE_RANK = 16
G_GATE_NORM = 16.0
X_HEADS = 4
X_HEAD_DIM = D_MODEL // X_HEADS
D_FF = 4 * D_MODEL
RP_BUCKETS = 32
RP_MAX_DIST = 128
EPS = 1e-6

LANES = 128
NEG = -1e30
GLA_CHUNK = 128
GLA_STEP = 512
LOG2E = float(np.log2(np.e))
ROW_TILE = 512
POST_TILE = 1024
POST_PARTS = 2
VMEM_LIMIT = 56 * 1024 * 1024

_C_QK = 2 * A_WIDTH
_C_QG = _C_QK
_C_KG = _C_QG + G_KEY_WIDTH
_C_VG = _C_KG + G_KEY_WIDTH
_C_RG = _C_VG + G_WIDTH
_C_GLR = _C_RG + G_WIDTH
_C_END = _C_GLR + LANES


def _nt(a, b):
    return lax.dot_general(a, b, (((1,), (1,)), ((), ())), preferred_element_type=F32)


def _tn(a, b):
    return lax.dot_general(a, b, (((0,), (0,)), ((), ())), preferred_element_type=F32)


def _dot(a, b):
    return jnp.dot(a, b, preferred_element_type=F32)


def _rms(x, g):
    return x * lax.rsqrt(jnp.mean(x * x, axis=-1, keepdims=True) + EPS) * g


def _bucket_thresholds():
    max_exact = RP_BUCKETS // 2
    d = np.arange(1, 4 * RP_MAX_DIST)
    val = (np.log(d.astype(np.float32) / np.float32(max_exact)) / np.float32(math.log(RP_MAX_DIST / max_exact))
           * np.float32(RP_BUCKETS - max_exact))
    bucket = np.minimum(max_exact + val.astype(np.int32), RP_BUCKETS - 1)
    return [int(d[(d >= max_exact) & (bucket >= max_exact + k)][0]) for k in range(1, RP_BUCKETS - max_exact)]


_THRESHOLDS = _bucket_thresholds()


def _bias_kernel(tab_ref, out_ref):
    h = pl.program_id(0)
    shape = (2 * MOBA_BLOCK, MOBA_BLOCK)
    d = lax.broadcasted_iota(jnp.int32, shape, 1) - lax.broadcasted_iota(jnp.int32, shape, 0) + MOBA_BLOCK
    max_exact = RP_BUCKETS // 2
    bucket = jnp.where(d < max_exact, d, max_exact)
    for t in _THRESHOLDS:
        bucket = bucket + jnp.where(d >= t, 1, 0)
    m = jnp.full(shape, NEG, F32)
    for b in range(RP_BUCKETS):
        m = jnp.where(bucket == b, tab_ref[b, h] * LOG2E, m)
    out_ref[0] = m


def _bias_tiles(rp_table):
    return pl.pallas_call(
        _bias_kernel,
        grid=(A_HEADS,),
        in_specs=[pl.BlockSpec(memory_space=pltpu.SMEM)],
        out_specs=pl.BlockSpec((1, 2 * MOBA_BLOCK, MOBA_BLOCK), lambda h: (h, 0, 0)),
        out_shape=jax.ShapeDtypeStruct((A_HEADS, 2 * MOBA_BLOCK, MOBA_BLOCK), F32),
        name="bias",
    )(rp_table)


def _inproj_kernel(x_ref, g_ref, w_ref, wvt_ref, wgu_ref, bg_ref, qk_ref, vt_ref, qg_ref, kg_ref, la_ref, vg_ref,
                   rg_ref):
    h = _rms(x_ref[...], g_ref[...]).astype(BF16)

    def proj(lo, hi):
        return _dot(h, w_ref[:, lo:hi])

    qk_ref[:, 0:A_WIDTH] = (proj(0, A_WIDTH) * (LOG2E * A_HEAD_DIM ** -0.5)).astype(BF16)
    qk_ref[:, A_WIDTH:_C_QK] = proj(A_WIDTH, _C_QK).astype(BF16)
    vt_ref[...] = _nt(wvt_ref[...], h).astype(BF16)
    qg_ref[...] = proj(_C_QG, _C_KG) * (G_HEAD_K ** -0.5)
    kg_ref[...] = proj(_C_KG, _C_VG)
    vg_ref[...] = proj(_C_VG, _C_RG).astype(BF16)
    rg_ref[...] = proj(_C_RG, _C_GLR)
    glr = proj(_C_GLR, _C_END).astype(BF16)
    z = _dot(glr, wgu_ref[...]) + bg_ref[...]
    log_sig = jnp.minimum(z, 0.0) - jnp.log(1.0 + jnp.exp(-jnp.abs(z)))
    la_ref[...] = log_sig * (1.0 / G_GATE_NORM)


def _inproj(x2, g, w, wvt, wgu, bg):
    n = x2.shape[0]
    tm = ROW_TILE
    const = functools.partial(pl.BlockSpec, pipeline_mode=pl.Buffered(1))
    row = lambda width: pl.BlockSpec((tm, width), lambda i: (i, 0))
    return pl.pallas_call(
        _inproj_kernel,
        grid=(n // tm,),
        in_specs=[row(D_MODEL),
                  const((1, D_MODEL), lambda i: (0, 0)),
                  const((D_MODEL, _C_END), lambda i: (0, 0)),
                  const((A_WIDTH, D_MODEL), lambda i: (0, 0)),
                  const((LANES, G_KEY_WIDTH), lambda i: (0, 0)),
                  const((1, G_KEY_WIDTH), lambda i: (0, 0))],
        out_specs=[row(_C_QK), pl.BlockSpec((A_WIDTH, tm), lambda i: (0, i)),
                   row(G_KEY_WIDTH), row(G_KEY_WIDTH), row(G_KEY_WIDTH), row(G_WIDTH), row(G_WIDTH)],
        out_shape=[jax.ShapeDtypeStruct((n, _C_QK), BF16),
                   jax.ShapeDtypeStruct((A_WIDTH, n), BF16),
                   jax.ShapeDtypeStruct((n, G_KEY_WIDTH), F32),
                   jax.ShapeDtypeStruct((n, G_KEY_WIDTH), F32),
                   jax.ShapeDtypeStruct((n, G_KEY_WIDTH), F32),
                   jax.ShapeDtypeStruct((n, G_WIDTH), BF16),
                   jax.ShapeDtypeStruct((n, G_WIDTH), F32)],
        compiler_params=pltpu.CompilerParams(dimension_semantics=("parallel",), vmem_limit_bytes=VMEM_LIMIT),
        name="inproj",
    )(x2, g, w, wvt, wgu, bg)


def _moba_kernel(c31_ref, q_ref, k_ref, vt_ref, bias_ref, o_ref, kext, s_scr, *, seq):
    p = pl.program_id(1)
    nblk = seq // MOBA_BLOCK
    half = LANES // 2

    k = k_ref[...]
    rowblk = lax.broadcasted_iota(jnp.int32, (seq, LANES), 0) // MOBA_BLOCK
    lane2 = lax.broadcasted_iota(jnp.int32, (seq, LANES), 1)
    lm0 = jnp.where(lane2 < half, 1.0, 0.0).astype(BF16)
    lm1 = jnp.where(lane2 >= half, 1.0, 0.0).astype(BF16)
    kext[0] = k * lm0 + jnp.where(lane2 - half == rowblk, 1.0, 0.0).astype(BF16)
    kext[1] = k * lm1 + jnp.where(lane2 == rowblk, 1.0, 0.0).astype(BF16)
    ones_rows = jnp.ones((half, MOBA_BLOCK), BF16)
    n_i = lax.broadcasted_iota(jnp.int32, (16, seq), 0)
    t_i = lax.broadcasted_iota(jnp.int32, (16, seq), 1)
    avg = jnp.where(t_i // MOBA_BLOCK == n_i, 1.0 / MOBA_BLOCK, 0.0).astype(BF16)
    km = _dot(avg, k)
    kmh = km.astype(BF16)
    kml = (km - kmh.astype(F32)).astype(BF16)
    r_i = lax.broadcasted_iota(jnp.int32, (MOBA_BLOCK, MOBA_BLOCK), 0)
    c_i = lax.broadcasted_iota(jnp.int32, (MOBA_BLOCK, MOBA_BLOCK), 1)
    ident = jnp.where(r_i == c_i, 1.0, 0.0).astype(BF16)

    lane_q = lax.broadcasted_iota(jnp.int32, (MOBA_BLOCK, LANES), 1)
    rowi = lax.broadcasted_iota(jnp.int32, (16, MOBA_BLOCK), 0)
    blk = lambda n: slice(n * MOBA_BLOCK, (n + 1) * MOBA_BLOCK)

    def prepare(qi, h, slot):
        qm = q_ref[blk(qi), :] * jnp.where((lane_q < half) if h == 0 else (lane_q >= half), 1.0, 0.0).astype(BF16)
        qext = qm
        if qi > MOBA_TOPK:
            g = _nt(kmh, qm) + _nt(kml, qm)
            cnt = jnp.zeros((16, MOBA_BLOCK), F32)
            for m in range(qi):
                gm = g[m:m + 1, :]
                cnt = cnt + jnp.where((gm > g) | ((gm == g) & (m < rowi)), 1.0, 0.0)
            keep = ((rowi < qi) & (cnt < MOBA_TOPK)) | (rowi >= qi)
            pen_t = jnp.where(keep, 0.0, NEG).astype(BF16)
            off = half if h == 0 else 0
            parts = ([jnp.zeros((off, MOBA_BLOCK), BF16)] if off else []) + [
                pen_t, jnp.zeros((LANES - 16 - off, MOBA_BLOCK), BF16)]
            pen_q = _nt(ident, jnp.concatenate(parts, axis=0))
            qext = qm + pen_q.astype(BF16)
        return dict(qi=qi, h=h, qext=qext, c31=c31_ref[2 * p + h] * LOG2E, sbuf=s_scr.at[slot], near=None,
                    far=None, acc=None)

    def scores(u):
        qi, h = u["qi"], u["h"]
        s_all = _nt(kext[h, 0:(qi + 1) * MOBA_BLOCK, :], u["qext"])
        for n in range(qi + 1):
            s = s_all[blk(n)]
            if n == qi:
                s = s + bias_ref[h, MOBA_BLOCK:2 * MOBA_BLOCK, :]
            elif n == qi - 1:
                s = s + bias_ref[h, 0:MOBA_BLOCK, :]
            u["sbuf"][blk(n), :] = s
            t = jnp.max(s.reshape(MOBA_BLOCK // 8, 8, MOBA_BLOCK), axis=0)
            key = "far" if n < qi - 1 else "near"
            u[key] = t if u[key] is None else jnp.maximum(u[key], t)
        mx = u["near"] if u["far"] is None else jnp.maximum(u["near"], u["far"] + u["c31"])
        u["m_near"] = jnp.max(mx, axis=0, keepdims=True)
        u["m_far"] = u["m_near"] - u["c31"]

    def probs(u):
        qi, h = u["qi"], u["h"]
        width = (qi + 1) * MOBA_BLOCK
        pt = jnp.concatenate(
            [jnp.exp2(u["sbuf"][blk(n), :] - (u["m_far"] if n < qi - 1 else u["m_near"])).astype(BF16)
             for n in range(qi + 1)], axis=0)
        vrows = vt_ref[half * h:half * (h + 1), 0:width]
        ones_rows = jnp.ones((half, width), BF16)
        vt = jnp.concatenate([vrows, ones_rows] if h == 0 else [ones_rows, vrows], axis=0)
        u["acc"] = _dot(vt, pt)

    outs = {}

    def finish(u):
        acc, h = u["acc"], u["h"]
        num = acc[half * h:half * (h + 1)]
        den = acc[half * (1 - h):half * (1 - h) + 1]
        outs[h] = num / den
        if h == 1:
            o_ref[blk(u["qi"]), :] = jnp.concatenate([outs[0], outs[1]], axis=0).T.astype(BF16)

    prev = []
    group = 2
    for g in range(nblk // group):
        cur = [prepare(qi, h, (g % 2) * 2 * group + 2 * (qi % group) + h)
               for qi in range(g * group, (g + 1) * group) for h in range(2)]
        for u in cur:
            scores(u)
        for u in prev:
            probs(u)
        for u in prev:
            finish(u)
        prev = cur
    for u in prev:
        probs(u)
    for u in prev:
        finish(u)


def _moba(qk, vt, bias, c31, batch, seq):
    n = qk.shape[0]
    npair = A_HEADS // 2
    kern = functools.partial(_moba_kernel, seq=seq)
    return pl.pallas_call(
        kern,
        grid=(batch, npair),
        in_specs=[pl.BlockSpec(memory_space=pltpu.SMEM),
                  pl.BlockSpec((seq, LANES), lambda b, p: (b, p)),
                  pl.BlockSpec((seq, LANES), lambda b, p: (b, npair + p)),
                  pl.BlockSpec((LANES, seq), lambda b, p: (p, b)),
                  pl.BlockSpec((2, 2 * MOBA_BLOCK, MOBA_BLOCK), lambda b, p: (p, 0, 0))],
        out_specs=pl.BlockSpec((seq, LANES), lambda b, p: (b, p)),
        out_shape=jax.ShapeDtypeStruct((n, A_WIDTH), BF16),
        scratch_shapes=[pltpu.VMEM((2, seq, LANES), BF16),
                        pltpu.VMEM((8, seq, MOBA_BLOCK), F32)],
        compiler_params=pltpu.CompilerParams(dimension_semantics=("parallel", "arbitrary"),
                                             vmem_limit_bytes=VMEM_LIMIT),
        name="moba",
    )(c31, qk, qk, vt, bias)


_GLA_LEVELS = int(math.log2(GLA_CHUNK))


_GLA_BIG_LEVELS = _GLA_LEVELS - 3


def _gla_weights():
    c = GLA_CHUNK
    i = np.arange(c)[:, None]
    j = np.arange(c)[None, :]
    mats = [(j <= i)]
    for lvl in range(_GLA_BIG_LEVELS, _GLA_LEVELS):
        s = (c // 2) >> lvl
        ref = (i // (2 * s)) * (2 * s) + s - 1
        mats.append(np.where(i > ref, (j > ref) & (j <= i), (j > i) & (j <= ref)))
    return np.concatenate(mats, axis=0).astype(np.float32)


def _gla_level_map():
    c = GLA_CHUNK
    i = np.arange(c)[:, None]
    j = np.arange(c)[None, :]
    top_bit = np.floor(np.log2(np.maximum(i ^ j, 1))).astype(np.int64)
    lvl = np.where(j < i, _GLA_LEVELS - 1 - top_bit, np.where(j == i, _GLA_LEVELS, _GLA_LEVELS + 1))
    return np.concatenate([lvl, lvl], axis=1).astype(np.int32)


def _gla_kernel(q_ref, k_ref, la_ref, v_ref, rg_ref, w_ref, lmap_ref, gn_ref, o_ref, st_ref):
    c = GLA_CHUNK
    half = LANES // 2
    npair = G_HEADS // 2

    @pl.when(pl.program_id(1) == 0)
    def _init():
        st_ref[...] = jnp.zeros(st_ref.shape, F32)

    w = w_ref[...]
    lmap = lmap_ref[...]
    lane = lax.broadcasted_iota(jnp.int32, (1, LANES), 1)
    lane_c = lax.broadcasted_iota(jnp.int32, (c, LANES), 1)
    lm = [jnp.where(lane_c < half, 1.0, 0.0).astype(BF16), jnp.where(lane_c >= half, 1.0, 0.0).astype(BF16)]
    row_small = lax.broadcasted_iota(jnp.int32, (c, G_KEY_WIDTH), 0)

    def level_factors(q, k, b, d_all, lvl):
        s = (c // 2) >> lvl
        if s < 8:
            j = lvl - _GLA_BIG_LEVELS
            e = jnp.exp2(d_all[c * (j + 1):c * (j + 2)])
            odd = (row_small // s) % 2 == 1
            return jnp.where(odd, q * e, 0.0), jnp.where(odd, 0.0, k * e)
        zeros = jnp.zeros((s, q.shape[1]), F32)
        qp, kp = [], []
        for blk in range(c // s):
            rows = slice(blk * s, (blk + 1) * s)
            ref = (blk // 2) * 2 * s + s - 1
            if blk % 2 == 1:
                qp.append(q[rows] * jnp.exp2(b[rows] - b[ref:ref + 1]))
                kp.append(zeros)
            else:
                qp.append(zeros)
                kp.append(k[rows] * jnp.exp2(b[ref:ref + 1] - b[rows]))
        return jnp.concatenate(qp, axis=0), jnp.concatenate(kp, axis=0)

    nch = GLA_STEP // c
    rows_of = [slice(ch * c, (ch + 1) * c) for ch in range(nch)]
    pair_lanes = [slice(LANES * p, LANES * (p + 1)) for p in range(npair)]
    d_alls = []
    for ch in range(nch):
        la = la_ref[rows_of[ch], :] * LOG2E
        hi = la.astype(BF16)
        lo = (la - hi.astype(F32)).astype(BF16)
        d_alls.append(_dot(w, hi) + _dot(w, lo))
    chunks = []
    for ch in range(nch):
        d_all = d_alls[ch]
        q = q_ref[rows_of[ch], :]
        k = k_ref[rows_of[ch], :]
        b = d_all[0:c]
        blast = b[c - 1:c, :]
        qts, kts = [], []
        for lvl in range(_GLA_LEVELS):
            qt, kt = level_factors(q, k, b, d_all, lvl)
            qts.append(qt.astype(BF16))
            kts.append(kt.astype(BF16))
        qts.append(q.astype(BF16))
        kts.append(k.astype(BF16))
        chunks.append(dict(qts=qts, kts=kts, qe=(q * jnp.exp2(b)).astype(BF16),
                           khat=(k * jnp.exp2(blast - b)).astype(BF16), dec=jnp.exp2(blast),
                           a=[jnp.zeros((c, 2 * c), F32)] * npair))
    for lvl in range(_GLA_LEVELS + 1):
        mask = lmap == lvl
        for u in chunks:
            for p, sl in enumerate(pair_lanes):
                kp = u["kts"][lvl][:, sl]
                pm = _nt(u["qts"][lvl][:, sl], jnp.concatenate([kp * lm[0], kp * lm[1]], axis=0))
                u["a"] = [jnp.where(mask, pm, a) if i == p else a for i, a in enumerate(u["a"])]
    states = [st_ref[p] for p in range(npair)]
    for ch, u in enumerate(chunks):
        rows = rows_of[ch]
        for p, sl in enumerate(pair_lanes):
            stb = states[p].astype(BF16)
            ups = []
            for hh in range(2):
                h = 2 * p + hh
                vh = v_ref[rows, LANES * h:LANES * (h + 1)]
                o = _dot(u["a"][p][:, hh * c:(hh + 1) * c].astype(BF16), vh) + _nt(u["qe"][:, sl] * lm[hh], stb)
                y = _rms(o, gn_ref[...])
                rg = rg_ref[rows, LANES * h:LANES * (h + 1)]
                y = y * (rg / (1.0 + jnp.exp(-rg)))
                o_ref[rows, LANES * h:LANES * (h + 1)] = y.astype(BF16)
                ups.append(_tn(vh, u["khat"][:, sl]))
            states[p] = states[p] * u["dec"][:, sl] + jnp.where(lane < half, ups[0], ups[1])
    for p in range(npair):
        st_ref[p] = states[p]


def _gla(qg, kg, la, vg, rg, gn, batch, seq):
    n = qg.shape[0]
    nchunk = seq // GLA_STEP
    w = jnp.asarray(_gla_weights(), BF16)
    lmap = jnp.asarray(_gla_level_map())
    row = lambda width: pl.BlockSpec((GLA_STEP, width), lambda b, c: (b * nchunk + c, 0))
    const = functools.partial(pl.BlockSpec, pipeline_mode=pl.Buffered(1))
    return pl.pallas_call(
        _gla_kernel,
        grid=(batch, nchunk),
        in_specs=[row(G_KEY_WIDTH), row(G_KEY_WIDTH), row(G_KEY_WIDTH), row(G_WIDTH), row(G_WIDTH),
                  const(w.shape, lambda b, c: (0, 0)),
                  const(lmap.shape, lambda b, c: (0, 0)),
                  const((1, G_HEAD_V), lambda b, c: (0, 0))],
        out_specs=row(G_WIDTH),
        out_shape=jax.ShapeDtypeStruct((n, G_WIDTH), BF16),
        scratch_shapes=[pltpu.VMEM((G_HEADS // 2, G_HEAD_V, LANES), F32)],
        compiler_params=pltpu.CompilerParams(dimension_semantics=("parallel", "arbitrary"),
                                             vmem_limit_bytes=VMEM_LIMIT),
        name="gla",
    )(qg, kg, la, vg, rg, w, lmap, gn)


def _memkv_kernel(m_ref, g_ref, w_ref, k_ref, v_ref):
    h = _rms(m_ref[0], g_ref[...]).astype(BF16)
    k_ref[0] = _dot(h, w_ref[:, 0:D_MODEL]).astype(BF16)
    v_ref[0] = _dot(h, w_ref[:, D_MODEL:2 * D_MODEL]).astype(BF16)


def _memkv(mem, g, w):
    batch, mlen, _ = mem.shape
    const = functools.partial(pl.BlockSpec, pipeline_mode=pl.Buffered(1))
    blk = pl.BlockSpec((1, mlen, D_MODEL), lambda b: (b, 0, 0))
    return pl.pallas_call(
        _memkv_kernel,
        grid=(batch,),
        in_specs=[blk, const((1, D_MODEL), lambda b: (0, 0)), const((D_MODEL, 2 * D_MODEL), lambda b: (0, 0))],
        out_specs=[blk, blk],
        out_shape=[jax.ShapeDtypeStruct(mem.shape, BF16)] * 2,
        compiler_params=pltpu.CompilerParams(dimension_semantics=("parallel",), vmem_limit_bytes=VMEM_LIMIT),
        name="memkv",
    )(mem, g, w)


def _post_kernel(x_ref, oa_ref, og_ref, wo_ref, gx_ref, wq_ref, kx_ref, vx_ref, wxo_ref, o_ref):
    tm = x_ref.shape[0]
    halves = [slice(i * tm // POST_PARTS, (i + 1) * tm // POST_PARTS) for i in range(POST_PARTS)]
    x1 = [x_ref[r, :] + _dot(oa_ref[r, :], wo_ref[0:A_WIDTH, :]) + _dot(og_ref[r, :], wo_ref[A_WIDTH:D_MODEL, :])
          for r in halves]
    h = [_rms(v, gx_ref[...]).astype(BF16) for v in x1]
    qx = [(_dot(v, wq_ref[...]) * (LOG2E * X_HEAD_DIM ** -0.5)).astype(BF16) for v in h]
    outs = [[] for _ in halves]
    for hd in range(X_HEADS):
        sl = slice(X_HEAD_DIM * hd, X_HEAD_DIM * (hd + 1))
        s = [_nt(q[:, sl], kx_ref[0, :, sl]) for q in qx]
        pexp = [jnp.exp2(v - jnp.max(v, axis=1, keepdims=True)) for v in s]
        pn = [(v / jnp.sum(v, axis=1, keepdims=True)).astype(BF16) for v in pexp]
        for i, v in enumerate(pn):
            outs[i].append(_dot(v, vx_ref[0, :, sl]).astype(BF16))
    for i, r in enumerate(halves):
        o_ref[r, :] = x1[i] + _dot(jnp.concatenate(outs[i], axis=1), wxo_ref[...])


def _post(x2, oa, og, wo, gx, wq, kx, vx, wxo, seq):
    n = x2.shape[0]
    tm = POST_TILE
    per_b = seq // tm
    mlen = kx.shape[1]
    const = functools.partial(pl.BlockSpec, pipeline_mode=pl.Buffered(1))
    row = lambda width: pl.BlockSpec((tm, width), lambda i: (i, 0))
    mem = pl.BlockSpec((1, mlen, D_MODEL), lambda i: (i // per_b, 0, 0))
    sq = const((D_MODEL, D_MODEL), lambda i: (0, 0))
    return pl.pallas_call(
        _post_kernel,
        grid=(n // tm,),
        in_specs=[row(D_MODEL), row(A_WIDTH), row(G_WIDTH), sq, const((1, D_MODEL), lambda i: (0, 0)), sq, mem, mem, sq],
        out_specs=row(D_MODEL),
        out_shape=jax.ShapeDtypeStruct((n, D_MODEL), F32),
        compiler_params=pltpu.CompilerParams(dimension_semantics=("parallel",), vmem_limit_bytes=VMEM_LIMIT),
        name="post",
    )(x2, oa, og, wo, gx, wq, kx, vx, wxo)


def _mlp_kernel(x_ref, g_ref, wu_ref, wd_ref, gf_ref, o_ref, *, final_norm):
    x = x_ref[...]
    h = _rms(x, g_ref[...]).astype(BF16)
    acc = x
    for c in range(D_FF // D_MODEL):
        sl = slice(D_MODEL * c, D_MODEL * (c + 1))
        u = jnp.square(jnp.maximum(_dot(h, wu_ref[:, sl]), 0.0)).astype(BF16)
        acc = acc + _dot(u, wd_ref[sl, :])
    o_ref[...] = _rms(acc, gf_ref[...]) if final_norm else acc


def _mlp(x2, g, wu, wd, gf, final_norm):
    n = x2.shape[0]
    tm = ROW_TILE
    const = functools.partial(pl.BlockSpec, pipeline_mode=pl.Buffered(1))
    row = pl.BlockSpec((tm, D_MODEL), lambda i: (i, 0))
    vec = const((1, D_MODEL), lambda i: (0, 0))
    return pl.pallas_call(
        functools.partial(_mlp_kernel, final_norm=final_norm),
        grid=(n // tm,),
        in_specs=[row, vec, const((D_MODEL, D_FF), lambda i: (0, 0)), const((D_FF, D_MODEL), lambda i: (0, 0)), vec],
        out_specs=row,
        out_shape=jax.ShapeDtypeStruct((n, D_MODEL), F32),
        compiler_params=pltpu.CompilerParams(dimension_semantics=("parallel",), vmem_limit_bytes=VMEM_LIMIT),
        name="mlp",
    )(x2, g, wu, wd, gf)


def kernel(x, mem, rp_table, norm_mix, w_in, w_gate_up, b_gate, g_norm, w_out, norm_xattn, norm_mem, w_xq, w_xkv,
           w_xo, norm_mlp, w_up, w_down, norm_final):
    batch, seq, _ = x.shape
    depth = w_in.shape[0]
    assert seq % ROW_TILE == 0 and seq % MOBA_BLOCK == 0 and seq // MOBA_BLOCK <= 8
    x2 = x.reshape(batch * seq, D_MODEL)
    bias = _bias_tiles(rp_table)
    c31 = rp_table[RP_BUCKETS - 1]
    va_lo = 2 * A_WIDTH
    glr_lo = 3 * A_WIDTH + 2 * G_KEY_WIDTH + G_WIDTH
    for l in range(depth):
        wl = w_in[l]
        w = jnp.concatenate([wl[:, :va_lo], wl[:, va_lo + A_WIDTH:glr_lo], wl[:, glr_lo + G_GATE_RANK:],
                             wl[:, glr_lo:glr_lo + G_GATE_RANK],
                             jnp.zeros((D_MODEL, LANES - G_GATE_RANK), F32)], axis=1).astype(BF16)
        wvt = wl[:, va_lo:va_lo + A_WIDTH].T.astype(BF16)
        wgu = jnp.concatenate([w_gate_up[l], jnp.zeros((LANES - G_GATE_RANK, G_KEY_WIDTH), F32)], axis=0).astype(BF16)
        qk, vt, qg, kg, la, vg, rg = _inproj(x2, norm_mix[l][None], w, wvt, wgu, b_gate[l][None])
        oa = _moba(qk, vt, bias, c31, batch, seq)
        og = _gla(qg, kg, la, vg, rg, g_norm[l][None], batch, seq)
        kx, vx = _memkv(mem, norm_mem[l][None], w_xkv[l].astype(BF16))
        x2 = _post(x2, oa, og, w_out[l].astype(BF16), norm_xattn[l][None], w_xq[l].astype(BF16), kx, vx,
                   w_xo[l].astype(BF16), seq)
        last = l == depth - 1
        x2 = _mlp(x2, norm_mlp[l][None], w_up[l].astype(BF16), w_down[l].astype(BF16), norm_final[None], last)
    return x2.reshape(batch, seq, D_MODEL)
```

```python
import functools
import math

import numpy as np
import jax
import jax.numpy as jnp
from jax import lax
from jax.experimental import pallas as pl
from jax.experimental.pallas import tpu as pltpu

F32 = jnp.float32
BF16 = jnp.bfloat16

D_MODEL = 1024
A_HEADS = 8
A_HEAD_DIM = 64
A_WIDTH = A_HEADS * A_HEAD_DIM
MOBA_BLOCK = 256
MOBA_TOPK = 3
G_HEADS = 4
G_WIDTH = D_MODEL - A_WIDTH
G_HEAD_V = G_WIDTH // G_HEADS
G_KEY_WIDTH = G_WIDTH // 2
G_HEAD_K = G_KEY_WIDTH // G_HEADS
G_GATE_RANK = 16
G_GATE_NORM = 16.0
X_HEADS = 4
X_HEAD_DIM = D_MODEL // X_HEADS
D_FF = 4 * D_MODEL
RP_BUCKETS = 32
RP_MAX_DIST = 128
EPS = 1e-6

LANES = 128
NEG = -1e30
GLA_CHUNK = 128
GLA_STEP = 512
LOG2E = float(np.log2(np.e))
ROW_TILE = 512
MLP_TILE = 1024
MLP_PARTS = 2
POST_TILE = 1024
POST_PARTS = 2
VMEM_LIMIT = 56 * 1024 * 1024

_C_QK = 2 * A_WIDTH
_C_QG = _C_QK
_C_KG = _C_QG + G_KEY_WIDTH
_C_VG = _C_KG + G_KEY_WIDTH
_C_RG = _C_VG + G_WIDTH
_C_GLR = _C_RG + G_WIDTH
_C_END = _C_GLR + LANES


def _nt(a, b):
    return lax.dot_general(a, b, (((1,), (1,)), ((), ())), preferred_element_type=F32)


def _tn(a, b):
    return lax.dot_general(a, b, (((0,), (0,)), ((), ())), preferred_element_type=F32)


def _dot(a, b):
    return jnp.dot(a, b, preferred_element_type=F32)


def _rms(x, g):
    return x * lax.rsqrt(jnp.mean(x * x, axis=-1, keepdims=True) + EPS) * g


def _bucket_thresholds():
    max_exact = RP_BUCKETS // 2
    d = np.arange(1, 4 * RP_MAX_DIST)
    val = (np.log(d.astype(np.float32) / np.float32(max_exact)) / np.float32(math.log(RP_MAX_DIST / max_exact))
           * np.float32(RP_BUCKETS - max_exact))
    bucket = np.minimum(max_exact + val.astype(np.int32), RP_BUCKETS - 1)
    return [int(d[(d >= max_exact) & (bucket >= max_exact + k)][0]) for k in range(1, RP_BUCKETS - max_exact)]


_THRESHOLDS = _bucket_thresholds()


def _bias_kernel(tab_ref, out_ref):
    h = pl.program_id(0)
    shape = (2 * MOBA_BLOCK, MOBA_BLOCK)
    d = lax.broadcasted_iota(jnp.int32, shape, 1) - lax.broadcasted_iota(jnp.int32, shape, 0) + MOBA_BLOCK
    max_exact = RP_BUCKETS // 2
    bucket = jnp.where(d < max_exact, d, max_exact)
    for t in _THRESHOLDS:
        bucket = bucket + jnp.where(d >= t, 1, 0)
    m = jnp.full(shape, NEG, F32)
    for b in range(RP_BUCKETS):
        m = jnp.where(bucket == b, tab_ref[b, h] * LOG2E, m)
    out_ref[0] = m


def _bias_tiles(rp_table):
    return pl.pallas_call(
        _bias_kernel,
        grid=(A_HEADS,),
        in_specs=[pl.BlockSpec(memory_space=pltpu.SMEM)],
        out_specs=pl.BlockSpec((1, 2 * MOBA_BLOCK, MOBA_BLOCK), lambda h: (h, 0, 0)),
        out_shape=jax.ShapeDtypeStruct((A_HEADS, 2 * MOBA_BLOCK, MOBA_BLOCK), F32),
        name="bias",
    )(rp_table)


def _inproj_kernel(x_ref, g_ref, w_ref, wvt_ref, wgu_ref, bg_ref, qk_ref, vt_ref, qg_ref, kg_ref, la_ref, vg_ref,
                   rg_ref):
    h = _rms(x_ref[...], g_ref[...]).astype(BF16)

    def proj(lo, hi):
        return _dot(h, w_ref[:, lo:hi])

    qk_ref[:, 0:A_WIDTH] = (proj(0, A_WIDTH) * (LOG2E * A_HEAD_DIM ** -0.5)).astype(BF16)
    qk_ref[:, A_WIDTH:_C_QK] = proj(A_WIDTH, _C_QK).astype(BF16)
    vt_ref[...] = _nt(wvt_ref[...], h).astype(BF16)
    qg_ref[...] = proj(_C_QG, _C_KG) * (G_HEAD_K ** -0.5)
    kg_ref[...] = proj(_C_KG, _C_VG)
    vg_ref[...] = proj(_C_VG, _C_RG).astype(BF16)
    rg_ref[...] = proj(_C_RG, _C_GLR)
    glr = proj(_C_GLR, _C_END).astype(BF16)
    z = _dot(glr, wgu_ref[...]) + bg_ref[...]
    log_sig = jnp.minimum(z, 0.0) - jnp.log(1.0 + jnp.exp(-jnp.abs(z)))
    la_ref[...] = log_sig * (1.0 / G_GATE_NORM)


def _inproj(x2, g, w, wvt, wgu, bg):
    n = x2.shape[0]
    tm = ROW_TILE
    const = functools.partial(pl.BlockSpec, pipeline_mode=pl.Buffered(1))
    row = lambda width: pl.BlockSpec((tm, width), lambda i: (i, 0))
    return pl.pallas_call(
        _inproj_kernel,
        grid=(n // tm,),
        in_specs=[row(D_MODEL),
                  const((1, D_MODEL), lambda i: (0, 0)),
                  const((D_MODEL, _C_END), lambda i: (0, 0)),
                  const((A_WIDTH, D_MODEL), lambda i: (0, 0)),
                  const((LANES, G_KEY_WIDTH), lambda i: (0, 0)),
                  const((1, G_KEY_WIDTH), lambda i: (0, 0))],
        out_specs=[row(_C_QK), pl.BlockSpec((A_WIDTH, tm), lambda i: (0, i)),
                   row(G_KEY_WIDTH), row(G_KEY_WIDTH), row(G_KEY_WIDTH), row(G_WIDTH), row(G_WIDTH)],
        out_shape=[jax.ShapeDtypeStruct((n, _C_QK), BF16),
                   jax.ShapeDtypeStruct((A_WIDTH, n), BF16),
                   jax.ShapeDtypeStruct((n, G_KEY_WIDTH), F32),
                   jax.ShapeDtypeStruct((n, G_KEY_WIDTH), F32),
                   jax.ShapeDtypeStruct((n, G_KEY_WIDTH), F32),
                   jax.ShapeDtypeStruct((n, G_WIDTH), BF16),
                   jax.ShapeDtypeStruct((n, G_WIDTH), F32)],
        compiler_params=pltpu.CompilerParams(dimension_semantics=("parallel",), vmem_limit_bytes=VMEM_LIMIT),
        name="inproj",
    )(x2, g, w, wvt, wgu, bg)


def _moba_kernel(c31_ref, q_ref, k_ref, vt_ref, bias_ref, o_ref, kext, s_scr, *, seq):
    p = pl.program_id(1)
    nblk = seq // MOBA_BLOCK
    half = LANES // 2

    k = k_ref[...]
    rowblk = lax.broadcasted_iota(jnp.int32, (seq, LANES), 0) // MOBA_BLOCK
    lane2 = lax.broadcasted_iota(jnp.int32, (seq, LANES), 1)
    lm0 = jnp.where(lane2 < half, 1.0, 0.0).astype(BF16)
    lm1 = jnp.where(lane2 >= half, 1.0, 0.0).astype(BF16)
    kext[0] = k * lm0 + jnp.where(lane2 - half == rowblk, 1.0, 0.0).astype(BF16)
    kext[1] = k * lm1 + jnp.where(lane2 == rowblk, 1.0, 0.0).astype(BF16)
    n_i = lax.broadcasted_iota(jnp.int32, (16, seq), 0)
    t_i = lax.broadcasted_iota(jnp.int32, (16, seq), 1)
    avg = jnp.where(t_i // MOBA_BLOCK == n_i, 1.0 / MOBA_BLOCK, 0.0).astype(BF16)
    km = _dot(avg, k)
    kmh = km.astype(BF16)
    kml = (km - kmh.astype(F32)).astype(BF16)
    r_i = lax.broadcasted_iota(jnp.int32, (MOBA_BLOCK, MOBA_BLOCK), 0)
    c_i = lax.broadcasted_iota(jnp.int32, (MOBA_BLOCK, MOBA_BLOCK), 1)
    ident = jnp.where(r_i == c_i, 1.0, 0.0).astype(BF16)

    lane_q = lax.broadcasted_iota(jnp.int32, (MOBA_BLOCK, LANES), 1)
    rowi = lax.broadcasted_iota(jnp.int32, (16, MOBA_BLOCK), 0)
    blk = lambda n: slice(n * MOBA_BLOCK, (n + 1) * MOBA_BLOCK)

    def prepare(qi, h, slot):
        qm = q_ref[blk(qi), :] * jnp.where((lane_q < half) if h == 0 else (lane_q >= half), 1.0, 0.0).astype(BF16)
        qext = qm
        if qi > MOBA_TOPK:
            g = _nt(kmh, qm) + _nt(kml, qm)
            cnt = jnp.zeros((16, MOBA_BLOCK), F32)
            for m in range(qi):
                gm = g[m:m + 1, :]
                cnt = cnt + jnp.where((gm > g) | ((gm == g) & (m < rowi)), 1.0, 0.0)
            keep = ((rowi < qi) & (cnt < MOBA_TOPK)) | (rowi >= qi)
            pen_t = jnp.where(keep, 0.0, NEG).astype(BF16)
            off = half if h == 0 else 0
            parts = ([jnp.zeros((off, MOBA_BLOCK), BF16)] if off else []) + [
                pen_t, jnp.zeros((LANES - 16 - off, MOBA_BLOCK), BF16)]
            pen_q = _nt(ident, jnp.concatenate(parts, axis=0))
            qext = qm + pen_q.astype(BF16)
        return dict(qi=qi, h=h, qext=qext, c31=c31_ref[2 * p + h] * LOG2E, sbuf=s_scr.at[slot], near=None,
                    far=None, acc=None)

    def score_matmul(u):
        u["s_all"] = _nt(kext[u["h"], 0:(u["qi"] + 1) * MOBA_BLOCK, :], u["qext"])

    def score_reduce(u):
        qi, h = u["qi"], u["h"]
        for n in range(qi + 1):
            s = u["s_all"][blk(n)]
            if n == qi:
                s = s + bias_ref[h, MOBA_BLOCK:2 * MOBA_BLOCK, :]
            elif n == qi - 1:
                s = s + bias_ref[h, 0:MOBA_BLOCK, :]
            u["sbuf"][blk(n), :] = s
            t = jnp.max(s.reshape(MOBA_BLOCK // 8, 8, MOBA_BLOCK), axis=0)
            key = "far" if n < qi - 1 else "near"
            u[key] = t if u[key] is None else jnp.maximum(u[key], t)
        mx = u["near"] if u["far"] is None else jnp.maximum(u["near"], u["far"] + u["c31"])
        u["m_near"] = jnp.max(mx, axis=0, keepdims=True)
        u["m_far"] = u["m_near"] - u["c31"]

    def prob_exp(u):
        qi = u["qi"]
        u["pt"] = jnp.concatenate(
            [jnp.exp2(u["sbuf"][blk(n), :] - (u["m_far"] if n < qi - 1 else u["m_near"])).astype(BF16)
             for n in range(qi + 1)], axis=0)

    def prob_matmul(u):
        qi, h = u["qi"], u["h"]
        width = (qi + 1) * MOBA_BLOCK
        vrows = vt_ref[half * h:half * (h + 1), 0:width]
        ones_rows = jnp.ones((half, width), BF16)
        vt = jnp.concatenate([vrows, ones_rows] if h == 0 else [ones_rows, vrows], axis=0)
        u["acc"] = _dot(vt, u["pt"])

    outs = {}

    def finish(u):
        acc, h = u["acc"], u["h"]
        num = acc[half * h:half * (h + 1)]
        den = acc[half * (1 - h):half * (1 - h) + 1]
        outs[h] = num / den
        if h == 1:
            o_ref[blk(u["qi"]), :] = jnp.concatenate([outs[0], outs[1]], axis=0).T.astype(BF16)

    group = 2
    stages = [[(qi, h) for qi in range(g * group, (g + 1) * group) for h in range(2)] for g in range(nblk // group)]
    prev = []
    for g, members in enumerate(stages + [[]]):
        cur = [prepare(qi, h, (g % 2) * 2 * group + 2 * (qi % group) + h) for qi, h in members]
        for u in prev:
            prob_exp(u)
        for u in cur:
            score_matmul(u)
        for u in prev:
            prob_matmul(u)
        for u in cur:
            score_reduce(u)
        for u in prev:
            finish(u)
        prev = cur


def _moba(qk, vt, bias, c31, batch, seq):
    n = qk.shape[0]
    npair = A_HEADS // 2
    kern = functools.partial(_moba_kernel, seq=seq)
    return pl.pallas_call(
        kern,
        grid=(batch, npair),
        in_specs=[pl.BlockSpec(memory_space=pltpu.SMEM),
                  pl.BlockSpec((seq, LANES), lambda b, p: (b, p)),
                  pl.BlockSpec((seq, LANES), lambda b, p: (b, npair + p)),
                  pl.BlockSpec((LANES, seq), lambda b, p: (p, b)),
                  pl.BlockSpec((2, 2 * MOBA_BLOCK, MOBA_BLOCK), lambda b, p: (p, 0, 0))],
        out_specs=pl.BlockSpec((seq, LANES), lambda b, p: (b, p)),
        out_shape=jax.ShapeDtypeStruct((n, A_WIDTH), BF16),
        scratch_shapes=[pltpu.VMEM((2, seq, LANES), BF16),
                        pltpu.VMEM((8, seq, MOBA_BLOCK), F32)],
        compiler_params=pltpu.CompilerParams(dimension_semantics=("parallel", "arbitrary"),
                                             vmem_limit_bytes=VMEM_LIMIT),
        name="moba",
    )(c31, qk, qk, vt, bias)


_GLA_LEVELS = int(math.log2(GLA_CHUNK))


_GLA_BIG_LEVELS = _GLA_LEVELS - 3


def _gla_weights():
    c = GLA_CHUNK
    i = np.arange(c)[:, None]
    j = np.arange(c)[None, :]
    mats = [(j <= i)]
    for lvl in range(_GLA_BIG_LEVELS, _GLA_LEVELS):
        s = (c // 2) >> lvl
        ref = (i // (2 * s)) * (2 * s) + s - 1
        mats.append(np.where(i > ref, (j > ref) & (j <= i), (j > i) & (j <= ref)))
    return np.concatenate(mats, axis=0).astype(np.float32)


def _gla_level_map():
    c = GLA_CHUNK
    i = np.arange(c)[:, None]
    j = np.arange(c)[None, :]
    top_bit = np.floor(np.log2(np.maximum(i ^ j, 1))).astype(np.int64)
    lvl = np.where(j < i, _GLA_LEVELS - 1 - top_bit, np.where(j == i, _GLA_LEVELS, _GLA_LEVELS + 1))
    return np.concatenate([lvl, lvl], axis=1).astype(np.int32)


def _gla_kernel(q_ref, k_ref, la_ref, v_ref, rg_ref, w_ref, lmap_ref, gn_ref, o_ref, st_ref):
    c = GLA_CHUNK
    half = LANES // 2
    npair = G_HEADS // 2

    @pl.when(pl.program_id(1) == 0)
    def _init():
        st_ref[...] = jnp.zeros(st_ref.shape, F32)

    w = w_ref[...]
    lmap = lmap_ref[...]
    lane = lax.broadcasted_iota(jnp.int32, (1, LANES), 1)
    lane_c = lax.broadcasted_iota(jnp.int32, (c, LANES), 1)
    lm = [jnp.where(lane_c < half, 1.0, 0.0).astype(BF16), jnp.where(lane_c >= half, 1.0, 0.0).astype(BF16)]
    row_small = lax.broadcasted_iota(jnp.int32, (c, G_KEY_WIDTH), 0)

    def level_factors(q, k, b, d_all, lvl):
        s = (c // 2) >> lvl
        if s < 8:
            j = lvl - _GLA_BIG_LEVELS
            e = jnp.exp2(d_all[c * (j + 1):c * (j + 2)])
            odd = (row_small // s) % 2 == 1
            return jnp.where(odd, q * e, 0.0), jnp.where(odd, 0.0, k * e)
        zeros = jnp.zeros((s, q.shape[1]), F32)
        qp, kp = [], []
        for blk in range(c // s):
            rows = slice(blk * s, (blk + 1) * s)
            ref = (blk // 2) * 2 * s + s - 1
            if blk % 2 == 1:
                qp.append(q[rows] * jnp.exp2(b[rows] - b[ref:ref + 1]))
                kp.append(zeros)
            else:
                qp.append(zeros)
                kp.append(k[rows] * jnp.exp2(b[ref:ref + 1] - b[rows]))
        return jnp.concatenate(qp, axis=0), jnp.concatenate(kp, axis=0)

    nch = GLA_STEP // c
    rows_of = [slice(ch * c, (ch + 1) * c) for ch in range(nch)]
    pair_lanes = [slice(LANES * p, LANES * (p + 1)) for p in range(npair)]
    d_alls = []
    for ch in range(nch):
        la = la_ref[rows_of[ch], :] * LOG2E
        hi = la.astype(BF16)
        lo = (la - hi.astype(F32)).astype(BF16)
        d_alls.append(_dot(w, hi) + _dot(w, lo))
    chunks = []
    for ch in range(nch):
        d_all = d_alls[ch]
        q = q_ref[rows_of[ch], :]
        k = k_ref[rows_of[ch], :]
        b = d_all[0:c]
        blast = b[c - 1:c, :]
        qts, kts = [], []
        for lvl in range(_GLA_LEVELS):
            qt, kt = level_factors(q, k, b, d_all, lvl)
            qts.append(qt.astype(BF16))
            kts.append(kt.astype(BF16))
        qts.append(q.astype(BF16))
        kts.append(k.astype(BF16))
        chunks.append(dict(qts=qts, kts=kts, qe=(q * jnp.exp2(b)).astype(BF16),
                           khat=(k * jnp.exp2(blast - b)).astype(BF16), dec=jnp.exp2(blast),
                           a=[jnp.zeros((c, 2 * c), F32)] * npair))
    for lvl in range(_GLA_LEVELS + 1):
        mask = lmap == lvl
        for u in chunks:
            for p, sl in enumerate(pair_lanes):
                kp = u["kts"][lvl][:, sl]
                pm = _nt(u["qts"][lvl][:, sl], jnp.concatenate([kp * lm[0], kp * lm[1]], axis=0))
                u["a"] = [jnp.where(mask, pm, a) if i == p else a for i, a in enumerate(u["a"])]
    states = [st_ref[p] for p in range(npair)]
    for ch, u in enumerate(chunks):
        rows = rows_of[ch]
        for p, sl in enumerate(pair_lanes):
            stb = states[p].astype(BF16)
            ups = []
            for hh in range(2):
                h = 2 * p + hh
                vh = v_ref[rows, LANES * h:LANES * (h + 1)]
                o = _dot(u["a"][p][:, hh * c:(hh + 1) * c].astype(BF16), vh) + _nt(u["qe"][:, sl] * lm[hh], stb)
                y = _rms(o, gn_ref[...])
                rg = rg_ref[rows, LANES * h:LANES * (h + 1)]
                y = y * (rg / (1.0 + jnp.exp(-rg)))
                o_ref[rows, LANES * h:LANES * (h + 1)] = y.astype(BF16)
                ups.append(_tn(vh, u["khat"][:, sl]))
            states[p] = states[p] * u["dec"][:, sl] + jnp.where(lane < half, ups[0], ups[1])
    for p in range(npair):
        st_ref[p] = states[p]


def _gla(qg, kg, la, vg, rg, gn, batch, seq):
    n = qg.shape[0]
    nchunk = seq // GLA_STEP
    w = jnp.asarray(_gla_weights(), BF16)
    lmap = jnp.asarray(_gla_level_map())
    row = lambda width: pl.BlockSpec((GLA_STEP, width), lambda b, c: (b * nchunk + c, 0))
    const = functools.partial(pl.BlockSpec, pipeline_mode=pl.Buffered(1))
    return pl.pallas_call(
        _gla_kernel,
        grid=(batch, nchunk),
        in_specs=[row(G_KEY_WIDTH), row(G_KEY_WIDTH), row(G_KEY_WIDTH), row(G_WIDTH), row(G_WIDTH),
                  const(w.shape, lambda b, c: (0, 0)),
                  const(lmap.shape, lambda b, c: (0, 0)),
                  const((1, G_HEAD_V), lambda b, c: (0, 0))],
        out_specs=row(G_WIDTH),
        out_shape=jax.ShapeDtypeStruct((n, G_WIDTH), BF16),
        scratch_shapes=[pltpu.VMEM((G_HEADS // 2, G_HEAD_V, LANES), F32)],
        compiler_params=pltpu.CompilerParams(dimension_semantics=("parallel", "arbitrary"),
                                             vmem_limit_bytes=VMEM_LIMIT),
        name="gla",
    )(qg, kg, la, vg, rg, w, lmap, gn)


def _memkv_kernel(m_ref, g_ref, w_ref, k_ref, v_ref):
    h = _rms(m_ref[0], g_ref[...]).astype(BF16)
    k_ref[0] = _dot(h, w_ref[:, 0:D_MODEL]).astype(BF16)
    v_ref[0] = _dot(h, w_ref[:, D_MODEL:2 * D_MODEL]).astype(BF16)


def _memkv(mem, g, w):
    batch, mlen, _ = mem.shape
    const = functools.partial(pl.BlockSpec, pipeline_mode=pl.Buffered(1))
    blk = pl.BlockSpec((1, mlen, D_MODEL), lambda b: (b, 0, 0))
    return pl.pallas_call(
        _memkv_kernel,
        grid=(batch,),
        in_specs=[blk, const((1, D_MODEL), lambda b: (0, 0)), const((D_MODEL, 2 * D_MODEL), lambda b: (0, 0))],
        out_specs=[blk, blk],
        out_shape=[jax.ShapeDtypeStruct(mem.shape, BF16)] * 2,
        compiler_params=pltpu.CompilerParams(dimension_semantics=("parallel",), vmem_limit_bytes=VMEM_LIMIT),
        name="memkv",
    )(mem, g, w)


def _post_kernel(x_ref, oa_ref, og_ref, wo_ref, gx_ref, wq_ref, kx_ref, vx_ref, wxo_ref, o_ref):
    tm = x_ref.shape[0]
    halves = [slice(i * tm // POST_PARTS, (i + 1) * tm // POST_PARTS) for i in range(POST_PARTS)]
    x1 = [x_ref[r, :] + _dot(oa_ref[r, :], wo_ref[0:A_WIDTH, :]) + _dot(og_ref[r, :], wo_ref[A_WIDTH:D_MODEL, :])
          for r in halves]
    h = [_rms(v, gx_ref[...]).astype(BF16) for v in x1]
    qx = [(_dot(v, wq_ref[...]) * (LOG2E * X_HEAD_DIM ** -0.5)).astype(BF16) for v in h]
    outs = [[] for _ in halves]
    for hd in range(X_HEADS):
        sl = slice(X_HEAD_DIM * hd, X_HEAD_DIM * (hd + 1))
        s = [_nt(q[:, sl], kx_ref[0, :, sl]) for q in qx]
        pexp = [jnp.exp2(v - jnp.max(v, axis=1, keepdims=True)) for v in s]
        pn = [(v / jnp.sum(v, axis=1, keepdims=True)).astype(BF16) for v in pexp]
        for i, v in enumerate(pn):
            outs[i].append(_dot(v, vx_ref[0, :, sl]).astype(BF16))
    for i, r in enumerate(halves):
        o_ref[r, :] = x1[i] + _dot(jnp.concatenate(outs[i], axis=1), wxo_ref[...])


def _post(x2, oa, og, wo, gx, wq, kx, vx, wxo, seq):
    n = x2.shape[0]
    tm = POST_TILE
    per_b = seq // tm
    mlen = kx.shape[1]
    const = functools.partial(pl.BlockSpec, pipeline_mode=pl.Buffered(1))
    row = lambda width: pl.BlockSpec((tm, width), lambda i: (i, 0))
    mem = pl.BlockSpec((1, mlen, D_MODEL), lambda i: (i // per_b, 0, 0))
    sq = const((D_MODEL, D_MODEL), lambda i: (0, 0))
    return pl.pallas_call(
        _post_kernel,
        grid=(n // tm,),
        in_specs=[row(D_MODEL), row(A_WIDTH), row(G_WIDTH), sq, const((1, D_MODEL), lambda i: (0, 0)), sq, mem, mem, sq],
        out_specs=row(D_MODEL),
        out_shape=jax.ShapeDtypeStruct((n, D_MODEL), F32),
        compiler_params=pltpu.CompilerParams(dimension_semantics=("parallel",), vmem_limit_bytes=VMEM_LIMIT),
        name="post",
    )(x2, oa, og, wo, gx, wq, kx, vx, wxo)


def _mlp_kernel(x_ref, g_ref, wu_ref, wd_ref, gf_ref, o_ref, *, final_norm):
    tm = x_ref.shape[0]
    parts = [slice(i * tm // MLP_PARTS, (i + 1) * tm // MLP_PARTS) for i in range(MLP_PARTS)]
    acc = [x_ref[r, :] for r in parts]
    h = [_rms(x, g_ref[...]).astype(BF16) for x in acc]
    for c in range(D_FF // D_MODEL):
        sl = slice(D_MODEL * c, D_MODEL * (c + 1))
        u = [jnp.square(jnp.maximum(_dot(v, wu_ref[:, sl]), 0.0)).astype(BF16) for v in h]
        acc = [a + _dot(v, wd_ref[sl, :]) for a, v in zip(acc, u)]
    for r, a in zip(parts, acc):
        o_ref[r, :] = _rms(a, gf_ref[...]) if final_norm else a


def _mlp(x2, g, wu, wd, gf, final_norm):
    n = x2.shape[0]
    tm = MLP_TILE
    const = functools.partial(pl.BlockSpec, pipeline_mode=pl.Buffered(1))
    row = pl.BlockSpec((tm, D_MODEL), lambda i: (i, 0))
    vec = const((1, D_MODEL), lambda i: (0, 0))
    return pl.pallas_call(
        functools.partial(_mlp_kernel, final_norm=final_norm),
        grid=(n // tm,),
        in_specs=[row, vec, const((D_MODEL, D_FF), lambda i: (0, 0)), const((D_FF, D_MODEL), lambda i: (0, 0)), vec],
        out_specs=row,
        out_shape=jax.ShapeDtypeStruct((n, D_MODEL), F32),
        compiler_params=pltpu.CompilerParams(dimension_semantics=("parallel",), vmem_limit_bytes=VMEM_LIMIT),
        name="mlp",
    )(x2, g, wu, wd, gf)


def kernel(x, mem, rp_table, norm_mix, w_in, w_gate_up, b_gate, g_norm, w_out, norm_xattn, norm_mem, w_xq, w_xkv,
           w_xo, norm_mlp, w_up, w_down, norm_final):
    batch, seq, _ = x.shape
    depth = w_in.shape[0]
    assert seq % max(ROW_TILE, POST_TILE, MLP_TILE, GLA_STEP) == 0 and seq // MOBA_BLOCK <= 8
    x2 = x.reshape(batch * seq, D_MODEL)
    bias = _bias_tiles(rp_table)
    c31 = rp_table[RP_BUCKETS - 1]
    va_lo = 2 * A_WIDTH
    glr_lo = 3 * A_WIDTH + 2 * G_KEY_WIDTH + G_WIDTH
    for l in range(depth):
        wl = w_in[l]
        w = jnp.concatenate([wl[:, :va_lo], wl[:, va_lo + A_WIDTH:glr_lo], wl[:, glr_lo + G_GATE_RANK:],
                             wl[:, glr_lo:glr_lo + G_GATE_RANK],
                             jnp.zeros((D_MODEL, LANES - G_GATE_RANK), F32)], axis=1).astype(BF16)
        wvt = wl[:, va_lo:va_lo + A_WIDTH].T.astype(BF16)
        wgu = jnp.concatenate([w_gate_up[l], jnp.zeros((LANES - G_GATE_RANK, G_KEY_WIDTH), F32)], axis=0).astype(BF16)
        qk, vt, qg, kg, la, vg, rg = _inproj(x2, norm_mix[l][None], w, wvt, wgu, b_gate[l][None])
        oa = _moba(qk, vt, bias, c31, batch, seq)
        og = _gla(qg, kg, la, vg, rg, g_norm[l][None], batch, seq)
        kx, vx = _memkv(mem, norm_mem[l][None], w_xkv[l].astype(BF16))
        x2 = _post(x2, oa, og, w_out[l].astype(BF16), norm_xattn[l][None], w_xq[l].astype(BF16), kx, vx,
                   w_xo[l].astype(BF16), seq)
        last = l == depth - 1
        x2 = _mlp(x2, norm_mlp[l][None], w_up[l].astype(BF16), w_down[l].astype(BF16), norm_final[None], last)
    return x2.reshape(batch, seq, D_MODEL)
```

```python
import functools
import math

import numpy as np
import jax
import jax.numpy as jnp
from jax import lax
from jax.experimental import pallas as pl
from jax.experimental.pallas import tpu as pltpu

F32 = jnp.float32
BF16 = jnp.bfloat16

D_MODEL = 1024
A_HEADS = 8
A_HEAD_DIM = 64
A_WIDTH = A_HEADS * A_HEAD_DIM
MOBA_BLOCK = 256
MOBA_TOPK = 3
G_HEADS = 4
G_WIDTH = D_MODEL - A_WIDTH
G_HEAD_V = G_WIDTH // G_HEADS
G_KEY_WIDTH = G_WIDTH // 2
G_HEAD_K = G_KEY_WIDTH // G_HEADS
G_GATE_RANK = 16
G_GATE_NORM = 16.0
X_HEADS = 4
X_HEAD_DIM = D_MODEL // X_HEADS
D_FF = 4 * D_MODEL
RP_BUCKETS = 32
RP_MAX_DIST = 128
EPS = 1e-6

LANES = 128
NEG = -1e30
GLA_CHUNK = 128
GLA_STEP = 1024
LOG2E = float(np.log2(np.e))
INPROJ_TILE = 1024
INPROJ_PARTS = 2
MLP_TILE = 1024
MLP_PARTS = 2
POST_TILE = 1024
POST_PARTS = 2
VMEM_LIMIT = 56 * 1024 * 1024

_C_QK = 2 * A_WIDTH
_C_QG = _C_QK
_C_KG = _C_QG + G_KEY_WIDTH
_C_VG = _C_KG + G_KEY_WIDTH
_C_RG = _C_VG + G_WIDTH
_C_GLR = _C_RG + G_WIDTH
_C_END = _C_GLR + LANES


def _nt(a, b):
    return lax.dot_general(a, b, (((1,), (1,)), ((), ())), preferred_element_type=F32)


def _tn(a, b):
    return lax.dot_general(a, b, (((0,), (0,)), ((), ())), preferred_element_type=F32)


def _dot(a, b):
    return jnp.dot(a, b, preferred_element_type=F32)


def _rms(x, g):
    return x * lax.rsqrt(jnp.mean(x * x, axis=-1, keepdims=True) + EPS) * g


def _bucket_thresholds():
    max_exact = RP_BUCKETS // 2
    d = np.arange(1, 4 * RP_MAX_DIST)
    val = (np.log(d.astype(np.float32) / np.float32(max_exact)) / np.float32(math.log(RP_MAX_DIST / max_exact))
           * np.float32(RP_BUCKETS - max_exact))
    bucket = np.minimum(max_exact + val.astype(np.int32), RP_BUCKETS - 1)
    return [int(d[(d >= max_exact) & (bucket >= max_exact + k)][0]) for k in range(1, RP_BUCKETS - max_exact)]


_THRESHOLDS = _bucket_thresholds()


def _bias_kernel(tab_ref, out_ref):
    h = pl.program_id(0)
    shape = (2 * MOBA_BLOCK, MOBA_BLOCK)
    d = lax.broadcasted_iota(jnp.int32, shape, 1) - lax.broadcasted_iota(jnp.int32, shape, 0) + MOBA_BLOCK
    max_exact = RP_BUCKETS // 2
    bucket = jnp.where(d < max_exact, d, max_exact)
    for t in _THRESHOLDS:
        bucket = bucket + jnp.where(d >= t, 1, 0)
    m = jnp.full(shape, NEG, F32)
    for b in range(RP_BUCKETS):
        m = jnp.where(bucket == b, tab_ref[b, h] * LOG2E, m)
    out_ref[0] = m


def _bias_tiles(rp_table):
    return pl.pallas_call(
        _bias_kernel,
        grid=(A_HEADS,),
        in_specs=[pl.BlockSpec(memory_space=pltpu.SMEM)],
        out_specs=pl.BlockSpec((1, 2 * MOBA_BLOCK, MOBA_BLOCK), lambda h: (h, 0, 0)),
        out_shape=jax.ShapeDtypeStruct((A_HEADS, 2 * MOBA_BLOCK, MOBA_BLOCK), F32),
        name="bias",
    )(rp_table)


def _inproj_kernel(x_ref, g_ref, w_ref, wvt_ref, wgu_ref, bg_ref, qk_ref, vt_ref, qg_ref, kg_ref, la_ref, vg_ref,
                   rg_ref):
    tm = x_ref.shape[0]
    parts = [slice(i * tm // INPROJ_PARTS, (i + 1) * tm // INPROJ_PARTS) for i in range(INPROJ_PARTS)]
    hs = [_rms(x_ref[r, :], g_ref[...]).astype(BF16) for r in parts]

    def proj(h, lo, hi):
        return _dot(h, w_ref[:, lo:hi])

    for r, h in zip(parts, hs):
        qk_ref[r, 0:A_WIDTH] = (proj(h, 0, A_WIDTH) * (LOG2E * A_HEAD_DIM ** -0.5)).astype(BF16)
    for r, h in zip(parts, hs):
        qk_ref[r, A_WIDTH:_C_QK] = proj(h, A_WIDTH, _C_QK).astype(BF16)
    for r, h in zip(parts, hs):
        vt_ref[:, r] = _nt(wvt_ref[...], h).astype(BF16)
    for r, h in zip(parts, hs):
        qg_ref[r, :] = proj(h, _C_QG, _C_KG) * (G_HEAD_K ** -0.5)
        kg_ref[r, :] = proj(h, _C_KG, _C_VG)
    for r, h in zip(parts, hs):
        vg_ref[r, :] = proj(h, _C_VG, _C_RG).astype(BF16)
    for r, h in zip(parts, hs):
        rg_ref[r, :] = proj(h, _C_RG, _C_GLR)
    for r, h in zip(parts, hs):
        glr = proj(h, _C_GLR, _C_END).astype(BF16)
        z = _dot(glr, wgu_ref[...]) + bg_ref[...]
        log_sig = jnp.minimum(z, 0.0) - jnp.log(1.0 + jnp.exp(-jnp.abs(z)))
        la_ref[r, :] = log_sig * (1.0 / G_GATE_NORM)


def _inproj(x2, g, w, wvt, wgu, bg):
    n = x2.shape[0]
    tm = INPROJ_TILE
    const = functools.partial(pl.BlockSpec, pipeline_mode=pl.Buffered(1))
    row = lambda width: pl.BlockSpec((tm, width), lambda i: (i, 0))
    return pl.pallas_call(
        _inproj_kernel,
        grid=(n // tm,),
        in_specs=[row(D_MODEL),
                  const((1, D_MODEL), lambda i: (0, 0)),
                  const((D_MODEL, _C_END), lambda i: (0, 0)),
                  const((A_WIDTH, D_MODEL), lambda i: (0, 0)),
                  const((LANES, G_KEY_WIDTH), lambda i: (0, 0)),
                  const((1, G_KEY_WIDTH), lambda i: (0, 0))],
        out_specs=[row(_C_QK), pl.BlockSpec((A_WIDTH, tm), lambda i: (0, i)),
                   row(G_KEY_WIDTH), row(G_KEY_WIDTH), row(G_KEY_WIDTH), row(G_WIDTH), row(G_WIDTH)],
        out_shape=[jax.ShapeDtypeStruct((n, _C_QK), BF16),
                   jax.ShapeDtypeStruct((A_WIDTH, n), BF16),
                   jax.ShapeDtypeStruct((n, G_KEY_WIDTH), F32),
                   jax.ShapeDtypeStruct((n, G_KEY_WIDTH), F32),
                   jax.ShapeDtypeStruct((n, G_KEY_WIDTH), F32),
                   jax.ShapeDtypeStruct((n, G_WIDTH), BF16),
                   jax.ShapeDtypeStruct((n, G_WIDTH), F32)],
        compiler_params=pltpu.CompilerParams(dimension_semantics=("parallel",), vmem_limit_bytes=VMEM_LIMIT),
        name="inproj",
    )(x2, g, w, wvt, wgu, bg)


def _moba_kernel(c31_ref, q_ref, k_ref, vt_ref, bias_ref, o_ref, kext, s_scr, *, seq):
    p = pl.program_id(1)
    nblk = seq // MOBA_BLOCK
    half = LANES // 2

    k = k_ref[...]
    rowblk = lax.broadcasted_iota(jnp.int32, (seq, LANES), 0) // MOBA_BLOCK
    lane2 = lax.broadcasted_iota(jnp.int32, (seq, LANES), 1)
    lm0 = jnp.where(lane2 < half, 1.0, 0.0).astype(BF16)
    lm1 = jnp.where(lane2 >= half, 1.0, 0.0).astype(BF16)
    kext[0] = k * lm0 + jnp.where(lane2 - half == rowblk, 1.0, 0.0).astype(BF16)
    kext[1] = k * lm1 + jnp.where(lane2 == rowblk, 1.0, 0.0).astype(BF16)
    n_i = lax.broadcasted_iota(jnp.int32, (16, seq), 0)
    t_i = lax.broadcasted_iota(jnp.int32, (16, seq), 1)
    avg = jnp.where(t_i // MOBA_BLOCK == n_i, 1.0 / MOBA_BLOCK, 0.0).astype(BF16)
    km = _dot(avg, k)
    kmh = km.astype(BF16)
    kml = (km - kmh.astype(F32)).astype(BF16)
    r_i = lax.broadcasted_iota(jnp.int32, (MOBA_BLOCK, MOBA_BLOCK), 0)
    c_i = lax.broadcasted_iota(jnp.int32, (MOBA_BLOCK, MOBA_BLOCK), 1)
    ident = jnp.where(r_i == c_i, 1.0, 0.0).astype(BF16)

    lane_q = lax.broadcasted_iota(jnp.int32, (MOBA_BLOCK, LANES), 1)
    rowi = lax.broadcasted_iota(jnp.int32, (16, MOBA_BLOCK), 0)
    blk = lambda n: slice(n * MOBA_BLOCK, (n + 1) * MOBA_BLOCK)

    def prepare(qi, h, slot):
        qm = q_ref[blk(qi), :] * jnp.where((lane_q < half) if h == 0 else (lane_q >= half), 1.0, 0.0).astype(BF16)
        qext = qm
        if qi > MOBA_TOPK:
            g = _nt(kmh, qm) + _nt(kml, qm)
            cnt = jnp.zeros((16, MOBA_BLOCK), F32)
            for m in range(qi):
                gm = g[m:m + 1, :]
                cnt = cnt + jnp.where((gm > g) | ((gm == g) & (m < rowi)), 1.0, 0.0)
            keep = ((rowi < qi) & (cnt < MOBA_TOPK)) | (rowi >= qi)
            pen_t = jnp.where(keep, 0.0, NEG).astype(BF16)
            off = half if h == 0 else 0
            parts = ([jnp.zeros((off, MOBA_BLOCK), BF16)] if off else []) + [
                pen_t, jnp.zeros((LANES - 16 - off, MOBA_BLOCK), BF16)]
            pen_q = _nt(ident, jnp.concatenate(parts, axis=0))
            qext = qm + pen_q.astype(BF16)
        return dict(qi=qi, h=h, qext=qext, c31=c31_ref[2 * p + h] * LOG2E, sbuf=s_scr.at[slot], near=None,
                    far=None, acc=None)

    def score_matmul(u):
        u["s_all"] = _nt(kext[u["h"], 0:(u["qi"] + 1) * MOBA_BLOCK, :], u["qext"])

    def score_reduce(u):
        qi, h = u["qi"], u["h"]
        for n in range(qi + 1):
            s = u["s_all"][blk(n)]
            if n == qi:
                s = s + bias_ref[h, MOBA_BLOCK:2 * MOBA_BLOCK, :]
            elif n == qi - 1:
                s = s + bias_ref[h, 0:MOBA_BLOCK, :]
            u["sbuf"][blk(n), :] = s
            t = jnp.max(s.reshape(MOBA_BLOCK // 8, 8, MOBA_BLOCK), axis=0)
            key = "far" if n < qi - 1 else "near"
            u[key] = t if u[key] is None else jnp.maximum(u[key], t)
        mx = u["near"] if u["far"] is None else jnp.maximum(u["near"], u["far"] + u["c31"])
        u["m_near"] = jnp.max(mx, axis=0, keepdims=True)
        u["m_far"] = u["m_near"] - u["c31"]

    def prob_exp(u):
        qi = u["qi"]
        u["pt"] = jnp.concatenate(
            [jnp.exp2(u["sbuf"][blk(n), :] - (u["m_far"] if n < qi - 1 else u["m_near"])).astype(BF16)
             for n in range(qi + 1)], axis=0)

    def prob_matmul(u):
        qi, h = u["qi"], u["h"]
        width = (qi + 1) * MOBA_BLOCK
        vrows = vt_ref[half * h:half * (h + 1), 0:width]
        ones_rows = jnp.ones((half, width), BF16)
        vt = jnp.concatenate([vrows, ones_rows] if h == 0 else [ones_rows, vrows], axis=0)
        u["acc"] = _dot(vt, u["pt"])

    outs = {}

    def finish(u):
        acc, h = u["acc"], u["h"]
        num = acc[half * h:half * (h + 1)]
        den = acc[half * (1 - h):half * (1 - h) + 1]
        outs[h] = num / den
        if h == 1:
            o_ref[blk(u["qi"]), :] = jnp.concatenate([outs[0], outs[1]], axis=0).T.astype(BF16)

    group = 2
    stages = [[(qi, h) for qi in range(g * group, (g + 1) * group) for h in range(2)] for g in range(nblk // group)]
    prev = []
    for g, members in enumerate(stages + [[]]):
        cur = [prepare(qi, h, (g % 2) * 2 * group + 2 * (qi % group) + h) for qi, h in members]
        for u in prev:
            prob_exp(u)
        for u in cur:
            score_matmul(u)
        for u in prev:
            prob_matmul(u)
        for u in cur:
            score_reduce(u)
        for u in prev:
            finish(u)
        prev = cur


def _moba(qk, vt, bias, c31, batch, seq):
    n = qk.shape[0]
    npair = A_HEADS // 2
    kern = functools.partial(_moba_kernel, seq=seq)
    return pl.pallas_call(
        kern,
        grid=(batch, npair),
        in_specs=[pl.BlockSpec(memory_space=pltpu.SMEM),
                  pl.BlockSpec((seq, LANES), lambda b, p: (b, p)),
                  pl.BlockSpec((seq, LANES), lambda b, p: (b, npair + p)),
                  pl.BlockSpec((LANES, seq), lambda b, p: (p, b)),
                  pl.BlockSpec((2, 2 * MOBA_BLOCK, MOBA_BLOCK), lambda b, p: (p, 0, 0))],
        out_specs=pl.BlockSpec((seq, LANES), lambda b, p: (b, p)),
        out_shape=jax.ShapeDtypeStruct((n, A_WIDTH), BF16),
        scratch_shapes=[pltpu.VMEM((2, seq, LANES), BF16),
                        pltpu.VMEM((8, seq, MOBA_BLOCK), F32)],
        compiler_params=pltpu.CompilerParams(dimension_semantics=("parallel", "arbitrary"),
                                             vmem_limit_bytes=VMEM_LIMIT),
        name="moba",
    )(c31, qk, qk, vt, bias)


_GLA_LEVELS = int(math.log2(GLA_CHUNK))


_GLA_BIG_LEVELS = _GLA_LEVELS - 3


def _gla_weights():
    c = GLA_CHUNK
    i = np.arange(c)[:, None]
    j = np.arange(c)[None, :]
    mats = [(j <= i)]
    for lvl in range(_GLA_BIG_LEVELS, _GLA_LEVELS):
        s = (c // 2) >> lvl
        ref = (i // (2 * s)) * (2 * s) + s - 1
        mats.append(np.where(i > ref, (j > ref) & (j <= i), (j > i) & (j <= ref)))
    return np.concatenate(mats, axis=0).astype(np.float32)


def _gla_level_map():
    c = GLA_CHUNK
    i = np.arange(c)[:, None]
    j = np.arange(c)[None, :]
    top_bit = np.floor(np.log2(np.maximum(i ^ j, 1))).astype(np.int64)
    lvl = np.where(j < i, _GLA_LEVELS - 1 - top_bit, np.where(j == i, _GLA_LEVELS, _GLA_LEVELS + 1))
    return np.concatenate([lvl, lvl], axis=1).astype(np.int32)


def _gla_kernel(q_ref, k_ref, la_ref, v_ref, rg_ref, w_ref, lmap_ref, gn_ref, o_ref, st_ref):
    c = GLA_CHUNK
    half = LANES // 2
    npair = G_HEADS // 2

    @pl.when(pl.program_id(1) == 0)
    def _init():
        st_ref[...] = jnp.zeros(st_ref.shape, F32)

    w = w_ref[...]
    lmap = lmap_ref[...]
    lane = lax.broadcasted_iota(jnp.int32, (1, LANES), 1)
    lane_c = lax.broadcasted_iota(jnp.int32, (c, LANES), 1)
    lm = [jnp.where(lane_c < half, 1.0, 0.0).astype(BF16), jnp.where(lane_c >= half, 1.0, 0.0).astype(BF16)]
    row_small = lax.broadcasted_iota(jnp.int32, (c, G_KEY_WIDTH), 0)

    def level_factors(q, k, b, d_all, lvl):
        s = (c // 2) >> lvl
        if s < 8:
            j = lvl - _GLA_BIG_LEVELS
            e = jnp.exp2(d_all[c * (j + 1):c * (j + 2)])
            odd = (row_small // s) % 2 == 1
            return jnp.where(odd, q * e, 0.0), jnp.where(odd, 0.0, k * e)
        zeros = jnp.zeros((s, q.shape[1]), F32)
        qp, kp = [], []
        for blk in range(c // s):
            rows = slice(blk * s, (blk + 1) * s)
            ref = (blk // 2) * 2 * s + s - 1
            if blk % 2 == 1:
                qp.append(q[rows] * jnp.exp2(b[rows] - b[ref:ref + 1]))
                kp.append(zeros)
            else:
                qp.append(zeros)
                kp.append(k[rows] * jnp.exp2(b[ref:ref + 1] - b[rows]))
        return jnp.concatenate(qp, axis=0), jnp.concatenate(kp, axis=0)

    nch = GLA_STEP // c
    rows_of = [slice(ch * c, (ch + 1) * c) for ch in range(nch)]
    pair_lanes = [slice(LANES * p, LANES * (p + 1)) for p in range(npair)]
    d_alls = []
    for ch in range(nch):
        la = la_ref[rows_of[ch], :] * LOG2E
        hi = la.astype(BF16)
        lo = (la - hi.astype(F32)).astype(BF16)
        d_alls.append(_dot(w, hi) + _dot(w, lo))
    chunks = []
    for ch in range(nch):
        d_all = d_alls[ch]
        q = q_ref[rows_of[ch], :]
        k = k_ref[rows_of[ch], :]
        b = d_all[0:c]
        blast = b[c - 1:c, :]
        qts, kts = [], []
        for lvl in range(_GLA_LEVELS):
            qt, kt = level_factors(q, k, b, d_all, lvl)
            qts.append(qt.astype(BF16))
            kts.append(kt.astype(BF16))
        qts.append(q.astype(BF16))
        kts.append(k.astype(BF16))
        chunks.append(dict(qts=qts, kts=kts, qe=(q * jnp.exp2(b)).astype(BF16),
                           khat=(k * jnp.exp2(blast - b)).astype(BF16), dec=jnp.exp2(blast),
                           a=[jnp.zeros((c, 2 * c), F32)] * npair))
    for lvl in range(_GLA_LEVELS + 1):
        mask = lmap == lvl
        s = (c // 2) >> lvl if lvl < _GLA_BIG_LEVELS else c
        blocks = [slice(i * s, (i + 1) * s) for i in range(c // s)]
        live = [i for i in range(len(blocks)) if i % 2 == 1 or s == c]
        for u in chunks:
            for p, sl in enumerate(pair_lanes):
                kp = u["kts"][lvl][:, sl]
                ql = u["qts"][lvl]
                pm = _nt(jnp.concatenate([ql[blocks[i], sl] for i in live], axis=0),
                         jnp.concatenate([kp * lm[0], kp * lm[1]], axis=0))
                a = u["a"][p]
                pieces = [a[r] for r in blocks]
                for n, i in enumerate(live):
                    pieces[i] = jnp.where(mask[blocks[i]], pm[n * s:(n + 1) * s], pieces[i])
                u["a"] = [jnp.concatenate(pieces, axis=0) if j == p else v for j, v in enumerate(u["a"])]
    states = [st_ref[p] for p in range(npair)]
    for ch, u in enumerate(chunks):
        rows = rows_of[ch]
        for p, sl in enumerate(pair_lanes):
            stb = states[p].astype(BF16)
            ups = []
            for hh in range(2):
                h = 2 * p + hh
                vh = v_ref[rows, LANES * h:LANES * (h + 1)]
                o = _dot(u["a"][p][:, hh * c:(hh + 1) * c].astype(BF16), vh) + _nt(u["qe"][:, sl] * lm[hh], stb)
                y = _rms(o, gn_ref[...])
                rg = rg_ref[rows, LANES * h:LANES * (h + 1)]
                y = y * (rg / (1.0 + jnp.exp(-rg)))
                o_ref[rows, LANES * h:LANES * (h + 1)] = y.astype(BF16)
                ups.append(_tn(vh, u["khat"][:, sl]))
            states[p] = states[p] * u["dec"][:, sl] + jnp.where(lane < half, ups[0], ups[1])
    for p in range(npair):
        st_ref[p] = states[p]


def _gla(qg, kg, la, vg, rg, gn, batch, seq):
    n = qg.shape[0]
    nchunk = seq // GLA_STEP
    w = jnp.asarray(_gla_weights(), BF16)
    lmap = jnp.asarray(_gla_level_map())
    row = lambda width: pl.BlockSpec((GLA_STEP, width), lambda b, c: (b * nchunk + c, 0))
    const = functools.partial(pl.BlockSpec, pipeline_mode=pl.Buffered(1))
    return pl.pallas_call(
        _gla_kernel,
        grid=(batch, nchunk),
        in_specs=[row(G_KEY_WIDTH), row(G_KEY_WIDTH), row(G_KEY_WIDTH), row(G_WIDTH), row(G_WIDTH),
                  const(w.shape, lambda b, c: (0, 0)),
                  const(lmap.shape, lambda b, c: (0, 0)),
                  const((1, G_HEAD_V), lambda b, c: (0, 0))],
        out_specs=row(G_WIDTH),
        out_shape=jax.ShapeDtypeStruct((n, G_WIDTH), BF16),
        scratch_shapes=[pltpu.VMEM((G_HEADS // 2, G_HEAD_V, LANES), F32)],
        compiler_params=pltpu.CompilerParams(dimension_semantics=("parallel", "arbitrary"),
                                             vmem_limit_bytes=VMEM_LIMIT),
        name="gla",
    )(qg, kg, la, vg, rg, w, lmap, gn)


def _memkv_kernel(m_ref, g_ref, w_ref, k_ref, v_ref):
    nb, mlen, _ = m_ref.shape
    h = _rms(m_ref[...].reshape(nb * mlen, D_MODEL), g_ref[...]).astype(BF16)
    k_ref[...] = _dot(h, w_ref[:, 0:D_MODEL]).astype(BF16).reshape(nb, mlen, D_MODEL)
    v_ref[...] = _dot(h, w_ref[:, D_MODEL:2 * D_MODEL]).astype(BF16).reshape(nb, mlen, D_MODEL)


def _memkv(mem, g, w):
    batch, mlen, _ = mem.shape
    nb = 2 if batch % 2 == 0 else 1
    const = functools.partial(pl.BlockSpec, pipeline_mode=pl.Buffered(1))
    blk = pl.BlockSpec((nb, mlen, D_MODEL), lambda b: (b, 0, 0))
    return pl.pallas_call(
        _memkv_kernel,
        grid=(batch // nb,),
        in_specs=[blk, const((1, D_MODEL), lambda b: (0, 0)), const((D_MODEL, 2 * D_MODEL), lambda b: (0, 0))],
        out_specs=[blk, blk],
        out_shape=[jax.ShapeDtypeStruct(mem.shape, BF16)] * 2,
        compiler_params=pltpu.CompilerParams(dimension_semantics=("parallel",), vmem_limit_bytes=VMEM_LIMIT),
        name="memkv",
    )(mem, g, w)


def _post_kernel(x_ref, oa_ref, og_ref, wo_ref, gx_ref, wq_ref, kx_ref, vx_ref, wxo_ref, o_ref):
    tm = x_ref.shape[0]
    halves = [slice(i * tm // POST_PARTS, (i + 1) * tm // POST_PARTS) for i in range(POST_PARTS)]
    x1 = [x_ref[r, :] + _dot(oa_ref[r, :], wo_ref[0:A_WIDTH, :]) + _dot(og_ref[r, :], wo_ref[A_WIDTH:D_MODEL, :])
          for r in halves]
    h = [_rms(v, gx_ref[...]).astype(BF16) for v in x1]
    qx = [(_dot(v, wq_ref[...]) * (LOG2E * X_HEAD_DIM ** -0.5)).astype(BF16) for v in h]
    outs = [[] for _ in halves]
    for hd in range(X_HEADS):
        sl = slice(X_HEAD_DIM * hd, X_HEAD_DIM * (hd + 1))
        s = [_nt(q[:, sl], kx_ref[0, :, sl]) for q in qx]
        pexp = [jnp.exp2(v - jnp.max(v, axis=1, keepdims=True)) for v in s]
        pn = [(v / jnp.sum(v, axis=1, keepdims=True)).astype(BF16) for v in pexp]
        for i, v in enumerate(pn):
            outs[i].append(_dot(v, vx_ref[0, :, sl]).astype(BF16))
    for i, r in enumerate(halves):
        o_ref[r, :] = x1[i] + _dot(jnp.concatenate(outs[i], axis=1), wxo_ref[...])


def _post(x2, oa, og, wo, gx, wq, kx, vx, wxo, seq):
    n = x2.shape[0]
    tm = POST_TILE
    per_b = seq // tm
    mlen = kx.shape[1]
    const = functools.partial(pl.BlockSpec, pipeline_mode=pl.Buffered(1))
    row = lambda width: pl.BlockSpec((tm, width), lambda i: (i, 0))
    mem = pl.BlockSpec((1, mlen, D_MODEL), lambda i: (i // per_b, 0, 0))
    sq = const((D_MODEL, D_MODEL), lambda i: (0, 0))
    return pl.pallas_call(
        _post_kernel,
        grid=(n // tm,),
        in_specs=[row(D_MODEL), row(A_WIDTH), row(G_WIDTH), sq, const((1, D_MODEL), lambda i: (0, 0)), sq, mem, mem, sq],
        out_specs=row(D_MODEL),
        out_shape=jax.ShapeDtypeStruct((n, D_MODEL), F32),
        compiler_params=pltpu.CompilerParams(dimension_semantics=("parallel",), vmem_limit_bytes=VMEM_LIMIT),
        name="post",
    )(x2, oa, og, wo, gx, wq, kx, vx, wxo)


def _mlp_kernel(x_ref, g_ref, wu_ref, wd_ref, gf_ref, o_ref, *, final_norm):
    tm = x_ref.shape[0]
    parts = [slice(i * tm // MLP_PARTS, (i + 1) * tm // MLP_PARTS) for i in range(MLP_PARTS)]
    acc = [x_ref[r, :] for r in parts]
    h = [_rms(x, g_ref[...]).astype(BF16) for x in acc]
    for c in range(D_FF // D_MODEL):
        sl = slice(D_MODEL * c, D_MODEL * (c + 1))
        u = [jnp.square(jnp.maximum(_dot(v, wu_ref[:, sl]), 0.0)).astype(BF16) for v in h]
        acc = [a + _dot(v, wd_ref[sl, :]) for a, v in zip(acc, u)]
    for r, a in zip(parts, acc):
        o_ref[r, :] = _rms(a, gf_ref[...]) if final_norm else a


def _mlp(x2, g, wu, wd, gf, final_norm):
    n = x2.shape[0]
    tm = MLP_TILE
    const = functools.partial(pl.BlockSpec, pipeline_mode=pl.Buffered(1))
    row = pl.BlockSpec((tm, D_MODEL), lambda i: (i, 0))
    vec = const((1, D_MODEL), lambda i: (0, 0))
    return pl.pallas_call(
        functools.partial(_mlp_kernel, final_norm=final_norm),
        grid=(n // tm,),
        in_specs=[row, vec, const((D_MODEL, D_FF), lambda i: (0, 0)), const((D_FF, D_MODEL), lambda i: (0, 0)), vec],
        out_specs=row,
        out_shape=jax.ShapeDtypeStruct((n, D_MODEL), F32),
        compiler_params=pltpu.CompilerParams(dimension_semantics=("parallel",), vmem_limit_bytes=VMEM_LIMIT),
        name="mlp",
    )(x2, g, wu, wd, gf)


def kernel(x, mem, rp_table, norm_mix, w_in, w_gate_up, b_gate, g_norm, w_out, norm_xattn, norm_mem, w_xq, w_xkv,
           w_xo, norm_mlp, w_up, w_down, norm_final):
    batch, seq, _ = x.shape
    depth = w_in.shape[0]
    assert seq % max(INPROJ_TILE, POST_TILE, MLP_TILE, GLA_STEP) == 0 and seq // MOBA_BLOCK <= 8
    x2 = x.reshape(batch * seq, D_MODEL)
    bias = _bias_tiles(rp_table)
    c31 = rp_table[RP_BUCKETS - 1]
    va_lo = 2 * A_WIDTH
    glr_lo = 3 * A_WIDTH + 2 * G_KEY_WIDTH + G_WIDTH
    for l in range(depth):
        wl = w_in[l]
        w = jnp.concatenate([wl[:, :va_lo], wl[:, va_lo + A_WIDTH:glr_lo], wl[:, glr_lo + G_GATE_RANK:],
                             wl[:, glr_lo:glr_lo + G_GATE_RANK],
                             jnp.zeros((D_MODEL, LANES - G_GATE_RANK), F32)], axis=1).astype(BF16)
        wvt = wl[:, va_lo:va_lo + A_WIDTH].T.astype(BF16)
        wgu = jnp.concatenate([w_gate_up[l], jnp.zeros((LANES - G_GATE_RANK, G_KEY_WIDTH), F32)], axis=0).astype(BF16)
        qk, vt, qg, kg, la, vg, rg = _inproj(x2, norm_mix[l][None], w, wvt, wgu, b_gate[l][None])
        oa = _moba(qk, vt, bias, c31, batch, seq)
        og = _gla(qg, kg, la, vg, rg, g_norm[l][None], batch, seq)
        kx, vx = _memkv(mem, norm_mem[l][None], w_xkv[l].astype(BF16))
        x2 = _post(x2, oa, og, w_out[l].astype(BF16), norm_xattn[l][None], w_xq[l].astype(BF16), kx, vx,
                   w_xo[l].astype(BF16), seq)
        last = l == depth - 1
        x2 = _mlp(x2, norm_mlp[l][None], w_up[l].astype(BF16), w_down[l].astype(BF16), norm_final[None], last)
    return x2.reshape(batch, seq, D_MODEL)
```

```python
import functools
import math

import numpy as np
import jax
import jax.numpy as jnp
from jax import lax
from jax.experimental import pallas as pl
from jax.experimental.pallas import tpu as pltpu

F32 = jnp.float32
BF16 = jnp.bfloat16

D_MODEL = 1024
A_HEADS = 8
A_HEAD_DIM = 64
A_WIDTH = A_HEADS * A_HEAD_DIM
MOBA_BLOCK = 256
MOBA_TOPK = 3
G_HEADS = 4
G_WIDTH = D_MODEL - A_WIDTH
G_HEAD_V = G_WIDTH // G_HEADS
G_KEY_WIDTH = G_WIDTH // 2
G_HEAD_K = G_KEY_WIDTH // G_HEADS
G_GATE_RANK = 16
G_GATE_NORM = 16.0
X_HEADS = 4
X_HEAD_DIM = D_MODEL // X_HEADS
D_FF = 4 * D_MODEL
RP_BUCKETS = 32
RP_MAX_DIST = 128
EPS = 1e-6

LANES = 128
NEG = -1e30
GLA_CHUNK = 128
GLA_STEP = 1024
LOG2E = float(np.log2(np.e))
INPROJ_TILE = 1024
INPROJ_PARTS = 2
MLP_TILE = 1024
MLP_PARTS = 2
POST_TILE = 1024
POST_PARTS = 2
VMEM_LIMIT = 56 * 1024 * 1024

_C_QK = 2 * A_WIDTH
_C_QG = _C_QK
_C_KG = _C_QG + G_KEY_WIDTH
_C_VG = _C_KG + G_KEY_WIDTH
_C_RG = _C_VG + G_WIDTH
_C_GLR = _C_RG + G_WIDTH
_C_END = _C_GLR + LANES


def _nt(a, b):
    return lax.dot_general(a, b, (((1,), (1,)), ((), ())), preferred_element_type=F32)


def _tn(a, b):
    return lax.dot_general(a, b, (((0,), (0,)), ((), ())), preferred_element_type=F32)


def _dot(a, b):
    return jnp.dot(a, b, preferred_element_type=F32)


def _rms(x, g):
    return x * lax.rsqrt(jnp.mean(x * x, axis=-1, keepdims=True) + EPS) * g


def _bucket_thresholds():
    max_exact = RP_BUCKETS // 2
    d = np.arange(1, 4 * RP_MAX_DIST)
    val = (np.log(d.astype(np.float32) / np.float32(max_exact)) / np.float32(math.log(RP_MAX_DIST / max_exact))
           * np.float32(RP_BUCKETS - max_exact))
    bucket = np.minimum(max_exact + val.astype(np.int32), RP_BUCKETS - 1)
    return [int(d[(d >= max_exact) & (bucket >= max_exact + k)][0]) for k in range(1, RP_BUCKETS - max_exact)]


_THRESHOLDS = _bucket_thresholds()


def _bias_kernel(tab_ref, out_ref):
    h = pl.program_id(0)
    shape = (2 * MOBA_BLOCK, MOBA_BLOCK)
    d = lax.broadcasted_iota(jnp.int32, shape, 1) - lax.broadcasted_iota(jnp.int32, shape, 0) + MOBA_BLOCK
    max_exact = RP_BUCKETS // 2
    bucket = jnp.where(d < max_exact, d, max_exact)
    for t in _THRESHOLDS:
        bucket = bucket + jnp.where(d >= t, 1, 0)
    m = jnp.full(shape, NEG, F32)
    for b in range(RP_BUCKETS):
        m = jnp.where(bucket == b, tab_ref[b, h] * LOG2E, m)
    out_ref[0] = m


def _bias_tiles(rp_table):
    return pl.pallas_call(
        _bias_kernel,
        grid=(A_HEADS,),
        in_specs=[pl.BlockSpec(memory_space=pltpu.SMEM)],
        out_specs=pl.BlockSpec((1, 2 * MOBA_BLOCK, MOBA_BLOCK), lambda h: (h, 0, 0)),
        out_shape=jax.ShapeDtypeStruct((A_HEADS, 2 * MOBA_BLOCK, MOBA_BLOCK), F32),
        name="bias",
    )(rp_table)


def _inproj_kernel(x_ref, g_ref, w_ref, wvt_ref, wgu_ref, bg_ref, qk_ref, vt_ref, qg_ref, kg_ref, la_ref, vg_ref,
                   rg_ref):
    tm = x_ref.shape[0]
    parts = [slice(i * tm // INPROJ_PARTS, (i + 1) * tm // INPROJ_PARTS) for i in range(INPROJ_PARTS)]
    hs = [_rms(x_ref[r, :], g_ref[...]).astype(BF16) for r in parts]

    def proj(h, lo, hi):
        return _dot(h, w_ref[:, lo:hi])

    glrs = [proj(h, _C_GLR, _C_END).astype(BF16) for h in hs]
    for r, h in zip(parts, hs):
        qk_ref[r, 0:A_WIDTH] = (proj(h, 0, A_WIDTH) * (LOG2E * A_HEAD_DIM ** -0.5)).astype(BF16)
    for r, h in zip(parts, hs):
        qk_ref[r, A_WIDTH:_C_QK] = proj(h, A_WIDTH, _C_QK).astype(BF16)
    for r, glr in zip(parts, glrs):
        z = _dot(glr, wgu_ref[...]) + bg_ref[...]
        log_sig = jnp.minimum(z, 0.0) - jnp.log(1.0 + jnp.exp(-jnp.abs(z)))
        la_ref[r, :] = log_sig * (1.0 / G_GATE_NORM)
    for r, h in zip(parts, hs):
        vt_ref[:, r] = _dot(h, wvt_ref[...]).astype(BF16).T
    for r, h in zip(parts, hs):
        qg_ref[r, :] = proj(h, _C_QG, _C_KG) * (G_HEAD_K ** -0.5)
        kg_ref[r, :] = proj(h, _C_KG, _C_VG)
    for r, h in zip(parts, hs):
        vg_ref[r, :] = proj(h, _C_VG, _C_RG).astype(BF16)
    for r, h in zip(parts, hs):
        rg_ref[r, :] = proj(h, _C_RG, _C_GLR)


def _inproj(x2, g, w, wvt, wgu, bg):
    n = x2.shape[0]
    tm = INPROJ_TILE
    const = functools.partial(pl.BlockSpec, pipeline_mode=pl.Buffered(1))
    row = lambda width: pl.BlockSpec((tm, width), lambda i: (i, 0))
    return pl.pallas_call(
        _inproj_kernel,
        grid=(n // tm,),
        in_specs=[row(D_MODEL),
                  const((1, D_MODEL), lambda i: (0, 0)),
                  const((D_MODEL, _C_END), lambda i: (0, 0)),
                  const((D_MODEL, A_WIDTH), lambda i: (0, 0)),
                  const((LANES, G_KEY_WIDTH), lambda i: (0, 0)),
                  const((1, G_KEY_WIDTH), lambda i: (0, 0))],
        out_specs=[row(_C_QK), pl.BlockSpec((A_WIDTH, tm), lambda i: (0, i)),
                   row(G_KEY_WIDTH), row(G_KEY_WIDTH), row(G_KEY_WIDTH), row(G_WIDTH), row(G_WIDTH)],
        out_shape=[jax.ShapeDtypeStruct((n, _C_QK), BF16),
                   jax.ShapeDtypeStruct((A_WIDTH, n), BF16),
                   jax.ShapeDtypeStruct((n, G_KEY_WIDTH), F32),
                   jax.ShapeDtypeStruct((n, G_KEY_WIDTH), F32),
                   jax.ShapeDtypeStruct((n, G_KEY_WIDTH), F32),
                   jax.ShapeDtypeStruct((n, G_WIDTH), BF16),
                   jax.ShapeDtypeStruct((n, G_WIDTH), F32)],
        compiler_params=pltpu.CompilerParams(dimension_semantics=("parallel",), vmem_limit_bytes=VMEM_LIMIT),
        name="inproj",
    )(x2, g, w, wvt, wgu, bg)


def _moba_kernel(c31_ref, q_ref, k_ref, vt_ref, bias_ref, o_ref, kext, s_scr, *, seq):
    p = pl.program_id(1)
    nblk = seq // MOBA_BLOCK
    half = LANES // 2

    k = k_ref[...]
    rowblk = lax.broadcasted_iota(jnp.int32, (seq, LANES), 0) // MOBA_BLOCK
    lane2 = lax.broadcasted_iota(jnp.int32, (seq, LANES), 1)
    lm0 = jnp.where(lane2 < half, 1.0, 0.0).astype(BF16)
    lm1 = jnp.where(lane2 >= half, 1.0, 0.0).astype(BF16)
    kext[0] = k * lm0 + jnp.where(lane2 - half == rowblk, 1.0, 0.0).astype(BF16)
    kext[1] = k * lm1 + jnp.where(lane2 == rowblk, 1.0, 0.0).astype(BF16)
    n_i = lax.broadcasted_iota(jnp.int32, (16, seq), 0)
    t_i = lax.broadcasted_iota(jnp.int32, (16, seq), 1)
    avg = jnp.where(t_i // MOBA_BLOCK == n_i, 1.0 / MOBA_BLOCK, 0.0).astype(BF16)
    km = _dot(avg, k)
    kmh = km.astype(BF16)
    kml = (km - kmh.astype(F32)).astype(BF16)
    r_i = lax.broadcasted_iota(jnp.int32, (MOBA_BLOCK, MOBA_BLOCK), 0)
    c_i = lax.broadcasted_iota(jnp.int32, (MOBA_BLOCK, MOBA_BLOCK), 1)
    ident = jnp.where(r_i == c_i, 1.0, 0.0).astype(BF16)

    lane_q = lax.broadcasted_iota(jnp.int32, (MOBA_BLOCK, LANES), 1)
    rowi = lax.broadcasted_iota(jnp.int32, (16, MOBA_BLOCK), 0)
    blk = lambda n: slice(n * MOBA_BLOCK, (n + 1) * MOBA_BLOCK)

    def prepare(qi, h, slot):
        qm = q_ref[blk(qi), :] * jnp.where((lane_q < half) if h == 0 else (lane_q >= half), 1.0, 0.0).astype(BF16)
        qext = qm
        if qi > MOBA_TOPK:
            g = _nt(kmh, qm) + _nt(kml, qm)
            cnt = jnp.zeros((16, MOBA_BLOCK), F32)
            for m in range(qi):
                gm = g[m:m + 1, :]
                cnt = cnt + jnp.where((gm > g) | ((gm == g) & (m < rowi)), 1.0, 0.0)
            keep = ((rowi < qi) & (cnt < MOBA_TOPK)) | (rowi >= qi)
            pen_t = jnp.where(keep, 0.0, NEG).astype(BF16)
            off = half if h == 0 else 0
            parts = ([jnp.zeros((off, MOBA_BLOCK), BF16)] if off else []) + [
                pen_t, jnp.zeros((LANES - 16 - off, MOBA_BLOCK), BF16)]
            pen_q = _nt(ident, jnp.concatenate(parts, axis=0))
            qext = qm + pen_q.astype(BF16)
        return dict(qi=qi, h=h, qext=qext, c31=c31_ref[2 * p + h] * LOG2E, sbuf=s_scr.at[slot], near=None,
                    far=None, acc=None)

    def score_matmul(u):
        u["s_all"] = _nt(kext[u["h"], 0:(u["qi"] + 1) * MOBA_BLOCK, :], u["qext"])

    def score_reduce(u):
        qi, h = u["qi"], u["h"]
        for n in range(qi + 1):
            s = u["s_all"][blk(n)]
            if n == qi:
                s = s + bias_ref[h, MOBA_BLOCK:2 * MOBA_BLOCK, :]
            elif n == qi - 1:
                s = s + bias_ref[h, 0:MOBA_BLOCK, :]
            u["sbuf"][blk(n), :] = s
            t = jnp.max(s.reshape(MOBA_BLOCK // 8, 8, MOBA_BLOCK), axis=0)
            key = "far" if n < qi - 1 else "near"
            u[key] = t if u[key] is None else jnp.maximum(u[key], t)
        mx = u["near"] if u["far"] is None else jnp.maximum(u["near"], u["far"] + u["c31"])
        u["m_near"] = jnp.max(mx, axis=0, keepdims=True)
        u["m_far"] = u["m_near"] - u["c31"]

    def prob_exp(u):
        qi = u["qi"]
        u["pt"] = jnp.concatenate(
            [jnp.exp2(u["sbuf"][blk(n), :] - (u["m_far"] if n < qi - 1 else u["m_near"])).astype(BF16)
             for n in range(qi + 1)], axis=0)

    def prob_matmul(u):
        qi, h = u["qi"], u["h"]
        width = (qi + 1) * MOBA_BLOCK
        vrows = vt_ref[half * h:half * (h + 1), 0:width]
        ones_rows = jnp.ones((half, width), BF16)
        vt = jnp.concatenate([vrows, ones_rows] if h == 0 else [ones_rows, vrows], axis=0)
        u["acc"] = _dot(vt, u["pt"])

    outs = {}

    def finish(u):
        acc, h = u["acc"], u["h"]
        num = acc[half * h:half * (h + 1)]
        den = acc[half * (1 - h):half * (1 - h) + 1]
        outs[h] = num / den
        if h == 1:
            o_ref[blk(u["qi"]), :] = jnp.concatenate([outs[0], outs[1]], axis=0).T.astype(BF16)

    group = 2
    stages = [[(qi, h) for qi in range(g * group, (g + 1) * group) for h in range(2)] for g in range(nblk // group)]
    prev = []
    for g, members in enumerate(stages + [[]]):
        cur = [prepare(qi, h, (g % 2) * 2 * group + 2 * (qi % group) + h) for qi, h in members]
        for u in prev:
            prob_exp(u)
        for u in cur:
            score_matmul(u)
        for u in prev:
            prob_matmul(u)
        for u in cur:
            score_reduce(u)
        for u in prev:
            finish(u)
        prev = cur


def _moba(qk, vt, bias, c31, batch, seq):
    n = qk.shape[0]
    npair = A_HEADS // 2
    kern = functools.partial(_moba_kernel, seq=seq)
    return pl.pallas_call(
        kern,
        grid=(batch, npair),
        in_specs=[pl.BlockSpec(memory_space=pltpu.SMEM),
                  pl.BlockSpec((seq, LANES), lambda b, p: (b, p)),
                  pl.BlockSpec((seq, LANES), lambda b, p: (b, npair + p)),
                  pl.BlockSpec((LANES, seq), lambda b, p: (p, b)),
                  pl.BlockSpec((2, 2 * MOBA_BLOCK, MOBA_BLOCK), lambda b, p: (p, 0, 0))],
        out_specs=pl.BlockSpec((seq, LANES), lambda b, p: (b, p)),
        out_shape=jax.ShapeDtypeStruct((n, A_WIDTH), BF16),
        scratch_shapes=[pltpu.VMEM((2, seq, LANES), BF16),
                        pltpu.VMEM((8, seq, MOBA_BLOCK), F32)],
        compiler_params=pltpu.CompilerParams(dimension_semantics=("parallel", "arbitrary"),
                                             vmem_limit_bytes=VMEM_LIMIT),
        name="moba",
    )(c31, qk, qk, vt, bias)


_GLA_LEVELS = int(math.log2(GLA_CHUNK))


_GLA_BIG_LEVELS = _GLA_LEVELS - 3


def _gla_weights():
    c = GLA_CHUNK
    i = np.arange(c)[:, None]
    j = np.arange(c)[None, :]
    mats = [(j <= i)]
    for lvl in range(_GLA_BIG_LEVELS, _GLA_LEVELS):
        s = (c // 2) >> lvl
        ref = (i // (2 * s)) * (2 * s) + s - 1
        mats.append(np.where(i > ref, (j > ref) & (j <= i), (j > i) & (j <= ref)))
    return np.concatenate(mats, axis=0).astype(np.float32)


def _gla_level_map():
    c = GLA_CHUNK
    i = np.arange(c)[:, None]
    j = np.arange(c)[None, :]
    top_bit = np.floor(np.log2(np.maximum(i ^ j, 1))).astype(np.int64)
    lvl = np.where(j < i, _GLA_LEVELS - 1 - top_bit, np.where(j == i, _GLA_LEVELS, _GLA_LEVELS + 1))
    return np.concatenate([lvl, lvl], axis=1).astype(np.int32)


def _gla_kernel(q_ref, k_ref, la_ref, v_ref, rg_ref, w_ref, lmap_ref, gn_ref, o_ref, st_ref):
    c = GLA_CHUNK
    half = LANES // 2
    npair = G_HEADS // 2

    @pl.when(pl.program_id(1) == 0)
    def _init():
        st_ref[...] = jnp.zeros(st_ref.shape, F32)

    w = w_ref[...]
    lmap = lmap_ref[...]
    lane = lax.broadcasted_iota(jnp.int32, (1, LANES), 1)
    lane_c = lax.broadcasted_iota(jnp.int32, (c, LANES), 1)
    lm = [jnp.where(lane_c < half, 1.0, 0.0).astype(BF16), jnp.where(lane_c >= half, 1.0, 0.0).astype(BF16)]
    row_small = lax.broadcasted_iota(jnp.int32, (c, G_KEY_WIDTH), 0)

    def level_factors(q, k, b, d_all, lvl):
        s = (c // 2) >> lvl
        if s < 8:
            j = lvl - _GLA_BIG_LEVELS
            e = jnp.exp2(d_all[c * (j + 1):c * (j + 2)])
            odd = (row_small // s) % 2 == 1
            return jnp.where(odd, q * e, 0.0), jnp.where(odd, 0.0, k * e)
        zeros = jnp.zeros((s, q.shape[1]), F32)
        qp, kp = [], []
        for blk in range(c // s):
            rows = slice(blk * s, (blk + 1) * s)
            ref = (blk // 2) * 2 * s + s - 1
            if blk % 2 == 1:
                qp.append(q[rows] * jnp.exp2(b[rows] - b[ref:ref + 1]))
                kp.append(zeros)
            else:
                qp.append(zeros)
                kp.append(k[rows] * jnp.exp2(b[ref:ref + 1] - b[rows]))
        return jnp.concatenate(qp, axis=0), jnp.concatenate(kp, axis=0)

    nch = GLA_STEP // c
    rows_of = [slice(ch * c, (ch + 1) * c) for ch in range(nch)]
    pair_lanes = [slice(LANES * p, LANES * (p + 1)) for p in range(npair)]
    d_alls = []
    for ch in range(nch):
        la = la_ref[rows_of[ch], :] * LOG2E
        hi = la.astype(BF16)
        lo = (la - hi.astype(F32)).astype(BF16)
        d_alls.append(_dot(w, hi) + _dot(w, lo))
    chunks = []
    for ch in range(nch):
        d_all = d_alls[ch]
        q = q_ref[rows_of[ch], :]
        k = k_ref[rows_of[ch], :]
        b = d_all[0:c]
        blast = b[c - 1:c, :]
        qts, kts = [], []
        for lvl in range(_GLA_LEVELS):
            qt, kt = level_factors(q, k, b, d_all, lvl)
            qts.append(qt.astype(BF16))
            kts.append(kt.astype(BF16))
        qts.append(q.astype(BF16))
        kts.append(k.astype(BF16))
        chunks.append(dict(qts=qts, kts=kts, qe=(q * jnp.exp2(b)).astype(BF16),
                           khat=(k * jnp.exp2(blast - b)).astype(BF16), dec=jnp.exp2(blast),
                           a=[jnp.zeros((c, 2 * c), F32)] * npair))
    for lvl in range(_GLA_LEVELS + 1):
        mask = lmap == lvl
        s = (c // 2) >> lvl if lvl < _GLA_BIG_LEVELS else c
        blocks = [slice(i * s, (i + 1) * s) for i in range(c // s)]
        live = [i for i in range(len(blocks)) if i % 2 == 1 or s == c]
        for u in chunks:
            for p, sl in enumerate(pair_lanes):
                kp = u["kts"][lvl][:, sl]
                ql = u["qts"][lvl]
                pm = _nt(jnp.concatenate([ql[blocks[i], sl] for i in live], axis=0),
                         jnp.concatenate([kp * lm[0], kp * lm[1]], axis=0))
                a = u["a"][p]
                pieces = [a[r] for r in blocks]
                for n, i in enumerate(live):
                    pieces[i] = jnp.where(mask[blocks[i]], pm[n * s:(n + 1) * s], pieces[i])
                u["a"] = [jnp.concatenate(pieces, axis=0) if j == p else v for j, v in enumerate(u["a"])]
    states = [st_ref[p] for p in range(npair)]
    for ch, u in enumerate(chunks):
        rows = rows_of[ch]
        for p, sl in enumerate(pair_lanes):
            stb = states[p].astype(BF16)
            ups = []
            for hh in range(2):
                h = 2 * p + hh
                vh = v_ref[rows, LANES * h:LANES * (h + 1)]
                o = _dot(u["a"][p][:, hh * c:(hh + 1) * c].astype(BF16), vh) + _nt(u["qe"][:, sl] * lm[hh], stb)
                y = _rms(o, gn_ref[...])
                rg = rg_ref[rows, LANES * h:LANES * (h + 1)]
                y = y * (rg / (1.0 + jnp.exp(-rg)))
                o_ref[rows, LANES * h:LANES * (h + 1)] = y.astype(BF16)
                ups.append(_tn(vh, u["khat"][:, sl]))
            states[p] = states[p] * u["dec"][:, sl] + jnp.where(lane < half, ups[0], ups[1])
    for p in range(npair):
        st_ref[p] = states[p]


def _gla(qg, kg, la, vg, rg, gn, batch, seq):
    n = qg.shape[0]
    nchunk = seq // GLA_STEP
    w = jnp.asarray(_gla_weights(), BF16)
    lmap = jnp.asarray(_gla_level_map())
    row = lambda width: pl.BlockSpec((GLA_STEP, width), lambda b, c: (b * nchunk + c, 0))
    const = functools.partial(pl.BlockSpec, pipeline_mode=pl.Buffered(1))
    return pl.pallas_call(
        _gla_kernel,
        grid=(batch, nchunk),
        in_specs=[row(G_KEY_WIDTH), row(G_KEY_WIDTH), row(G_KEY_WIDTH), row(G_WIDTH), row(G_WIDTH),
                  const(w.shape, lambda b, c: (0, 0)),
                  const(lmap.shape, lambda b, c: (0, 0)),
                  const((1, G_HEAD_V), lambda b, c: (0, 0))],
        out_specs=row(G_WIDTH),
        out_shape=jax.ShapeDtypeStruct((n, G_WIDTH), BF16),
        scratch_shapes=[pltpu.VMEM((G_HEADS // 2, G_HEAD_V, LANES), F32)],
        compiler_params=pltpu.CompilerParams(dimension_semantics=("parallel", "arbitrary"),
                                             vmem_limit_bytes=VMEM_LIMIT),
        name="gla",
    )(qg, kg, la, vg, rg, w, lmap, gn)


def _memkv_kernel(m_ref, g_ref, w_ref, k_ref, v_ref):
    nb, mlen, _ = m_ref.shape
    h = _rms(m_ref[...].reshape(nb * mlen, D_MODEL), g_ref[...]).astype(BF16)
    k_ref[...] = _dot(h, w_ref[:, 0:D_MODEL]).astype(BF16).reshape(nb, mlen, D_MODEL)
    v_ref[...] = _dot(h, w_ref[:, D_MODEL:2 * D_MODEL]).astype(BF16).reshape(nb, mlen, D_MODEL)


def _memkv(mem, g, w):
    batch, mlen, _ = mem.shape
    nb = 2 if batch % 2 == 0 else 1
    const = functools.partial(pl.BlockSpec, pipeline_mode=pl.Buffered(1))
    blk = pl.BlockSpec((nb, mlen, D_MODEL), lambda b: (b, 0, 0))
    return pl.pallas_call(
        _memkv_kernel,
        grid=(batch // nb,),
        in_specs=[blk, const((1, D_MODEL), lambda b: (0, 0)), const((D_MODEL, 2 * D_MODEL), lambda b: (0, 0))],
        out_specs=[blk, blk],
        out_shape=[jax.ShapeDtypeStruct(mem.shape, BF16)] * 2,
        compiler_params=pltpu.CompilerParams(dimension_semantics=("parallel",), vmem_limit_bytes=VMEM_LIMIT),
        name="memkv",
    )(mem, g, w)


def _post_kernel(x_ref, oa_ref, og_ref, wo_ref, gx_ref, wq_ref, kx_ref, vx_ref, wxo_ref, o_ref):
    tm = x_ref.shape[0]
    halves = [slice(i * tm // POST_PARTS, (i + 1) * tm // POST_PARTS) for i in range(POST_PARTS)]
    x1 = [x_ref[r, :] + _dot(oa_ref[r, :], wo_ref[0:A_WIDTH, :]) + _dot(og_ref[r, :], wo_ref[A_WIDTH:D_MODEL, :])
          for r in halves]
    h = [_rms(v, gx_ref[...]).astype(BF16) for v in x1]
    qx = [(_dot(v, wq_ref[...]) * (LOG2E * X_HEAD_DIM ** -0.5)).astype(BF16) for v in h]
    outs = [[] for _ in halves]
    for hd in range(X_HEADS):
        sl = slice(X_HEAD_DIM * hd, X_HEAD_DIM * (hd + 1))
        s = [_nt(q[:, sl], kx_ref[0, :, sl]) for q in qx]
        pexp = [jnp.exp2(v - jnp.max(v, axis=1, keepdims=True)) for v in s]
        pn = [(v / jnp.sum(v, axis=1, keepdims=True)).astype(BF16) for v in pexp]
        for i, v in enumerate(pn):
            outs[i].append(_dot(v, vx_ref[0, :, sl]).astype(BF16))
    for i, r in enumerate(halves):
        o_ref[r, :] = x1[i] + _dot(jnp.concatenate(outs[i], axis=1), wxo_ref[...])


def _post(x2, oa, og, wo, gx, wq, kx, vx, wxo, seq):
    n = x2.shape[0]
    tm = POST_TILE
    per_b = seq // tm
    mlen = kx.shape[1]
    const = functools.partial(pl.BlockSpec, pipeline_mode=pl.Buffered(1))
    row = lambda width: pl.BlockSpec((tm, width), lambda i: (i, 0))
    mem = pl.BlockSpec((1, mlen, D_MODEL), lambda i: (i // per_b, 0, 0))
    sq = const((D_MODEL, D_MODEL), lambda i: (0, 0))
    return pl.pallas_call(
        _post_kernel,
        grid=(n // tm,),
        in_specs=[row(D_MODEL), row(A_WIDTH), row(G_WIDTH), sq, const((1, D_MODEL), lambda i: (0, 0)), sq, mem, mem, sq],
        out_specs=row(D_MODEL),
        out_shape=jax.ShapeDtypeStruct((n, D_MODEL), F32),
        compiler_params=pltpu.CompilerParams(dimension_semantics=("parallel",), vmem_limit_bytes=VMEM_LIMIT),
        name="post",
    )(x2, oa, og, wo, gx, wq, kx, vx, wxo)


def _mlp_kernel(x_ref, g_ref, wu_ref, wd_ref, gf_ref, o_ref, *, final_norm):
    tm = x_ref.shape[0]
    parts = [slice(i * tm // MLP_PARTS, (i + 1) * tm // MLP_PARTS) for i in range(MLP_PARTS)]
    acc = [x_ref[r, :] for r in parts]
    h = [_rms(x, g_ref[...]).astype(BF16) for x in acc]
    for c in range(D_FF // D_MODEL):
        sl = slice(D_MODEL * c, D_MODEL * (c + 1))
        u = [jnp.square(jnp.maximum(_dot(v, wu_ref[:, sl]), 0.0)).astype(BF16) for v in h]
        acc = [a + _dot(v, wd_ref[sl, :]) for a, v in zip(acc, u)]
    for r, a in zip(parts, acc):
        o_ref[r, :] = _rms(a, gf_ref[...]) if final_norm else a


def _mlp(x2, g, wu, wd, gf, final_norm):
    n = x2.shape[0]
    tm = MLP_TILE
    const = functools.partial(pl.BlockSpec, pipeline_mode=pl.Buffered(1))
    row = pl.BlockSpec((tm, D_MODEL), lambda i: (i, 0))
    vec = const((1, D_MODEL), lambda i: (0, 0))
    return pl.pallas_call(
        functools.partial(_mlp_kernel, final_norm=final_norm),
        grid=(n // tm,),
        in_specs=[row, vec, const((D_MODEL, D_FF), lambda i: (0, 0)), const((D_FF, D_MODEL), lambda i: (0, 0)), vec],
        out_specs=row,
        out_shape=jax.ShapeDtypeStruct((n, D_MODEL), F32),
        compiler_params=pltpu.CompilerParams(dimension_semantics=("parallel",), vmem_limit_bytes=VMEM_LIMIT),
        name="mlp",
    )(x2, g, wu, wd, gf)


def kernel(x, mem, rp_table, norm_mix, w_in, w_gate_up, b_gate, g_norm, w_out, norm_xattn, norm_mem, w_xq, w_xkv,
           w_xo, norm_mlp, w_up, w_down, norm_final):
    batch, seq, _ = x.shape
    depth = w_in.shape[0]
    assert seq % max(INPROJ_TILE, POST_TILE, MLP_TILE, GLA_STEP) == 0 and seq // MOBA_BLOCK <= 8
    x2 = x.reshape(batch * seq, D_MODEL)
    bias = _bias_tiles(rp_table)
    c31 = rp_table[RP_BUCKETS - 1]
    va_lo = 2 * A_WIDTH
    glr_lo = 3 * A_WIDTH + 2 * G_KEY_WIDTH + G_WIDTH
    for l in range(depth):
        wl = w_in[l]
        w = jnp.concatenate([wl[:, :va_lo], wl[:, va_lo + A_WIDTH:glr_lo], wl[:, glr_lo + G_GATE_RANK:],
                             wl[:, glr_lo:glr_lo + G_GATE_RANK],
                             jnp.zeros((D_MODEL, LANES - G_GATE_RANK), F32)], axis=1).astype(BF16)
        wvt = wl[:, va_lo:va_lo + A_WIDTH].astype(BF16)
        wgu = jnp.concatenate([w_gate_up[l], jnp.zeros((LANES - G_GATE_RANK, G_KEY_WIDTH), F32)], axis=0).astype(BF16)
        qk, vt, qg, kg, la, vg, rg = _inproj(x2, norm_mix[l][None], w, wvt, wgu, b_gate[l][None])
        oa = _moba(qk, vt, bias, c31, batch, seq)
        og = _gla(qg, kg, la, vg, rg, g_norm[l][None], batch, seq)
        kx, vx = _memkv(mem, norm_mem[l][None], w_xkv[l].astype(BF16))
        x2 = _post(x2, oa, og, w_out[l].astype(BF16), norm_xattn[l][None], w_xq[l].astype(BF16), kx, vx,
                   w_xo[l].astype(BF16), seq)
        last = l == depth - 1
        x2 = _mlp(x2, norm_mlp[l][None], w_up[l].astype(BF16), w_down[l].astype(BF16), norm_final[None], last)
    return x2.reshape(batch, seq, D_MODEL)
```

```python
import functools
import math

import numpy as np
import jax
import jax.numpy as jnp
from jax import lax
from jax.experimental import pallas as pl
from jax.experimental.pallas import tpu as pltpu

F32 = jnp.float32
BF16 = jnp.bfloat16

D_MODEL = 1024
A_HEADS = 8
A_HEAD_DIM = 64
A_WIDTH = A_HEADS * A_HEAD_DIM
MOBA_BLOCK = 256
MOBA_TOPK = 3
G_HEADS = 4
G_WIDTH = D_MODEL - A_WIDTH
G_HEAD_V = G_WIDTH // G_HEADS
G_KEY_WIDTH = G_WIDTH // 2
G_HEAD_K = G_KEY_WIDTH // G_HEADS
G_GATE_RANK = 16
G_GATE_NORM = 16.0
X_HEADS = 4
X_HEAD_DIM = D_MODEL // X_HEADS
D_FF = 4 * D_MODEL
RP_BUCKETS = 32
RP_MAX_DIST = 128
EPS = 1e-6

LANES = 128
NEG = -1e30
GLA_CHUNK = 128
GLA_STEP = 1024
LOG2E = float(np.log2(np.e))
INPROJ_TILE = 1024
INPROJ_PARTS = 2
MLP_TILE = 1024
MLP_PARTS = 2
POST_TILE = 1024
POST_PARTS = 2
VMEM_LIMIT = 56 * 1024 * 1024

def _nt(a, b):
    return lax.dot_general(a, b, (((1,), (1,)), ((), ())), preferred_element_type=F32)


def _tn(a, b):
    return lax.dot_general(a, b, (((0,), (0,)), ((), ())), preferred_element_type=F32)


def _dot(a, b):
    return jnp.dot(a, b, preferred_element_type=F32)


def _rms(x, g):
    return x * lax.rsqrt(jnp.mean(x * x, axis=-1, keepdims=True) + EPS) * g


def _bucket_thresholds():
    max_exact = RP_BUCKETS // 2
    d = np.arange(1, 4 * RP_MAX_DIST)
    val = (np.log(d.astype(np.float32) / np.float32(max_exact)) / np.float32(math.log(RP_MAX_DIST / max_exact))
           * np.float32(RP_BUCKETS - max_exact))
    bucket = np.minimum(max_exact + val.astype(np.int32), RP_BUCKETS - 1)
    return [int(d[(d >= max_exact) & (bucket >= max_exact + k)][0]) for k in range(1, RP_BUCKETS - max_exact)]


_THRESHOLDS = _bucket_thresholds()


def _bias_kernel(tab_ref, out_ref):
    h = pl.program_id(0)
    rows, width = 2 * MOBA_BLOCK, 3 * MOBA_BLOCK
    d = lax.broadcasted_iota(jnp.int32, (8, width), 1)
    max_exact = RP_BUCKETS // 2
    bucket = jnp.where(d < max_exact, d, max_exact)
    for t in _THRESHOLDS:
        bucket = bucket + jnp.where(d >= t, 1, 0)
    prof = jnp.full((8, width), NEG, F32)
    for b in range(RP_BUCKETS):
        prof = jnp.where(bucket == b, tab_ref[b, h] * LOG2E, prof)
    prof = jnp.where(d < rows, prof, NEG)
    tile = pltpu.roll(jnp.broadcast_to(prof[0:1], (rows, width)), 0, 1, stride=1, stride_axis=0)
    out_ref[0] = tile[:, MOBA_BLOCK:2 * MOBA_BLOCK]


def _bias_tiles(rp_table):
    return pl.pallas_call(
        _bias_kernel,
        grid=(A_HEADS,),
        in_specs=[pl.BlockSpec(memory_space=pltpu.SMEM)],
        out_specs=pl.BlockSpec((1, 2 * MOBA_BLOCK, MOBA_BLOCK), lambda h: (h, 0, 0)),
        out_shape=jax.ShapeDtypeStruct((A_HEADS, 2 * MOBA_BLOCK, MOBA_BLOCK), F32),
        name="bias",
    )(rp_table)


def _inproj_kernel(x_ref, g_ref, wqk_ref, wva_ref, wg_ref, wrg_ref, wglr_ref, wgu_ref, bg_ref,
                   qk_ref, vt_ref, qg_ref, kg_ref, la_ref, vg_ref, rg_ref):
    tm = x_ref.shape[0]
    parts = [slice(i * tm // INPROJ_PARTS, (i + 1) * tm // INPROJ_PARTS) for i in range(INPROJ_PARTS)]
    hs = [_rms(x_ref[r, :], g_ref[...]).astype(BF16) for r in parts]
    kg_lo, vg_lo = G_KEY_WIDTH, 2 * G_KEY_WIDTH

    glrs = [_dot(h, wglr_ref[...]).astype(BF16) for h in hs]
    for r, h in zip(parts, hs):
        qk_ref[r, 0:A_WIDTH] = (_dot(h, wqk_ref[:, 0:A_WIDTH]) * (LOG2E * A_HEAD_DIM ** -0.5)).astype(BF16)
    for r, h in zip(parts, hs):
        qk_ref[r, A_WIDTH:2 * A_WIDTH] = _dot(h, wqk_ref[:, A_WIDTH:2 * A_WIDTH]).astype(BF16)
    for r, glr in zip(parts, glrs):
        z = _dot(glr, wgu_ref[...]) + bg_ref[...]
        log_sig = jnp.minimum(z, 0.0) - jnp.log(1.0 + jnp.exp(-jnp.abs(z)))
        la_ref[r, :] = log_sig * (1.0 / G_GATE_NORM)
    for r, h in zip(parts, hs):
        vt_ref[:, r] = _dot(h, wva_ref[...]).astype(BF16).T
    for r, h in zip(parts, hs):
        qg_ref[r, :] = _dot(h, wg_ref[:, 0:kg_lo]) * (G_HEAD_K ** -0.5)
        kg_ref[r, :] = _dot(h, wg_ref[:, kg_lo:vg_lo])
    for r, h in zip(parts, hs):
        vg_ref[r, :] = _dot(h, wg_ref[:, vg_lo:vg_lo + G_WIDTH]).astype(BF16)
    for r, h in zip(parts, hs):
        rg_ref[r, :] = _dot(h, wrg_ref[...])


def _inproj(x2, g, wqk, wva, wg, wrg, wglr, wgu, bg):
    n = x2.shape[0]
    tm = INPROJ_TILE
    const = lambda a: pl.BlockSpec(a.shape, lambda i: (0, 0), pipeline_mode=pl.Buffered(1))
    row = lambda width: pl.BlockSpec((tm, width), lambda i: (i, 0))
    return pl.pallas_call(
        _inproj_kernel,
        grid=(n // tm,),
        in_specs=[row(D_MODEL)] + [const(a) for a in (g, wqk, wva, wg, wrg, wglr, wgu, bg)],
        out_specs=[row(2 * A_WIDTH), pl.BlockSpec((A_WIDTH, tm), lambda i: (0, i)),
                   row(G_KEY_WIDTH), row(G_KEY_WIDTH), row(G_KEY_WIDTH), row(G_WIDTH), row(G_WIDTH)],
        out_shape=[jax.ShapeDtypeStruct((n, 2 * A_WIDTH), BF16),
                   jax.ShapeDtypeStruct((A_WIDTH, n), BF16),
                   jax.ShapeDtypeStruct((n, G_KEY_WIDTH), F32),
                   jax.ShapeDtypeStruct((n, G_KEY_WIDTH), F32),
                   jax.ShapeDtypeStruct((n, G_KEY_WIDTH), F32),
                   jax.ShapeDtypeStruct((n, G_WIDTH), BF16),
                   jax.ShapeDtypeStruct((n, G_WIDTH), F32)],
        compiler_params=pltpu.CompilerParams(dimension_semantics=("parallel",), vmem_limit_bytes=VMEM_LIMIT),
        name="inproj",
    )(x2, g, wqk, wva, wg, wrg, wglr, wgu, bg)


def _moba_kernel(c31_ref, q_ref, k_ref, vt_ref, bias_ref, o_ref, kext, s_scr, *, seq):
    p = pl.program_id(1)
    nblk = seq // MOBA_BLOCK
    half = LANES // 2

    k = k_ref[...]
    rowblk = lax.broadcasted_iota(jnp.int32, (seq, LANES), 0) // MOBA_BLOCK
    lane2 = lax.broadcasted_iota(jnp.int32, (seq, LANES), 1)
    lm0 = jnp.where(lane2 < half, 1.0, 0.0).astype(BF16)
    lm1 = jnp.where(lane2 >= half, 1.0, 0.0).astype(BF16)
    kext[0] = k * lm0 + jnp.where(lane2 - half == rowblk, 1.0, 0.0).astype(BF16)
    kext[1] = k * lm1 + jnp.where(lane2 == rowblk, 1.0, 0.0).astype(BF16)
    n_i = lax.broadcasted_iota(jnp.int32, (16, seq), 0)
    t_i = lax.broadcasted_iota(jnp.int32, (16, seq), 1)
    avg = jnp.where(t_i // MOBA_BLOCK == n_i, 1.0 / MOBA_BLOCK, 0.0).astype(BF16)
    km = _dot(avg, k)
    kmh = km.astype(BF16)
    kml = (km - kmh.astype(F32)).astype(BF16)
    r_i = lax.broadcasted_iota(jnp.int32, (MOBA_BLOCK, MOBA_BLOCK), 0)
    c_i = lax.broadcasted_iota(jnp.int32, (MOBA_BLOCK, MOBA_BLOCK), 1)
    ident = jnp.where(r_i == c_i, 1.0, 0.0).astype(BF16)

    lane_q = lax.broadcasted_iota(jnp.int32, (MOBA_BLOCK, LANES), 1)
    rowi = lax.broadcasted_iota(jnp.int32, (16, MOBA_BLOCK), 0)
    blk = lambda n: slice(n * MOBA_BLOCK, (n + 1) * MOBA_BLOCK)

    def prepare(qi, h, slot):
        qm = q_ref[blk(qi), :] * jnp.where((lane_q < half) if h == 0 else (lane_q >= half), 1.0, 0.0).astype(BF16)
        qext = qm
        if qi > MOBA_TOPK:
            g = _nt(kmh, qm) + _nt(kml, qm)
            cnt = jnp.zeros((16, MOBA_BLOCK), F32)
            for m in range(qi):
                gm = g[m:m + 1, :]
                cnt = cnt + jnp.where((gm > g) | ((gm == g) & (m < rowi)), 1.0, 0.0)
            keep = ((rowi < qi) & (cnt < MOBA_TOPK)) | (rowi >= qi)
            pen_t = jnp.where(keep, 0.0, NEG).astype(BF16)
            off = half if h == 0 else 0
            parts = ([jnp.zeros((off, MOBA_BLOCK), BF16)] if off else []) + [
                pen_t, jnp.zeros((LANES - 16 - off, MOBA_BLOCK), BF16)]
            pen_q = _nt(ident, jnp.concatenate(parts, axis=0))
            qext = qm + pen_q.astype(BF16)
        return dict(qi=qi, h=h, qext=qext, c31=c31_ref[2 * p + h] * LOG2E, sbuf=s_scr.at[slot], near=None,
                    far=None, acc=None)

    def score_matmul(u):
        u["s_all"] = _nt(kext[u["h"], 0:(u["qi"] + 1) * MOBA_BLOCK, :], u["qext"])

    def score_reduce(u):
        qi, h = u["qi"], u["h"]
        for n in range(qi + 1):
            s = u["s_all"][blk(n)]
            if n == qi:
                s = s + bias_ref[h, MOBA_BLOCK:2 * MOBA_BLOCK, :]
            elif n == qi - 1:
                s = s + bias_ref[h, 0:MOBA_BLOCK, :]
            u["sbuf"][blk(n), :] = s
            t = jnp.max(s.reshape(MOBA_BLOCK // 8, 8, MOBA_BLOCK), axis=0)
            key = "far" if n < qi - 1 else "near"
            u[key] = t if u[key] is None else jnp.maximum(u[key], t)
        mx = u["near"] if u["far"] is None else jnp.maximum(u["near"], u["far"] + u["c31"])
        u["m_near"] = jnp.max(mx, axis=0, keepdims=True)
        u["m_far"] = u["m_near"] - u["c31"]

    def prob_exp(u):
        qi = u["qi"]
        u["pt"] = jnp.concatenate(
            [jnp.exp2(u["sbuf"][blk(n), :] - (u["m_far"] if n < qi - 1 else u["m_near"])).astype(BF16)
             for n in range(qi + 1)], axis=0)

    def prob_matmul(u):
        qi, h = u["qi"], u["h"]
        width = (qi + 1) * MOBA_BLOCK
        vrows = vt_ref[half * h:half * (h + 1), 0:width]
        ones_rows = jnp.ones((half, width), BF16)
        vt = jnp.concatenate([vrows, ones_rows] if h == 0 else [ones_rows, vrows], axis=0)
        u["acc"] = _dot(vt, u["pt"])

    outs = {}

    def finish(u):
        acc, h = u["acc"], u["h"]
        num = acc[half * h:half * (h + 1)]
        den = acc[half * (1 - h):half * (1 - h) + 1]
        outs[h] = num / den
        if h == 1:
            o_ref[blk(u["qi"]), :] = jnp.concatenate([outs[0], outs[1]], axis=0).T.astype(BF16)

    group = 2
    stages = [[(qi, h) for qi in range(g * group, (g + 1) * group) for h in range(2)] for g in range(nblk // group)]
    prev = []
    for g, members in enumerate(stages + [[]]):
        cur = [prepare(qi, h, (g % 2) * 2 * group + 2 * (qi % group) + h) for qi, h in members]
        for u in prev:
            prob_exp(u)
        for u in cur:
            score_matmul(u)
        for u in prev:
            prob_matmul(u)
        for u in cur:
            score_reduce(u)
        for u in prev:
            finish(u)
        prev = cur


def _moba(qk, vt, bias, c31, batch, seq):
    n = qk.shape[0]
    npair = A_HEADS // 2
    kern = functools.partial(_moba_kernel, seq=seq)
    return pl.pallas_call(
        kern,
        grid=(batch, npair),
        in_specs=[pl.BlockSpec(memory_space=pltpu.SMEM),
                  pl.BlockSpec((seq, LANES), lambda b, p: (b, p)),
                  pl.BlockSpec((seq, LANES), lambda b, p: (b, npair + p)),
                  pl.BlockSpec((LANES, seq), lambda b, p: (p, b)),
                  pl.BlockSpec((2, 2 * MOBA_BLOCK, MOBA_BLOCK), lambda b, p: (p, 0, 0))],
        out_specs=pl.BlockSpec((seq, LANES), lambda b, p: (b, p)),
        out_shape=jax.ShapeDtypeStruct((n, A_WIDTH), BF16),
        scratch_shapes=[pltpu.VMEM((2, seq, LANES), BF16),
                        pltpu.VMEM((8, seq, MOBA_BLOCK), F32)],
        compiler_params=pltpu.CompilerParams(dimension_semantics=("parallel", "arbitrary"),
                                             vmem_limit_bytes=VMEM_LIMIT),
        name="moba",
    )(c31, qk, qk, vt, bias)


_GLA_LEVELS = int(math.log2(GLA_CHUNK))


_GLA_BIG_LEVELS = _GLA_LEVELS - 3


def _gla_weights():
    c = GLA_CHUNK
    i = np.arange(c)[:, None]
    j = np.arange(c)[None, :]
    mats = [(j <= i)]
    for lvl in range(_GLA_BIG_LEVELS, _GLA_LEVELS):
        s = (c // 2) >> lvl
        ref = (i // (2 * s)) * (2 * s) + s - 1
        mats.append(np.where(i > ref, (j > ref) & (j <= i), (j > i) & (j <= ref)))
    return np.concatenate(mats, axis=0).astype(np.float32)


def _gla_level_map():
    c = GLA_CHUNK
    i = np.arange(c)[:, None]
    j = np.arange(c)[None, :]
    top_bit = np.floor(np.log2(np.maximum(i ^ j, 1))).astype(np.int64)
    lvl = np.where(j < i, _GLA_LEVELS - 1 - top_bit, np.where(j == i, _GLA_LEVELS, _GLA_LEVELS + 1))
    return np.concatenate([lvl, lvl], axis=1).astype(np.int32)


def _gla_kernel(q_ref, k_ref, la_ref, v_ref, rg_ref, w_ref, lmap_ref, gn_ref, o_ref, st_ref):
    c = GLA_CHUNK
    half = LANES // 2
    npair = G_HEADS // 2

    @pl.when(pl.program_id(1) == 0)
    def _init():
        st_ref[...] = jnp.zeros(st_ref.shape, F32)

    w = w_ref[...]
    lmap = lmap_ref[...]
    lane = lax.broadcasted_iota(jnp.int32, (1, LANES), 1)
    lane_c = lax.broadcasted_iota(jnp.int32, (c, LANES), 1)
    lm = [jnp.where(lane_c < half, 1.0, 0.0).astype(BF16), jnp.where(lane_c >= half, 1.0, 0.0).astype(BF16)]
    row_small = lax.broadcasted_iota(jnp.int32, (c, G_KEY_WIDTH), 0)

    def level_factors(q, k, b, d_all, lvl):
        s = (c // 2) >> lvl
        if s < 8:
            j = lvl - _GLA_BIG_LEVELS
            e = jnp.exp2(d_all[c * (j + 1):c * (j + 2)])
            odd = (row_small // s) % 2 == 1
            return jnp.where(odd, q * e, 0.0), jnp.where(odd, 0.0, k * e)
        zeros = jnp.zeros((s, q.shape[1]), F32)
        qp, kp = [], []
        for blk in range(c // s):
            rows = slice(blk * s, (blk + 1) * s)
            ref = (blk // 2) * 2 * s + s - 1
            if blk % 2 == 1:
                qp.append(q[rows] * jnp.exp2(b[rows] - b[ref:ref + 1]))
                kp.append(zeros)
            else:
                qp.append(zeros)
                kp.append(k[rows] * jnp.exp2(b[ref:ref + 1] - b[rows]))
        return jnp.concatenate(qp, axis=0), jnp.concatenate(kp, axis=0)

    nch = GLA_STEP // c
    rows_of = [slice(ch * c, (ch + 1) * c) for ch in range(nch)]
    pair_lanes = [slice(LANES * p, LANES * (p + 1)) for p in range(npair)]
    d_alls = []
    for ch in range(nch):
        la = la_ref[rows_of[ch], :] * LOG2E
        hi = la.astype(BF16)
        lo = (la - hi.astype(F32)).astype(BF16)
        d_alls.append(_dot(w, hi) + _dot(w, lo))
    chunks = []
    for ch in range(nch):
        d_all = d_alls[ch]
        q = q_ref[rows_of[ch], :]
        k = k_ref[rows_of[ch], :]
        b = d_all[0:c]
        blast = b[c - 1:c, :]
        qts, kts = [], []
        for lvl in range(_GLA_LEVELS):
            qt, kt = level_factors(q, k, b, d_all, lvl)
            qts.append(qt.astype(BF16))
            kts.append(kt.astype(BF16))
        qts.append(q.astype(BF16))
        kts.append(k.astype(BF16))
        chunks.append(dict(qts=qts, kts=kts, qe=(q * jnp.exp2(b)).astype(BF16),
                           khat=(k * jnp.exp2(blast - b)).astype(BF16), dec=jnp.exp2(blast),
                           a=[jnp.zeros((c, 2 * c), F32)] * npair))
    for lvl in range(_GLA_LEVELS + 1):
        mask = lmap == lvl
        s = (c // 2) >> lvl if lvl < _GLA_BIG_LEVELS else c
        blocks = [slice(i * s, (i + 1) * s) for i in range(c // s)]
        live = [i for i in range(len(blocks)) if i % 2 == 1 or s == c]
        for u in chunks:
            for p, sl in enumerate(pair_lanes):
                kp = u["kts"][lvl][:, sl]
                ql = u["qts"][lvl]
                pm = _nt(jnp.concatenate([ql[blocks[i], sl] for i in live], axis=0),
                         jnp.concatenate([kp * lm[0], kp * lm[1]], axis=0))
                a = u["a"][p]
                pieces = [a[r] for r in blocks]
                for n, i in enumerate(live):
                    pieces[i] = jnp.where(mask[blocks[i]], pm[n * s:(n + 1) * s], pieces[i])
                u["a"] = [jnp.concatenate(pieces, axis=0) if j == p else v for j, v in enumerate(u["a"])]
    states = [st_ref[p] for p in range(npair)]
    for ch, u in enumerate(chunks):
        rows = rows_of[ch]
        for p, sl in enumerate(pair_lanes):
            stb = states[p].astype(BF16)
            ups = []
            for hh in range(2):
                h = 2 * p + hh
                vh = v_ref[rows, LANES * h:LANES * (h + 1)]
                o = _dot(u["a"][p][:, hh * c:(hh + 1) * c].astype(BF16), vh) + _nt(u["qe"][:, sl] * lm[hh], stb)
                y = _rms(o, gn_ref[...])
                rg = rg_ref[rows, LANES * h:LANES * (h + 1)]
                y = y * (rg / (1.0 + jnp.exp(-rg)))
                o_ref[rows, LANES * h:LANES * (h + 1)] = y.astype(BF16)
                ups.append(_tn(vh, u["khat"][:, sl]))
            states[p] = states[p] * u["dec"][:, sl] + jnp.where(lane < half, ups[0], ups[1])
    for p in range(npair):
        st_ref[p] = states[p]


def _gla(qg, kg, la, vg, rg, gn, batch, seq):
    n = qg.shape[0]
    nchunk = seq // GLA_STEP
    w = jnp.asarray(_gla_weights(), BF16)
    lmap = jnp.asarray(_gla_level_map())
    row = lambda width: pl.BlockSpec((GLA_STEP, width), lambda b, c: (b * nchunk + c, 0))
    const = functools.partial(pl.BlockSpec, pipeline_mode=pl.Buffered(1))
    return pl.pallas_call(
        _gla_kernel,
        grid=(batch, nchunk),
        in_specs=[row(G_KEY_WIDTH), row(G_KEY_WIDTH), row(G_KEY_WIDTH), row(G_WIDTH), row(G_WIDTH),
                  const(w.shape, lambda b, c: (0, 0)),
                  const(lmap.shape, lambda b, c: (0, 0)),
                  const((1, G_HEAD_V), lambda b, c: (0, 0))],
        out_specs=row(G_WIDTH),
        out_shape=jax.ShapeDtypeStruct((n, G_WIDTH), BF16),
        scratch_shapes=[pltpu.VMEM((G_HEADS // 2, G_HEAD_V, LANES), F32)],
        compiler_params=pltpu.CompilerParams(dimension_semantics=("parallel", "arbitrary"),
                                             vmem_limit_bytes=VMEM_LIMIT),
        name="gla",
    )(qg, kg, la, vg, rg, w, lmap, gn)


def _memkv_kernel(m_ref, g_ref, w_ref, k_ref, v_ref):
    nb, mlen, _ = m_ref.shape
    h = _rms(m_ref[...].reshape(nb * mlen, D_MODEL), g_ref[...]).astype(BF16)
    k_ref[...] = _dot(h, w_ref[:, 0:D_MODEL]).astype(BF16).reshape(nb, mlen, D_MODEL)
    v_ref[...] = _dot(h, w_ref[:, D_MODEL:2 * D_MODEL]).astype(BF16).reshape(nb, mlen, D_MODEL)


def _memkv(mem, g, w):
    batch, mlen, _ = mem.shape
    nb = 2 if batch % 2 == 0 else 1
    const = functools.partial(pl.BlockSpec, pipeline_mode=pl.Buffered(1))
    blk = pl.BlockSpec((nb, mlen, D_MODEL), lambda b: (b, 0, 0))
    return pl.pallas_call(
        _memkv_kernel,
        grid=(batch // nb,),
        in_specs=[blk, const((1, D_MODEL), lambda b: (0, 0)), const((D_MODEL, 2 * D_MODEL), lambda b: (0, 0))],
        out_specs=[blk, blk],
        out_shape=[jax.ShapeDtypeStruct(mem.shape, BF16)] * 2,
        compiler_params=pltpu.CompilerParams(dimension_semantics=("parallel",), vmem_limit_bytes=VMEM_LIMIT),
        name="memkv",
    )(mem, g, w)


def _post_kernel(x_ref, oa_ref, og_ref, wo_ref, gx_ref, wq_ref, kx_ref, vx_ref, wxo_ref, o_ref):
    tm = x_ref.shape[0]
    halves = [slice(i * tm // POST_PARTS, (i + 1) * tm // POST_PARTS) for i in range(POST_PARTS)]
    x1 = [x_ref[r, :] + _dot(oa_ref[r, :], wo_ref[0:A_WIDTH, :]) + _dot(og_ref[r, :], wo_ref[A_WIDTH:D_MODEL, :])
          for r in halves]
    h = [_rms(v, gx_ref[...]).astype(BF16) for v in x1]
    qx = [(_dot(v, wq_ref[...]) * (LOG2E * X_HEAD_DIM ** -0.5)).astype(BF16) for v in h]
    outs = [[] for _ in halves]
    for hd in range(X_HEADS):
        sl = slice(X_HEAD_DIM * hd, X_HEAD_DIM * (hd + 1))
        s = [_nt(q[:, sl], kx_ref[0, :, sl]) for q in qx]
        pexp = [jnp.exp2(v - jnp.max(v, axis=1, keepdims=True)) for v in s]
        pn = [(v / jnp.sum(v, axis=1, keepdims=True)).astype(BF16) for v in pexp]
        for i, v in enumerate(pn):
            outs[i].append(_dot(v, vx_ref[0, :, sl]).astype(BF16))
    for i, r in enumerate(halves):
        o_ref[r, :] = x1[i] + _dot(jnp.concatenate(outs[i], axis=1), wxo_ref[...])


def _post(x2, oa, og, wo, gx, wq, kx, vx, wxo, seq):
    n = x2.shape[0]
    tm = POST_TILE
    per_b = seq // tm
    mlen = kx.shape[1]
    const = functools.partial(pl.BlockSpec, pipeline_mode=pl.Buffered(1))
    row = lambda width: pl.BlockSpec((tm, width), lambda i: (i, 0))
    mem = pl.BlockSpec((1, mlen, D_MODEL), lambda i: (i // per_b, 0, 0))
    sq = const((D_MODEL, D_MODEL), lambda i: (0, 0))
    return pl.pallas_call(
        _post_kernel,
        grid=(n // tm,),
        in_specs=[row(D_MODEL), row(A_WIDTH), row(G_WIDTH), sq, const((1, D_MODEL), lambda i: (0, 0)), sq, mem, mem, sq],
        out_specs=row(D_MODEL),
        out_shape=jax.ShapeDtypeStruct((n, D_MODEL), F32),
        compiler_params=pltpu.CompilerParams(dimension_semantics=("parallel",), vmem_limit_bytes=VMEM_LIMIT),
        name="post",
    )(x2, oa, og, wo, gx, wq, kx, vx, wxo)


def _mlp_kernel(x_ref, g_ref, wu_ref, wd_ref, gf_ref, o_ref, *, final_norm):
    tm = x_ref.shape[0]
    parts = [slice(i * tm // MLP_PARTS, (i + 1) * tm // MLP_PARTS) for i in range(MLP_PARTS)]
    acc = [x_ref[r, :] for r in parts]
    h = [_rms(x, g_ref[...]).astype(BF16) for x in acc]
    for c in range(D_FF // D_MODEL):
        sl = slice(D_MODEL * c, D_MODEL * (c + 1))
        u = [jnp.square(jnp.maximum(_dot(v, wu_ref[:, sl]), 0.0)).astype(BF16) for v in h]
        acc = [a + _dot(v, wd_ref[sl, :]) for a, v in zip(acc, u)]
    for r, a in zip(parts, acc):
        o_ref[r, :] = _rms(a, gf_ref[...]) if final_norm else a


def _mlp(x2, g, wu, wd, gf, final_norm):
    n = x2.shape[0]
    tm = MLP_TILE
    const = functools.partial(pl.BlockSpec, pipeline_mode=pl.Buffered(1))
    row = pl.BlockSpec((tm, D_MODEL), lambda i: (i, 0))
    vec = const((1, D_MODEL), lambda i: (0, 0))
    return pl.pallas_call(
        functools.partial(_mlp_kernel, final_norm=final_norm),
        grid=(n // tm,),
        in_specs=[row, vec, const((D_MODEL, D_FF), lambda i: (0, 0)), const((D_FF, D_MODEL), lambda i: (0, 0)), vec],
        out_specs=row,
        out_shape=jax.ShapeDtypeStruct((n, D_MODEL), F32),
        compiler_params=pltpu.CompilerParams(dimension_semantics=("parallel",), vmem_limit_bytes=VMEM_LIMIT),
        name="mlp",
    )(x2, g, wu, wd, gf)


def kernel(x, mem, rp_table, norm_mix, w_in, w_gate_up, b_gate, g_norm, w_out, norm_xattn, norm_mem, w_xq, w_xkv,
           w_xo, norm_mlp, w_up, w_down, norm_final):
    batch, seq, _ = x.shape
    depth = w_in.shape[0]
    assert seq % max(INPROJ_TILE, POST_TILE, MLP_TILE, GLA_STEP) == 0 and seq // MOBA_BLOCK <= 8
    x2 = x.reshape(batch * seq, D_MODEL)
    bias = _bias_tiles(rp_table)
    c31 = rp_table[RP_BUCKETS - 1]
    va_lo = 2 * A_WIDTH
    g_lo = va_lo + A_WIDTH
    glr_lo = g_lo + 2 * G_KEY_WIDTH + G_WIDTH
    rg_lo = glr_lo + G_GATE_RANK
    pad_rank = LANES - G_GATE_RANK
    for l in range(depth):
        col = lambda lo, hi: w_in[l, :, lo:hi].astype(BF16)
        wglr = jnp.pad(col(glr_lo, rg_lo), ((0, 0), (0, pad_rank)))
        wgu = jnp.pad(w_gate_up[l].astype(BF16), ((0, pad_rank), (0, 0)))
        qk, vt, qg, kg, la, vg, rg = _inproj(x2, norm_mix[l][None], col(0, va_lo), col(va_lo, g_lo), col(g_lo, glr_lo),
                                             col(rg_lo, rg_lo + G_WIDTH), wglr, wgu, b_gate[l][None])
        oa = _moba(qk, vt, bias, c31, batch, seq)
        og = _gla(qg, kg, la, vg, rg, g_norm[l][None], batch, seq)
        kx, vx = _memkv(mem, norm_mem[l][None], w_xkv[l].astype(BF16))
        x2 = _post(x2, oa, og, w_out[l].astype(BF16), norm_xattn[l][None], w_xq[l].astype(BF16), kx, vx,
                   w_xo[l].astype(BF16), seq)
        last = l == depth - 1
        x2 = _mlp(x2, norm_mlp[l][None], w_up[l].astype(BF16), w_down[l].astype(BF16), norm_final[None], last)
    return x2.reshape(batch, seq, D_MODEL)
```

```python
import functools
import math

import numpy as np
import jax
import jax.numpy as jnp
from jax import lax
from jax.experimental import pallas as pl
from jax.experimental.pallas import tpu as pltpu

F32 = jnp.float32
BF16 = jnp.bfloat16

D_MODEL = 1024
A_HEADS = 8
A_HEAD_DIM = 64
A_WIDTH = A_HEADS * A_HEAD_DIM
MOBA_BLOCK = 256
MOBA_TOPK = 3
G_HEADS = 4
G_WIDTH = D_MODEL - A_WIDTH
G_HEAD_V = G_WIDTH // G_HEADS
G_KEY_WIDTH = G_WIDTH // 2
G_HEAD_K = G_KEY_WIDTH // G_HEADS
G_GATE_RANK = 16
G_GATE_NORM = 16.0
X_HEADS = 4
X_HEAD_DIM = D_MODEL // X_HEADS
D_FF = 4 * D_MODEL
RP_BUCKETS = 32
RP_MAX_DIST = 128
EPS = 1e-6

LANES = 128
NEG = -1e30
GLA_CHUNK = 128
GLA_STEP = 1024
LOG2E = float(np.log2(np.e))
INPROJ_TILE = 1024
INPROJ_PARTS = 2
MLP_TILE = 1024
MLP_PARTS = 2
POST_TILE = 1024
POST_PARTS = 2
VMEM_LIMIT = 56 * 1024 * 1024

def _nt(a, b):
    return lax.dot_general(a, b, (((1,), (1,)), ((), ())), preferred_element_type=F32)


def _tn(a, b):
    return lax.dot_general(a, b, (((0,), (0,)), ((), ())), preferred_element_type=F32)


def _dot(a, b):
    return jnp.dot(a, b, preferred_element_type=F32)


def _rms(x, g):
    return x * lax.rsqrt(jnp.mean(x * x, axis=-1, keepdims=True) + EPS) * g


def _bucket_thresholds():
    max_exact = RP_BUCKETS // 2
    d = np.arange(1, 4 * RP_MAX_DIST)
    val = (np.log(d.astype(np.float32) / np.float32(max_exact)) / np.float32(math.log(RP_MAX_DIST / max_exact))
           * np.float32(RP_BUCKETS - max_exact))
    bucket = np.minimum(max_exact + val.astype(np.int32), RP_BUCKETS - 1)
    return [int(d[(d >= max_exact) & (bucket >= max_exact + k)][0]) for k in range(1, RP_BUCKETS - max_exact)]


_THRESHOLDS = _bucket_thresholds()


def _bias_kernel(tab_ref, out_ref):
    h = pl.program_id(0)
    rows, width = 2 * MOBA_BLOCK, 3 * MOBA_BLOCK
    d = lax.broadcasted_iota(jnp.int32, (8, width), 1)
    max_exact = RP_BUCKETS // 2
    bucket = jnp.where(d < max_exact, d, max_exact)
    for t in _THRESHOLDS:
        bucket = bucket + jnp.where(d >= t, 1, 0)
    prof = jnp.full((8, width), NEG, F32)
    for b in range(RP_BUCKETS):
        prof = jnp.where(bucket == b, tab_ref[b, h] * LOG2E, prof)
    prof = jnp.where(d < rows, prof, NEG)
    tile = pltpu.roll(jnp.broadcast_to(prof[0:1], (rows, width)), 0, 1, stride=1, stride_axis=0)
    out_ref[0] = tile[:, MOBA_BLOCK:2 * MOBA_BLOCK]


def _bias_tiles(rp_table):
    return pl.pallas_call(
        _bias_kernel,
        grid=(A_HEADS,),
        in_specs=[pl.BlockSpec(memory_space=pltpu.SMEM)],
        out_specs=pl.BlockSpec((1, 2 * MOBA_BLOCK, MOBA_BLOCK), lambda h: (h, 0, 0)),
        out_shape=jax.ShapeDtypeStruct((A_HEADS, 2 * MOBA_BLOCK, MOBA_BLOCK), F32),
        name="bias",
    )(rp_table)


def _inproj_kernel(x_ref, g_ref, wqk_ref, wva_ref, wg_ref, wrg_ref, wglr_ref, wgu_ref, bg_ref,
                   qk_ref, vt_ref, qg_ref, kg_ref, la_ref, vg_ref, rg_ref):
    tm = x_ref.shape[0]
    parts = [slice(i * tm // INPROJ_PARTS, (i + 1) * tm // INPROJ_PARTS) for i in range(INPROJ_PARTS)]
    hs = [_rms(x_ref[r, :], g_ref[...]).astype(BF16) for r in parts]
    kg_lo, vg_lo = G_KEY_WIDTH, 2 * G_KEY_WIDTH

    glrs = [_dot(h, wglr_ref[...]).astype(BF16) for h in hs]
    for r, h in zip(parts, hs):
        qk_ref[r, 0:A_WIDTH] = (_dot(h, wqk_ref[:, 0:A_WIDTH]) * (LOG2E * A_HEAD_DIM ** -0.5)).astype(BF16)
    for r, h in zip(parts, hs):
        qk_ref[r, A_WIDTH:2 * A_WIDTH] = _dot(h, wqk_ref[:, A_WIDTH:2 * A_WIDTH]).astype(BF16)
    for r, glr in zip(parts, glrs):
        z = _dot(glr, wgu_ref[...]) + bg_ref[...]
        log_sig = jnp.minimum(z, 0.0) - jnp.log(1.0 + jnp.exp(-jnp.abs(z)))
        la_ref[r, :] = log_sig * (1.0 / G_GATE_NORM)
    for r, h in zip(parts, hs):
        vt_ref[:, r] = _dot(h, wva_ref[...]).astype(BF16).T
    for r, h in zip(parts, hs):
        qg_ref[r, :] = _dot(h, wg_ref[:, 0:kg_lo]) * (G_HEAD_K ** -0.5)
        kg_ref[r, :] = _dot(h, wg_ref[:, kg_lo:vg_lo])
    for r, h in zip(parts, hs):
        vg_ref[r, :] = _dot(h, wg_ref[:, vg_lo:vg_lo + G_WIDTH]).astype(BF16)
    for r, h in zip(parts, hs):
        rg_ref[r, :] = _dot(h, wrg_ref[...])


def _inproj(x2, g, wqk, wva, wg, wrg, wglr, wgu, bg):
    n = x2.shape[0]
    tm = INPROJ_TILE
    const = lambda a: pl.BlockSpec(a.shape, lambda i: (0, 0), pipeline_mode=pl.Buffered(1))
    row = lambda width: pl.BlockSpec((tm, width), lambda i: (i, 0))
    return pl.pallas_call(
        _inproj_kernel,
        grid=(n // tm,),
        in_specs=[row(D_MODEL)] + [const(a) for a in (g, wqk, wva, wg, wrg, wglr, wgu, bg)],
        out_specs=[row(2 * A_WIDTH), pl.BlockSpec((A_WIDTH, tm), lambda i: (0, i)),
                   row(G_KEY_WIDTH), row(G_KEY_WIDTH), row(G_KEY_WIDTH), row(G_WIDTH), row(G_WIDTH)],
        out_shape=[jax.ShapeDtypeStruct((n, 2 * A_WIDTH), BF16),
                   jax.ShapeDtypeStruct((A_WIDTH, n), BF16),
                   jax.ShapeDtypeStruct((n, G_KEY_WIDTH), F32),
                   jax.ShapeDtypeStruct((n, G_KEY_WIDTH), F32),
                   jax.ShapeDtypeStruct((n, G_KEY_WIDTH), F32),
                   jax.ShapeDtypeStruct((n, G_WIDTH), BF16),
                   jax.ShapeDtypeStruct((n, G_WIDTH), F32)],
        compiler_params=pltpu.CompilerParams(dimension_semantics=("parallel",), vmem_limit_bytes=VMEM_LIMIT),
        name="inproj",
    )(x2, g, wqk, wva, wg, wrg, wglr, wgu, bg)


def _moba_kernel(c31_ref, q_ref, k_ref, vt_ref, bias_ref, o_ref, kext, s_scr, *, seq):
    p = pl.program_id(1)
    nblk = seq // MOBA_BLOCK
    half = LANES // 2

    k = k_ref[...]
    lane2 = lax.broadcasted_iota(jnp.int32, (seq, LANES), 1)
    lm0 = jnp.where(lane2 < half, 1.0, 0.0).astype(BF16)
    lm1 = jnp.where(lane2 >= half, 1.0, 0.0).astype(BF16)
    kext[0] = k * lm0
    kext[1] = k * lm1
    n_i = lax.broadcasted_iota(jnp.int32, (16, seq), 0)
    t_i = lax.broadcasted_iota(jnp.int32, (16, seq), 1)
    avg = jnp.where(t_i // MOBA_BLOCK == n_i, 1.0 / MOBA_BLOCK, 0.0).astype(BF16)
    km = _dot(avg, k)
    kmh = km.astype(BF16)
    kml = (km - kmh.astype(F32)).astype(BF16)

    lane_q = lax.broadcasted_iota(jnp.int32, (MOBA_BLOCK, LANES), 1)
    rowi = lax.broadcasted_iota(jnp.int32, (16, MOBA_BLOCK), 0)
    blk = lambda n: slice(n * MOBA_BLOCK, (n + 1) * MOBA_BLOCK)

    def prepare(qi, h, slot):
        qm = q_ref[blk(qi), :] * jnp.where((lane_q < half) if h == 0 else (lane_q >= half), 1.0, 0.0).astype(BF16)
        gate = _nt(kmh, qm) + _nt(kml, qm) if qi > MOBA_TOPK else None
        return dict(qi=qi, h=h, qm=qm, gate=gate, c31=c31_ref[2 * p + h] * LOG2E, sbuf=s_scr.at[slot], near=None,
                    far=None, acc=None)

    def penalty(u, n):
        return 0.0 if u["pen"] is None else u["pen"][n:n + 1, :]

    def score_matmul(u):
        u["s_all"] = _nt(kext[u["h"], 0:(u["qi"] + 1) * MOBA_BLOCK, :], u["qm"])

    def score_reduce(u):
        qi, h = u["qi"], u["h"]
        u["pen"] = None
        if u["gate"] is not None:
            g = u["gate"]
            cnt = jnp.zeros((16, MOBA_BLOCK), F32)
            for m in range(qi):
                gm = g[m:m + 1, :]
                cnt = cnt + jnp.where((gm > g) | ((gm == g) & (m < rowi)), 1.0, 0.0)
            u["pen"] = jnp.where((rowi < qi) & (cnt < MOBA_TOPK), 0.0, NEG)
        for n in range(qi + 1):
            s = u["s_all"][blk(n)]
            if n == qi:
                s = s + bias_ref[h, MOBA_BLOCK:2 * MOBA_BLOCK, :]
            elif n == qi - 1:
                s = s + bias_ref[h, 0:MOBA_BLOCK, :]
            u["sbuf"][blk(n), :] = s
            t = jnp.max(s.reshape(MOBA_BLOCK // 8, 8, MOBA_BLOCK), axis=0)
            if n < qi:
                t = t + penalty(u, n)
            key = "far" if n < qi - 1 else "near"
            u[key] = t if u[key] is None else jnp.maximum(u[key], t)
        mx = u["near"] if u["far"] is None else jnp.maximum(u["near"], u["far"] + u["c31"])
        u["m_near"] = jnp.max(mx, axis=0, keepdims=True)
        u["m_far"] = u["m_near"] - u["c31"]

    def prob_exp(u):
        qi = u["qi"]
        shift = lambda n: (u["m_far"] if n < qi - 1 else u["m_near"]) - (penalty(u, n) if n < qi else 0.0)
        u["pt"] = jnp.concatenate([jnp.exp2(u["sbuf"][blk(n), :] - shift(n)).astype(BF16) for n in range(qi + 1)],
                                  axis=0)

    def prob_matmul(u):
        qi, h = u["qi"], u["h"]
        width = (qi + 1) * MOBA_BLOCK
        vrows = vt_ref[half * h:half * (h + 1), 0:width]
        ones_rows = jnp.ones((half, width), BF16)
        vt = jnp.concatenate([vrows, ones_rows] if h == 0 else [ones_rows, vrows], axis=0)
        u["acc"] = _dot(vt, u["pt"])

    outs = {}

    def finish(u):
        acc, h = u["acc"], u["h"]
        num = acc[half * h:half * (h + 1)]
        den = acc[half * (1 - h):half * (1 - h) + 1]
        outs[h] = num / den
        if h == 1:
            o_ref[blk(u["qi"]), :] = jnp.concatenate([outs[0], outs[1]], axis=0).T.astype(BF16)

    group = 2
    stages = [[(qi, h) for qi in range(g * group, (g + 1) * group) for h in range(2)] for g in range(nblk // group)]
    prev = []
    for g, members in enumerate(stages + [[]]):
        cur = [prepare(qi, h, (g % 2) * 2 * group + 2 * (qi % group) + h) for qi, h in members]
        for u in prev:
            prob_exp(u)
        for u in cur:
            score_matmul(u)
        for u in prev:
            prob_matmul(u)
        for u in cur:
            score_reduce(u)
        for u in prev:
            finish(u)
        prev = cur


def _moba(qk, vt, bias, c31, batch, seq):
    n = qk.shape[0]
    npair = A_HEADS // 2
    kern = functools.partial(_moba_kernel, seq=seq)
    return pl.pallas_call(
        kern,
        grid=(batch, npair),
        in_specs=[pl.BlockSpec(memory_space=pltpu.SMEM),
                  pl.BlockSpec((seq, LANES), lambda b, p: (b, p)),
                  pl.BlockSpec((seq, LANES), lambda b, p: (b, npair + p)),
                  pl.BlockSpec((LANES, seq), lambda b, p: (p, b)),
                  pl.BlockSpec((2, 2 * MOBA_BLOCK, MOBA_BLOCK), lambda b, p: (p, 0, 0))],
        out_specs=pl.BlockSpec((seq, LANES), lambda b, p: (b, p)),
        out_shape=jax.ShapeDtypeStruct((n, A_WIDTH), BF16),
        scratch_shapes=[pltpu.VMEM((2, seq, LANES), BF16),
                        pltpu.VMEM((8, seq, MOBA_BLOCK), F32)],
        compiler_params=pltpu.CompilerParams(dimension_semantics=("parallel", "arbitrary"),
                                             vmem_limit_bytes=VMEM_LIMIT),
        name="moba",
    )(c31, qk, qk, vt, bias)


_GLA_LEVELS = int(math.log2(GLA_CHUNK))


_GLA_BIG_LEVELS = _GLA_LEVELS - 3


def _gla_weights():
    c = GLA_CHUNK
    i = np.arange(c)[:, None]
    j = np.arange(c)[None, :]
    mats = [(j <= i)]
    for lvl in range(_GLA_BIG_LEVELS, _GLA_LEVELS):
        s = (c // 2) >> lvl
        ref = (i // (2 * s)) * (2 * s) + s - 1
        mats.append(np.where(i > ref, (j > ref) & (j <= i), (j > i) & (j <= ref)))
    return np.concatenate(mats, axis=0).astype(np.float32)


def _gla_level_map():
    c = GLA_CHUNK
    i = np.arange(c)[:, None]
    j = np.arange(c)[None, :]
    top_bit = np.floor(np.log2(np.maximum(i ^ j, 1))).astype(np.int64)
    lvl = np.where(j < i, _GLA_LEVELS - 1 - top_bit, np.where(j == i, _GLA_LEVELS, _GLA_LEVELS + 1))
    return np.concatenate([lvl, lvl], axis=1).astype(np.int32)


def _gla_kernel(q_ref, k_ref, la_ref, v_ref, rg_ref, w_ref, lmap_ref, gn_ref, o_ref, st_ref):
    c = GLA_CHUNK
    half = LANES // 2
    npair = G_HEADS // 2

    @pl.when(pl.program_id(1) == 0)
    def _init():
        st_ref[...] = jnp.zeros(st_ref.shape, F32)

    w = w_ref[...]
    lmap = lmap_ref[...]
    lane = lax.broadcasted_iota(jnp.int32, (1, LANES), 1)
    lane_c = lax.broadcasted_iota(jnp.int32, (c, LANES), 1)
    lm = [jnp.where(lane_c < half, 1.0, 0.0).astype(BF16), jnp.where(lane_c >= half, 1.0, 0.0).astype(BF16)]
    row_small = lax.broadcasted_iota(jnp.int32, (c, G_KEY_WIDTH), 0)

    def level_factors(q, k, b, d_all, lvl):
        s = (c // 2) >> lvl
        if s < 8:
            j = lvl - _GLA_BIG_LEVELS
            e = jnp.exp2(d_all[c * (j + 1):c * (j + 2)])
            odd = (row_small // s) % 2 == 1
            return jnp.where(odd, q * e, 0.0), jnp.where(odd, 0.0, k * e)
        zeros = jnp.zeros((s, q.shape[1]), F32)
        qp, kp = [], []
        for blk in range(c // s):
            rows = slice(blk * s, (blk + 1) * s)
            ref = (blk // 2) * 2 * s + s - 1
            if blk % 2 == 1:
                qp.append(q[rows] * jnp.exp2(b[rows] - b[ref:ref + 1]))
                kp.append(zeros)
            else:
                qp.append(zeros)
                kp.append(k[rows] * jnp.exp2(b[ref:ref + 1] - b[rows]))
        return jnp.concatenate(qp, axis=0), jnp.concatenate(kp, axis=0)

    nch = GLA_STEP // c
    rows_of = [slice(ch * c, (ch + 1) * c) for ch in range(nch)]
    pair_lanes = [slice(LANES * p, LANES * (p + 1)) for p in range(npair)]
    d_alls = []
    for ch in range(nch):
        la = la_ref[rows_of[ch], :] * LOG2E
        hi = la.astype(BF16)
        lo = (la - hi.astype(F32)).astype(BF16)
        d_alls.append(_dot(w, hi) + _dot(w, lo))
    chunks = []
    for ch in range(nch):
        d_all = d_alls[ch]
        q = q_ref[rows_of[ch], :]
        k = k_ref[rows_of[ch], :]
        b = d_all[0:c]
        blast = b[c - 1:c, :]
        qts, kts = [], []
        for lvl in range(_GLA_LEVELS):
            qt, kt = level_factors(q, k, b, d_all, lvl)
            qts.append(qt.astype(BF16))
            kts.append(kt.astype(BF16))
        qts.append(q.astype(BF16))
        kts.append(k.astype(BF16))
        chunks.append(dict(qts=qts, kts=kts, qe=(q * jnp.exp2(b)).astype(BF16),
                           khat=(k * jnp.exp2(blast - b)).astype(BF16), dec=jnp.exp2(blast),
                           a=[jnp.zeros((c, 2 * c), F32)] * npair))
    for lvl in range(_GLA_LEVELS + 1):
        mask = lmap == lvl
        s = (c // 2) >> lvl if lvl < _GLA_BIG_LEVELS else c
        blocks = [slice(i * s, (i + 1) * s) for i in range(c // s)]
        live = [i for i in range(len(blocks)) if i % 2 == 1 or s == c]
        for u in chunks:
            for p, sl in enumerate(pair_lanes):
                kp = u["kts"][lvl][:, sl]
                ql = u["qts"][lvl]
                pm = _nt(jnp.concatenate([ql[blocks[i], sl] for i in live], axis=0),
                         jnp.concatenate([kp * lm[0], kp * lm[1]], axis=0))
                a = u["a"][p]
                pieces = [a[r] for r in blocks]
                for n, i in enumerate(live):
                    pieces[i] = jnp.where(mask[blocks[i]], pm[n * s:(n + 1) * s], pieces[i])
                u["a"] = [jnp.concatenate(pieces, axis=0) if j == p else v for j, v in enumerate(u["a"])]
    states = [st_ref[p] for p in range(npair)]
    for ch, u in enumerate(chunks):
        rows = rows_of[ch]
        for p, sl in enumerate(pair_lanes):
            stb = states[p].astype(BF16)
            ups = []
            for hh in range(2):
                h = 2 * p + hh
                vh = v_ref[rows, LANES * h:LANES * (h + 1)]
                o = _dot(u["a"][p][:, hh * c:(hh + 1) * c].astype(BF16), vh) + _nt(u["qe"][:, sl] * lm[hh], stb)
                y = _rms(o, gn_ref[...])
                rg = rg_ref[rows, LANES * h:LANES * (h + 1)]
                y = y * (rg / (1.0 + jnp.exp(-rg)))
                o_ref[rows, LANES * h:LANES * (h + 1)] = y.astype(BF16)
                ups.append(_tn(vh, u["khat"][:, sl]))
            states[p] = states[p] * u["dec"][:, sl] + jnp.where(lane < half, ups[0], ups[1])
    for p in range(npair):
        st_ref[p] = states[p]


def _gla(qg, kg, la, vg, rg, gn, batch, seq):
    n = qg.shape[0]
    nchunk = seq // GLA_STEP
    w = jnp.asarray(_gla_weights(), BF16)
    lmap = jnp.asarray(_gla_level_map())
    row = lambda width: pl.BlockSpec((GLA_STEP, width), lambda b, c: (b * nchunk + c, 0))
    const = functools.partial(pl.BlockSpec, pipeline_mode=pl.Buffered(1))
    return pl.pallas_call(
        _gla_kernel,
        grid=(batch, nchunk),
        in_specs=[row(G_KEY_WIDTH), row(G_KEY_WIDTH), row(G_KEY_WIDTH), row(G_WIDTH), row(G_WIDTH),
                  const(w.shape, lambda b, c: (0, 0)),
                  const(lmap.shape, lambda b, c: (0, 0)),
                  const((1, G_HEAD_V), lambda b, c: (0, 0))],
        out_specs=row(G_WIDTH),
        out_shape=jax.ShapeDtypeStruct((n, G_WIDTH), BF16),
        scratch_shapes=[pltpu.VMEM((G_HEADS // 2, G_HEAD_V, LANES), F32)],
        compiler_params=pltpu.CompilerParams(dimension_semantics=("parallel", "arbitrary"),
                                             vmem_limit_bytes=VMEM_LIMIT),
        name="gla",
    )(qg, kg, la, vg, rg, w, lmap, gn)


def _memkv_kernel(m_ref, g_ref, w_ref, k_ref, v_ref):
    nb, mlen, _ = m_ref.shape
    h = _rms(m_ref[...].reshape(nb * mlen, D_MODEL), g_ref[...]).astype(BF16)
    k_ref[...] = _dot(h, w_ref[:, 0:D_MODEL]).astype(BF16).reshape(nb, mlen, D_MODEL)
    v_ref[...] = _dot(h, w_ref[:, D_MODEL:2 * D_MODEL]).astype(BF16).reshape(nb, mlen, D_MODEL)


def _memkv(mem, g, w):
    batch, mlen, _ = mem.shape
    nb = 2 if batch % 2 == 0 else 1
    const = functools.partial(pl.BlockSpec, pipeline_mode=pl.Buffered(1))
    blk = pl.BlockSpec((nb, mlen, D_MODEL), lambda b: (b, 0, 0))
    return pl.pallas_call(
        _memkv_kernel,
        grid=(batch // nb,),
        in_specs=[blk, const((1, D_MODEL), lambda b: (0, 0)), const((D_MODEL, 2 * D_MODEL), lambda b: (0, 0))],
        out_specs=[blk, blk],
        out_shape=[jax.ShapeDtypeStruct(mem.shape, BF16)] * 2,
        compiler_params=pltpu.CompilerParams(dimension_semantics=("parallel",), vmem_limit_bytes=VMEM_LIMIT),
        name="memkv",
    )(mem, g, w)


def _post_kernel(x_ref, oa_ref, og_ref, wo_ref, gx_ref, wq_ref, kx_ref, vx_ref, wxo_ref, o_ref):
    tm = x_ref.shape[0]
    halves = [slice(i * tm // POST_PARTS, (i + 1) * tm // POST_PARTS) for i in range(POST_PARTS)]
    x1 = [x_ref[r, :] + _dot(oa_ref[r, :], wo_ref[0:A_WIDTH, :]) + _dot(og_ref[r, :], wo_ref[A_WIDTH:D_MODEL, :])
          for r in halves]
    h = [_rms(v, gx_ref[...]).astype(BF16) for v in x1]
    qx = [(_dot(v, wq_ref[...]) * (LOG2E * X_HEAD_DIM ** -0.5)).astype(BF16) for v in h]
    outs = [[] for _ in halves]
    for hd in range(X_HEADS):
        sl = slice(X_HEAD_DIM * hd, X_HEAD_DIM * (hd + 1))
        s = [_nt(q[:, sl], kx_ref[0, :, sl]) for q in qx]
        pexp = [jnp.exp2(v - jnp.max(v, axis=1, keepdims=True)) for v in s]
        pn = [(v / jnp.sum(v, axis=1, keepdims=True)).astype(BF16) for v in pexp]
        for i, v in enumerate(pn):
            outs[i].append(_dot(v, vx_ref[0, :, sl]).astype(BF16))
    for i, r in enumerate(halves):
        o_ref[r, :] = x1[i] + _dot(jnp.concatenate(outs[i], axis=1), wxo_ref[...])


def _post(x2, oa, og, wo, gx, wq, kx, vx, wxo, seq):
    n = x2.shape[0]
    tm = POST_TILE
    per_b = seq // tm
    mlen = kx.shape[1]
    const = functools.partial(pl.BlockSpec, pipeline_mode=pl.Buffered(1))
    row = lambda width: pl.BlockSpec((tm, width), lambda i: (i, 0))
    mem = pl.BlockSpec((1, mlen, D_MODEL), lambda i: (i // per_b, 0, 0))
    sq = const((D_MODEL, D_MODEL), lambda i: (0, 0))
    return pl.pallas_call(
        _post_kernel,
        grid=(n // tm,),
        in_specs=[row(D_MODEL), row(A_WIDTH), row(G_WIDTH), sq, const((1, D_MODEL), lambda i: (0, 0)), sq, mem, mem, sq],
        out_specs=row(D_MODEL),
        out_shape=jax.ShapeDtypeStruct((n, D_MODEL), F32),
        compiler_params=pltpu.CompilerParams(dimension_semantics=("parallel",), vmem_limit_bytes=VMEM_LIMIT),
        name="post",
    )(x2, oa, og, wo, gx, wq, kx, vx, wxo)


def _mlp_kernel(x_ref, g_ref, wu_ref, wd_ref, gf_ref, o_ref, *, final_norm):
    tm = x_ref.shape[0]
    parts = [slice(i * tm // MLP_PARTS, (i + 1) * tm // MLP_PARTS) for i in range(MLP_PARTS)]
    acc = [x_ref[r, :] for r in parts]
    h = [_rms(x, g_ref[...]).astype(BF16) for x in acc]
    for c in range(D_FF // D_MODEL):
        sl = slice(D_MODEL * c, D_MODEL * (c + 1))
        u = [jnp.square(jnp.maximum(_dot(v, wu_ref[:, sl]), 0.0)).astype(BF16) for v in h]
        acc = [a + _dot(v, wd_ref[sl, :]) for a, v in zip(acc, u)]
    for r, a in zip(parts, acc):
        o_ref[r, :] = _rms(a, gf_ref[...]) if final_norm else a


def _mlp(x2, g, wu, wd, gf, final_norm):
    n = x2.shape[0]
    tm = MLP_TILE
    const = functools.partial(pl.BlockSpec, pipeline_mode=pl.Buffered(1))
    row = pl.BlockSpec((tm, D_MODEL), lambda i: (i, 0))
    vec = const((1, D_MODEL), lambda i: (0, 0))
    return pl.pallas_call(
        functools.partial(_mlp_kernel, final_norm=final_norm),
        grid=(n // tm,),
        in_specs=[row, vec, const((D_MODEL, D_FF), lambda i: (0, 0)), const((D_FF, D_MODEL), lambda i: (0, 0)), vec],
        out_specs=row,
        out_shape=jax.ShapeDtypeStruct((n, D_MODEL), F32),
        compiler_params=pltpu.CompilerParams(dimension_semantics=("parallel",), vmem_limit_bytes=VMEM_LIMIT),
        name="mlp",
    )(x2, g, wu, wd, gf)


def kernel(x, mem, rp_table, norm_mix, w_in, w_gate_up, b_gate, g_norm, w_out, norm_xattn, norm_mem, w_xq, w_xkv,
           w_xo, norm_mlp, w_up, w_down, norm_final):
    batch, seq, _ = x.shape
    depth = w_in.shape[0]
    assert seq % max(INPROJ_TILE, POST_TILE, MLP_TILE, GLA_STEP) == 0 and seq // MOBA_BLOCK <= 8
    x2 = x.reshape(batch * seq, D_MODEL)
    bias = _bias_tiles(rp_table)
    c31 = rp_table[RP_BUCKETS - 1]
    va_lo = 2 * A_WIDTH
    g_lo = va_lo + A_WIDTH
    glr_lo = g_lo + 2 * G_KEY_WIDTH + G_WIDTH
    rg_lo = glr_lo + G_GATE_RANK
    pad_rank = LANES - G_GATE_RANK
    for l in range(depth):
        col = lambda lo, hi: w_in[l, :, lo:hi].astype(BF16)
        wglr = jnp.pad(col(glr_lo, rg_lo), ((0, 0), (0, pad_rank)))
        wgu = jnp.pad(w_gate_up[l].astype(BF16), ((0, pad_rank), (0, 0)))
        qk, vt, qg, kg, la, vg, rg = _inproj(x2, norm_mix[l][None], col(0, va_lo), col(va_lo, g_lo), col(g_lo, glr_lo),
                                             col(rg_lo, rg_lo + G_WIDTH), wglr, wgu, b_gate[l][None])
        oa = _moba(qk, vt, bias, c31, batch, seq)
        og = _gla(qg, kg, la, vg, rg, g_norm[l][None], batch, seq)
        kx, vx = _memkv(mem, norm_mem[l][None], w_xkv[l].astype(BF16))
        x2 = _post(x2, oa, og, w_out[l].astype(BF16), norm_xattn[l][None], w_xq[l].astype(BF16), kx, vx,
                   w_xo[l].astype(BF16), seq)
        last = l == depth - 1
        x2 = _mlp(x2, norm_mlp[l][None], w_up[l].astype(BF16), w_down[l].astype(BF16), norm_final[None], last)
    return x2.reshape(batch, seq, D_MODEL)
```

```python
import functools
import math

import numpy as np
import jax
import jax.numpy as jnp
from jax import lax
from jax.experimental import pallas as pl
from jax.experimental.pallas import tpu as pltpu

F32 = jnp.float32
BF16 = jnp.bfloat16

D_MODEL = 1024
A_HEADS = 8
A_HEAD_DIM = 64
A_WIDTH = A_HEADS * A_HEAD_DIM
MOBA_BLOCK = 256
MOBA_TOPK = 3
G_HEADS = 4
G_WIDTH = D_MODEL - A_WIDTH
G_HEAD_V = G_WIDTH // G_HEADS
G_KEY_WIDTH = G_WIDTH // 2
G_HEAD_K = G_KEY_WIDTH // G_HEADS
G_GATE_RANK = 16
G_GATE_NORM = 16.0
X_HEADS = 4
X_HEAD_DIM = D_MODEL // X_HEADS
D_FF = 4 * D_MODEL
RP_BUCKETS = 32
RP_MAX_DIST = 128
EPS = 1e-6

LANES = 128
NEG = -1e30
GLA_CHUNK = 128
GLA_STEP = 1024
LOG2E = float(np.log2(np.e))
INPROJ_TILE = 1024
INPROJ_PARTS = 2
MLP_TILE = 1024
MLP_PARTS = 2
POST_TILE = 1024
POST_PARTS = 2
VMEM_LIMIT = 56 * 1024 * 1024

def _nt(a, b):
    return lax.dot_general(a, b, (((1,), (1,)), ((), ())), preferred_element_type=F32)


def _tn(a, b):
    return lax.dot_general(a, b, (((0,), (0,)), ((), ())), preferred_element_type=F32)


def _dot(a, b):
    return jnp.dot(a, b, preferred_element_type=F32)


def _rms(x, g):
    return x * lax.rsqrt(jnp.mean(x * x, axis=-1, keepdims=True) + EPS) * g


def _bucket_thresholds():
    max_exact = RP_BUCKETS // 2
    d = np.arange(1, 4 * RP_MAX_DIST)
    val = (np.log(d.astype(np.float32) / np.float32(max_exact)) / np.float32(math.log(RP_MAX_DIST / max_exact))
           * np.float32(RP_BUCKETS - max_exact))
    bucket = np.minimum(max_exact + val.astype(np.int32), RP_BUCKETS - 1)
    return [int(d[(d >= max_exact) & (bucket >= max_exact + k)][0]) for k in range(1, RP_BUCKETS - max_exact)]


_THRESHOLDS = _bucket_thresholds()


def _bias_kernel(tab_ref, out_ref):
    h = pl.program_id(0)
    rows, width = 2 * MOBA_BLOCK, 3 * MOBA_BLOCK
    d = lax.broadcasted_iota(jnp.int32, (8, width), 1)
    max_exact = RP_BUCKETS // 2
    bucket = jnp.where(d < max_exact, d, max_exact)
    for t in _THRESHOLDS:
        bucket = bucket + jnp.where(d >= t, 1, 0)
    prof = jnp.full((8, width), NEG, F32)
    for b in range(RP_BUCKETS):
        prof = jnp.where(bucket == b, tab_ref[b, h] * LOG2E, prof)
    prof = jnp.where(d < rows, prof, NEG)
    tile = pltpu.roll(jnp.broadcast_to(prof[0:1], (rows, width)), 0, 1, stride=1, stride_axis=0)
    out_ref[0] = tile[:, MOBA_BLOCK:2 * MOBA_BLOCK]


def _bias_tiles(rp_table):
    return pl.pallas_call(
        _bias_kernel,
        grid=(A_HEADS,),
        in_specs=[pl.BlockSpec(memory_space=pltpu.SMEM)],
        out_specs=pl.BlockSpec((1, 2 * MOBA_BLOCK, MOBA_BLOCK), lambda h: (h, 0, 0)),
        out_shape=jax.ShapeDtypeStruct((A_HEADS, 2 * MOBA_BLOCK, MOBA_BLOCK), F32),
        name="bias",
    )(rp_table)


def _inproj_kernel(x_ref, g_ref, wqk_ref, wva_ref, wg_ref, wrg_ref, wglr_ref, wgu_ref, bg_ref,
                   qk_ref, vt_ref, qg_ref, kg_ref, la_ref, vg_ref, rg_ref):
    tm = x_ref.shape[0]
    parts = [slice(i * tm // INPROJ_PARTS, (i + 1) * tm // INPROJ_PARTS) for i in range(INPROJ_PARTS)]
    hs = [_rms(x_ref[r, :], g_ref[...]).astype(BF16) for r in parts]
    kg_lo, vg_lo = G_KEY_WIDTH, 2 * G_KEY_WIDTH

    glrs = [_dot(h, wglr_ref[...]).astype(BF16) for h in hs]
    for r, h in zip(parts, hs):
        qk_ref[r, 0:A_WIDTH] = (_dot(h, wqk_ref[:, 0:A_WIDTH]) * (LOG2E * A_HEAD_DIM ** -0.5)).astype(BF16)
    for r, h in zip(parts, hs):
        qk_ref[r, A_WIDTH:2 * A_WIDTH] = _dot(h, wqk_ref[:, A_WIDTH:2 * A_WIDTH]).astype(BF16)
    for r, glr in zip(parts, glrs):
        z = _dot(glr, wgu_ref[...]) + bg_ref[...]
        log_sig = jnp.minimum(z, 0.0) - jnp.log(1.0 + jnp.exp(-jnp.abs(z)))
        la_ref[r, :] = log_sig * (1.0 / G_GATE_NORM)
    for r, h in zip(parts, hs):
        vt_ref[:, r] = _dot(h, wva_ref[...]).astype(BF16).T
    for r, h in zip(parts, hs):
        qg_ref[r, :] = _dot(h, wg_ref[:, 0:kg_lo]) * (G_HEAD_K ** -0.5)
        kg_ref[r, :] = _dot(h, wg_ref[:, kg_lo:vg_lo])
    for r, h in zip(parts, hs):
        vg_ref[r, :] = _dot(h, wg_ref[:, vg_lo:vg_lo + G_WIDTH]).astype(BF16)
    for r, h in zip(parts, hs):
        rg_ref[r, :] = _dot(h, wrg_ref[...]).astype(BF16)


def _inproj(x2, g, wqk, wva, wg, wrg, wglr, wgu, bg):
    n = x2.shape[0]
    tm = INPROJ_TILE
    const = lambda a: pl.BlockSpec(a.shape, lambda i: (0, 0), pipeline_mode=pl.Buffered(1))
    row = lambda width: pl.BlockSpec((tm, width), lambda i: (i, 0))
    return pl.pallas_call(
        _inproj_kernel,
        grid=(n // tm,),
        in_specs=[row(D_MODEL)] + [const(a) for a in (g, wqk, wva, wg, wrg, wglr, wgu, bg)],
        out_specs=[row(2 * A_WIDTH), pl.BlockSpec((A_WIDTH, tm), lambda i: (0, i)),
                   row(G_KEY_WIDTH), row(G_KEY_WIDTH), row(G_KEY_WIDTH), row(G_WIDTH), row(G_WIDTH)],
        out_shape=[jax.ShapeDtypeStruct((n, 2 * A_WIDTH), BF16),
                   jax.ShapeDtypeStruct((A_WIDTH, n), BF16),
                   jax.ShapeDtypeStruct((n, G_KEY_WIDTH), F32),
                   jax.ShapeDtypeStruct((n, G_KEY_WIDTH), F32),
                   jax.ShapeDtypeStruct((n, G_KEY_WIDTH), F32),
                   jax.ShapeDtypeStruct((n, G_WIDTH), BF16),
                   jax.ShapeDtypeStruct((n, G_WIDTH), BF16)],
        compiler_params=pltpu.CompilerParams(dimension_semantics=("parallel",), vmem_limit_bytes=VMEM_LIMIT),
        name="inproj",
    )(x2, g, wqk, wva, wg, wrg, wglr, wgu, bg)


def _moba_kernel(c31_ref, q_ref, k_ref, vt_ref, bias_ref, o_ref, kext, s_scr, *, seq):
    p = pl.program_id(1)
    nblk = seq // MOBA_BLOCK
    half = LANES // 2

    k = k_ref[...]
    lane2 = lax.broadcasted_iota(jnp.int32, (seq, LANES), 1)
    lm0 = jnp.where(lane2 < half, 1.0, 0.0).astype(BF16)
    lm1 = jnp.where(lane2 >= half, 1.0, 0.0).astype(BF16)
    kext[0] = k * lm0
    kext[1] = k * lm1
    n_i = lax.broadcasted_iota(jnp.int32, (16, seq), 0)
    t_i = lax.broadcasted_iota(jnp.int32, (16, seq), 1)
    avg = jnp.where(t_i // MOBA_BLOCK == n_i, 1.0 / MOBA_BLOCK, 0.0).astype(BF16)
    km = _dot(avg, k)
    kmh = km.astype(BF16)
    kml = (km - kmh.astype(F32)).astype(BF16)

    lane_q = lax.broadcasted_iota(jnp.int32, (MOBA_BLOCK, LANES), 1)
    rowi = lax.broadcasted_iota(jnp.int32, (16, MOBA_BLOCK), 0)
    blk = lambda n: slice(n * MOBA_BLOCK, (n + 1) * MOBA_BLOCK)

    def prepare(qi, h, slot):
        qm = q_ref[blk(qi), :] * jnp.where((lane_q < half) if h == 0 else (lane_q >= half), 1.0, 0.0).astype(BF16)
        gate = _nt(kmh, qm) + _nt(kml, qm) if qi > MOBA_TOPK else None
        return dict(qi=qi, h=h, qm=qm, gate=gate, c31=c31_ref[2 * p + h] * LOG2E, sbuf=s_scr.at[slot], near=None,
                    far=None, acc=None)

    def penalty(u, n):
        return 0.0 if u["pen"] is None else u["pen"][n:n + 1, :]

    def score_matmul(u):
        u["s_all"] = _nt(kext[u["h"], 0:(u["qi"] + 1) * MOBA_BLOCK, :], u["qm"])

    def score_reduce(u):
        qi, h = u["qi"], u["h"]
        u["pen"] = None
        if u["gate"] is not None:
            g = u["gate"]
            cnt = jnp.zeros((16, MOBA_BLOCK), F32)
            for m in range(qi):
                gm = g[m:m + 1, :]
                cnt = cnt + jnp.where((gm > g) | ((gm == g) & (m < rowi)), 1.0, 0.0)
            u["pen"] = jnp.where((rowi < qi) & (cnt < MOBA_TOPK), 0.0, NEG)
        for n in range(qi + 1):
            s = u["s_all"][blk(n)]
            if n == qi:
                s = s + bias_ref[h, MOBA_BLOCK:2 * MOBA_BLOCK, :]
            elif n == qi - 1:
                s = s + bias_ref[h, 0:MOBA_BLOCK, :]
            u["sbuf"][blk(n), :] = s
            t = jnp.max(s.reshape(MOBA_BLOCK // 8, 8, MOBA_BLOCK), axis=0)
            if n < qi:
                t = t + penalty(u, n)
            key = "far" if n < qi - 1 else "near"
            u[key] = t if u[key] is None else jnp.maximum(u[key], t)
        mx = u["near"] if u["far"] is None else jnp.maximum(u["near"], u["far"] + u["c31"])
        u["m_near"] = jnp.max(mx, axis=0, keepdims=True)
        u["m_far"] = u["m_near"] - u["c31"]

    def prob_exp(u):
        qi = u["qi"]
        shift = lambda n: (u["m_far"] if n < qi - 1 else u["m_near"]) - (penalty(u, n) if n < qi else 0.0)
        u["pt"] = jnp.concatenate([jnp.exp2(u["sbuf"][blk(n), :] - shift(n)).astype(BF16) for n in range(qi + 1)],
                                  axis=0)

    def prob_matmul(u):
        qi, h = u["qi"], u["h"]
        width = (qi + 1) * MOBA_BLOCK
        vrows = vt_ref[half * h:half * (h + 1), 0:width]
        ones_rows = jnp.ones((half, width), BF16)
        vt = jnp.concatenate([vrows, ones_rows] if h == 0 else [ones_rows, vrows], axis=0)
        u["acc"] = _dot(vt, u["pt"])

    outs = {}

    def finish(u):
        acc, h = u["acc"], u["h"]
        num = acc[half * h:half * (h + 1)]
        den = acc[half * (1 - h):half * (1 - h) + 1]
        outs[h] = num / den
        if h == 1:
            o_ref[blk(u["qi"]), :] = jnp.concatenate([outs[0], outs[1]], axis=0).T.astype(BF16)

    group = 2
    stages = [[(qi, h) for qi in range(g * group, (g + 1) * group) for h in range(2)] for g in range(nblk // group)]
    prev = []
    for g, members in enumerate(stages + [[]]):
        cur = [prepare(qi, h, (g % 2) * 2 * group + 2 * (qi % group) + h) for qi, h in members]
        for u in prev:
            prob_exp(u)
        for u in cur:
            score_matmul(u)
        for u in prev:
            prob_matmul(u)
        for u in cur:
            score_reduce(u)
        for u in prev:
            finish(u)
        prev = cur


def _moba(qk, vt, bias, c31, batch, seq):
    n = qk.shape[0]
    npair = A_HEADS // 2
    kern = functools.partial(_moba_kernel, seq=seq)
    return pl.pallas_call(
        kern,
        grid=(batch, npair),
        in_specs=[pl.BlockSpec(memory_space=pltpu.SMEM),
                  pl.BlockSpec((seq, LANES), lambda b, p: (b, p)),
                  pl.BlockSpec((seq, LANES), lambda b, p: (b, npair + p)),
                  pl.BlockSpec((LANES, seq), lambda b, p: (p, b)),
                  pl.BlockSpec((2, 2 * MOBA_BLOCK, MOBA_BLOCK), lambda b, p: (p, 0, 0))],
        out_specs=pl.BlockSpec((seq, LANES), lambda b, p: (b, p)),
        out_shape=jax.ShapeDtypeStruct((n, A_WIDTH), BF16),
        scratch_shapes=[pltpu.VMEM((2, seq, LANES), BF16),
                        pltpu.VMEM((8, seq, MOBA_BLOCK), F32)],
        compiler_params=pltpu.CompilerParams(dimension_semantics=("parallel", "arbitrary"),
                                             vmem_limit_bytes=VMEM_LIMIT),
        name="moba",
    )(c31, qk, qk, vt, bias)


_GLA_LEVELS = int(math.log2(GLA_CHUNK))


_GLA_BIG_LEVELS = _GLA_LEVELS - 3


def _gla_weights():
    c = GLA_CHUNK
    i = np.arange(c)[:, None]
    j = np.arange(c)[None, :]
    mats = [(j <= i)]
    for lvl in range(_GLA_BIG_LEVELS, _GLA_LEVELS):
        s = (c // 2) >> lvl
        ref = (i // (2 * s)) * (2 * s) + s - 1
        mats.append(np.where(i > ref, (j > ref) & (j <= i), (j > i) & (j <= ref)))
    return np.concatenate(mats, axis=0).astype(np.float32)


def _gla_level_map():
    c = GLA_CHUNK
    i = np.arange(c)[:, None]
    j = np.arange(c)[None, :]
    top_bit = np.floor(np.log2(np.maximum(i ^ j, 1))).astype(np.int64)
    lvl = np.where(j < i, _GLA_LEVELS - 1 - top_bit, np.where(j == i, _GLA_LEVELS, _GLA_LEVELS + 1))
    return np.concatenate([lvl, lvl], axis=1).astype(np.int32)


def _gla_kernel(q_ref, k_ref, la_ref, v_ref, rg_ref, w_ref, lmap_ref, gn_ref, o_ref, st_ref):
    c = GLA_CHUNK
    half = LANES // 2
    npair = G_HEADS // 2

    @pl.when(pl.program_id(1) == 0)
    def _init():
        st_ref[...] = jnp.zeros(st_ref.shape, F32)

    w = w_ref[...]
    lmap = lmap_ref[...]
    lane = lax.broadcasted_iota(jnp.int32, (1, LANES), 1)
    lane_c = lax.broadcasted_iota(jnp.int32, (c, LANES), 1)
    lm = [jnp.where(lane_c < half, 1.0, 0.0).astype(BF16), jnp.where(lane_c >= half, 1.0, 0.0).astype(BF16)]
    row_small = lax.broadcasted_iota(jnp.int32, (c, G_KEY_WIDTH), 0)

    def level_factors(q, k, b, d_all, lvl):
        s = (c // 2) >> lvl
        if s < 8:
            j = lvl - _GLA_BIG_LEVELS
            e = jnp.exp2(d_all[c * (j + 1):c * (j + 2)])
            odd = (row_small // s) % 2 == 1
            return jnp.where(odd, q * e, 0.0), jnp.where(odd, 0.0, k * e)
        zeros = jnp.zeros((s, q.shape[1]), F32)
        qp, kp = [], []
        for blk in range(c // s):
            rows = slice(blk * s, (blk + 1) * s)
            ref = (blk // 2) * 2 * s + s - 1
            if blk % 2 == 1:
                qp.append(q[rows] * jnp.exp2(b[rows] - b[ref:ref + 1]))
                kp.append(zeros)
            else:
                qp.append(zeros)
                kp.append(k[rows] * jnp.exp2(b[ref:ref + 1] - b[rows]))
        return jnp.concatenate(qp, axis=0), jnp.concatenate(kp, axis=0)

    nch = GLA_STEP // c
    rows_of = [slice(ch * c, (ch + 1) * c) for ch in range(nch)]
    pair_lanes = [slice(LANES * p, LANES * (p + 1)) for p in range(npair)]
    d_alls = []
    for ch in range(nch):
        la = la_ref[rows_of[ch], :] * LOG2E
        hi = la.astype(BF16)
        lo = (la - hi.astype(F32)).astype(BF16)
        d_alls.append(_dot(w, hi) + _dot(w, lo))
    chunks = []
    for ch in range(nch):
        d_all = d_alls[ch]
        q = q_ref[rows_of[ch], :]
        k = k_ref[rows_of[ch], :]
        b = d_all[0:c]
        blast = b[c - 1:c, :]
        qts, kts = [], []
        for lvl in range(_GLA_LEVELS):
            qt, kt = level_factors(q, k, b, d_all, lvl)
            qts.append(qt.astype(BF16))
            kts.append(kt.astype(BF16))
        qts.append(q.astype(BF16))
        kts.append(k.astype(BF16))
        chunks.append(dict(qts=qts, kts=kts, qe=(q * jnp.exp2(b)).astype(BF16),
                           khat=(k * jnp.exp2(blast - b)).astype(BF16), dec=jnp.exp2(blast),
                           a=[jnp.zeros((c, 2 * c), F32)] * npair))
    for lvl in range(_GLA_LEVELS + 1):
        mask = lmap == lvl
        s = (c // 2) >> lvl if lvl < _GLA_BIG_LEVELS else c
        blocks = [slice(i * s, (i + 1) * s) for i in range(c // s)]
        live = [i for i in range(len(blocks)) if i % 2 == 1 or s == c]
        for u in chunks:
            for p, sl in enumerate(pair_lanes):
                kp = u["kts"][lvl][:, sl]
                ql = u["qts"][lvl]
                pm = _nt(jnp.concatenate([ql[blocks[i], sl] for i in live], axis=0),
                         jnp.concatenate([kp * lm[0], kp * lm[1]], axis=0))
                a = u["a"][p]
                pieces = [a[r] for r in blocks]
                for n, i in enumerate(live):
                    pieces[i] = jnp.where(mask[blocks[i]], pm[n * s:(n + 1) * s], pieces[i])
                u["a"] = [jnp.concatenate(pieces, axis=0) if j == p else v for j, v in enumerate(u["a"])]
    states = [st_ref[p] for p in range(npair)]
    for ch, u in enumerate(chunks):
        rows = rows_of[ch]
        for p, sl in enumerate(pair_lanes):
            stb = states[p].astype(BF16)
            ups = []
            for hh in range(2):
                h = 2 * p + hh
                vh = v_ref[rows, LANES * h:LANES * (h + 1)]
                o = _dot(u["a"][p][:, hh * c:(hh + 1) * c].astype(BF16), vh) + _nt(u["qe"][:, sl] * lm[hh], stb)
                y = _rms(o, gn_ref[...])
                rg = rg_ref[rows, LANES * h:LANES * (h + 1)].astype(F32)
                y = y * (rg / (1.0 + jnp.exp(-rg)))
                o_ref[rows, LANES * h:LANES * (h + 1)] = y.astype(BF16)
                ups.append(_tn(vh, u["khat"][:, sl]))
            states[p] = states[p] * u["dec"][:, sl] + jnp.where(lane < half, ups[0], ups[1])
    for p in range(npair):
        st_ref[p] = states[p]


def _gla(qg, kg, la, vg, rg, gn, batch, seq):
    n = qg.shape[0]
    nchunk = seq // GLA_STEP
    w = jnp.asarray(_gla_weights(), BF16)
    lmap = jnp.asarray(_gla_level_map())
    row = lambda width: pl.BlockSpec((GLA_STEP, width), lambda b, c: (b * nchunk + c, 0))
    const = functools.partial(pl.BlockSpec, pipeline_mode=pl.Buffered(1))
    return pl.pallas_call(
        _gla_kernel,
        grid=(batch, nchunk),
        in_specs=[row(G_KEY_WIDTH), row(G_KEY_WIDTH), row(G_KEY_WIDTH), row(G_WIDTH), row(G_WIDTH),
                  const(w.shape, lambda b, c: (0, 0)),
                  const(lmap.shape, lambda b, c: (0, 0)),
                  const((1, G_HEAD_V), lambda b, c: (0, 0))],
        out_specs=row(G_WIDTH),
        out_shape=jax.ShapeDtypeStruct((n, G_WIDTH), BF16),
        scratch_shapes=[pltpu.VMEM((G_HEADS // 2, G_HEAD_V, LANES), F32)],
        compiler_params=pltpu.CompilerParams(dimension_semantics=("parallel", "arbitrary"),
                                             vmem_limit_bytes=VMEM_LIMIT),
        name="gla",
    )(qg, kg, la, vg, rg, w, lmap, gn)


def _memkv_kernel(m_ref, g_ref, w_ref, k_ref, v_ref):
    nb, mlen, _ = m_ref.shape
    h = _rms(m_ref[...].reshape(nb * mlen, D_MODEL), g_ref[...]).astype(BF16)
    k_ref[...] = _dot(h, w_ref[:, 0:D_MODEL]).astype(BF16).reshape(nb, mlen, D_MODEL)
    v_ref[...] = _dot(h, w_ref[:, D_MODEL:2 * D_MODEL]).astype(BF16).reshape(nb, mlen, D_MODEL)


def _memkv(mem, g, w):
    batch, mlen, _ = mem.shape
    nb = 2 if batch % 2 == 0 else 1
    const = functools.partial(pl.BlockSpec, pipeline_mode=pl.Buffered(1))
    blk = pl.BlockSpec((nb, mlen, D_MODEL), lambda b: (b, 0, 0))
    return pl.pallas_call(
        _memkv_kernel,
        grid=(batch // nb,),
        in_specs=[blk, const((1, D_MODEL), lambda b: (0, 0)), const((D_MODEL, 2 * D_MODEL), lambda b: (0, 0))],
        out_specs=[blk, blk],
        out_shape=[jax.ShapeDtypeStruct(mem.shape, BF16)] * 2,
        compiler_params=pltpu.CompilerParams(dimension_semantics=("parallel",), vmem_limit_bytes=VMEM_LIMIT),
        name="memkv",
    )(mem, g, w)


def _post_kernel(x_ref, oa_ref, og_ref, wo_ref, gx_ref, wq_ref, kx_ref, vx_ref, wxo_ref, o_ref):
    tm = x_ref.shape[0]
    halves = [slice(i * tm // POST_PARTS, (i + 1) * tm // POST_PARTS) for i in range(POST_PARTS)]
    x1 = [x_ref[r, :] + _dot(oa_ref[r, :], wo_ref[0:A_WIDTH, :]) + _dot(og_ref[r, :], wo_ref[A_WIDTH:D_MODEL, :])
          for r in halves]
    h = [_rms(v, gx_ref[...]).astype(BF16) for v in x1]
    qx = [(_dot(v, wq_ref[...]) * (LOG2E * X_HEAD_DIM ** -0.5)).astype(BF16) for v in h]
    outs = [[] for _ in halves]
    for hd in range(X_HEADS):
        sl = slice(X_HEAD_DIM * hd, X_HEAD_DIM * (hd + 1))
        s = [_nt(q[:, sl], kx_ref[0, :, sl]) for q in qx]
        pexp = [jnp.exp2(v - jnp.max(v, axis=1, keepdims=True)) for v in s]
        pn = [(v / jnp.sum(v, axis=1, keepdims=True)).astype(BF16) for v in pexp]
        for i, v in enumerate(pn):
            outs[i].append(_dot(v, vx_ref[0, :, sl]).astype(BF16))
    for i, r in enumerate(halves):
        o_ref[r, :] = x1[i] + _dot(jnp.concatenate(outs[i], axis=1), wxo_ref[...])


def _post(x2, oa, og, wo, gx, wq, kx, vx, wxo, seq):
    n = x2.shape[0]
    tm = POST_TILE
    per_b = seq // tm
    mlen = kx.shape[1]
    const = functools.partial(pl.BlockSpec, pipeline_mode=pl.Buffered(1))
    row = lambda width: pl.BlockSpec((tm, width), lambda i: (i, 0))
    mem = pl.BlockSpec((1, mlen, D_MODEL), lambda i: (i // per_b, 0, 0))
    sq = const((D_MODEL, D_MODEL), lambda i: (0, 0))
    return pl.pallas_call(
        _post_kernel,
        grid=(n // tm,),
        in_specs=[row(D_MODEL), row(A_WIDTH), row(G_WIDTH), sq, const((1, D_MODEL), lambda i: (0, 0)), sq, mem, mem, sq],
        out_specs=row(D_MODEL),
        out_shape=jax.ShapeDtypeStruct((n, D_MODEL), F32),
        compiler_params=pltpu.CompilerParams(dimension_semantics=("parallel",), vmem_limit_bytes=VMEM_LIMIT),
        name="post",
    )(x2, oa, og, wo, gx, wq, kx, vx, wxo)


def _mlp_kernel(x_ref, g_ref, wu_ref, wd_ref, gf_ref, o_ref, *, final_norm):
    tm = x_ref.shape[0]
    parts = [slice(i * tm // MLP_PARTS, (i + 1) * tm // MLP_PARTS) for i in range(MLP_PARTS)]
    acc = [x_ref[r, :] for r in parts]
    h = [_rms(x, g_ref[...]).astype(BF16) for x in acc]
    for c in range(D_FF // D_MODEL):
        sl = slice(D_MODEL * c, D_MODEL * (c + 1))
        u = [jnp.square(jnp.maximum(_dot(v, wu_ref[:, sl]), 0.0)).astype(BF16) for v in h]
        acc = [a + _dot(v, wd_ref[sl, :]) for a, v in zip(acc, u)]
    for r, a in zip(parts, acc):
        o_ref[r, :] = _rms(a, gf_ref[...]) if final_norm else a


def _mlp(x2, g, wu, wd, gf, final_norm):
    n = x2.shape[0]
    tm = MLP_TILE
    const = functools.partial(pl.BlockSpec, pipeline_mode=pl.Buffered(1))
    row = pl.BlockSpec((tm, D_MODEL), lambda i: (i, 0))
    vec = const((1, D_MODEL), lambda i: (0, 0))
    return pl.pallas_call(
        functools.partial(_mlp_kernel, final_norm=final_norm),
        grid=(n // tm,),
        in_specs=[row, vec, const((D_MODEL, D_FF), lambda i: (0, 0)), const((D_FF, D_MODEL), lambda i: (0, 0)), vec],
        out_specs=row,
        out_shape=jax.ShapeDtypeStruct((n, D_MODEL), F32),
        compiler_params=pltpu.CompilerParams(dimension_semantics=("parallel",), vmem_limit_bytes=VMEM_LIMIT),
        name="mlp",
    )(x2, g, wu, wd, gf)


def kernel(x, mem, rp_table, norm_mix, w_in, w_gate_up, b_gate, g_norm, w_out, norm_xattn, norm_mem, w_xq, w_xkv,
           w_xo, norm_mlp, w_up, w_down, norm_final):
    batch, seq, _ = x.shape
    depth = w_in.shape[0]
    assert seq % max(INPROJ_TILE, POST_TILE, MLP_TILE, GLA_STEP) == 0 and seq // MOBA_BLOCK <= 8
    x2 = x.reshape(batch * seq, D_MODEL)
    bias = _bias_tiles(rp_table)
    c31 = rp_table[RP_BUCKETS - 1]
    va_lo = 2 * A_WIDTH
    g_lo = va_lo + A_WIDTH
    glr_lo = g_lo + 2 * G_KEY_WIDTH + G_WIDTH
    rg_lo = glr_lo + G_GATE_RANK
    pad_rank = LANES - G_GATE_RANK
    for l in range(depth):
        col = lambda lo, hi: w_in[l, :, lo:hi].astype(BF16)
        wglr = jnp.pad(col(glr_lo, rg_lo), ((0, 0), (0, pad_rank)))
        wgu = jnp.pad(w_gate_up[l].astype(BF16), ((0, pad_rank), (0, 0)))
        qk, vt, qg, kg, la, vg, rg = _inproj(x2, norm_mix[l][None], col(0, va_lo), col(va_lo, g_lo), col(g_lo, glr_lo),
                                             col(rg_lo, rg_lo + G_WIDTH), wglr, wgu, b_gate[l][None])
        oa = _moba(qk, vt, bias, c31, batch, seq)
        og = _gla(qg, kg, la, vg, rg, g_norm[l][None], batch, seq)
        kx, vx = _memkv(mem, norm_mem[l][None], w_xkv[l].astype(BF16))
        x2 = _post(x2, oa, og, w_out[l].astype(BF16), norm_xattn[l][None], w_xq[l].astype(BF16), kx, vx,
                   w_xo[l].astype(BF16), seq)
        last = l == depth - 1
        x2 = _mlp(x2, norm_mlp[l][None], w_up[l].astype(BF16), w_down[l].astype(BF16), norm_final[None], last)
    return x2.reshape(batch, seq, D_MODEL)
```

```python
import functools
import math

import numpy as np
import jax
import jax.numpy as jnp
from jax import lax
from jax.experimental import pallas as pl
from jax.experimental.pallas import tpu as pltpu

F32 = jnp.float32
BF16 = jnp.bfloat16

D_MODEL = 1024
A_HEADS = 8
A_HEAD_DIM = 64
A_WIDTH = A_HEADS * A_HEAD_DIM
MOBA_BLOCK = 256
MOBA_TOPK = 3
G_HEADS = 4
G_WIDTH = D_MODEL - A_WIDTH
G_HEAD_V = G_WIDTH // G_HEADS
G_KEY_WIDTH = G_WIDTH // 2
G_HEAD_K = G_KEY_WIDTH // G_HEADS
G_GATE_RANK = 16
G_GATE_NORM = 16.0
X_HEADS = 4
X_HEAD_DIM = D_MODEL // X_HEADS
D_FF = 4 * D_MODEL
RP_BUCKETS = 32
RP_MAX_DIST = 128
EPS = 1e-6

LANES = 128
SUBLANES = 8
GATE_ROWS = 16
NEG = -1e30
GLA_CHUNK = 128
GLA_STEP = 1024
LOG2E = float(np.log2(np.e))
INPROJ_TILE = 1024
INPROJ_PARTS = 2
MLP_TILE = 1024
MLP_PARTS = 2
POST_TILE = 1024
POST_PARTS = 2
VMEM_LIMIT = 56 * 1024 * 1024

def _nt(a, b):
    return lax.dot_general(a, b, (((1,), (1,)), ((), ())), preferred_element_type=F32)


def _tn(a, b):
    return lax.dot_general(a, b, (((0,), (0,)), ((), ())), preferred_element_type=F32)


def _dot(a, b):
    return jnp.dot(a, b, preferred_element_type=F32)


def _rms(x, g):
    return x * lax.rsqrt(jnp.mean(x * x, axis=-1, keepdims=True) + EPS) * g


def _bucket_thresholds():
    max_exact = RP_BUCKETS // 2
    d = np.arange(1, 4 * RP_MAX_DIST)
    val = (np.log(d.astype(np.float32) / np.float32(max_exact)) / np.float32(math.log(RP_MAX_DIST / max_exact))
           * np.float32(RP_BUCKETS - max_exact))
    bucket = np.minimum(max_exact + val.astype(np.int32), RP_BUCKETS - 1)
    return [int(d[(d >= max_exact) & (bucket >= max_exact + k)][0]) for k in range(1, RP_BUCKETS - max_exact)]


_THRESHOLDS = _bucket_thresholds()


def _bias_kernel(tab_ref, out_ref):
    h = pl.program_id(0)
    rows, width = 2 * MOBA_BLOCK, 3 * MOBA_BLOCK
    d = lax.broadcasted_iota(jnp.int32, (8, width), 1)
    max_exact = RP_BUCKETS // 2
    bucket = jnp.where(d < max_exact, d, max_exact)
    for t in _THRESHOLDS:
        bucket = bucket + jnp.where(d >= t, 1, 0)
    prof = jnp.full((8, width), NEG, F32)
    for b in range(RP_BUCKETS):
        prof = jnp.where(bucket == b, tab_ref[b, h] * LOG2E, prof)
    prof = jnp.where(d < rows, prof, NEG)
    tile = pltpu.roll(jnp.broadcast_to(prof[0:1], (rows, width)), 0, 1, stride=1, stride_axis=0)
    out_ref[0] = tile[:, MOBA_BLOCK:2 * MOBA_BLOCK]


def _bias_tiles(rp_table):
    return pl.pallas_call(
        _bias_kernel,
        grid=(A_HEADS,),
        in_specs=[pl.BlockSpec(memory_space=pltpu.SMEM)],
        out_specs=pl.BlockSpec((1, 2 * MOBA_BLOCK, MOBA_BLOCK), lambda h: (h, 0, 0)),
        out_shape=jax.ShapeDtypeStruct((A_HEADS, 2 * MOBA_BLOCK, MOBA_BLOCK), F32),
        name="bias",
    )(rp_table)


def _inproj_kernel(x_ref, g_ref, w_ref, wrg_ref, wglr_ref, wgu_ref, bg_ref,
                   qk_ref, vt_ref, qg_ref, kg_ref, la_ref, vg_ref, rg_ref):
    tm = x_ref.shape[0]
    parts = [slice(i * tm // INPROJ_PARTS, (i + 1) * tm // INPROJ_PARTS) for i in range(INPROJ_PARTS)]
    hs = [_rms(x_ref[r, :], g_ref[...]).astype(BF16) for r in parts]
    va_lo = 2 * A_WIDTH
    qg_lo = va_lo + A_WIDTH
    kg_lo = qg_lo + G_KEY_WIDTH
    vg_lo = kg_lo + G_KEY_WIDTH

    glrs = [_dot(h, wglr_ref[...]).astype(BF16) for h in hs]
    for r, h in zip(parts, hs):
        qk_ref[r, 0:A_WIDTH] = (_dot(h, w_ref[:, 0:A_WIDTH]) * (LOG2E * A_HEAD_DIM ** -0.5)).astype(BF16)
    for r, h in zip(parts, hs):
        qk_ref[r, A_WIDTH:va_lo] = _dot(h, w_ref[:, A_WIDTH:va_lo]).astype(BF16)
    for r, glr in zip(parts, glrs):
        z = _dot(glr, wgu_ref[...]) + bg_ref[...]
        log_sig = jnp.minimum(z, 0.0) - jnp.log(1.0 + jnp.exp(-jnp.abs(z)))
        la_ref[r, :] = log_sig * (1.0 / G_GATE_NORM)
    for r, h in zip(parts, hs):
        vt_ref[:, r] = _dot(h, w_ref[:, va_lo:qg_lo]).astype(BF16).T
    for r, h in zip(parts, hs):
        qg_ref[r, :] = _dot(h, w_ref[:, qg_lo:kg_lo]) * (G_HEAD_K ** -0.5)
        kg_ref[r, :] = _dot(h, w_ref[:, kg_lo:vg_lo])
    for r, h in zip(parts, hs):
        vg_ref[r, :] = _dot(h, w_ref[:, vg_lo:vg_lo + G_WIDTH]).astype(BF16)
    for r, h in zip(parts, hs):
        rg_ref[r, :] = _dot(h, wrg_ref[...])


def _inproj(x2, g, w, wrg, wglr, wgu, bg):
    n = x2.shape[0]
    tm = INPROJ_TILE
    const = lambda a: pl.BlockSpec(a.shape, lambda i: (0, 0), pipeline_mode=pl.Buffered(1))
    row = lambda width: pl.BlockSpec((tm, width), lambda i: (i, 0))
    return pl.pallas_call(
        _inproj_kernel,
        grid=(n // tm,),
        in_specs=[row(D_MODEL)] + [const(a) for a in (g, w, wrg, wglr, wgu, bg)],
        out_specs=[row(2 * A_WIDTH), pl.BlockSpec((A_WIDTH, tm), lambda i: (0, i)),
                   row(G_KEY_WIDTH), row(G_KEY_WIDTH), row(G_KEY_WIDTH), row(G_WIDTH), row(G_WIDTH)],
        out_shape=[jax.ShapeDtypeStruct((n, 2 * A_WIDTH), BF16),
                   jax.ShapeDtypeStruct((A_WIDTH, n), BF16),
                   jax.ShapeDtypeStruct((n, G_KEY_WIDTH), F32),
                   jax.ShapeDtypeStruct((n, G_KEY_WIDTH), F32),
                   jax.ShapeDtypeStruct((n, G_KEY_WIDTH), F32),
                   jax.ShapeDtypeStruct((n, G_WIDTH), BF16),
                   jax.ShapeDtypeStruct((n, G_WIDTH), F32)],
        compiler_params=pltpu.CompilerParams(dimension_semantics=("parallel",), vmem_limit_bytes=VMEM_LIMIT),
        name="inproj",
    )(x2, g, w, wrg, wglr, wgu, bg)


def _moba_kernel(c31_ref, q_ref, k_ref, vt_ref, bias_ref, o_ref, khead, s_scr, *, seq):
    p = pl.program_id(1)
    nblk = seq // MOBA_BLOCK
    half = LANES // 2

    k = k_ref[...]
    lane2 = lax.broadcasted_iota(jnp.int32, (seq, LANES), 1)
    lm0 = jnp.where(lane2 < half, 1.0, 0.0).astype(BF16)
    lm1 = jnp.where(lane2 >= half, 1.0, 0.0).astype(BF16)
    khead[0] = k * lm0
    khead[1] = k * lm1
    n_i = lax.broadcasted_iota(jnp.int32, (GATE_ROWS, seq), 0)
    t_i = lax.broadcasted_iota(jnp.int32, (GATE_ROWS, seq), 1)
    avg = jnp.where(t_i // MOBA_BLOCK == n_i, 1.0 / MOBA_BLOCK, 0.0).astype(BF16)
    km = _dot(avg, k)
    kmh = km.astype(BF16)
    kml = (km - kmh.astype(F32)).astype(BF16)

    lane_q = lax.broadcasted_iota(jnp.int32, (MOBA_BLOCK, LANES), 1)
    rowi = lax.broadcasted_iota(jnp.int32, (GATE_ROWS, MOBA_BLOCK), 0)
    blk = lambda n: slice(n * MOBA_BLOCK, (n + 1) * MOBA_BLOCK)

    def prepare(qi, h, slot):
        qm = q_ref[blk(qi), :] * jnp.where((lane_q < half) if h == 0 else (lane_q >= half), 1.0, 0.0).astype(BF16)
        gate = _nt(kmh, qm) + _nt(kml, qm) if qi > MOBA_TOPK else None
        return dict(qi=qi, h=h, qm=qm, gate=gate, c31=c31_ref[2 * p + h] * LOG2E, sbuf=s_scr.at[slot], near=None,
                    far=None, acc=None)

    def penalty(u, n):
        return 0.0 if u["pen"] is None else u["pen"][n:n + 1, :]

    def score_matmul(u):
        u["s_all"] = _nt(khead[u["h"], 0:(u["qi"] + 1) * MOBA_BLOCK, :], u["qm"])

    def score_reduce(u):
        qi, h = u["qi"], u["h"]
        u["pen"] = None
        if u["gate"] is not None:
            g = u["gate"]
            cnt = jnp.zeros((GATE_ROWS, MOBA_BLOCK), F32)
            for m in range(qi):
                gm = g[m:m + 1, :]
                cnt = cnt + jnp.where((gm > g) | ((gm == g) & (m < rowi)), 1.0, 0.0)
            u["pen"] = jnp.where((rowi < qi) & (cnt < MOBA_TOPK), 0.0, NEG)
        for n in range(qi + 1):
            s = u["s_all"][blk(n)]
            if n == qi:
                s = s + bias_ref[h, MOBA_BLOCK:2 * MOBA_BLOCK, :]
            elif n == qi - 1:
                s = s + bias_ref[h, 0:MOBA_BLOCK, :]
            u["sbuf"][blk(n), :] = s
            t = jnp.max(s.reshape(MOBA_BLOCK // SUBLANES, SUBLANES, MOBA_BLOCK), axis=0)
            if n < qi:
                t = t + penalty(u, n)
            key = "far" if n < qi - 1 else "near"
            u[key] = t if u[key] is None else jnp.maximum(u[key], t)
        mx = u["near"] if u["far"] is None else jnp.maximum(u["near"], u["far"] + u["c31"])
        u["m_near"] = jnp.max(mx, axis=0, keepdims=True)
        u["m_far"] = u["m_near"] - u["c31"]

    def prob_exp(u):
        qi = u["qi"]
        shift = lambda n: (u["m_far"] if n < qi - 1 else u["m_near"]) - (penalty(u, n) if n < qi else 0.0)
        u["pt"] = jnp.concatenate([jnp.exp2(u["sbuf"][blk(n), :] - shift(n)).astype(BF16) for n in range(qi + 1)],
                                  axis=0)

    def prob_matmul(u):
        qi, h = u["qi"], u["h"]
        width = (qi + 1) * MOBA_BLOCK
        vrows = vt_ref[half * h:half * (h + 1), 0:width]
        ones_rows = jnp.ones((half, width), BF16)
        vt = jnp.concatenate([vrows, ones_rows] if h == 0 else [ones_rows, vrows], axis=0)
        u["acc"] = _dot(vt, u["pt"])

    outs = {}

    def finish(u):
        acc, h = u["acc"], u["h"]
        num = acc[half * h:half * (h + 1)]
        den = acc[half * (1 - h):half * (1 - h) + 1]
        outs[h] = num / den
        if h == 1:
            o_ref[blk(u["qi"]), :] = jnp.concatenate([outs[0], outs[1]], axis=0).T.astype(BF16)

    group = 2
    stages = [[(qi, h) for qi in range(g * group, (g + 1) * group) for h in range(2)] for g in range(nblk // group)]
    prev = []
    for g, members in enumerate(stages + [[]]):
        cur = [prepare(qi, h, (g % 2) * 2 * group + 2 * (qi % group) + h) for qi, h in members]
        for u in prev:
            prob_exp(u)
        for u in cur:
            score_matmul(u)
        for u in prev:
            prob_matmul(u)
        for u in cur:
            score_reduce(u)
        for u in prev:
            finish(u)
        prev = cur


def _moba(qk, vt, bias, c31, batch, seq):
    n = qk.shape[0]
    npair = A_HEADS // 2
    kern = functools.partial(_moba_kernel, seq=seq)
    return pl.pallas_call(
        kern,
        grid=(batch, npair),
        in_specs=[pl.BlockSpec(memory_space=pltpu.SMEM),
                  pl.BlockSpec((seq, LANES), lambda b, p: (b, p)),
                  pl.BlockSpec((seq, LANES), lambda b, p: (b, npair + p)),
                  pl.BlockSpec((LANES, seq), lambda b, p: (p, b)),
                  pl.BlockSpec((2, 2 * MOBA_BLOCK, MOBA_BLOCK), lambda b, p: (p, 0, 0))],
        out_specs=pl.BlockSpec((seq, LANES), lambda b, p: (b, p)),
        out_shape=jax.ShapeDtypeStruct((n, A_WIDTH), BF16),
        scratch_shapes=[pltpu.VMEM((2, seq, LANES), BF16),
                        pltpu.VMEM((8, seq, MOBA_BLOCK), F32)],
        compiler_params=pltpu.CompilerParams(dimension_semantics=("parallel", "arbitrary"),
                                             vmem_limit_bytes=VMEM_LIMIT),
        name="moba",
    )(c31, qk, qk, vt, bias)


_GLA_LEVELS = int(math.log2(GLA_CHUNK))


_GLA_BIG_LEVELS = _GLA_LEVELS - int(math.log2(SUBLANES))


def _gla_weights():
    c = GLA_CHUNK
    i = np.arange(c)[:, None]
    j = np.arange(c)[None, :]
    mats = [(j <= i)]
    for lvl in range(_GLA_BIG_LEVELS, _GLA_LEVELS):
        s = (c // 2) >> lvl
        ref = (i // (2 * s)) * (2 * s) + s - 1
        mats.append(np.where(i > ref, (j > ref) & (j <= i), (j > i) & (j <= ref)))
    return np.concatenate(mats, axis=0).astype(np.float32)


def _gla_level_map():
    c = GLA_CHUNK
    i = np.arange(c)[:, None]
    j = np.arange(c)[None, :]
    top_bit = np.floor(np.log2(np.maximum(i ^ j, 1))).astype(np.int64)
    lvl = np.where(j < i, _GLA_LEVELS - 1 - top_bit, np.where(j == i, _GLA_LEVELS, _GLA_LEVELS + 1))
    return np.concatenate([lvl, lvl], axis=1).astype(np.int32)


def _gla_kernel(q_ref, k_ref, la_ref, v_ref, rg_ref, w_ref, lmap_ref, gn_ref, o_ref, st_ref):
    c = GLA_CHUNK
    half = LANES // 2
    npair = G_HEADS // 2

    @pl.when(pl.program_id(1) == 0)
    def _init():
        st_ref[...] = jnp.zeros(st_ref.shape, F32)

    w = w_ref[...]
    lmap = lmap_ref[...]
    lane = lax.broadcasted_iota(jnp.int32, (1, LANES), 1)
    lane_c = lax.broadcasted_iota(jnp.int32, (c, LANES), 1)
    lm = [jnp.where(lane_c < half, 1.0, 0.0).astype(BF16), jnp.where(lane_c >= half, 1.0, 0.0).astype(BF16)]
    row_small = lax.broadcasted_iota(jnp.int32, (c, G_KEY_WIDTH), 0)

    def level_factors(q, k, b, d_all, lvl):
        s = (c // 2) >> lvl
        if s < SUBLANES:
            j = lvl - _GLA_BIG_LEVELS
            e = jnp.exp2(d_all[c * (j + 1):c * (j + 2)])
            odd = (row_small // s) % 2 == 1
            return jnp.where(odd, q * e, 0.0), jnp.where(odd, 0.0, k * e)
        zeros = jnp.zeros((s, q.shape[1]), F32)
        qp, kp = [], []
        for blk in range(c // s):
            rows = slice(blk * s, (blk + 1) * s)
            ref = (blk // 2) * 2 * s + s - 1
            if blk % 2 == 1:
                qp.append(q[rows] * jnp.exp2(b[rows] - b[ref:ref + 1]))
                kp.append(zeros)
            else:
                qp.append(zeros)
                kp.append(k[rows] * jnp.exp2(b[ref:ref + 1] - b[rows]))
        return jnp.concatenate(qp, axis=0), jnp.concatenate(kp, axis=0)

    nch = GLA_STEP // c
    rows_of = [slice(ch * c, (ch + 1) * c) for ch in range(nch)]
    pair_lanes = [slice(LANES * p, LANES * (p + 1)) for p in range(npair)]
    d_alls = []
    for ch in range(nch):
        la = la_ref[rows_of[ch], :] * LOG2E
        hi = la.astype(BF16)
        lo = (la - hi.astype(F32)).astype(BF16)
        d_alls.append(_dot(w, hi) + _dot(w, lo))
    chunks = []
    for ch in range(nch):
        d_all = d_alls[ch]
        q = q_ref[rows_of[ch], :]
        k = k_ref[rows_of[ch], :]
        b = d_all[0:c]
        blast = b[c - 1:c, :]
        qts, kts = [], []
        for lvl in range(_GLA_LEVELS):
            qt, kt = level_factors(q, k, b, d_all, lvl)
            qts.append(qt.astype(BF16))
            kts.append(kt.astype(BF16))
        qts.append(q.astype(BF16))
        kts.append(k.astype(BF16))
        chunks.append(dict(qts=qts, kts=kts, qe=(q * jnp.exp2(b)).astype(BF16),
                           khat=(k * jnp.exp2(blast - b)).astype(BF16), dec=jnp.exp2(blast),
                           a=[jnp.zeros((c, 2 * c), F32)] * npair))
    for lvl in range(_GLA_LEVELS + 1):
        mask = lmap == lvl
        s = (c // 2) >> lvl if lvl < _GLA_BIG_LEVELS else c
        blocks = [slice(i * s, (i + 1) * s) for i in range(c // s)]
        live = [i for i in range(len(blocks)) if i % 2 == 1 or s == c]
        for u in chunks:
            for p, sl in enumerate(pair_lanes):
                kp = u["kts"][lvl][:, sl]
                ql = u["qts"][lvl]
                pm = _nt(jnp.concatenate([ql[blocks[i], sl] for i in live], axis=0),
                         jnp.concatenate([kp * lm[0], kp * lm[1]], axis=0))
                a = u["a"][p]
                pieces = [a[r] for r in blocks]
                for n, i in enumerate(live):
                    pieces[i] = jnp.where(mask[blocks[i]], pm[n * s:(n + 1) * s], pieces[i])
                u["a"] = [jnp.concatenate(pieces, axis=0) if j == p else v for j, v in enumerate(u["a"])]
    states = [st_ref[p] for p in range(npair)]
    for ch, u in enumerate(chunks):
        rows = rows_of[ch]
        for p, sl in enumerate(pair_lanes):
            stb = states[p].astype(BF16)
            ups = []
            for hh in range(2):
                h = 2 * p + hh
                vh = v_ref[rows, LANES * h:LANES * (h + 1)]
                o = _dot(u["a"][p][:, hh * c:(hh + 1) * c].astype(BF16), vh) + _nt(u["qe"][:, sl] * lm[hh], stb)
                y = _rms(o, gn_ref[...])
                rg = rg_ref[rows, LANES * h:LANES * (h + 1)]
                y = y * (rg / (1.0 + jnp.exp(-rg)))
                o_ref[rows, LANES * h:LANES * (h + 1)] = y.astype(BF16)
                ups.append(_tn(vh, u["khat"][:, sl]))
            states[p] = states[p] * u["dec"][:, sl] + jnp.where(lane < half, ups[0], ups[1])
    for p in range(npair):
        st_ref[p] = states[p]


def _gla(qg, kg, la, vg, rg, gn, batch, seq):
    n = qg.shape[0]
    nchunk = seq // GLA_STEP
    w = jnp.asarray(_gla_weights(), BF16)
    lmap = jnp.asarray(_gla_level_map())
    row = lambda width: pl.BlockSpec((GLA_STEP, width), lambda b, c: (b * nchunk + c, 0))
    const = functools.partial(pl.BlockSpec, pipeline_mode=pl.Buffered(1))
    return pl.pallas_call(
        _gla_kernel,
        grid=(batch, nchunk),
        in_specs=[row(G_KEY_WIDTH), row(G_KEY_WIDTH), row(G_KEY_WIDTH), row(G_WIDTH), row(G_WIDTH),
                  const(w.shape, lambda b, c: (0, 0)),
                  const(lmap.shape, lambda b, c: (0, 0)),
                  const((1, G_HEAD_V), lambda b, c: (0, 0))],
        out_specs=row(G_WIDTH),
        out_shape=jax.ShapeDtypeStruct((n, G_WIDTH), BF16),
        scratch_shapes=[pltpu.VMEM((G_HEADS // 2, G_HEAD_V, LANES), F32)],
        compiler_params=pltpu.CompilerParams(dimension_semantics=("parallel", "arbitrary"),
                                             vmem_limit_bytes=VMEM_LIMIT),
        name="gla",
    )(qg, kg, la, vg, rg, w, lmap, gn)


def _memkv_kernel(m_ref, g_ref, w_ref, k_ref, v_ref):
    nb, mlen, _ = m_ref.shape
    h = _rms(m_ref[...].reshape(nb * mlen, D_MODEL), g_ref[...]).astype(BF16)
    k_ref[...] = _dot(h, w_ref[:, 0:D_MODEL]).astype(BF16).reshape(nb, mlen, D_MODEL)
    v_ref[...] = _dot(h, w_ref[:, D_MODEL:2 * D_MODEL]).astype(BF16).reshape(nb, mlen, D_MODEL)


def _memkv(mem, g, w):
    batch, mlen, _ = mem.shape
    nb = 2 if batch % 2 == 0 else 1
    const = functools.partial(pl.BlockSpec, pipeline_mode=pl.Buffered(1))
    blk = pl.BlockSpec((nb, mlen, D_MODEL), lambda b: (b, 0, 0))
    return pl.pallas_call(
        _memkv_kernel,
        grid=(batch // nb,),
        in_specs=[blk, const((1, D_MODEL), lambda b: (0, 0)), const((D_MODEL, 2 * D_MODEL), lambda b: (0, 0))],
        out_specs=[blk, blk],
        out_shape=[jax.ShapeDtypeStruct(mem.shape, BF16)] * 2,
        compiler_params=pltpu.CompilerParams(dimension_semantics=("parallel",), vmem_limit_bytes=VMEM_LIMIT),
        name="memkv",
    )(mem, g, w)


def _post_kernel(x_ref, oa_ref, og_ref, wo_ref, gx_ref, wq_ref, kx_ref, vx_ref, wxo_ref, o_ref):
    tm = x_ref.shape[0]
    halves = [slice(i * tm // POST_PARTS, (i + 1) * tm // POST_PARTS) for i in range(POST_PARTS)]
    x1 = [x_ref[r, :] + _dot(oa_ref[r, :], wo_ref[0:A_WIDTH, :]) + _dot(og_ref[r, :], wo_ref[A_WIDTH:D_MODEL, :])
          for r in halves]
    h = [_rms(v, gx_ref[...]).astype(BF16) for v in x1]
    qx = [(_dot(v, wq_ref[...]) * (LOG2E * X_HEAD_DIM ** -0.5)).astype(BF16) for v in h]
    outs = [[] for _ in halves]
    for hd in range(X_HEADS):
        sl = slice(X_HEAD_DIM * hd, X_HEAD_DIM * (hd + 1))
        s = [_nt(q[:, sl], kx_ref[0, :, sl]) for q in qx]
        pexp = [jnp.exp2(v - jnp.max(v, axis=1, keepdims=True)) for v in s]
        pn = [(v / jnp.sum(v, axis=1, keepdims=True)).astype(BF16) for v in pexp]
        for i, v in enumerate(pn):
            outs[i].append(_dot(v, vx_ref[0, :, sl]).astype(BF16))
    for i, r in enumerate(halves):
        o_ref[r, :] = x1[i] + _dot(jnp.concatenate(outs[i], axis=1), wxo_ref[...])


def _post(x2, oa, og, wo, gx, wq, kx, vx, wxo, seq):
    n = x2.shape[0]
    tm = POST_TILE
    per_b = seq // tm
    mlen = kx.shape[1]
    const = functools.partial(pl.BlockSpec, pipeline_mode=pl.Buffered(1))
    row = lambda width: pl.BlockSpec((tm, width), lambda i: (i, 0))
    mem = pl.BlockSpec((1, mlen, D_MODEL), lambda i: (i // per_b, 0, 0))
    sq = const((D_MODEL, D_MODEL), lambda i: (0, 0))
    return pl.pallas_call(
        _post_kernel,
        grid=(n // tm,),
        in_specs=[row(D_MODEL), row(A_WIDTH), row(G_WIDTH), sq, const((1, D_MODEL), lambda i: (0, 0)), sq, mem, mem, sq],
        out_specs=row(D_MODEL),
        out_shape=jax.ShapeDtypeStruct((n, D_MODEL), F32),
        compiler_params=pltpu.CompilerParams(dimension_semantics=("parallel",), vmem_limit_bytes=VMEM_LIMIT),
        name="post",
    )(x2, oa, og, wo, gx, wq, kx, vx, wxo)


def _mlp_kernel(x_ref, g_ref, wu_ref, wd_ref, gf_ref, o_ref, *, final_norm):
    tm = x_ref.shape[0]
    parts = [slice(i * tm // MLP_PARTS, (i + 1) * tm // MLP_PARTS) for i in range(MLP_PARTS)]
    acc = [x_ref[r, :] for r in parts]
    h = [_rms(x, g_ref[...]).astype(BF16) for x in acc]
    for c in range(D_FF // D_MODEL):
        sl = slice(D_MODEL * c, D_MODEL * (c + 1))
        u = [jnp.square(jnp.maximum(_dot(v, wu_ref[:, sl]), 0.0)).astype(BF16) for v in h]
        acc = [a + _dot(v, wd_ref[sl, :]) for a, v in zip(acc, u)]
    for r, a in zip(parts, acc):
        o_ref[r, :] = _rms(a, gf_ref[...]) if final_norm else a


def _mlp(x2, g, wu, wd, gf, final_norm):
    n = x2.shape[0]
    tm = MLP_TILE
    const = functools.partial(pl.BlockSpec, pipeline_mode=pl.Buffered(1))
    row = pl.BlockSpec((tm, D_MODEL), lambda i: (i, 0))
    vec = const((1, D_MODEL), lambda i: (0, 0))
    return pl.pallas_call(
        functools.partial(_mlp_kernel, final_norm=final_norm),
        grid=(n // tm,),
        in_specs=[row, vec, const((D_MODEL, D_FF), lambda i: (0, 0)), const((D_FF, D_MODEL), lambda i: (0, 0)), vec],
        out_specs=row,
        out_shape=jax.ShapeDtypeStruct((n, D_MODEL), F32),
        compiler_params=pltpu.CompilerParams(dimension_semantics=("parallel",), vmem_limit_bytes=VMEM_LIMIT),
        name="mlp",
    )(x2, g, wu, wd, gf)


def kernel(x, mem, rp_table, norm_mix, w_in, w_gate_up, b_gate, g_norm, w_out, norm_xattn, norm_mem, w_xq, w_xkv,
           w_xo, norm_mlp, w_up, w_down, norm_final):
    batch, seq, _ = x.shape
    depth = w_in.shape[0]
    assert seq % max(INPROJ_TILE, POST_TILE, MLP_TILE, GLA_STEP) == 0 and seq // MOBA_BLOCK <= 8
    x2 = x.reshape(batch * seq, D_MODEL)
    bias = _bias_tiles(rp_table)
    c31 = rp_table[RP_BUCKETS - 1]
    glr_lo = 3 * A_WIDTH + 2 * G_KEY_WIDTH + G_WIDTH
    rg_lo = glr_lo + G_GATE_RANK
    pad_rank = LANES - G_GATE_RANK
    for l in range(depth):
        col = lambda lo, hi: w_in[l, :, lo:hi].astype(BF16)
        wglr = jnp.pad(col(glr_lo, rg_lo), ((0, 0), (0, pad_rank)))
        wgu = jnp.pad(w_gate_up[l].astype(BF16), ((0, pad_rank), (0, 0)))
        qk, vt, qg, kg, la, vg, rg = _inproj(x2, norm_mix[l][None], col(0, glr_lo), col(rg_lo, rg_lo + G_WIDTH), wglr,
                                             wgu, b_gate[l][None])
        oa = _moba(qk, vt, bias, c31, batch, seq)
        og = _gla(qg, kg, la, vg, rg, g_norm[l][None], batch, seq)
        kx, vx = _memkv(mem, norm_mem[l][None], w_xkv[l].astype(BF16))
        x2 = _post(x2, oa, og, w_out[l].astype(BF16), norm_xattn[l][None], w_xq[l].astype(BF16), kx, vx,
                   w_xo[l].astype(BF16), seq)
        last = l == depth - 1
        x2 = _mlp(x2, norm_mlp[l][None], w_up[l].astype(BF16), w_down[l].astype(BF16), norm_final[None], last)
    return x2.reshape(batch, seq, D_MODEL)
```

```python
import functools
import math

import numpy as np
import jax
import jax.numpy as jnp
from jax import lax
from jax.experimental import pallas as pl
from jax.experimental.pallas import tpu as pltpu

F32 = jnp.float32
BF16 = jnp.bfloat16

D_MODEL = 1024
A_HEADS = 8
A_HEAD_DIM = 64
A_WIDTH = A_HEADS * A_HEAD_DIM
MOBA_BLOCK = 256
MOBA_TOPK = 3
G_HEADS = 4
G_WIDTH = D_MODEL - A_WIDTH
G_HEAD_V = G_WIDTH // G_HEADS
G_KEY_WIDTH = G_WIDTH // 2
G_HEAD_K = G_KEY_WIDTH // G_HEADS
G_GATE_RANK = 16
G_GATE_NORM = 16.0
X_HEADS = 4
X_HEAD_DIM = D_MODEL // X_HEADS
D_FF = 4 * D_MODEL
RP_BUCKETS = 32
RP_MAX_DIST = 128
EPS = 1e-6

LANES = 128
SUBLANES = 8
GATE_ROWS = 16
NEG = -1e30
GLA_CHUNK = 128
GLA_STEP = 1024
LOG2E = float(np.log2(np.e))
INPROJ_TILE = 1024
INPROJ_PARTS = 2
MLP_TILE = 1024
MLP_PARTS = 2
POST_TILE = 1024
POST_PARTS = 2
VMEM_LIMIT = 56 * 1024 * 1024

def _nt(a, b):
    return lax.dot_general(a, b, (((1,), (1,)), ((), ())), preferred_element_type=F32)


def _tn(a, b):
    return lax.dot_general(a, b, (((0,), (0,)), ((), ())), preferred_element_type=F32)


def _dot(a, b):
    return jnp.dot(a, b, preferred_element_type=F32)


def _rms(x, g):
    return x * lax.rsqrt(jnp.mean(x * x, axis=-1, keepdims=True) + EPS) * g


def _bucket_thresholds():
    max_exact = RP_BUCKETS // 2
    d = np.arange(1, 4 * RP_MAX_DIST)
    val = (np.log(d.astype(np.float32) / np.float32(max_exact)) / np.float32(math.log(RP_MAX_DIST / max_exact))
           * np.float32(RP_BUCKETS - max_exact))
    bucket = np.minimum(max_exact + val.astype(np.int32), RP_BUCKETS - 1)
    return [int(d[(d >= max_exact) & (bucket >= max_exact + k)][0]) for k in range(1, RP_BUCKETS - max_exact)]


_THRESHOLDS = _bucket_thresholds()


def _bias_kernel(tab_ref, out_ref):
    h = pl.program_id(0)
    rows, width = 2 * MOBA_BLOCK, 3 * MOBA_BLOCK
    d = lax.broadcasted_iota(jnp.int32, (8, width), 1)
    max_exact = RP_BUCKETS // 2
    bucket = jnp.where(d < max_exact, d, max_exact)
    for t in _THRESHOLDS:
        bucket = bucket + jnp.where(d >= t, 1, 0)
    prof = jnp.full((8, width), NEG, F32)
    for b in range(RP_BUCKETS):
        prof = jnp.where(bucket == b, tab_ref[b, h] * LOG2E, prof)
    prof = jnp.where(d < rows, prof, NEG)
    tile = pltpu.roll(jnp.broadcast_to(prof[0:1], (rows, width)), 0, 1, stride=1, stride_axis=0)
    out_ref[0] = tile[:, MOBA_BLOCK:2 * MOBA_BLOCK]


def _bias_tiles(rp_table):
    return pl.pallas_call(
        _bias_kernel,
        grid=(A_HEADS,),
        in_specs=[pl.BlockSpec(memory_space=pltpu.SMEM)],
        out_specs=pl.BlockSpec((1, 2 * MOBA_BLOCK, MOBA_BLOCK), lambda h: (h, 0, 0)),
        out_shape=jax.ShapeDtypeStruct((A_HEADS, 2 * MOBA_BLOCK, MOBA_BLOCK), F32),
        name="bias",
    )(rp_table)


def _inproj_kernel(x_ref, g_ref, w_ref, wrg_ref, wglr_ref, wgu_ref, bg_ref,
                   qk_ref, vt_ref, qg_ref, kg_ref, la_ref, vg_ref, rg_ref):
    tm = x_ref.shape[0]
    parts = [slice(i * tm // INPROJ_PARTS, (i + 1) * tm // INPROJ_PARTS) for i in range(INPROJ_PARTS)]
    hs = [_rms(x_ref[r, :], g_ref[...]).astype(BF16) for r in parts]
    va_lo = 2 * A_WIDTH
    qg_lo = va_lo + A_WIDTH
    kg_lo = qg_lo + G_KEY_WIDTH
    vg_lo = kg_lo + G_KEY_WIDTH

    glrs = [_dot(h, wglr_ref[...]).astype(BF16) for h in hs]
    for r, h in zip(parts, hs):
        qk_ref[r, 0:A_WIDTH] = (_dot(h, w_ref[:, 0:A_WIDTH]) * (LOG2E * A_HEAD_DIM ** -0.5)).astype(BF16)
    for r, h in zip(parts, hs):
        qk_ref[r, A_WIDTH:va_lo] = _dot(h, w_ref[:, A_WIDTH:va_lo]).astype(BF16)
    for r, glr in zip(parts, glrs):
        z = _dot(glr, wgu_ref[...]) + bg_ref[...]
        log_sig = jnp.minimum(z, 0.0) - jnp.log(1.0 + jnp.exp(-jnp.abs(z)))
        la_ref[r, :] = log_sig * (1.0 / G_GATE_NORM)
    for r, h in zip(parts, hs):
        vt_ref[:, r] = _dot(h, w_ref[:, va_lo:qg_lo]).astype(BF16).T
    for r, h in zip(parts, hs):
        qg_ref[r, :] = _dot(h, w_ref[:, qg_lo:kg_lo]) * (G_HEAD_K ** -0.5)
        kg_ref[r, :] = _dot(h, w_ref[:, kg_lo:vg_lo])
    for r, h in zip(parts, hs):
        vg_ref[r, :] = _dot(h, w_ref[:, vg_lo:vg_lo + G_WIDTH]).astype(BF16)
    for r, h in zip(parts, hs):
        rg_ref[r, :] = _dot(h, wrg_ref[...])


def _inproj(x2, g, w, wrg, wglr, wgu, bg):
    n = x2.shape[0]
    tm = INPROJ_TILE
    const = lambda a: pl.BlockSpec(a.shape, lambda i: (0, 0), pipeline_mode=pl.Buffered(1))
    row = lambda width: pl.BlockSpec((tm, width), lambda i: (i, 0))
    return pl.pallas_call(
        _inproj_kernel,
        grid=(n // tm,),
        in_specs=[row(D_MODEL)] + [const(a) for a in (g, w, wrg, wglr, wgu, bg)],
        out_specs=[row(2 * A_WIDTH), pl.BlockSpec((A_WIDTH, tm), lambda i: (0, i)),
                   row(G_KEY_WIDTH), row(G_KEY_WIDTH), row(G_KEY_WIDTH), row(G_WIDTH), row(G_WIDTH)],
        out_shape=[jax.ShapeDtypeStruct((n, 2 * A_WIDTH), BF16),
                   jax.ShapeDtypeStruct((A_WIDTH, n), BF16),
                   jax.ShapeDtypeStruct((n, G_KEY_WIDTH), F32),
                   jax.ShapeDtypeStruct((n, G_KEY_WIDTH), F32),
                   jax.ShapeDtypeStruct((n, G_KEY_WIDTH), F32),
                   jax.ShapeDtypeStruct((n, G_WIDTH), BF16),
                   jax.ShapeDtypeStruct((n, G_WIDTH), F32)],
        compiler_params=pltpu.CompilerParams(dimension_semantics=("parallel",), vmem_limit_bytes=VMEM_LIMIT),
        name="inproj",
    )(x2, g, w, wrg, wglr, wgu, bg)


def _moba_kernel(c31_ref, q_ref, k_ref, vt_ref, bias_ref, o_ref, khead, s_scr, *, seq):
    p = pl.program_id(1)
    nblk = seq // MOBA_BLOCK
    half = LANES // 2

    k = k_ref[...]
    lane2 = lax.broadcasted_iota(jnp.int32, (seq, LANES), 1)
    lm0 = jnp.where(lane2 < half, 1.0, 0.0).astype(BF16)
    lm1 = jnp.where(lane2 >= half, 1.0, 0.0).astype(BF16)
    khead[0] = k * lm0
    khead[1] = k * lm1
    n_i = lax.broadcasted_iota(jnp.int32, (GATE_ROWS, seq), 0)
    t_i = lax.broadcasted_iota(jnp.int32, (GATE_ROWS, seq), 1)
    avg = jnp.where(t_i // MOBA_BLOCK == n_i, 1.0 / MOBA_BLOCK, 0.0).astype(BF16)
    km = _dot(avg, k)
    kmh = km.astype(BF16)
    kml = (km - kmh.astype(F32)).astype(BF16)

    lane_q = lax.broadcasted_iota(jnp.int32, (MOBA_BLOCK, LANES), 1)
    rowi = lax.broadcasted_iota(jnp.int32, (GATE_ROWS, MOBA_BLOCK), 0)
    blk = lambda n: slice(n * MOBA_BLOCK, (n + 1) * MOBA_BLOCK)

    def prepare(qi, h, slot):
        qm = q_ref[blk(qi), :] * jnp.where((lane_q < half) if h == 0 else (lane_q >= half), 1.0, 0.0).astype(BF16)
        gate = _nt(kmh, qm) + _nt(kml, qm) if qi > MOBA_TOPK else None
        return dict(qi=qi, h=h, qm=qm, gate=gate, c31=c31_ref[2 * p + h] * LOG2E, sbuf=s_scr.at[slot], near=None,
                    far=None, acc=None)

    def penalty(u, n):
        return 0.0 if u["pen"] is None else u["pen"][n:n + 1, :]

    def score_matmul(u):
        u["s_all"] = _nt(khead[u["h"], 0:(u["qi"] + 1) * MOBA_BLOCK, :], u["qm"])

    def score_reduce(u):
        qi, h = u["qi"], u["h"]
        u["pen"] = None
        if u["gate"] is not None:
            g = u["gate"]
            cnt = jnp.zeros((GATE_ROWS, MOBA_BLOCK), F32)
            for m in range(qi):
                gm = g[m:m + 1, :]
                cnt = cnt + jnp.where((gm > g) | ((gm == g) & (m < rowi)), 1.0, 0.0)
            u["pen"] = jnp.where((rowi < qi) & (cnt < MOBA_TOPK), 0.0, NEG)
        for n in range(qi + 1):
            s = u["s_all"][blk(n)]
            if n == qi:
                s = s + bias_ref[h, MOBA_BLOCK:2 * MOBA_BLOCK, :]
            elif n == qi - 1:
                s = s + bias_ref[h, 0:MOBA_BLOCK, :]
            u["sbuf"][blk(n), :] = s
            t = jnp.max(s.reshape(MOBA_BLOCK // SUBLANES, SUBLANES, MOBA_BLOCK), axis=0)
            if n < qi:
                t = t + penalty(u, n)
            key = "far" if n < qi - 1 else "near"
            u[key] = t if u[key] is None else jnp.maximum(u[key], t)
        mx = u["near"] if u["far"] is None else jnp.maximum(u["near"], u["far"] + u["c31"])
        u["m_near"] = jnp.max(mx, axis=0, keepdims=True)
        u["m_far"] = u["m_near"] - u["c31"]

    def prob_exp(u):
        qi = u["qi"]
        shift = lambda n: (u["m_far"] if n < qi - 1 else u["m_near"]) - (penalty(u, n) if n < qi else 0.0)
        u["pt"] = jnp.concatenate([jnp.exp2(u["sbuf"][blk(n), :] - shift(n)).astype(BF16) for n in range(qi + 1)],
                                  axis=0)

    def prob_matmul(u):
        qi, h = u["qi"], u["h"]
        width = (qi + 1) * MOBA_BLOCK
        vrows = vt_ref[half * h:half * (h + 1), 0:width]
        ones_rows = jnp.ones((half, width), BF16)
        vt = jnp.concatenate([vrows, ones_rows] if h == 0 else [ones_rows, vrows], axis=0)
        u["acc"] = _dot(vt, u["pt"])

    outs = {}

    def finish(u):
        acc, h = u["acc"], u["h"]
        num = acc[half * h:half * (h + 1)]
        den = acc[half * (1 - h):half * (1 - h) + 1]
        outs[h] = num / den
        if h == 1:
            o_ref[blk(u["qi"]), :] = jnp.concatenate([outs[0], outs[1]], axis=0).T.astype(BF16)

    group = 2
    stages = [[(qi, h) for qi in range(g * group, (g + 1) * group) for h in range(2)] for g in range(nblk // group)]
    prev = []
    for g, members in enumerate(stages + [[]]):
        cur = [prepare(qi, h, (g % 2) * 2 * group + 2 * (qi % group) + h) for qi, h in members]
        for u in prev:
            prob_exp(u)
        for u in cur:
            score_matmul(u)
        for u in prev:
            prob_matmul(u)
        for u in cur:
            score_reduce(u)
        for u in prev:
            finish(u)
        prev = cur


def _moba(qk, vt, bias, c31, batch, seq):
    n = qk.shape[0]
    npair = A_HEADS // 2
    kern = functools.partial(_moba_kernel, seq=seq)
    return pl.pallas_call(
        kern,
        grid=(batch, npair),
        in_specs=[pl.BlockSpec(memory_space=pltpu.SMEM),
                  pl.BlockSpec((seq, LANES), lambda b, p: (b, p)),
                  pl.BlockSpec((seq, LANES), lambda b, p: (b, npair + p)),
                  pl.BlockSpec((LANES, seq), lambda b, p: (p, b)),
                  pl.BlockSpec((2, 2 * MOBA_BLOCK, MOBA_BLOCK), lambda b, p: (p, 0, 0))],
        out_specs=pl.BlockSpec((seq, LANES), lambda b, p: (b, p)),
        out_shape=jax.ShapeDtypeStruct((n, A_WIDTH), BF16),
        scratch_shapes=[pltpu.VMEM((2, seq, LANES), BF16),
                        pltpu.VMEM((8, seq, MOBA_BLOCK), F32)],
        compiler_params=pltpu.CompilerParams(dimension_semantics=("parallel", "arbitrary"),
                                             vmem_limit_bytes=VMEM_LIMIT),
        name="moba",
    )(c31, qk, qk, vt, bias)


_GLA_LEVELS = int(math.log2(GLA_CHUNK))


_GLA_BIG_LEVELS = _GLA_LEVELS - int(math.log2(SUBLANES))


def _gla_weights():
    c = GLA_CHUNK
    i = np.arange(c)[:, None]
    j = np.arange(c)[None, :]
    mats = [(j <= i)]
    for lvl in range(_GLA_BIG_LEVELS, _GLA_LEVELS):
        s = (c // 2) >> lvl
        ref = (i // (2 * s)) * (2 * s) + s - 1
        mats.append(np.where(i > ref, (j > ref) & (j <= i), (j > i) & (j <= ref)))
    return np.concatenate(mats, axis=0).astype(np.float32)


def _gla_level_map():
    c = GLA_CHUNK
    i = np.arange(c)[:, None]
    j = np.arange(c)[None, :]
    top_bit = np.floor(np.log2(np.maximum(i ^ j, 1))).astype(np.int64)
    lvl = np.where(j < i, _GLA_LEVELS - 1 - top_bit, np.where(j == i, _GLA_LEVELS, _GLA_LEVELS + 1))
    return np.concatenate([lvl, lvl], axis=1).astype(np.int32)


def _gla_kernel(q_ref, k_ref, la_ref, v_ref, rg_ref, w_ref, lmap_ref, gn_ref, o_ref, st_ref):
    c = GLA_CHUNK
    half = LANES // 2
    npair = G_HEADS // 2

    @pl.when(pl.program_id(1) == 0)
    def _init():
        st_ref[...] = jnp.zeros(st_ref.shape, F32)

    w = w_ref[...]
    lmap = lmap_ref[...]
    lane = lax.broadcasted_iota(jnp.int32, (1, LANES), 1)
    lane_c = lax.broadcasted_iota(jnp.int32, (c, LANES), 1)
    lm = [jnp.where(lane_c < half, 1.0, 0.0).astype(BF16), jnp.where(lane_c >= half, 1.0, 0.0).astype(BF16)]
    row_small = lax.broadcasted_iota(jnp.int32, (c, G_KEY_WIDTH), 0)

    def level_factors(q, k, b, d_all, lvl):
        s = (c // 2) >> lvl
        if s < SUBLANES:
            j = lvl - _GLA_BIG_LEVELS
            e = jnp.exp2(d_all[c * (j + 1):c * (j + 2)])
            odd = (row_small // s) % 2 == 1
            return jnp.where(odd, q * e, 0.0), jnp.where(odd, 0.0, k * e)
        zeros = jnp.zeros((s, q.shape[1]), F32)
        qp, kp = [], []
        for blk in range(c // s):
            rows = slice(blk * s, (blk + 1) * s)
            ref = (blk // 2) * 2 * s + s - 1
            if blk % 2 == 1:
                qp.append(q[rows] * jnp.exp2(b[rows] - b[ref:ref + 1]))
                kp.append(zeros)
            else:
                qp.append(zeros)
                kp.append(k[rows] * jnp.exp2(b[ref:ref + 1] - b[rows]))
        return jnp.concatenate(qp, axis=0), jnp.concatenate(kp, axis=0)

    nch = GLA_STEP // c
    rows_of = [slice(ch * c, (ch + 1) * c) for ch in range(nch)]
    pair_lanes = [slice(LANES * p, LANES * (p + 1)) for p in range(npair)]
    d_alls = []
    for ch in range(nch):
        la = la_ref[rows_of[ch], :] * LOG2E
        hi = la.astype(BF16)
        lo = (la - hi.astype(F32)).astype(BF16)
        d_alls.append(_dot(w, hi) + _dot(w, lo))
    chunks = []
    for ch in range(nch):
        d_all = d_alls[ch]
        q = q_ref[rows_of[ch], :]
        k = k_ref[rows_of[ch], :]
        b = d_all[0:c]
        blast = b[c - 1:c, :]
        qts, kts = [], []
        for lvl in range(_GLA_LEVELS):
            qt, kt = level_factors(q, k, b, d_all, lvl)
            qts.append(qt.astype(BF16))
            kts.append(kt.astype(BF16))
        qts.append(q.astype(BF16))
        kts.append(k.astype(BF16))
        chunks.append(dict(qts=qts, kts=kts, qe=(q * jnp.exp2(b)).astype(BF16),
                           khat=(k * jnp.exp2(blast - b)).astype(BF16), dec=jnp.exp2(blast),
                           a=[jnp.zeros((c, 2 * c), F32)] * npair))
    for lvl in range(_GLA_LEVELS + 1):
        mask = lmap == lvl
        s = (c // 2) >> lvl if lvl < _GLA_BIG_LEVELS else c
        blocks = [slice(i * s, (i + 1) * s) for i in range(c // s)]
        live = [i for i in range(len(blocks)) if i % 2 == 1 or s == c]
        for u in chunks:
            for p, sl in enumerate(pair_lanes):
                kp = u["kts"][lvl][:, sl]
                ql = u["qts"][lvl]
                pm = _nt(jnp.concatenate([ql[blocks[i], sl] for i in live], axis=0),
                         jnp.concatenate([kp * lm[0], kp * lm[1]], axis=0))
                a = u["a"][p]
                pieces = [a[r] for r in blocks]
                for n, i in enumerate(live):
                    pieces[i] = jnp.where(mask[blocks[i]], pm[n * s:(n + 1) * s], pieces[i])
                u["a"] = [jnp.concatenate(pieces, axis=0) if j == p else v for j, v in enumerate(u["a"])]
    states = [st_ref[p] for p in range(npair)]
    for ch, u in enumerate(chunks):
        rows = rows_of[ch]
        for p, sl in enumerate(pair_lanes):
            stb = states[p].astype(BF16)
            ups = []
            for hh in range(2):
                h = 2 * p + hh
                vh = v_ref[rows, LANES * h:LANES * (h + 1)]
                o = _dot(u["a"][p][:, hh * c:(hh + 1) * c].astype(BF16), vh) + _nt(u["qe"][:, sl] * lm[hh], stb)
                y = _rms(o, gn_ref[...])
                rg = rg_ref[rows, LANES * h:LANES * (h + 1)]
                y = y * (rg / (1.0 + jnp.exp(-rg)))
                o_ref[rows, LANES * h:LANES * (h + 1)] = y.astype(BF16)
                ups.append(_tn(vh, u["khat"][:, sl]))
            states[p] = states[p] * u["dec"][:, sl] + jnp.where(lane < half, ups[0], ups[1])
    for p in range(npair):
        st_ref[p] = states[p]


def _gla(qg, kg, la, vg, rg, gn, batch, seq):
    n = qg.shape[0]
    nchunk = seq // GLA_STEP
    w = jnp.asarray(_gla_weights(), BF16)
    lmap = jnp.asarray(_gla_level_map())
    row = lambda width: pl.BlockSpec((GLA_STEP, width), lambda b, c: (b * nchunk + c, 0))
    const = functools.partial(pl.BlockSpec, pipeline_mode=pl.Buffered(1))
    return pl.pallas_call(
        _gla_kernel,
        grid=(batch, nchunk),
        in_specs=[row(G_KEY_WIDTH), row(G_KEY_WIDTH), row(G_KEY_WIDTH), row(G_WIDTH), row(G_WIDTH),
                  const(w.shape, lambda b, c: (0, 0)),
                  const(lmap.shape, lambda b, c: (0, 0)),
                  const((1, G_HEAD_V), lambda b, c: (0, 0))],
        out_specs=row(G_WIDTH),
        out_shape=jax.ShapeDtypeStruct((n, G_WIDTH), BF16),
        scratch_shapes=[pltpu.VMEM((G_HEADS // 2, G_HEAD_V, LANES), F32)],
        compiler_params=pltpu.CompilerParams(dimension_semantics=("parallel", "arbitrary"),
                                             vmem_limit_bytes=VMEM_LIMIT),
        name="gla",
    )(qg, kg, la, vg, rg, w, lmap, gn)


def _memkv_kernel(m_ref, g_ref, w_ref, k_ref, v_ref):
    nb, mlen, _ = m_ref.shape
    h = _rms(m_ref[...].reshape(nb * mlen, D_MODEL), g_ref[...]).astype(BF16)
    k_ref[...] = _dot(h, w_ref[:, 0:D_MODEL]).astype(BF16).reshape(nb, mlen, D_MODEL)
    v_ref[...] = _dot(h, w_ref[:, D_MODEL:2 * D_MODEL]).astype(BF16).reshape(nb, mlen, D_MODEL)


def _memkv(mem, g, w):
    batch, mlen, _ = mem.shape
    nb = 2 if batch % 2 == 0 else 1
    const = functools.partial(pl.BlockSpec, pipeline_mode=pl.Buffered(1))
    blk = pl.BlockSpec((nb, mlen, D_MODEL), lambda b: (b, 0, 0))
    return pl.pallas_call(
        _memkv_kernel,
        grid=(batch // nb,),
        in_specs=[blk, const((1, D_MODEL), lambda b: (0, 0)), const((D_MODEL, 2 * D_MODEL), lambda b: (0, 0))],
        out_specs=[blk, blk],
        out_shape=[jax.ShapeDtypeStruct(mem.shape, BF16)] * 2,
        compiler_params=pltpu.CompilerParams(dimension_semantics=("parallel",), vmem_limit_bytes=VMEM_LIMIT),
        name="memkv",
    )(mem, g, w)


def _post_kernel(x_ref, oa_ref, og_ref, wo_ref, gx_ref, wq_ref, kx_ref, vx_ref, wxo_ref, o_ref):
    tm = x_ref.shape[0]
    halves = [slice(i * tm // POST_PARTS, (i + 1) * tm // POST_PARTS) for i in range(POST_PARTS)]
    x1 = [x_ref[r, :] + _dot(oa_ref[r, :], wo_ref[0:A_WIDTH, :]) + _dot(og_ref[r, :], wo_ref[A_WIDTH:D_MODEL, :])
          for r in halves]
    h = [_rms(v, gx_ref[...]).astype(BF16) for v in x1]
    qx = [(_dot(v, wq_ref[...]) * (LOG2E * X_HEAD_DIM ** -0.5)).astype(BF16) for v in h]
    outs = [[] for _ in halves]
    for hd in range(X_HEADS):
        sl = slice(X_HEAD_DIM * hd, X_HEAD_DIM * (hd + 1))
        s = [_nt(q[:, sl], kx_ref[0, :, sl]) for q in qx]
        pexp = [jnp.exp2(v - jnp.max(v, axis=1, keepdims=True)) for v in s]
        pn = [(v / jnp.sum(v, axis=1, keepdims=True)).astype(BF16) for v in pexp]
        for i, v in enumerate(pn):
            outs[i].append(_dot(v, vx_ref[0, :, sl]).astype(BF16))
    for i, r in enumerate(halves):
        o_ref[r, :] = x1[i] + _dot(jnp.concatenate(outs[i], axis=1), wxo_ref[...])


def _post(x2, oa, og, wo, gx, wq, kx, vx, wxo, seq):
    n = x2.shape[0]
    tm = POST_TILE
    per_b = seq // tm
    mlen = kx.shape[1]
    const = functools.partial(pl.BlockSpec, pipeline_mode=pl.Buffered(1))
    row = lambda width: pl.BlockSpec((tm, width), lambda i: (i, 0))
    mem = pl.BlockSpec((1, mlen, D_MODEL), lambda i: (i // per_b, 0, 0))
    sq = const((D_MODEL, D_MODEL), lambda i: (0, 0))
    return pl.pallas_call(
        _post_kernel,
        grid=(n // tm,),
        in_specs=[row(D_MODEL), row(A_WIDTH), row(G_WIDTH), sq, const((1, D_MODEL), lambda i: (0, 0)), sq, mem, mem, sq],
        out_specs=row(D_MODEL),
        out_shape=jax.ShapeDtypeStruct((n, D_MODEL), F32),
        compiler_params=pltpu.CompilerParams(dimension_semantics=("parallel",), vmem_limit_bytes=VMEM_LIMIT),
        name="post",
    )(x2, oa, og, wo, gx, wq, kx, vx, wxo)


def _mlp_kernel(x_ref, g_ref, wu_ref, wd_ref, gf_ref, o_ref, *, final_norm):
    tm = x_ref.shape[0]
    parts = [slice(i * tm // MLP_PARTS, (i + 1) * tm // MLP_PARTS) for i in range(MLP_PARTS)]
    acc = [x_ref[r, :] for r in parts]
    h = [_rms(x, g_ref[...]).astype(BF16) for x in acc]
    for c in range(D_FF // D_MODEL):
        sl = slice(D_MODEL * c, D_MODEL * (c + 1))
        u = [jnp.square(jnp.maximum(_dot(v, wu_ref[:, sl]), 0.0)).astype(BF16) for v in h]
        acc = [a + _dot(v, wd_ref[sl, :]) for a, v in zip(acc, u)]
    for r, a in zip(parts, acc):
        o_ref[r, :] = _rms(a, gf_ref[...]) if final_norm else a


def _mlp(x2, g, wu, wd, gf, final_norm):
    n = x2.shape[0]
    tm = MLP_TILE
    const = functools.partial(pl.BlockSpec, pipeline_mode=pl.Buffered(1))
    row = pl.BlockSpec((tm, D_MODEL), lambda i: (i, 0))
    vec = const((1, D_MODEL), lambda i: (0, 0))
    return pl.pallas_call(
        functools.partial(_mlp_kernel, final_norm=final_norm),
        grid=(n // tm,),
        in_specs=[row, vec, const((D_MODEL, D_FF), lambda i: (0, 0)), const((D_FF, D_MODEL), lambda i: (0, 0)), vec],
        out_specs=row,
        out_shape=jax.ShapeDtypeStruct((n, D_MODEL), F32),
        compiler_params=pltpu.CompilerParams(dimension_semantics=("parallel",), vmem_limit_bytes=VMEM_LIMIT),
        name="mlp",
    )(x2, g, wu, wd, gf)


def kernel(x, mem, rp_table, norm_mix, w_in, w_gate_up, b_gate, g_norm, w_out, norm_xattn, norm_mem, w_xq, w_xkv,
           w_xo, norm_mlp, w_up, w_down, norm_final):
    batch, seq, _ = x.shape
    depth = w_in.shape[0]
    assert seq % max(INPROJ_TILE, POST_TILE, MLP_TILE, GLA_STEP) == 0 and seq // MOBA_BLOCK <= 8
    x2 = x.reshape(batch * seq, D_MODEL)
    bias = _bias_tiles(rp_table)
    c31 = rp_table[RP_BUCKETS - 1]
    glr_lo = 3 * A_WIDTH + 2 * G_KEY_WIDTH + G_WIDTH
    rg_lo = glr_lo + G_GATE_RANK
    pad_rank = LANES - G_GATE_RANK
    for l in range(depth):
        col = lambda lo, hi: w_in[l, :, lo:hi].astype(BF16)
        wglr = jnp.pad(col(glr_lo, rg_lo), ((0, 0), (0, pad_rank)))
        wgu = jnp.pad(w_gate_up[l].astype(BF16), ((0, pad_rank), (0, 0)))
        qk, vt, qg, kg, la, vg, rg = _inproj(x2, norm_mix[l][None], w_in[l].astype(BF16), col(rg_lo, rg_lo + G_WIDTH),
                                             wglr, wgu, b_gate[l][None])
        oa = _moba(qk, vt, bias, c31, batch, seq)
        og = _gla(qg, kg, la, vg, rg, g_norm[l][None], batch, seq)
        kx, vx = _memkv(mem, norm_mem[l][None], w_xkv[l].astype(BF16))
        x2 = _post(x2, oa, og, w_out[l].astype(BF16), norm_xattn[l][None], w_xq[l].astype(BF16), kx, vx,
                   w_xo[l].astype(BF16), seq)
        last = l == depth - 1
        x2 = _mlp(x2, norm_mlp[l][None], w_up[l].astype(BF16), w_down[l].astype(BF16), norm_final[None], last)
    return x2.reshape(batch, seq, D_MODEL)
```

```python
import functools
import math

import numpy as np
import jax
import jax.numpy as jnp
from jax import lax
from jax.experimental import pallas as pl
from jax.experimental.pallas import tpu as pltpu

F32 = jnp.float32
BF16 = jnp.bfloat16

D_MODEL = 1024
A_HEADS = 8
A_HEAD_DIM = 64
A_WIDTH = A_HEADS * A_HEAD_DIM
MOBA_BLOCK = 256
MOBA_TOPK = 3
G_HEADS = 4
G_WIDTH = D_MODEL - A_WIDTH
G_HEAD_V = G_WIDTH // G_HEADS
G_KEY_WIDTH = G_WIDTH // 2
G_HEAD_K = G_KEY_WIDTH // G_HEADS
G_GATE_RANK = 16
G_GATE_NORM = 16.0
X_HEADS = 4
X_HEAD_DIM = D_MODEL // X_HEADS
D_FF = 4 * D_MODEL
RP_BUCKETS = 32
RP_MAX_DIST = 128
EPS = 1e-6

LANES = 128
SUBLANES = 8
GATE_ROWS = 16
NEG = -1e30
MOBA_PAIRS = 2
GLA_CHUNK = 128
GLA_STEP = 1024
LOG2E = float(np.log2(np.e))
INPROJ_TILE = 1024
INPROJ_PARTS = 2
MLP_TILE = 1024
MLP_PARTS = 2
POST_TILE = 1024
POST_PARTS = 2
VMEM_LIMIT = 56 * 1024 * 1024

def _nt(a, b):
    return lax.dot_general(a, b, (((1,), (1,)), ((), ())), preferred_element_type=F32)


def _tn(a, b):
    return lax.dot_general(a, b, (((0,), (0,)), ((), ())), preferred_element_type=F32)


def _dot(a, b):
    return jnp.dot(a, b, preferred_element_type=F32)


def _rms(x, g):
    return x * lax.rsqrt(jnp.mean(x * x, axis=-1, keepdims=True) + EPS) * g


def _bucket_thresholds():
    max_exact = RP_BUCKETS // 2
    d = np.arange(1, 4 * RP_MAX_DIST)
    val = (np.log(d.astype(np.float32) / np.float32(max_exact)) / np.float32(math.log(RP_MAX_DIST / max_exact))
           * np.float32(RP_BUCKETS - max_exact))
    bucket = np.minimum(max_exact + val.astype(np.int32), RP_BUCKETS - 1)
    return [int(d[(d >= max_exact) & (bucket >= max_exact + k)][0]) for k in range(1, RP_BUCKETS - max_exact)]


_THRESHOLDS = _bucket_thresholds()


def _bias_kernel(tab_ref, out_ref):
    h = pl.program_id(0)
    rows, width = 2 * MOBA_BLOCK, 3 * MOBA_BLOCK
    d = lax.broadcasted_iota(jnp.int32, (8, width), 1)
    max_exact = RP_BUCKETS // 2
    bucket = jnp.where(d < max_exact, d, max_exact)
    for t in _THRESHOLDS:
        bucket = bucket + jnp.where(d >= t, 1, 0)
    prof = jnp.full((8, width), NEG, F32)
    for b in range(RP_BUCKETS):
        prof = jnp.where(bucket == b, tab_ref[b, h] * LOG2E, prof)
    prof = jnp.where(d < rows, prof, NEG)
    tile = pltpu.roll(jnp.broadcast_to(prof[0:1], (rows, width)), 0, 1, stride=1, stride_axis=0)
    out_ref[0] = tile[:, MOBA_BLOCK:2 * MOBA_BLOCK]


def _bias_tiles(rp_table):
    return pl.pallas_call(
        _bias_kernel,
        grid=(A_HEADS,),
        in_specs=[pl.BlockSpec(memory_space=pltpu.SMEM)],
        out_specs=pl.BlockSpec((1, 2 * MOBA_BLOCK, MOBA_BLOCK), lambda h: (h, 0, 0)),
        out_shape=jax.ShapeDtypeStruct((A_HEADS, 2 * MOBA_BLOCK, MOBA_BLOCK), F32),
        name="bias",
    )(rp_table)


def _inproj_kernel(x_ref, g_ref, w_ref, wrg_ref, wglr_ref, wgu_ref, bg_ref,
                   qk_ref, vt_ref, qg_ref, kg_ref, la_ref, vg_ref, rg_ref):
    tm = x_ref.shape[0]
    parts = [slice(i * tm // INPROJ_PARTS, (i + 1) * tm // INPROJ_PARTS) for i in range(INPROJ_PARTS)]
    hs = [_rms(x_ref[r, :], g_ref[...]).astype(BF16) for r in parts]
    va_lo = 2 * A_WIDTH
    qg_lo = va_lo + A_WIDTH
    kg_lo = qg_lo + G_KEY_WIDTH
    vg_lo = kg_lo + G_KEY_WIDTH

    glrs = [_dot(h, wglr_ref[...]).astype(BF16) for h in hs]
    for r, h in zip(parts, hs):
        qk_ref[r, 0:A_WIDTH] = (_dot(h, w_ref[:, 0:A_WIDTH]) * (LOG2E * A_HEAD_DIM ** -0.5)).astype(BF16)
    for r, h in zip(parts, hs):
        qk_ref[r, A_WIDTH:va_lo] = _dot(h, w_ref[:, A_WIDTH:va_lo]).astype(BF16)
    for r, glr in zip(parts, glrs):
        z = _dot(glr, wgu_ref[...]) + bg_ref[...]
        log_sig = jnp.minimum(z, 0.0) - jnp.log(1.0 + jnp.exp(-jnp.abs(z)))
        la_ref[r, :] = log_sig * (1.0 / G_GATE_NORM)
    for r, h in zip(parts, hs):
        vt_ref[:, r] = _dot(h, w_ref[:, va_lo:qg_lo]).astype(BF16).T
    for r, h in zip(parts, hs):
        qg_ref[r, :] = _dot(h, w_ref[:, qg_lo:kg_lo]) * (G_HEAD_K ** -0.5)
        kg_ref[r, :] = _dot(h, w_ref[:, kg_lo:vg_lo])
    for r, h in zip(parts, hs):
        vg_ref[r, :] = _dot(h, w_ref[:, vg_lo:vg_lo + G_WIDTH]).astype(BF16)
    for r, h in zip(parts, hs):
        rg_ref[r, :] = _dot(h, wrg_ref[...])


def _inproj(x2, g, w, wrg, wglr, wgu, bg):
    n = x2.shape[0]
    tm = INPROJ_TILE
    const = lambda a: pl.BlockSpec(a.shape, lambda i: (0, 0), pipeline_mode=pl.Buffered(1))
    row = lambda width: pl.BlockSpec((tm, width), lambda i: (i, 0))
    return pl.pallas_call(
        _inproj_kernel,
        grid=(n // tm,),
        in_specs=[row(D_MODEL)] + [const(a) for a in (g, w, wrg, wglr, wgu, bg)],
        out_specs=[row(2 * A_WIDTH), pl.BlockSpec((A_WIDTH, tm), lambda i: (0, i)),
                   row(G_KEY_WIDTH), row(G_KEY_WIDTH), row(G_KEY_WIDTH), row(G_WIDTH), row(G_WIDTH)],
        out_shape=[jax.ShapeDtypeStruct((n, 2 * A_WIDTH), BF16),
                   jax.ShapeDtypeStruct((A_WIDTH, n), BF16),
                   jax.ShapeDtypeStruct((n, G_KEY_WIDTH), F32),
                   jax.ShapeDtypeStruct((n, G_KEY_WIDTH), F32),
                   jax.ShapeDtypeStruct((n, G_KEY_WIDTH), F32),
                   jax.ShapeDtypeStruct((n, G_WIDTH), BF16),
                   jax.ShapeDtypeStruct((n, G_WIDTH), F32)],
        compiler_params=pltpu.CompilerParams(dimension_semantics=("parallel",), vmem_limit_bytes=VMEM_LIMIT),
        name="inproj",
    )(x2, g, w, wrg, wglr, wgu, bg)


def _moba_kernel(c31_ref, q_ref, k_ref, vt_ref, bias_ref, o_ref, khead, s_scr, *, seq):
    step = pl.program_id(1)
    nblk = seq // MOBA_BLOCK
    half = LANES // 2
    lane2 = lax.broadcasted_iota(jnp.int32, (seq, LANES), 1)
    lane_q = lax.broadcasted_iota(jnp.int32, (MOBA_BLOCK, LANES), 1)
    rowi = lax.broadcasted_iota(jnp.int32, (GATE_ROWS, MOBA_BLOCK), 0)
    blk = lambda n: slice(n * MOBA_BLOCK, (n + 1) * MOBA_BLOCK)
    pair_lanes = lambda pp: slice(pp * LANES, (pp + 1) * LANES)

    def build(pp):
        k = k_ref[:, pair_lanes(pp)]
        khead[2 * pp] = k * jnp.where(lane2 < half, 1.0, 0.0).astype(BF16)
        khead[2 * pp + 1] = k * jnp.where(lane2 >= half, 1.0, 0.0).astype(BF16)
        n_i = lax.broadcasted_iota(jnp.int32, (GATE_ROWS, seq), 0)
        t_i = lax.broadcasted_iota(jnp.int32, (GATE_ROWS, seq), 1)
        avg = jnp.where(t_i // MOBA_BLOCK == n_i, 1.0 / MOBA_BLOCK, 0.0).astype(BF16)
        km = _dot(avg, k)
        kmh = km.astype(BF16)
        return kmh, (km - kmh.astype(F32)).astype(BF16)

    def prepare(pp, kmeans, qi, h, slot):
        qm = q_ref[blk(qi), pair_lanes(pp)] * jnp.where((lane_q < half) if h == 0 else (lane_q >= half), 1.0,
                                                        0.0).astype(BF16)
        gate = _nt(kmeans[0], qm) + _nt(kmeans[1], qm) if qi > MOBA_TOPK else None
        head = 2 * pp + h
        return dict(pp=pp, qi=qi, h=h, head=head, qm=qm, gate=gate, sbuf=s_scr.at[slot], near=None, far=None,
                    acc=None, c31=c31_ref[2 * MOBA_PAIRS * step + head] * LOG2E)

    def penalty(u, n):
        return 0.0 if u["pen"] is None else u["pen"][n:n + 1, :]

    def score_matmul(u):
        u["s_all"] = _nt(khead[u["head"], 0:(u["qi"] + 1) * MOBA_BLOCK, :], u["qm"])

    def score_reduce(u):
        qi, head = u["qi"], u["head"]
        u["pen"] = None
        if u["gate"] is not None:
            g = u["gate"]
            cnt = jnp.zeros((GATE_ROWS, MOBA_BLOCK), F32)
            for m in range(qi):
                gm = g[m:m + 1, :]
                cnt = cnt + jnp.where((gm > g) | ((gm == g) & (m < rowi)), 1.0, 0.0)
            u["pen"] = jnp.where((rowi < qi) & (cnt < MOBA_TOPK), 0.0, NEG)
        for n in range(qi + 1):
            s = u["s_all"][blk(n)]
            if n == qi:
                s = s + bias_ref[head, MOBA_BLOCK:2 * MOBA_BLOCK, :]
            elif n == qi - 1:
                s = s + bias_ref[head, 0:MOBA_BLOCK, :]
            u["sbuf"][blk(n), :] = s
            t = jnp.max(s.reshape(MOBA_BLOCK // SUBLANES, SUBLANES, MOBA_BLOCK), axis=0)
            if n < qi:
                t = t + penalty(u, n)
            key = "far" if n < qi - 1 else "near"
            u[key] = t if u[key] is None else jnp.maximum(u[key], t)
        mx = u["near"] if u["far"] is None else jnp.maximum(u["near"], u["far"] + u["c31"])
        u["m_near"] = jnp.max(mx, axis=0, keepdims=True)
        u["m_far"] = u["m_near"] - u["c31"]

    def prob_exp(u):
        qi = u["qi"]
        shift = lambda n: (u["m_far"] if n < qi - 1 else u["m_near"]) - (penalty(u, n) if n < qi else 0.0)
        u["pt"] = jnp.concatenate([jnp.exp2(u["sbuf"][blk(n), :] - shift(n)).astype(BF16) for n in range(qi + 1)],
                                  axis=0)

    def prob_matmul(u):
        qi, h = u["qi"], u["h"]
        width = (qi + 1) * MOBA_BLOCK
        r0 = u["head"] * half
        vrows = vt_ref[r0:r0 + half, 0:width]
        ones_rows = jnp.ones((half, width), BF16)
        vt = jnp.concatenate([vrows, ones_rows] if h == 0 else [ones_rows, vrows], axis=0)
        u["acc"] = _dot(vt, u["pt"])

    outs = {}

    def finish(u):
        acc, h = u["acc"], u["h"]
        num = acc[half * h:half * (h + 1)]
        den = acc[half * (1 - h):half * (1 - h) + 1]
        outs[h] = num / den
        if h == 1:
            o_ref[blk(u["qi"]), pair_lanes(u["pp"])] = jnp.concatenate([outs[0], outs[1]], axis=0).T.astype(BF16)

    group = 2
    stages = [(pp, [(qi, h) for qi in range(g * group, (g + 1) * group) for h in range(2)])
              for pp in range(MOBA_PAIRS) for g in range(nblk // group)]
    prev, kmeans = [], {}
    for g, (pp, members) in enumerate(stages + [(None, [])]):
        if members and pp not in kmeans:
            kmeans[pp] = build(pp)
        cur = [prepare(pp, kmeans[pp], qi, h, (g % 2) * 2 * group + 2 * (qi % group) + h) for qi, h in members]
        for u in prev:
            prob_exp(u)
        for u in cur:
            score_matmul(u)
        for u in prev:
            prob_matmul(u)
        for u in cur:
            score_reduce(u)
        for u in prev:
            finish(u)
        prev = cur


def _moba(qk, vt, bias, c31, batch, seq):
    n = qk.shape[0]
    width = MOBA_PAIRS * LANES
    nstep = A_WIDTH // width
    kern = functools.partial(_moba_kernel, seq=seq)
    return pl.pallas_call(
        kern,
        grid=(batch, nstep),
        in_specs=[pl.BlockSpec(memory_space=pltpu.SMEM),
                  pl.BlockSpec((seq, width), lambda b, p: (b, p)),
                  pl.BlockSpec((seq, width), lambda b, p: (b, nstep + p)),
                  pl.BlockSpec((width, seq), lambda b, p: (p, b)),
                  pl.BlockSpec((2 * MOBA_PAIRS, 2 * MOBA_BLOCK, MOBA_BLOCK), lambda b, p: (p, 0, 0))],
        out_specs=pl.BlockSpec((seq, width), lambda b, p: (b, p)),
        out_shape=jax.ShapeDtypeStruct((n, A_WIDTH), BF16),
        scratch_shapes=[pltpu.VMEM((2 * MOBA_PAIRS, seq, LANES), BF16),
                        pltpu.VMEM((8, seq, MOBA_BLOCK), F32)],
        compiler_params=pltpu.CompilerParams(dimension_semantics=("parallel", "arbitrary"),
                                             vmem_limit_bytes=VMEM_LIMIT),
        name="moba",
    )(c31, qk, qk, vt, bias)


_GLA_LEVELS = int(math.log2(GLA_CHUNK))


_GLA_BIG_LEVELS = _GLA_LEVELS - int(math.log2(SUBLANES))


def _gla_weights():
    c = GLA_CHUNK
    i = np.arange(c)[:, None]
    j = np.arange(c)[None, :]
    mats = [(j <= i)]
    for lvl in range(_GLA_BIG_LEVELS, _GLA_LEVELS):
        s = (c // 2) >> lvl
        ref = (i // (2 * s)) * (2 * s) + s - 1
        mats.append(np.where(i > ref, (j > ref) & (j <= i), (j > i) & (j <= ref)))
    return np.concatenate(mats, axis=0).astype(np.float32)


def _gla_level_map():
    c = GLA_CHUNK
    i = np.arange(c)[:, None]
    j = np.arange(c)[None, :]
    top_bit = np.floor(np.log2(np.maximum(i ^ j, 1))).astype(np.int64)
    lvl = np.where(j < i, _GLA_LEVELS - 1 - top_bit, np.where(j == i, _GLA_LEVELS, _GLA_LEVELS + 1))
    return np.concatenate([lvl, lvl], axis=1).astype(np.int32)


def _gla_kernel(q_ref, k_ref, la_ref, v_ref, rg_ref, w_ref, lmap_ref, gn_ref, o_ref, st_ref):
    c = GLA_CHUNK
    half = LANES // 2
    npair = G_HEADS // 2

    @pl.when(pl.program_id(1) == 0)
    def _init():
        st_ref[...] = jnp.zeros(st_ref.shape, F32)

    w = w_ref[...]
    lmap = lmap_ref[...]
    lane = lax.broadcasted_iota(jnp.int32, (1, LANES), 1)
    lane_c = lax.broadcasted_iota(jnp.int32, (c, LANES), 1)
    lm = [jnp.where(lane_c < half, 1.0, 0.0).astype(BF16), jnp.where(lane_c >= half, 1.0, 0.0).astype(BF16)]
    row_small = lax.broadcasted_iota(jnp.int32, (c, G_KEY_WIDTH), 0)

    def level_factors(q, k, b, d_all, lvl):
        s = (c // 2) >> lvl
        if s < SUBLANES:
            j = lvl - _GLA_BIG_LEVELS
            e = jnp.exp2(d_all[c * (j + 1):c * (j + 2)])
            odd = (row_small // s) % 2 == 1
            return jnp.where(odd, q * e, 0.0), jnp.where(odd, 0.0, k * e)
        zeros = jnp.zeros((s, q.shape[1]), F32)
        qp, kp = [], []
        for blk in range(c // s):
            rows = slice(blk * s, (blk + 1) * s)
            ref = (blk // 2) * 2 * s + s - 1
            if blk % 2 == 1:
                qp.append(q[rows] * jnp.exp2(b[rows] - b[ref:ref + 1]))
                kp.append(zeros)
            else:
                qp.append(zeros)
                kp.append(k[rows] * jnp.exp2(b[ref:ref + 1] - b[rows]))
        return jnp.concatenate(qp, axis=0), jnp.concatenate(kp, axis=0)

    nch = GLA_STEP // c
    rows_of = [slice(ch * c, (ch + 1) * c) for ch in range(nch)]
    pair_lanes = [slice(LANES * p, LANES * (p + 1)) for p in range(npair)]
    d_alls = []
    for ch in range(nch):
        la = la_ref[rows_of[ch], :] * LOG2E
        hi = la.astype(BF16)
        lo = (la - hi.astype(F32)).astype(BF16)
        d_alls.append(_dot(w, hi) + _dot(w, lo))
    chunks = []
    for ch in range(nch):
        d_all = d_alls[ch]
        q = q_ref[rows_of[ch], :]
        k = k_ref[rows_of[ch], :]
        b = d_all[0:c]
        blast = b[c - 1:c, :]
        qts, kts = [], []
        for lvl in range(_GLA_LEVELS):
            qt, kt = level_factors(q, k, b, d_all, lvl)
            qts.append(qt.astype(BF16))
            kts.append(kt.astype(BF16))
        qts.append(q.astype(BF16))
        kts.append(k.astype(BF16))
        chunks.append(dict(qts=qts, kts=kts, qe=(q * jnp.exp2(b)).astype(BF16),
                           khat=(k * jnp.exp2(blast - b)).astype(BF16), dec=jnp.exp2(blast),
                           a=[jnp.zeros((c, 2 * c), F32)] * npair))
    for lvl in range(_GLA_LEVELS + 1):
        mask = lmap == lvl
        s = (c // 2) >> lvl if lvl < _GLA_BIG_LEVELS else c
        blocks = [slice(i * s, (i + 1) * s) for i in range(c // s)]
        live = [i for i in range(len(blocks)) if i % 2 == 1 or s == c]
        for u in chunks:
            for p, sl in enumerate(pair_lanes):
                kp = u["kts"][lvl][:, sl]
                ql = u["qts"][lvl]
                pm = _nt(jnp.concatenate([ql[blocks[i], sl] for i in live], axis=0),
                         jnp.concatenate([kp * lm[0], kp * lm[1]], axis=0))
                a = u["a"][p]
                pieces = [a[r] for r in blocks]
                for n, i in enumerate(live):
                    pieces[i] = jnp.where(mask[blocks[i]], pm[n * s:(n + 1) * s], pieces[i])
                u["a"] = [jnp.concatenate(pieces, axis=0) if j == p else v for j, v in enumerate(u["a"])]
    states = [st_ref[p] for p in range(npair)]
    for ch, u in enumerate(chunks):
        rows = rows_of[ch]
        for p, sl in enumerate(pair_lanes):
            stb = states[p].astype(BF16)
            ups = []
            for hh in range(2):
                h = 2 * p + hh
                vh = v_ref[rows, LANES * h:LANES * (h + 1)]
                o = _dot(u["a"][p][:, hh * c:(hh + 1) * c].astype(BF16), vh) + _nt(u["qe"][:, sl] * lm[hh], stb)
                y = _rms(o, gn_ref[...])
                rg = rg_ref[rows, LANES * h:LANES * (h + 1)]
                y = y * (rg / (1.0 + jnp.exp(-rg)))
                o_ref[rows, LANES * h:LANES * (h + 1)] = y.astype(BF16)
                ups.append(_tn(vh, u["khat"][:, sl]))
            states[p] = states[p] * u["dec"][:, sl] + jnp.where(lane < half, ups[0], ups[1])
    for p in range(npair):
        st_ref[p] = states[p]


def _gla(qg, kg, la, vg, rg, gn, batch, seq):
    n = qg.shape[0]
    nchunk = seq // GLA_STEP
    w = jnp.asarray(_gla_weights(), BF16)
    lmap = jnp.asarray(_gla_level_map())
    row = lambda width: pl.BlockSpec((GLA_STEP, width), lambda b, c: (b * nchunk + c, 0))
    const = functools.partial(pl.BlockSpec, pipeline_mode=pl.Buffered(1))
    return pl.pallas_call(
        _gla_kernel,
        grid=(batch, nchunk),
        in_specs=[row(G_KEY_WIDTH), row(G_KEY_WIDTH), row(G_KEY_WIDTH), row(G_WIDTH), row(G_WIDTH),
                  const(w.shape, lambda b, c: (0, 0)),
                  const(lmap.shape, lambda b, c: (0, 0)),
                  const((1, G_HEAD_V), lambda b, c: (0, 0))],
        out_specs=row(G_WIDTH),
        out_shape=jax.ShapeDtypeStruct((n, G_WIDTH), BF16),
        scratch_shapes=[pltpu.VMEM((G_HEADS // 2, G_HEAD_V, LANES), F32)],
        compiler_params=pltpu.CompilerParams(dimension_semantics=("parallel", "arbitrary"),
                                             vmem_limit_bytes=VMEM_LIMIT),
        name="gla",
    )(qg, kg, la, vg, rg, w, lmap, gn)


def _memkv_kernel(m_ref, g_ref, w_ref, k_ref, v_ref):
    nb, mlen, _ = m_ref.shape
    h = _rms(m_ref[...].reshape(nb * mlen, D_MODEL), g_ref[...]).astype(BF16)
    k_ref[...] = _dot(h, w_ref[:, 0:D_MODEL]).astype(BF16).reshape(nb, mlen, D_MODEL)
    v_ref[...] = _dot(h, w_ref[:, D_MODEL:2 * D_MODEL]).astype(BF16).reshape(nb, mlen, D_MODEL)


def _memkv(mem, g, w):
    batch, mlen, _ = mem.shape
    nb = 2 if batch % 2 == 0 else 1
    const = functools.partial(pl.BlockSpec, pipeline_mode=pl.Buffered(1))
    blk = pl.BlockSpec((nb, mlen, D_MODEL), lambda b: (b, 0, 0))
    return pl.pallas_call(
        _memkv_kernel,
        grid=(batch // nb,),
        in_specs=[blk, const((1, D_MODEL), lambda b: (0, 0)), const((D_MODEL, 2 * D_MODEL), lambda b: (0, 0))],
        out_specs=[blk, blk],
        out_shape=[jax.ShapeDtypeStruct(mem.shape, BF16)] * 2,
        compiler_params=pltpu.CompilerParams(dimension_semantics=("parallel",), vmem_limit_bytes=VMEM_LIMIT),
        name="memkv",
    )(mem, g, w)


def _post_kernel(x_ref, oa_ref, og_ref, wo_ref, gx_ref, wq_ref, kx_ref, vx_ref, wxo_ref, o_ref):
    tm = x_ref.shape[0]
    halves = [slice(i * tm // POST_PARTS, (i + 1) * tm // POST_PARTS) for i in range(POST_PARTS)]
    x1 = [x_ref[r, :] + _dot(oa_ref[r, :], wo_ref[0:A_WIDTH, :]) + _dot(og_ref[r, :], wo_ref[A_WIDTH:D_MODEL, :])
          for r in halves]
    h = [_rms(v, gx_ref[...]).astype(BF16) for v in x1]
    qx = [(_dot(v, wq_ref[...]) * (LOG2E * X_HEAD_DIM ** -0.5)).astype(BF16) for v in h]
    outs = [[] for _ in halves]
    for hd in range(X_HEADS):
        sl = slice(X_HEAD_DIM * hd, X_HEAD_DIM * (hd + 1))
        s = [_nt(q[:, sl], kx_ref[0, :, sl]) for q in qx]
        pexp = [jnp.exp2(v - jnp.max(v, axis=1, keepdims=True)) for v in s]
        pn = [(v / jnp.sum(v, axis=1, keepdims=True)).astype(BF16) for v in pexp]
        for i, v in enumerate(pn):
            outs[i].append(_dot(v, vx_ref[0, :, sl]).astype(BF16))
    for i, r in enumerate(halves):
        o_ref[r, :] = x1[i] + _dot(jnp.concatenate(outs[i], axis=1), wxo_ref[...])


def _post(x2, oa, og, wo, gx, wq, kx, vx, wxo, seq):
    n = x2.shape[0]
    tm = POST_TILE
    per_b = seq // tm
    mlen = kx.shape[1]
    const = functools.partial(pl.BlockSpec, pipeline_mode=pl.Buffered(1))
    row = lambda width: pl.BlockSpec((tm, width), lambda i: (i, 0))
    mem = pl.BlockSpec((1, mlen, D_MODEL), lambda i: (i // per_b, 0, 0))
    sq = const((D_MODEL, D_MODEL), lambda i: (0, 0))
    return pl.pallas_call(
        _post_kernel,
        grid=(n // tm,),
        in_specs=[row(D_MODEL), row(A_WIDTH), row(G_WIDTH), sq, const((1, D_MODEL), lambda i: (0, 0)), sq, mem, mem, sq],
        out_specs=row(D_MODEL),
        out_shape=jax.ShapeDtypeStruct((n, D_MODEL), F32),
        compiler_params=pltpu.CompilerParams(dimension_semantics=("parallel",), vmem_limit_bytes=VMEM_LIMIT),
        name="post",
    )(x2, oa, og, wo, gx, wq, kx, vx, wxo)


def _mlp_kernel(x_ref, g_ref, wu_ref, wd_ref, gf_ref, o_ref, *, final_norm):
    tm = x_ref.shape[0]
    parts = [slice(i * tm // MLP_PARTS, (i + 1) * tm // MLP_PARTS) for i in range(MLP_PARTS)]
    acc = [x_ref[r, :] for r in parts]
    h = [_rms(x, g_ref[...]).astype(BF16) for x in acc]
    for c in range(D_FF // D_MODEL):
        sl = slice(D_MODEL * c, D_MODEL * (c + 1))
        u = [jnp.square(jnp.maximum(_dot(v, wu_ref[:, sl]), 0.0)).astype(BF16) for v in h]
        acc = [a + _dot(v, wd_ref[sl, :]) for a, v in zip(acc, u)]
    for r, a in zip(parts, acc):
        o_ref[r, :] = _rms(a, gf_ref[...]) if final_norm else a


def _mlp(x2, g, wu, wd, gf, final_norm):
    n = x2.shape[0]
    tm = MLP_TILE
    const = functools.partial(pl.BlockSpec, pipeline_mode=pl.Buffered(1))
    row = pl.BlockSpec((tm, D_MODEL), lambda i: (i, 0))
    vec = const((1, D_MODEL), lambda i: (0, 0))
    return pl.pallas_call(
        functools.partial(_mlp_kernel, final_norm=final_norm),
        grid=(n // tm,),
        in_specs=[row, vec, const((D_MODEL, D_FF), lambda i: (0, 0)), const((D_FF, D_MODEL), lambda i: (0, 0)), vec],
        out_specs=row,
        out_shape=jax.ShapeDtypeStruct((n, D_MODEL), F32),
        compiler_params=pltpu.CompilerParams(dimension_semantics=("parallel",), vmem_limit_bytes=VMEM_LIMIT),
        name="mlp",
    )(x2, g, wu, wd, gf)


def kernel(x, mem, rp_table, norm_mix, w_in, w_gate_up, b_gate, g_norm, w_out, norm_xattn, norm_mem, w_xq, w_xkv,
           w_xo, norm_mlp, w_up, w_down, norm_final):
    batch, seq, _ = x.shape
    depth = w_in.shape[0]
    assert seq % max(INPROJ_TILE, POST_TILE, MLP_TILE, GLA_STEP) == 0 and seq // MOBA_BLOCK <= 8
    x2 = x.reshape(batch * seq, D_MODEL)
    bias = _bias_tiles(rp_table)
    c31 = rp_table[RP_BUCKETS - 1]
    glr_lo = 3 * A_WIDTH + 2 * G_KEY_WIDTH + G_WIDTH
    rg_lo = glr_lo + G_GATE_RANK
    pad_rank = LANES - G_GATE_RANK
    for l in range(depth):
        col = lambda lo, hi: w_in[l, :, lo:hi].astype(BF16)
        wglr = jnp.pad(col(glr_lo, rg_lo), ((0, 0), (0, pad_rank)))
        wgu = jnp.pad(w_gate_up[l].astype(BF16), ((0, pad_rank), (0, 0)))
        qk, vt, qg, kg, la, vg, rg = _inproj(x2, norm_mix[l][None], w_in[l].astype(BF16), col(rg_lo, rg_lo + G_WIDTH),
                                             wglr, wgu, b_gate[l][None])
        oa = _moba(qk, vt, bias, c31, batch, seq)
        og = _gla(qg, kg, la, vg, rg, g_norm[l][None], batch, seq)
        kx, vx = _memkv(mem, norm_mem[l][None], w_xkv[l].astype(BF16))
        x2 = _post(x2, oa, og, w_out[l].astype(BF16), norm_xattn[l][None], w_xq[l].astype(BF16), kx, vx,
                   w_xo[l].astype(BF16), seq)
        last = l == depth - 1
        x2 = _mlp(x2, norm_mlp[l][None], w_up[l].astype(BF16), w_down[l].astype(BF16), norm_final[None], last)
    return x2.reshape(batch, seq, D_MODEL)
```

```python
import functools
import math

import numpy as np
import jax
import jax.numpy as jnp
from jax import lax
from jax.experimental import pallas as pl
from jax.experimental.pallas import tpu as pltpu

F32 = jnp.float32
BF16 = jnp.bfloat16

D_MODEL = 1024
A_HEADS = 8
A_HEAD_DIM = 64
A_WIDTH = A_HEADS * A_HEAD_DIM
MOBA_BLOCK = 256
MOBA_TOPK = 3
G_HEADS = 4
G_WIDTH = D_MODEL - A_WIDTH
G_HEAD_V = G_WIDTH // G_HEADS
G_KEY_WIDTH = G_WIDTH // 2
G_HEAD_K = G_KEY_WIDTH // G_HEADS
G_GATE_RANK = 16
G_GATE_NORM = 16.0
X_HEADS = 4
X_HEAD_DIM = D_MODEL // X_HEADS
D_FF = 4 * D_MODEL
RP_BUCKETS = 32
RP_MAX_DIST = 128
EPS = 1e-6

LANES = 128
SUBLANES = 8
GATE_ROWS = 16
NEG = -1e30
MOBA_PAIRS = 2
GLA_CHUNK = 128
GLA_STEP = 1024
LOG2E = float(np.log2(np.e))
INPROJ_TILE = 1024
INPROJ_PARTS = 2
MLP_TILE = 1024
MLP_PARTS = 2
POST_TILE = 1024
POST_PARTS = 2
VMEM_LIMIT = 56 * 1024 * 1024

def _nt(a, b):
    return lax.dot_general(a, b, (((1,), (1,)), ((), ())), preferred_element_type=F32)


def _tn(a, b):
    return lax.dot_general(a, b, (((0,), (0,)), ((), ())), preferred_element_type=F32)


def _dot(a, b):
    return jnp.dot(a, b, preferred_element_type=F32)


def _rms(x, g):
    return x * lax.rsqrt(jnp.mean(x * x, axis=-1, keepdims=True) + EPS) * g


def _bucket_thresholds():
    max_exact = RP_BUCKETS // 2
    d = np.arange(1, 4 * RP_MAX_DIST)
    val = (np.log(d.astype(np.float32) / np.float32(max_exact)) / np.float32(math.log(RP_MAX_DIST / max_exact))
           * np.float32(RP_BUCKETS - max_exact))
    bucket = np.minimum(max_exact + val.astype(np.int32), RP_BUCKETS - 1)
    return [int(d[(d >= max_exact) & (bucket >= max_exact + k)][0]) for k in range(1, RP_BUCKETS - max_exact)]


_THRESHOLDS = _bucket_thresholds()


def _bias_kernel(tab_ref, out_ref):
    h = pl.program_id(0)
    rows, width = 2 * MOBA_BLOCK, 3 * MOBA_BLOCK
    d = lax.broadcasted_iota(jnp.int32, (8, width), 1)
    max_exact = RP_BUCKETS // 2
    bucket = jnp.where(d < max_exact, d, max_exact)
    for t in _THRESHOLDS:
        bucket = bucket + jnp.where(d >= t, 1, 0)
    prof = jnp.full((8, width), NEG, F32)
    for b in range(RP_BUCKETS):
        prof = jnp.where(bucket == b, tab_ref[b, h] * LOG2E, prof)
    prof = jnp.where(d < rows, prof, NEG)
    tile = pltpu.roll(jnp.broadcast_to(prof[0:1], (rows, width)), 0, 1, stride=1, stride_axis=0)
    out_ref[0] = tile[:, MOBA_BLOCK:2 * MOBA_BLOCK]


def _bias_tiles(rp_table):
    return pl.pallas_call(
        _bias_kernel,
        grid=(A_HEADS,),
        in_specs=[pl.BlockSpec(memory_space=pltpu.SMEM)],
        out_specs=pl.BlockSpec((1, 2 * MOBA_BLOCK, MOBA_BLOCK), lambda h: (h, 0, 0)),
        out_shape=jax.ShapeDtypeStruct((A_HEADS, 2 * MOBA_BLOCK, MOBA_BLOCK), F32),
        name="bias",
    )(rp_table)


def _inproj_kernel(x_ref, g_ref, w_ref, wrg_ref, wglr_ref, wgu_ref, bg_ref, *rest):
    ncast = (len(rest) - 7) // 2
    cast_in, cast_out = rest[:ncast], rest[ncast + 7:]
    qk_ref, vt_ref, qg_ref, kg_ref, la_ref, vg_ref, rg_ref = rest[ncast:ncast + 7]
    tm = x_ref.shape[0]
    parts = [slice(i * tm // INPROJ_PARTS, (i + 1) * tm // INPROJ_PARTS) for i in range(INPROJ_PARTS)]
    hs = [_rms(x_ref[r, :], g_ref[...]).astype(BF16) for r in parts]
    va_lo = 2 * A_WIDTH
    qg_lo = va_lo + A_WIDTH
    kg_lo = qg_lo + G_KEY_WIDTH
    vg_lo = kg_lo + G_KEY_WIDTH

    glrs = [_dot(h, wglr_ref[...]).astype(BF16) for h in hs]
    for r, h in zip(parts, hs):
        qk_ref[r, 0:A_WIDTH] = (_dot(h, w_ref[:, 0:A_WIDTH]) * (LOG2E * A_HEAD_DIM ** -0.5)).astype(BF16)
    for r, h in zip(parts, hs):
        qk_ref[r, A_WIDTH:va_lo] = _dot(h, w_ref[:, A_WIDTH:va_lo]).astype(BF16)
    for r, glr in zip(parts, glrs):
        z = _dot(glr, wgu_ref[...]) + bg_ref[...]
        log_sig = jnp.minimum(z, 0.0) - jnp.log(1.0 + jnp.exp(-jnp.abs(z)))
        la_ref[r, :] = log_sig * (1.0 / G_GATE_NORM)
    for r, h in zip(parts, hs):
        vt_ref[:, r] = _dot(h, w_ref[:, va_lo:qg_lo]).astype(BF16).T
    for r, h in zip(parts, hs):
        qg_ref[r, :] = _dot(h, w_ref[:, qg_lo:kg_lo]) * (G_HEAD_K ** -0.5)
        kg_ref[r, :] = _dot(h, w_ref[:, kg_lo:vg_lo])
    for r, h in zip(parts, hs):
        vg_ref[r, :] = _dot(h, w_ref[:, vg_lo:vg_lo + G_WIDTH]).astype(BF16)
    for r, h in zip(parts, hs):
        rg_ref[r, :] = _dot(h, wrg_ref[...])
    for src, dst in zip(cast_in, cast_out):
        dst[...] = src[...].astype(BF16)


def _inproj(x2, g, w, wrg, wglr, wgu, bg, later):
    n = x2.shape[0]
    tm = INPROJ_TILE
    steps = n // tm
    const = lambda a: pl.BlockSpec(a.shape, lambda i: (0, 0), pipeline_mode=pl.Buffered(1))
    row = lambda width: pl.BlockSpec((tm, width), lambda i: (i, 0))
    assert all(a.shape[0] % (steps * GATE_ROWS) == 0 for a in later)
    part = [pl.BlockSpec((a.shape[0] // steps, a.shape[1]), lambda i: (i, 0)) for a in later]
    return pl.pallas_call(
        _inproj_kernel,
        grid=(steps,),
        in_specs=[row(D_MODEL)] + [const(a) for a in (g, w, wrg, wglr, wgu, bg)] + part,
        out_specs=[row(2 * A_WIDTH), pl.BlockSpec((A_WIDTH, tm), lambda i: (0, i)),
                   row(G_KEY_WIDTH), row(G_KEY_WIDTH), row(G_KEY_WIDTH), row(G_WIDTH), row(G_WIDTH)] + part,
        out_shape=[jax.ShapeDtypeStruct((n, 2 * A_WIDTH), BF16),
                   jax.ShapeDtypeStruct((A_WIDTH, n), BF16),
                   jax.ShapeDtypeStruct((n, G_KEY_WIDTH), F32),
                   jax.ShapeDtypeStruct((n, G_KEY_WIDTH), F32),
                   jax.ShapeDtypeStruct((n, G_KEY_WIDTH), F32),
                   jax.ShapeDtypeStruct((n, G_WIDTH), BF16),
                   jax.ShapeDtypeStruct((n, G_WIDTH), F32)] + [jax.ShapeDtypeStruct(a.shape, BF16) for a in later],
        compiler_params=pltpu.CompilerParams(dimension_semantics=("parallel",), vmem_limit_bytes=VMEM_LIMIT),
        name="inproj",
    )(x2, g, w, wrg, wglr, wgu, bg, *later)


def _moba_kernel(c31_ref, q_ref, k_ref, vt_ref, bias_ref, o_ref, khead, s_scr, *, seq):
    step = pl.program_id(1)
    nblk = seq // MOBA_BLOCK
    half = LANES // 2
    lane2 = lax.broadcasted_iota(jnp.int32, (seq, LANES), 1)
    lane_q = lax.broadcasted_iota(jnp.int32, (MOBA_BLOCK, LANES), 1)
    rowi = lax.broadcasted_iota(jnp.int32, (GATE_ROWS, MOBA_BLOCK), 0)
    blk = lambda n: slice(n * MOBA_BLOCK, (n + 1) * MOBA_BLOCK)
    pair_lanes = lambda pp: slice(pp * LANES, (pp + 1) * LANES)

    def build(pp):
        k = k_ref[:, pair_lanes(pp)]
        khead[2 * pp] = k * jnp.where(lane2 < half, 1.0, 0.0).astype(BF16)
        khead[2 * pp + 1] = k * jnp.where(lane2 >= half, 1.0, 0.0).astype(BF16)
        n_i = lax.broadcasted_iota(jnp.int32, (GATE_ROWS, seq), 0)
        t_i = lax.broadcasted_iota(jnp.int32, (GATE_ROWS, seq), 1)
        avg = jnp.where(t_i // MOBA_BLOCK == n_i, 1.0 / MOBA_BLOCK, 0.0).astype(BF16)
        km = _dot(avg, k)
        kmh = km.astype(BF16)
        return kmh, (km - kmh.astype(F32)).astype(BF16)

    def prepare(pp, kmeans, qi, h, slot):
        qm = q_ref[blk(qi), pair_lanes(pp)] * jnp.where((lane_q < half) if h == 0 else (lane_q >= half), 1.0,
                                                        0.0).astype(BF16)
        gate = _nt(kmeans[0], qm) + _nt(kmeans[1], qm) if qi > MOBA_TOPK else None
        head = 2 * pp + h
        return dict(pp=pp, qi=qi, h=h, head=head, qm=qm, gate=gate, sbuf=s_scr.at[slot], near=None, far=None,
                    acc=None, c31=c31_ref[2 * MOBA_PAIRS * step + head] * LOG2E)

    def penalty(u, n):
        return 0.0 if u["pen"] is None else u["pen"][n:n + 1, :]

    def score_matmul(u):
        u["s_all"] = _nt(khead[u["head"], 0:(u["qi"] + 1) * MOBA_BLOCK, :], u["qm"])

    def score_reduce(u):
        qi, head = u["qi"], u["head"]
        u["pen"] = None
        if u["gate"] is not None:
            g = u["gate"]
            cnt = jnp.zeros((GATE_ROWS, MOBA_BLOCK), F32)
            for m in range(qi):
                gm = g[m:m + 1, :]
                cnt = cnt + jnp.where((gm > g) | ((gm == g) & (m < rowi)), 1.0, 0.0)
            u["pen"] = jnp.where((rowi < qi) & (cnt < MOBA_TOPK), 0.0, NEG)
        for n in range(qi + 1):
            s = u["s_all"][blk(n)]
            if n == qi:
                s = s + bias_ref[head, MOBA_BLOCK:2 * MOBA_BLOCK, :]
            elif n == qi - 1:
                s = s + bias_ref[head, 0:MOBA_BLOCK, :]
            u["sbuf"][blk(n), :] = s
            t = jnp.max(s.reshape(MOBA_BLOCK // SUBLANES, SUBLANES, MOBA_BLOCK), axis=0)
            if n < qi:
                t = t + penalty(u, n)
            key = "far" if n < qi - 1 else "near"
            u[key] = t if u[key] is None else jnp.maximum(u[key], t)
        mx = u["near"] if u["far"] is None else jnp.maximum(u["near"], u["far"] + u["c31"])
        u["m_near"] = jnp.max(mx, axis=0, keepdims=True)
        u["m_far"] = u["m_near"] - u["c31"]

    def prob_exp(u):
        qi = u["qi"]
        shift = lambda n: (u["m_far"] if n < qi - 1 else u["m_near"]) - (penalty(u, n) if n < qi else 0.0)
        u["pt"] = jnp.concatenate([jnp.exp2(u["sbuf"][blk(n), :] - shift(n)).astype(BF16) for n in range(qi + 1)],
                                  axis=0)

    def prob_matmul(u):
        qi, h = u["qi"], u["h"]
        width = (qi + 1) * MOBA_BLOCK
        r0 = u["head"] * half
        vrows = vt_ref[r0:r0 + half, 0:width]
        ones_rows = jnp.ones((half, width), BF16)
        vt = jnp.concatenate([vrows, ones_rows] if h == 0 else [ones_rows, vrows], axis=0)
        u["acc"] = _dot(vt, u["pt"])

    outs = {}

    def finish(u):
        acc, h = u["acc"], u["h"]
        num = acc[half * h:half * (h + 1)]
        den = acc[half * (1 - h):half * (1 - h) + 1]
        outs[h] = num / den
        if h == 1:
            o_ref[blk(u["qi"]), pair_lanes(u["pp"])] = jnp.concatenate([outs[0], outs[1]], axis=0).T.astype(BF16)

    group = 2
    stages = [(pp, [(qi, h) for qi in range(g * group, (g + 1) * group) for h in range(2)])
              for pp in range(MOBA_PAIRS) for g in range(nblk // group)]
    prev, kmeans = [], {}
    for g, (pp, members) in enumerate(stages + [(None, [])]):
        if members and pp not in kmeans:
            kmeans[pp] = build(pp)
        cur = [prepare(pp, kmeans[pp], qi, h, (g % 2) * 2 * group + 2 * (qi % group) + h) for qi, h in members]
        for u in prev:
            prob_exp(u)
        for u in cur:
            score_matmul(u)
        for u in prev:
            prob_matmul(u)
        for u in cur:
            score_reduce(u)
        for u in prev:
            finish(u)
        prev = cur


def _moba(qk, vt, bias, c31, batch, seq):
    n = qk.shape[0]
    width = MOBA_PAIRS * LANES
    nstep = A_WIDTH // width
    kern = functools.partial(_moba_kernel, seq=seq)
    return pl.pallas_call(
        kern,
        grid=(batch, nstep),
        in_specs=[pl.BlockSpec(memory_space=pltpu.SMEM),
                  pl.BlockSpec((seq, width), lambda b, p: (b, p)),
                  pl.BlockSpec((seq, width), lambda b, p: (b, nstep + p)),
                  pl.BlockSpec((width, seq), lambda b, p: (p, b)),
                  pl.BlockSpec((2 * MOBA_PAIRS, 2 * MOBA_BLOCK, MOBA_BLOCK), lambda b, p: (p, 0, 0))],
        out_specs=pl.BlockSpec((seq, width), lambda b, p: (b, p)),
        out_shape=jax.ShapeDtypeStruct((n, A_WIDTH), BF16),
        scratch_shapes=[pltpu.VMEM((2 * MOBA_PAIRS, seq, LANES), BF16),
                        pltpu.VMEM((8, seq, MOBA_BLOCK), F32)],
        compiler_params=pltpu.CompilerParams(dimension_semantics=("parallel", "arbitrary"),
                                             vmem_limit_bytes=VMEM_LIMIT),
        name="moba",
    )(c31, qk, qk, vt, bias)


_GLA_LEVELS = int(math.log2(GLA_CHUNK))


_GLA_BIG_LEVELS = _GLA_LEVELS - int(math.log2(SUBLANES))


def _gla_weights():
    c = GLA_CHUNK
    i = np.arange(c)[:, None]
    j = np.arange(c)[None, :]
    mats = [(j <= i)]
    for lvl in range(_GLA_BIG_LEVELS, _GLA_LEVELS):
        s = (c // 2) >> lvl
        ref = (i // (2 * s)) * (2 * s) + s - 1
        mats.append(np.where(i > ref, (j > ref) & (j <= i), (j > i) & (j <= ref)))
    return np.concatenate(mats, axis=0).astype(np.float32)


def _gla_level_map():
    c = GLA_CHUNK
    i = np.arange(c)[:, None]
    j = np.arange(c)[None, :]
    top_bit = np.floor(np.log2(np.maximum(i ^ j, 1))).astype(np.int64)
    lvl = np.where(j < i, _GLA_LEVELS - 1 - top_bit, np.where(j == i, _GLA_LEVELS, _GLA_LEVELS + 1))
    return np.concatenate([lvl, lvl], axis=1).astype(np.int32)


def _gla_kernel(q_ref, k_ref, la_ref, v_ref, rg_ref, w_ref, lmap_ref, gn_ref, o_ref, st_ref):
    c = GLA_CHUNK
    half = LANES // 2
    npair = G_HEADS // 2

    @pl.when(pl.program_id(1) == 0)
    def _init():
        st_ref[...] = jnp.zeros(st_ref.shape, F32)

    w = w_ref[...]
    lmap = lmap_ref[...]
    lane = lax.broadcasted_iota(jnp.int32, (1, LANES), 1)
    lane_c = lax.broadcasted_iota(jnp.int32, (c, LANES), 1)
    lm = [jnp.where(lane_c < half, 1.0, 0.0).astype(BF16), jnp.where(lane_c >= half, 1.0, 0.0).astype(BF16)]
    row_small = lax.broadcasted_iota(jnp.int32, (c, G_KEY_WIDTH), 0)

    def level_factors(q, k, b, d_all, lvl):
        s = (c // 2) >> lvl
        if s < SUBLANES:
            j = lvl - _GLA_BIG_LEVELS
            e = jnp.exp2(d_all[c * (j + 1):c * (j + 2)])
            odd = (row_small // s) % 2 == 1
            return jnp.where(odd, q * e, 0.0), jnp.where(odd, 0.0, k * e)
        zeros = jnp.zeros((s, q.shape[1]), F32)
        qp, kp = [], []
        for blk in range(c // s):
            rows = slice(blk * s, (blk + 1) * s)
            ref = (blk // 2) * 2 * s + s - 1
            if blk % 2 == 1:
                qp.append(q[rows] * jnp.exp2(b[rows] - b[ref:ref + 1]))
                kp.append(zeros)
            else:
                qp.append(zeros)
                kp.append(k[rows] * jnp.exp2(b[ref:ref + 1] - b[rows]))
        return jnp.concatenate(qp, axis=0), jnp.concatenate(kp, axis=0)

    nch = GLA_STEP // c
    rows_of = [slice(ch * c, (ch + 1) * c) for ch in range(nch)]
    pair_lanes = [slice(LANES * p, LANES * (p + 1)) for p in range(npair)]
    d_alls = []
    for ch in range(nch):
        la = la_ref[rows_of[ch], :] * LOG2E
        hi = la.astype(BF16)
        lo = (la - hi.astype(F32)).astype(BF16)
        d_alls.append(_dot(w, hi) + _dot(w, lo))
    chunks = []
    for ch in range(nch):
        d_all = d_alls[ch]
        q = q_ref[rows_of[ch], :]
        k = k_ref[rows_of[ch], :]
        b = d_all[0:c]
        blast = b[c - 1:c, :]
        qts, kts = [], []
        for lvl in range(_GLA_LEVELS):
            qt, kt = level_factors(q, k, b, d_all, lvl)
            qts.append(qt.astype(BF16))
            kts.append(kt.astype(BF16))
        qts.append(q.astype(BF16))
        kts.append(k.astype(BF16))
        chunks.append(dict(qts=qts, kts=kts, qe=(q * jnp.exp2(b)).astype(BF16),
                           khat=(k * jnp.exp2(blast - b)).astype(BF16), dec=jnp.exp2(blast),
                           a=[jnp.zeros((c, 2 * c), F32)] * npair))
    for lvl in range(_GLA_LEVELS + 1):
        mask = lmap == lvl
        s = (c // 2) >> lvl if lvl < _GLA_BIG_LEVELS else c
        blocks = [slice(i * s, (i + 1) * s) for i in range(c // s)]
        live = [i for i in range(len(blocks)) if i % 2 == 1 or s == c]
        for u in chunks:
            for p, sl in enumerate(pair_lanes):
                kp = u["kts"][lvl][:, sl]
                ql = u["qts"][lvl]
                pm = _nt(jnp.concatenate([ql[blocks[i], sl] for i in live], axis=0),
                         jnp.concatenate([kp * lm[0], kp * lm[1]], axis=0))
                a = u["a"][p]
                pieces = [a[r] for r in blocks]
                for n, i in enumerate(live):
                    pieces[i] = jnp.where(mask[blocks[i]], pm[n * s:(n + 1) * s], pieces[i])
                u["a"] = [jnp.concatenate(pieces, axis=0) if j == p else v for j, v in enumerate(u["a"])]
    states = [st_ref[p] for p in range(npair)]
    for ch, u in enumerate(chunks):
        rows = rows_of[ch]
        for p, sl in enumerate(pair_lanes):
            stb = states[p].astype(BF16)
            ups = []
            for hh in range(2):
                h = 2 * p + hh
                vh = v_ref[rows, LANES * h:LANES * (h + 1)]
                o = _dot(u["a"][p][:, hh * c:(hh + 1) * c].astype(BF16), vh) + _nt(u["qe"][:, sl] * lm[hh], stb)
                y = _rms(o, gn_ref[...])
                rg = rg_ref[rows, LANES * h:LANES * (h + 1)]
                y = y * (rg / (1.0 + jnp.exp(-rg)))
                o_ref[rows, LANES * h:LANES * (h + 1)] = y.astype(BF16)
                ups.append(_tn(vh, u["khat"][:, sl]))
            states[p] = states[p] * u["dec"][:, sl] + jnp.where(lane < half, ups[0], ups[1])
    for p in range(npair):
        st_ref[p] = states[p]


def _gla(qg, kg, la, vg, rg, gn, batch, seq):
    n = qg.shape[0]
    nchunk = seq // GLA_STEP
    w = jnp.asarray(_gla_weights(), BF16)
    lmap = jnp.asarray(_gla_level_map())
    row = lambda width: pl.BlockSpec((GLA_STEP, width), lambda b, c: (b * nchunk + c, 0))
    const = functools.partial(pl.BlockSpec, pipeline_mode=pl.Buffered(1))
    return pl.pallas_call(
        _gla_kernel,
        grid=(batch, nchunk),
        in_specs=[row(G_KEY_WIDTH), row(G_KEY_WIDTH), row(G_KEY_WIDTH), row(G_WIDTH), row(G_WIDTH),
                  const(w.shape, lambda b, c: (0, 0)),
                  const(lmap.shape, lambda b, c: (0, 0)),
                  const((1, G_HEAD_V), lambda b, c: (0, 0))],
        out_specs=row(G_WIDTH),
        out_shape=jax.ShapeDtypeStruct((n, G_WIDTH), BF16),
        scratch_shapes=[pltpu.VMEM((G_HEADS // 2, G_HEAD_V, LANES), F32)],
        compiler_params=pltpu.CompilerParams(dimension_semantics=("parallel", "arbitrary"),
                                             vmem_limit_bytes=VMEM_LIMIT),
        name="gla",
    )(qg, kg, la, vg, rg, w, lmap, gn)


def _memkv_kernel(m_ref, g_ref, w_ref, k_ref, v_ref):
    nb, mlen, _ = m_ref.shape
    h = _rms(m_ref[...].reshape(nb * mlen, D_MODEL), g_ref[...]).astype(BF16)
    k_ref[...] = _dot(h, w_ref[:, 0:D_MODEL]).astype(BF16).reshape(nb, mlen, D_MODEL)
    v_ref[...] = _dot(h, w_ref[:, D_MODEL:2 * D_MODEL]).astype(BF16).reshape(nb, mlen, D_MODEL)


def _memkv(mem, g, w):
    batch, mlen, _ = mem.shape
    nb = 2 if batch % 2 == 0 else 1
    const = functools.partial(pl.BlockSpec, pipeline_mode=pl.Buffered(1))
    blk = pl.BlockSpec((nb, mlen, D_MODEL), lambda b: (b, 0, 0))
    return pl.pallas_call(
        _memkv_kernel,
        grid=(batch // nb,),
        in_specs=[blk, const((1, D_MODEL), lambda b: (0, 0)), const((D_MODEL, 2 * D_MODEL), lambda b: (0, 0))],
        out_specs=[blk, blk],
        out_shape=[jax.ShapeDtypeStruct(mem.shape, BF16)] * 2,
        compiler_params=pltpu.CompilerParams(dimension_semantics=("parallel",), vmem_limit_bytes=VMEM_LIMIT),
        name="memkv",
    )(mem, g, w)


def _post_kernel(x_ref, oa_ref, og_ref, wo_ref, gx_ref, wq_ref, kx_ref, vx_ref, wxo_ref, o_ref):
    tm = x_ref.shape[0]
    halves = [slice(i * tm // POST_PARTS, (i + 1) * tm // POST_PARTS) for i in range(POST_PARTS)]
    x1 = [x_ref[r, :] + _dot(oa_ref[r, :], wo_ref[0:A_WIDTH, :]) + _dot(og_ref[r, :], wo_ref[A_WIDTH:D_MODEL, :])
          for r in halves]
    h = [_rms(v, gx_ref[...]).astype(BF16) for v in x1]
    qx = [(_dot(v, wq_ref[...]) * (LOG2E * X_HEAD_DIM ** -0.5)).astype(BF16) for v in h]
    outs = [[] for _ in halves]
    for hd in range(X_HEADS):
        sl = slice(X_HEAD_DIM * hd, X_HEAD_DIM * (hd + 1))
        s = [_nt(q[:, sl], kx_ref[0, :, sl]) for q in qx]
        pexp = [jnp.exp2(v - jnp.max(v, axis=1, keepdims=True)) for v in s]
        pn = [(v / jnp.sum(v, axis=1, keepdims=True)).astype(BF16) for v in pexp]
        for i, v in enumerate(pn):
            outs[i].append(_dot(v, vx_ref[0, :, sl]).astype(BF16))
    for i, r in enumerate(halves):
        o_ref[r, :] = x1[i] + _dot(jnp.concatenate(outs[i], axis=1), wxo_ref[...])


def _post(x2, oa, og, wo, gx, wq, kx, vx, wxo, seq):
    n = x2.shape[0]
    tm = POST_TILE
    per_b = seq // tm
    mlen = kx.shape[1]
    const = functools.partial(pl.BlockSpec, pipeline_mode=pl.Buffered(1))
    row = lambda width: pl.BlockSpec((tm, width), lambda i: (i, 0))
    mem = pl.BlockSpec((1, mlen, D_MODEL), lambda i: (i // per_b, 0, 0))
    sq = const((D_MODEL, D_MODEL), lambda i: (0, 0))
    return pl.pallas_call(
        _post_kernel,
        grid=(n // tm,),
        in_specs=[row(D_MODEL), row(A_WIDTH), row(G_WIDTH), sq, const((1, D_MODEL), lambda i: (0, 0)), sq, mem, mem, sq],
        out_specs=row(D_MODEL),
        out_shape=jax.ShapeDtypeStruct((n, D_MODEL), F32),
        compiler_params=pltpu.CompilerParams(dimension_semantics=("parallel",), vmem_limit_bytes=VMEM_LIMIT),
        name="post",
    )(x2, oa, og, wo, gx, wq, kx, vx, wxo)


def _mlp_kernel(x_ref, g_ref, wu_ref, wd_ref, gf_ref, o_ref, *, final_norm):
    tm = x_ref.shape[0]
    parts = [slice(i * tm // MLP_PARTS, (i + 1) * tm // MLP_PARTS) for i in range(MLP_PARTS)]
    acc = [x_ref[r, :] for r in parts]
    h = [_rms(x, g_ref[...]).astype(BF16) for x in acc]
    for c in range(D_FF // D_MODEL):
        sl = slice(D_MODEL * c, D_MODEL * (c + 1))
        u = [jnp.square(jnp.maximum(_dot(v, wu_ref[:, sl]), 0.0)).astype(BF16) for v in h]
        acc = [a + _dot(v, wd_ref[sl, :]) for a, v in zip(acc, u)]
    for r, a in zip(parts, acc):
        o_ref[r, :] = _rms(a, gf_ref[...]) if final_norm else a


def _mlp(x2, g, wu, wd, gf, final_norm):
    n = x2.shape[0]
    tm = MLP_TILE
    const = functools.partial(pl.BlockSpec, pipeline_mode=pl.Buffered(1))
    row = pl.BlockSpec((tm, D_MODEL), lambda i: (i, 0))
    vec = const((1, D_MODEL), lambda i: (0, 0))
    return pl.pallas_call(
        functools.partial(_mlp_kernel, final_norm=final_norm),
        grid=(n // tm,),
        in_specs=[row, vec, const((D_MODEL, D_FF), lambda i: (0, 0)), const((D_FF, D_MODEL), lambda i: (0, 0)), vec],
        out_specs=row,
        out_shape=jax.ShapeDtypeStruct((n, D_MODEL), F32),
        compiler_params=pltpu.CompilerParams(dimension_semantics=("parallel",), vmem_limit_bytes=VMEM_LIMIT),
        name="mlp",
    )(x2, g, wu, wd, gf)


def kernel(x, mem, rp_table, norm_mix, w_in, w_gate_up, b_gate, g_norm, w_out, norm_xattn, norm_mem, w_xq, w_xkv,
           w_xo, norm_mlp, w_up, w_down, norm_final):
    batch, seq, _ = x.shape
    depth = w_in.shape[0]
    assert seq % max(INPROJ_TILE, POST_TILE, MLP_TILE, GLA_STEP) == 0 and seq // MOBA_BLOCK <= 8
    x2 = x.reshape(batch * seq, D_MODEL)
    bias = _bias_tiles(rp_table)
    c31 = rp_table[RP_BUCKETS - 1]
    glr_lo = 3 * A_WIDTH + 2 * G_KEY_WIDTH + G_WIDTH
    rg_lo = glr_lo + G_GATE_RANK
    pad_rank = LANES - G_GATE_RANK
    for l in range(depth):
        col = lambda lo, hi: w_in[l, :, lo:hi].astype(BF16)
        wglr = jnp.pad(col(glr_lo, rg_lo), ((0, 0), (0, pad_rank)))
        wgu = jnp.pad(w_gate_up[l].astype(BF16), ((0, pad_rank), (0, 0)))
        later = (w_out[l], w_xq[l], w_xkv[l], w_xo[l], w_up[l], w_down[l])
        qk, vt, qg, kg, la, vg, rg, wo, wxq, wxkv, wxo, wup, wdown = _inproj(
            x2, norm_mix[l][None], w_in[l].astype(BF16), col(rg_lo, rg_lo + G_WIDTH), wglr, wgu, b_gate[l][None], later)
        oa = _moba(qk, vt, bias, c31, batch, seq)
        og = _gla(qg, kg, la, vg, rg, g_norm[l][None], batch, seq)
        kx, vx = _memkv(mem, norm_mem[l][None], wxkv)
        x2 = _post(x2, oa, og, wo, norm_xattn[l][None], wxq, kx, vx, wxo, seq)
        last = l == depth - 1
        x2 = _mlp(x2, norm_mlp[l][None], wup, wdown, norm_final[None], last)
    return x2.reshape(batch, seq, D_MODEL)
```

```python
import functools
import math

import numpy as np
import jax
import jax.numpy as jnp
from jax import lax
from jax.experimental import pallas as pl
from jax.experimental.pallas import tpu as pltpu

F32 = jnp.float32
BF16 = jnp.bfloat16

D_MODEL = 1024
A_HEADS = 8
A_HEAD_DIM = 64
A_WIDTH = A_HEADS * A_HEAD_DIM
MOBA_BLOCK = 256
MOBA_TOPK = 3
G_HEADS = 4
G_WIDTH = D_MODEL - A_WIDTH
G_HEAD_V = G_WIDTH // G_HEADS
G_KEY_WIDTH = G_WIDTH // 2
G_HEAD_K = G_KEY_WIDTH // G_HEADS
G_GATE_RANK = 16
G_GATE_NORM = 16.0
X_HEADS = 4
X_HEAD_DIM = D_MODEL // X_HEADS
D_FF = 4 * D_MODEL
RP_BUCKETS = 32
RP_MAX_DIST = 128
EPS = 1e-6

LANES = 128
SUBLANES = 8
GATE_ROWS = 16
NEG = -1e30
MOBA_PAIRS = 2
GLA_CHUNK = 128
GLA_STEP = 1024
LOG2E = float(np.log2(np.e))
INPROJ_TILE = 1024
INPROJ_PARTS = 2
MLP_TILE = 1024
MLP_PARTS = 2
POST_TILE = 1024
POST_PARTS = 2
VMEM_LIMIT = 56 * 1024 * 1024

def _nt(a, b):
    return lax.dot_general(a, b, (((1,), (1,)), ((), ())), preferred_element_type=F32)


def _tn(a, b):
    return lax.dot_general(a, b, (((0,), (0,)), ((), ())), preferred_element_type=F32)


def _dot(a, b):
    return jnp.dot(a, b, preferred_element_type=F32)


def _rms(x, g):
    return x * lax.rsqrt(jnp.mean(x * x, axis=-1, keepdims=True) + EPS) * g


def _bucket_thresholds():
    max_exact = RP_BUCKETS // 2
    d = np.arange(1, 4 * RP_MAX_DIST)
    val = (np.log(d.astype(np.float32) / np.float32(max_exact)) / np.float32(math.log(RP_MAX_DIST / max_exact))
           * np.float32(RP_BUCKETS - max_exact))
    bucket = np.minimum(max_exact + val.astype(np.int32), RP_BUCKETS - 1)
    return [int(d[(d >= max_exact) & (bucket >= max_exact + k)][0]) for k in range(1, RP_BUCKETS - max_exact)]


_THRESHOLDS = _bucket_thresholds()


def _bias_kernel(tab_ref, w_ref, out_ref, wb_ref):
    wb_ref[...] = w_ref[...].astype(BF16)
    h = pl.program_id(0)
    rows, width = 2 * MOBA_BLOCK, 3 * MOBA_BLOCK
    d = lax.broadcasted_iota(jnp.int32, (SUBLANES, width), 1)
    max_exact = RP_BUCKETS // 2
    bucket = jnp.where(d < max_exact, d, max_exact)
    for t in _THRESHOLDS:
        bucket = bucket + jnp.where(d >= t, 1, 0)
    prof = jnp.full((SUBLANES, width), NEG, F32)
    for b in range(RP_BUCKETS):
        prof = jnp.where(bucket == b, tab_ref[b, h] * LOG2E, prof)
    prof = jnp.where(d < rows, prof, NEG)
    tile = pltpu.roll(jnp.broadcast_to(prof[0:1], (rows, width)), 0, 1, stride=1, stride_axis=0)
    out_ref[0] = tile[:, MOBA_BLOCK:2 * MOBA_BLOCK]


def _bias_tiles(rp_table, w):
    assert w.shape[0] % (A_HEADS * GATE_ROWS) == 0
    part = pl.BlockSpec((w.shape[0] // A_HEADS, w.shape[1]), lambda h: (h, 0))
    return pl.pallas_call(
        _bias_kernel,
        grid=(A_HEADS,),
        in_specs=[pl.BlockSpec(memory_space=pltpu.SMEM), part],
        out_specs=[pl.BlockSpec((1, 2 * MOBA_BLOCK, MOBA_BLOCK), lambda h: (h, 0, 0)), part],
        out_shape=[jax.ShapeDtypeStruct((A_HEADS, 2 * MOBA_BLOCK, MOBA_BLOCK), F32),
                   jax.ShapeDtypeStruct(w.shape, BF16)],
        name="bias",
    )(rp_table, w)


def _inproj_kernel(x_ref, g_ref, w_ref, wrg_ref, wglr_ref, wgu_ref, bg_ref, *rest):
    ncast = (len(rest) - 7) // 2
    cast_in, cast_out = rest[:ncast], rest[ncast + 7:]
    qk_ref, vt_ref, qg_ref, kg_ref, la_ref, vg_ref, rg_ref = rest[ncast:ncast + 7]
    tm = x_ref.shape[0]
    parts = [slice(i * tm // INPROJ_PARTS, (i + 1) * tm // INPROJ_PARTS) for i in range(INPROJ_PARTS)]
    hs = [_rms(x_ref[r, :], g_ref[...]).astype(BF16) for r in parts]
    va_lo = 2 * A_WIDTH
    qg_lo = va_lo + A_WIDTH
    kg_lo = qg_lo + G_KEY_WIDTH
    vg_lo = kg_lo + G_KEY_WIDTH

    glrs = [_dot(h, wglr_ref[...]).astype(BF16) for h in hs]
    for r, h in zip(parts, hs):
        qk_ref[r, 0:A_WIDTH] = (_dot(h, w_ref[:, 0:A_WIDTH]) * (LOG2E * A_HEAD_DIM ** -0.5)).astype(BF16)
    for r, h in zip(parts, hs):
        qk_ref[r, A_WIDTH:va_lo] = _dot(h, w_ref[:, A_WIDTH:va_lo]).astype(BF16)
    for r, glr in zip(parts, glrs):
        z = _dot(glr, wgu_ref[...]) + bg_ref[...]
        log_sig = jnp.minimum(z, 0.0) - jnp.log(1.0 + jnp.exp(-jnp.abs(z)))
        la_ref[r, :] = log_sig * (1.0 / G_GATE_NORM)
    for r, h in zip(parts, hs):
        vt_ref[:, r] = _dot(h, w_ref[:, va_lo:qg_lo]).astype(BF16).T
    for r, h in zip(parts, hs):
        qg_ref[r, :] = _dot(h, w_ref[:, qg_lo:kg_lo]) * (G_HEAD_K ** -0.5)
        kg_ref[r, :] = _dot(h, w_ref[:, kg_lo:vg_lo])
    for r, h in zip(parts, hs):
        vg_ref[r, :] = _dot(h, w_ref[:, vg_lo:vg_lo + G_WIDTH]).astype(BF16)
    for r, h in zip(parts, hs):
        rg_ref[r, :] = _dot(h, wrg_ref[...])
    for src, dst in zip(cast_in, cast_out):
        dst[...] = src[...].astype(BF16)


def _inproj(x2, g, w, wrg, wglr, wgu, bg, later):
    n = x2.shape[0]
    tm = INPROJ_TILE
    steps = n // tm
    const = lambda a: pl.BlockSpec(a.shape, lambda i: (0, 0), pipeline_mode=pl.Buffered(1))
    row = lambda width: pl.BlockSpec((tm, width), lambda i: (i, 0))
    assert all(a.shape[0] % (steps * GATE_ROWS) == 0 for a in later)
    part = [pl.BlockSpec((a.shape[0] // steps, a.shape[1]), lambda i: (i, 0)) for a in later]
    return pl.pallas_call(
        _inproj_kernel,
        grid=(steps,),
        in_specs=[row(D_MODEL)] + [const(a) for a in (g, w, wrg, wglr, wgu, bg)] + part,
        out_specs=[row(2 * A_WIDTH), pl.BlockSpec((A_WIDTH, tm), lambda i: (0, i)),
                   row(G_KEY_WIDTH), row(G_KEY_WIDTH), row(G_KEY_WIDTH), row(G_WIDTH), row(G_WIDTH)] + part,
        out_shape=[jax.ShapeDtypeStruct((n, 2 * A_WIDTH), BF16),
                   jax.ShapeDtypeStruct((A_WIDTH, n), BF16),
                   jax.ShapeDtypeStruct((n, G_KEY_WIDTH), F32),
                   jax.ShapeDtypeStruct((n, G_KEY_WIDTH), F32),
                   jax.ShapeDtypeStruct((n, G_KEY_WIDTH), F32),
                   jax.ShapeDtypeStruct((n, G_WIDTH), BF16),
                   jax.ShapeDtypeStruct((n, G_WIDTH), F32)] + [jax.ShapeDtypeStruct(a.shape, BF16) for a in later],
        compiler_params=pltpu.CompilerParams(dimension_semantics=("parallel",), vmem_limit_bytes=VMEM_LIMIT),
        name="inproj",
    )(x2, g, w, wrg, wglr, wgu, bg, *later)


def _moba_kernel(c31_ref, q_ref, k_ref, vt_ref, bias_ref, o_ref, khead, s_scr, *, seq):
    step = pl.program_id(1)
    nblk = seq // MOBA_BLOCK
    half = LANES // 2
    lane2 = lax.broadcasted_iota(jnp.int32, (seq, LANES), 1)
    lane_q = lax.broadcasted_iota(jnp.int32, (MOBA_BLOCK, LANES), 1)
    rowi = lax.broadcasted_iota(jnp.int32, (GATE_ROWS, MOBA_BLOCK), 0)
    blk = lambda n: slice(n * MOBA_BLOCK, (n + 1) * MOBA_BLOCK)
    pair_lanes = lambda pp: slice(pp * LANES, (pp + 1) * LANES)

    def build(pp):
        k = k_ref[:, pair_lanes(pp)]
        khead[2 * pp] = k * jnp.where(lane2 < half, 1.0, 0.0).astype(BF16)
        khead[2 * pp + 1] = k * jnp.where(lane2 >= half, 1.0, 0.0).astype(BF16)
        n_i = lax.broadcasted_iota(jnp.int32, (GATE_ROWS, seq), 0)
        t_i = lax.broadcasted_iota(jnp.int32, (GATE_ROWS, seq), 1)
        avg = jnp.where(t_i // MOBA_BLOCK == n_i, 1.0 / MOBA_BLOCK, 0.0).astype(BF16)
        km = _dot(avg, k)
        kmh = km.astype(BF16)
        return kmh, (km - kmh.astype(F32)).astype(BF16)

    def prepare(pp, kmeans, qi, h, slot):
        qm = q_ref[blk(qi), pair_lanes(pp)] * jnp.where((lane_q < half) if h == 0 else (lane_q >= half), 1.0,
                                                        0.0).astype(BF16)
        gate = _nt(kmeans[0], qm) + _nt(kmeans[1], qm) if qi > MOBA_TOPK else None
        head = 2 * pp + h
        return dict(pp=pp, qi=qi, h=h, head=head, qm=qm, gate=gate, sbuf=s_scr.at[slot], near=None, far=None,
                    acc=None, c31=c31_ref[2 * MOBA_PAIRS * step + head] * LOG2E)

    def penalty(u, n):
        return 0.0 if u["pen"] is None else u["pen"][n:n + 1, :]

    def score_matmul(u):
        u["s_all"] = _nt(khead[u["head"], 0:(u["qi"] + 1) * MOBA_BLOCK, :], u["qm"])

    def score_reduce(u):
        qi, head = u["qi"], u["head"]
        u["pen"] = None
        if u["gate"] is not None:
            g = u["gate"]
            cnt = jnp.zeros((GATE_ROWS, MOBA_BLOCK), F32)
            for m in range(qi):
                gm = g[m:m + 1, :]
                cnt = cnt + jnp.where((gm > g) | ((gm == g) & (m < rowi)), 1.0, 0.0)
            u["pen"] = jnp.where((rowi < qi) & (cnt < MOBA_TOPK), 0.0, NEG)
        for n in range(qi + 1):
            s = u["s_all"][blk(n)]
            if n == qi:
                s = s + bias_ref[head, MOBA_BLOCK:2 * MOBA_BLOCK, :]
            elif n == qi - 1:
                s = s + bias_ref[head, 0:MOBA_BLOCK, :]
            u["sbuf"][blk(n), :] = s
            t = jnp.max(s.reshape(MOBA_BLOCK // SUBLANES, SUBLANES, MOBA_BLOCK), axis=0)
            if n < qi:
                t = t + penalty(u, n)
            key = "far" if n < qi - 1 else "near"
            u[key] = t if u[key] is None else jnp.maximum(u[key], t)
        mx = u["near"] if u["far"] is None else jnp.maximum(u["near"], u["far"] + u["c31"])
        u["m_near"] = jnp.max(mx, axis=0, keepdims=True)
        u["m_far"] = u["m_near"] - u["c31"]

    def prob_exp(u):
        qi = u["qi"]
        shift = lambda n: (u["m_far"] if n < qi - 1 else u["m_near"]) - (penalty(u, n) if n < qi else 0.0)
        u["pt"] = jnp.concatenate([jnp.exp2(u["sbuf"][blk(n), :] - shift(n)).astype(BF16) for n in range(qi + 1)],
                                  axis=0)

    def prob_matmul(u):
        qi, h = u["qi"], u["h"]
        width = (qi + 1) * MOBA_BLOCK
        r0 = u["head"] * half
        vrows = vt_ref[r0:r0 + half, 0:width]
        ones_rows = jnp.ones((half, width), BF16)
        vt = jnp.concatenate([vrows, ones_rows] if h == 0 else [ones_rows, vrows], axis=0)
        u["acc"] = _dot(vt, u["pt"])

    outs = {}

    def finish(u):
        acc, h = u["acc"], u["h"]
        num = acc[half * h:half * (h + 1)]
        den = acc[half * (1 - h):half * (1 - h) + 1]
        outs[h] = num / den
        if h == 1:
            o_ref[blk(u["qi"]), pair_lanes(u["pp"])] = jnp.concatenate([outs[0], outs[1]], axis=0).T.astype(BF16)

    group = 2
    stages = [(pp, [(qi, h) for qi in range(g * group, (g + 1) * group) for h in range(2)])
              for pp in range(MOBA_PAIRS) for g in range(nblk // group)]
    prev, kmeans = [], {}
    for g, (pp, members) in enumerate(stages + [(None, [])]):
        if members and pp not in kmeans:
            kmeans[pp] = build(pp)
        cur = [prepare(pp, kmeans[pp], qi, h, (g % 2) * 2 * group + 2 * (qi % group) + h) for qi, h in members]
        for u in prev:
            prob_exp(u)
        for u in cur:
            score_matmul(u)
        for u in prev:
            prob_matmul(u)
        for u in cur:
            score_reduce(u)
        for u in prev:
            finish(u)
        prev = cur


def _moba(qk, vt, bias, c31, batch, seq):
    n = qk.shape[0]
    width = MOBA_PAIRS * LANES
    nstep = A_WIDTH // width
    kern = functools.partial(_moba_kernel, seq=seq)
    return pl.pallas_call(
        kern,
        grid=(batch, nstep),
        in_specs=[pl.BlockSpec(memory_space=pltpu.SMEM),
                  pl.BlockSpec((seq, width), lambda b, p: (b, p)),
                  pl.BlockSpec((seq, width), lambda b, p: (b, nstep + p)),
                  pl.BlockSpec((width, seq), lambda b, p: (p, b)),
                  pl.BlockSpec((2 * MOBA_PAIRS, 2 * MOBA_BLOCK, MOBA_BLOCK), lambda b, p: (p, 0, 0))],
        out_specs=pl.BlockSpec((seq, width), lambda b, p: (b, p)),
        out_shape=jax.ShapeDtypeStruct((n, A_WIDTH), BF16),
        scratch_shapes=[pltpu.VMEM((2 * MOBA_PAIRS, seq, LANES), BF16),
                        pltpu.VMEM((8, seq, MOBA_BLOCK), F32)],
        compiler_params=pltpu.CompilerParams(dimension_semantics=("parallel", "arbitrary"),
                                             vmem_limit_bytes=VMEM_LIMIT),
        name="moba",
    )(c31, qk, qk, vt, bias)


_GLA_LEVELS = int(math.log2(GLA_CHUNK))


_GLA_BIG_LEVELS = _GLA_LEVELS - int(math.log2(SUBLANES))


def _gla_weights():
    c = GLA_CHUNK
    i = np.arange(c)[:, None]
    j = np.arange(c)[None, :]
    mats = [(j <= i)]
    for lvl in range(_GLA_BIG_LEVELS, _GLA_LEVELS):
        s = (c // 2) >> lvl
        ref = (i // (2 * s)) * (2 * s) + s - 1
        mats.append(np.where(i > ref, (j > ref) & (j <= i), (j > i) & (j <= ref)))
    return np.concatenate(mats, axis=0).astype(np.float32)


def _gla_level_map():
    c = GLA_CHUNK
    i = np.arange(c)[:, None]
    j = np.arange(c)[None, :]
    top_bit = np.floor(np.log2(np.maximum(i ^ j, 1))).astype(np.int64)
    lvl = np.where(j < i, _GLA_LEVELS - 1 - top_bit, np.where(j == i, _GLA_LEVELS, _GLA_LEVELS + 1))
    return np.concatenate([lvl, lvl], axis=1).astype(np.int32)


def _gla_kernel(q_ref, k_ref, la_ref, v_ref, rg_ref, w_ref, lmap_ref, gn_ref, o_ref, st_ref):
    c = GLA_CHUNK
    half = LANES // 2
    npair = G_HEADS // 2

    @pl.when(pl.program_id(1) == 0)
    def _init():
        st_ref[...] = jnp.zeros(st_ref.shape, F32)

    w = w_ref[...]
    lmap = lmap_ref[...]
    lane = lax.broadcasted_iota(jnp.int32, (1, LANES), 1)
    lane_c = lax.broadcasted_iota(jnp.int32, (c, LANES), 1)
    lm = [jnp.where(lane_c < half, 1.0, 0.0).astype(BF16), jnp.where(lane_c >= half, 1.0, 0.0).astype(BF16)]
    row_small = lax.broadcasted_iota(jnp.int32, (c, G_KEY_WIDTH), 0)

    def level_factors(q, k, b, d_all, lvl):
        s = (c // 2) >> lvl
        if s < SUBLANES:
            j = lvl - _GLA_BIG_LEVELS
            e = jnp.exp2(d_all[c * (j + 1):c * (j + 2)])
            odd = (row_small // s) % 2 == 1
            return jnp.where(odd, q * e, 0.0), jnp.where(odd, 0.0, k * e)
        zeros = jnp.zeros((s, q.shape[1]), F32)
        qp, kp = [], []
        for blk in range(c // s):
            rows = slice(blk * s, (blk + 1) * s)
            ref = (blk // 2) * 2 * s + s - 1
            if blk % 2 == 1:
                qp.append(q[rows] * jnp.exp2(b[rows] - b[ref:ref + 1]))
                kp.append(zeros)
            else:
                qp.append(zeros)
                kp.append(k[rows] * jnp.exp2(b[ref:ref + 1] - b[rows]))
        return jnp.concatenate(qp, axis=0), jnp.concatenate(kp, axis=0)

    nch = GLA_STEP // c
    rows_of = [slice(ch * c, (ch + 1) * c) for ch in range(nch)]
    pair_lanes = [slice(LANES * p, LANES * (p + 1)) for p in range(npair)]
    d_alls = []
    for ch in range(nch):
        la = la_ref[rows_of[ch], :] * LOG2E
        hi = la.astype(BF16)
        lo = (la - hi.astype(F32)).astype(BF16)
        d_alls.append(_dot(w, hi) + _dot(w, lo))
    chunks = []
    for ch in range(nch):
        d_all = d_alls[ch]
        q = q_ref[rows_of[ch], :]
        k = k_ref[rows_of[ch], :]
        b = d_all[0:c]
        blast = b[c - 1:c, :]
        qts, kts = [], []
        for lvl in range(_GLA_LEVELS):
            qt, kt = level_factors(q, k, b, d_all, lvl)
            qts.append(qt.astype(BF16))
            kts.append(kt.astype(BF16))
        qts.append(q.astype(BF16))
        kts.append(k.astype(BF16))
        chunks.append(dict(qts=qts, kts=kts, qe=(q * jnp.exp2(b)).astype(BF16),
                           khat=(k * jnp.exp2(blast - b)).astype(BF16), dec=jnp.exp2(blast),
                           a=[jnp.zeros((c, 2 * c), F32)] * npair))
    for lvl in range(_GLA_LEVELS + 1):
        mask = lmap == lvl
        s = (c // 2) >> lvl if lvl < _GLA_BIG_LEVELS else c
        blocks = [slice(i * s, (i + 1) * s) for i in range(c // s)]
        live = [i for i in range(len(blocks)) if i % 2 == 1 or s == c]
        for u in chunks:
            for p, sl in enumerate(pair_lanes):
                kp = u["kts"][lvl][:, sl]
                ql = u["qts"][lvl]
                pm = _nt(jnp.concatenate([ql[blocks[i], sl] for i in live], axis=0),
                         jnp.concatenate([kp * lm[0], kp * lm[1]], axis=0))
                a = u["a"][p]
                pieces = [a[r] for r in blocks]
                for n, i in enumerate(live):
                    pieces[i] = jnp.where(mask[blocks[i]], pm[n * s:(n + 1) * s], pieces[i])
                u["a"] = [jnp.concatenate(pieces, axis=0) if j == p else v for j, v in enumerate(u["a"])]
    states = [st_ref[p] for p in range(npair)]
    for ch, u in enumerate(chunks):
        rows = rows_of[ch]
        for p, sl in enumerate(pair_lanes):
            stb = states[p].astype(BF16)
            ups = []
            for hh in range(2):
                h = 2 * p + hh
                vh = v_ref[rows, LANES * h:LANES * (h + 1)]
                o = _dot(u["a"][p][:, hh * c:(hh + 1) * c].astype(BF16), vh) + _nt(u["qe"][:, sl] * lm[hh], stb)
                y = _rms(o, gn_ref[...])
                rg = rg_ref[rows, LANES * h:LANES * (h + 1)]
                y = y * (rg / (1.0 + jnp.exp(-rg)))
                o_ref[rows, LANES * h:LANES * (h + 1)] = y.astype(BF16)
                ups.append(_tn(vh, u["khat"][:, sl]))
            states[p] = states[p] * u["dec"][:, sl] + jnp.where(lane < half, ups[0], ups[1])
    for p in range(npair):
        st_ref[p] = states[p]


def _gla(qg, kg, la, vg, rg, gn, batch, seq):
    n = qg.shape[0]
    nchunk = seq // GLA_STEP
    w = jnp.asarray(_gla_weights(), BF16)
    lmap = jnp.asarray(_gla_level_map())
    row = lambda width: pl.BlockSpec((GLA_STEP, width), lambda b, c: (b * nchunk + c, 0))
    const = functools.partial(pl.BlockSpec, pipeline_mode=pl.Buffered(1))
    return pl.pallas_call(
        _gla_kernel,
        grid=(batch, nchunk),
        in_specs=[row(G_KEY_WIDTH), row(G_KEY_WIDTH), row(G_KEY_WIDTH), row(G_WIDTH), row(G_WIDTH),
                  const(w.shape, lambda b, c: (0, 0)),
                  const(lmap.shape, lambda b, c: (0, 0)),
                  const((1, G_HEAD_V), lambda b, c: (0, 0))],
        out_specs=row(G_WIDTH),
        out_shape=jax.ShapeDtypeStruct((n, G_WIDTH), BF16),
        scratch_shapes=[pltpu.VMEM((G_HEADS // 2, G_HEAD_V, LANES), F32)],
        compiler_params=pltpu.CompilerParams(dimension_semantics=("parallel", "arbitrary"),
                                             vmem_limit_bytes=VMEM_LIMIT),
        name="gla",
    )(qg, kg, la, vg, rg, w, lmap, gn)


def _memkv_kernel(m_ref, g_ref, w_ref, k_ref, v_ref):
    nb, mlen, _ = m_ref.shape
    h = _rms(m_ref[...].reshape(nb * mlen, D_MODEL), g_ref[...]).astype(BF16)
    k_ref[...] = _dot(h, w_ref[:, 0:D_MODEL]).astype(BF16).reshape(nb, mlen, D_MODEL)
    v_ref[...] = _dot(h, w_ref[:, D_MODEL:2 * D_MODEL]).astype(BF16).reshape(nb, mlen, D_MODEL)


def _memkv(mem, g, w):
    batch, mlen, _ = mem.shape
    nb = 2 if batch % 2 == 0 else 1
    const = functools.partial(pl.BlockSpec, pipeline_mode=pl.Buffered(1))
    blk = pl.BlockSpec((nb, mlen, D_MODEL), lambda b: (b, 0, 0))
    return pl.pallas_call(
        _memkv_kernel,
        grid=(batch // nb,),
        in_specs=[blk, const((1, D_MODEL), lambda b: (0, 0)), const((D_MODEL, 2 * D_MODEL), lambda b: (0, 0))],
        out_specs=[blk, blk],
        out_shape=[jax.ShapeDtypeStruct(mem.shape, BF16)] * 2,
        compiler_params=pltpu.CompilerParams(dimension_semantics=("parallel",), vmem_limit_bytes=VMEM_LIMIT),
        name="memkv",
    )(mem, g, w)


def _post_kernel(x_ref, oa_ref, og_ref, wo_ref, gx_ref, wq_ref, kx_ref, vx_ref, wxo_ref, o_ref):
    tm = x_ref.shape[0]
    halves = [slice(i * tm // POST_PARTS, (i + 1) * tm // POST_PARTS) for i in range(POST_PARTS)]
    x1 = [x_ref[r, :] + _dot(oa_ref[r, :], wo_ref[0:A_WIDTH, :]) + _dot(og_ref[r, :], wo_ref[A_WIDTH:D_MODEL, :])
          for r in halves]
    h = [_rms(v, gx_ref[...]).astype(BF16) for v in x1]
    qx = [(_dot(v, wq_ref[...]) * (LOG2E * X_HEAD_DIM ** -0.5)).astype(BF16) for v in h]
    outs = [[] for _ in halves]
    for hd in range(X_HEADS):
        sl = slice(X_HEAD_DIM * hd, X_HEAD_DIM * (hd + 1))
        s = [_nt(q[:, sl], kx_ref[0, :, sl]) for q in qx]
        pexp = [jnp.exp2(v - jnp.max(v, axis=1, keepdims=True)) for v in s]
        pn = [(v / jnp.sum(v, axis=1, keepdims=True)).astype(BF16) for v in pexp]
        for i, v in enumerate(pn):
            outs[i].append(_dot(v, vx_ref[0, :, sl]).astype(BF16))
    for i, r in enumerate(halves):
        o_ref[r, :] = x1[i] + _dot(jnp.concatenate(outs[i], axis=1), wxo_ref[...])


def _post(x2, oa, og, wo, gx, wq, kx, vx, wxo, seq):
    n = x2.shape[0]
    tm = POST_TILE
    per_b = seq // tm
    mlen = kx.shape[1]
    const = functools.partial(pl.BlockSpec, pipeline_mode=pl.Buffered(1))
    row = lambda width: pl.BlockSpec((tm, width), lambda i: (i, 0))
    mem = pl.BlockSpec((1, mlen, D_MODEL), lambda i: (i // per_b, 0, 0))
    sq = const((D_MODEL, D_MODEL), lambda i: (0, 0))
    return pl.pallas_call(
        _post_kernel,
        grid=(n // tm,),
        in_specs=[row(D_MODEL), row(A_WIDTH), row(G_WIDTH), sq, const((1, D_MODEL), lambda i: (0, 0)), sq, mem, mem, sq],
        out_specs=row(D_MODEL),
        out_shape=jax.ShapeDtypeStruct((n, D_MODEL), F32),
        compiler_params=pltpu.CompilerParams(dimension_semantics=("parallel",), vmem_limit_bytes=VMEM_LIMIT),
        name="post",
    )(x2, oa, og, wo, gx, wq, kx, vx, wxo)


def _mlp_kernel(x_ref, g_ref, wu_ref, wd_ref, gf_ref, o_ref, *, final_norm):
    tm = x_ref.shape[0]
    parts = [slice(i * tm // MLP_PARTS, (i + 1) * tm // MLP_PARTS) for i in range(MLP_PARTS)]
    acc = [x_ref[r, :] for r in parts]
    h = [_rms(x, g_ref[...]).astype(BF16) for x in acc]
    for c in range(D_FF // D_MODEL):
        sl = slice(D_MODEL * c, D_MODEL * (c + 1))
        u = [jnp.square(jnp.maximum(_dot(v, wu_ref[:, sl]), 0.0)).astype(BF16) for v in h]
        acc = [a + _dot(v, wd_ref[sl, :]) for a, v in zip(acc, u)]
    for r, a in zip(parts, acc):
        o_ref[r, :] = _rms(a, gf_ref[...]) if final_norm else a


def _mlp(x2, g, wu, wd, gf, final_norm):
    n = x2.shape[0]
    tm = MLP_TILE
    const = functools.partial(pl.BlockSpec, pipeline_mode=pl.Buffered(1))
    row = pl.BlockSpec((tm, D_MODEL), lambda i: (i, 0))
    vec = const((1, D_MODEL), lambda i: (0, 0))
    return pl.pallas_call(
        functools.partial(_mlp_kernel, final_norm=final_norm),
        grid=(n // tm,),
        in_specs=[row, vec, const((D_MODEL, D_FF), lambda i: (0, 0)), const((D_FF, D_MODEL), lambda i: (0, 0)), vec],
        out_specs=row,
        out_shape=jax.ShapeDtypeStruct((n, D_MODEL), F32),
        compiler_params=pltpu.CompilerParams(dimension_semantics=("parallel",), vmem_limit_bytes=VMEM_LIMIT),
        name="mlp",
    )(x2, g, wu, wd, gf)


def kernel(x, mem, rp_table, norm_mix, w_in, w_gate_up, b_gate, g_norm, w_out, norm_xattn, norm_mem, w_xq, w_xkv,
           w_xo, norm_mlp, w_up, w_down, norm_final):
    batch, seq, _ = x.shape
    depth = w_in.shape[0]
    assert seq % max(INPROJ_TILE, POST_TILE, MLP_TILE, GLA_STEP) == 0 and seq // MOBA_BLOCK <= 8
    x2 = x.reshape(batch * seq, D_MODEL)
    bias, w = _bias_tiles(rp_table, w_in[0])
    c31 = rp_table[RP_BUCKETS - 1]
    glr_lo = 3 * A_WIDTH + 2 * G_KEY_WIDTH + G_WIDTH
    rg_lo = glr_lo + G_GATE_RANK
    pad_rank = LANES - G_GATE_RANK
    for l in range(depth):
        wglr = jnp.pad(w[:, glr_lo:rg_lo], ((0, 0), (0, pad_rank)))
        wgu = jnp.pad(w_gate_up[l].astype(BF16), ((0, pad_rank), (0, 0)))
        later = (w_out[l], w_xq[l], w_xkv[l], w_xo[l], w_up[l], w_down[l]) + ((w_in[l + 1],) if l + 1 < depth else ())
        qk, vt, qg, kg, la, vg, rg, wo, wxq, wxkv, wxo, wup, wdown, *w_next = _inproj(
            x2, norm_mix[l][None], w, w[:, rg_lo:rg_lo + G_WIDTH], wglr, wgu, b_gate[l][None], later)
        w = w_next[0] if w_next else None
        oa = _moba(qk, vt, bias, c31, batch, seq)
        og = _gla(qg, kg, la, vg, rg, g_norm[l][None], batch, seq)
        kx, vx = _memkv(mem, norm_mem[l][None], wxkv)
        x2 = _post(x2, oa, og, wo, norm_xattn[l][None], wxq, kx, vx, wxo, seq)
        last = l == depth - 1
        x2 = _mlp(x2, norm_mlp[l][None], wup, wdown, norm_final[None], last)
    return x2.reshape(batch, seq, D_MODEL)
```

```python
import functools
import math

import numpy as np
import jax
import jax.numpy as jnp
from jax import lax
from jax.experimental import pallas as pl
from jax.experimental.pallas import tpu as pltpu

F32 = jnp.float32
BF16 = jnp.bfloat16

D_MODEL = 1024
A_HEADS = 8
A_HEAD_DIM = 64
A_WIDTH = A_HEADS * A_HEAD_DIM
MOBA_BLOCK = 256
MOBA_TOPK = 3
G_HEADS = 4
G_WIDTH = D_MODEL - A_WIDTH
G_HEAD_V = G_WIDTH // G_HEADS
G_KEY_WIDTH = G_WIDTH // 2
G_HEAD_K = G_KEY_WIDTH // G_HEADS
G_GATE_RANK = 16
G_GATE_NORM = 16.0
X_HEADS = 4
X_HEAD_DIM = D_MODEL // X_HEADS
D_FF = 4 * D_MODEL
RP_BUCKETS = 32
RP_MAX_DIST = 128
EPS = 1e-6

LANES = 128
SUBLANES = 8
GATE_ROWS = 16
NEG = -1e30
MOBA_PAIRS = 2
GLA_CHUNK = 128
GLA_STEP = 1024
LOG2E = float(np.log2(np.e))
INPROJ_TILE = 1024
INPROJ_PARTS = 2
MLP_TILE = 1024
MLP_PARTS = 2
POST_TILE = 1024
POST_PARTS = 2
VMEM_LIMIT = 56 * 1024 * 1024

def _nt(a, b):
    return lax.dot_general(a, b, (((1,), (1,)), ((), ())), preferred_element_type=F32)


def _tn(a, b):
    return lax.dot_general(a, b, (((0,), (0,)), ((), ())), preferred_element_type=F32)


def _dot(a, b):
    return jnp.dot(a, b, preferred_element_type=F32)


def _rms(x, g):
    return x * lax.rsqrt(jnp.mean(x * x, axis=-1, keepdims=True) + EPS) * g


def _bucket_thresholds():
    max_exact = RP_BUCKETS // 2
    d = np.arange(1, 4 * RP_MAX_DIST)
    val = (np.log(d.astype(np.float32) / np.float32(max_exact)) / np.float32(math.log(RP_MAX_DIST / max_exact))
           * np.float32(RP_BUCKETS - max_exact))
    bucket = np.minimum(max_exact + val.astype(np.int32), RP_BUCKETS - 1)
    return [int(d[(d >= max_exact) & (bucket >= max_exact + k)][0]) for k in range(1, RP_BUCKETS - max_exact)]


_THRESHOLDS = _bucket_thresholds()


def _bias_kernel(tab_ref, out_ref):
    h = pl.program_id(0)
    rows, width = 2 * MOBA_BLOCK, 3 * MOBA_BLOCK
    d = lax.broadcasted_iota(jnp.int32, (8, width), 1)
    max_exact = RP_BUCKETS // 2
    bucket = jnp.where(d < max_exact, d, max_exact)
    for t in _THRESHOLDS:
        bucket = bucket + jnp.where(d >= t, 1, 0)
    prof = jnp.full((8, width), NEG, F32)
    for b in range(RP_BUCKETS):
        prof = jnp.where(bucket == b, tab_ref[b, h] * LOG2E, prof)
    prof = jnp.where(d < rows, prof, NEG)
    tile = pltpu.roll(jnp.broadcast_to(prof[0:1], (rows, width)), 0, 1, stride=1, stride_axis=0)
    out_ref[0] = tile[:, MOBA_BLOCK:2 * MOBA_BLOCK]


def _bias_tiles(rp_table):
    return pl.pallas_call(
        _bias_kernel,
        grid=(A_HEADS,),
        in_specs=[pl.BlockSpec(memory_space=pltpu.SMEM)],
        out_specs=pl.BlockSpec((1, 2 * MOBA_BLOCK, MOBA_BLOCK), lambda h: (h, 0, 0)),
        out_shape=jax.ShapeDtypeStruct((A_HEADS, 2 * MOBA_BLOCK, MOBA_BLOCK), F32),
        name="bias",
    )(rp_table)


def _inproj_kernel(x_ref, g_ref, w_ref, wrg_ref, wglr_ref, wgu_ref, bg_ref, *rest):
    ncast = (len(rest) - 7) // 2
    cast_in, cast_out = rest[:ncast], rest[ncast + 7:]
    qk_ref, vt_ref, qg_ref, kg_ref, la_ref, vg_ref, rg_ref = rest[ncast:ncast + 7]
    tm = x_ref.shape[0]
    parts = [slice(i * tm // INPROJ_PARTS, (i + 1) * tm // INPROJ_PARTS) for i in range(INPROJ_PARTS)]
    hs = [_rms(x_ref[r, :], g_ref[...]).astype(BF16) for r in parts]
    va_lo = 2 * A_WIDTH
    qg_lo = va_lo + A_WIDTH
    kg_lo = qg_lo + G_KEY_WIDTH
    vg_lo = kg_lo + G_KEY_WIDTH

    glrs = [_dot(h, wglr_ref[...]).astype(BF16) for h in hs]
    for r, h in zip(parts, hs):
        qk_ref[r, 0:A_WIDTH] = (_dot(h, w_ref[:, 0:A_WIDTH]) * (LOG2E * A_HEAD_DIM ** -0.5)).astype(BF16)
    for r, h in zip(parts, hs):
        qk_ref[r, A_WIDTH:va_lo] = _dot(h, w_ref[:, A_WIDTH:va_lo]).astype(BF16)
    for r, glr in zip(parts, glrs):
        z = _dot(glr, wgu_ref[...]) + bg_ref[...]
        log_sig = jnp.minimum(z, 0.0) - jnp.log(1.0 + jnp.exp(-jnp.abs(z)))
        la_ref[r, :] = log_sig * (1.0 / G_GATE_NORM)
    for r, h in zip(parts, hs):
        vt_ref[:, r] = _dot(h, w_ref[:, va_lo:qg_lo]).astype(BF16).T
    for r, h in zip(parts, hs):
        qg_ref[r, :] = _dot(h, w_ref[:, qg_lo:kg_lo]) * (G_HEAD_K ** -0.5)
        kg_ref[r, :] = _dot(h, w_ref[:, kg_lo:vg_lo])
    for r, h in zip(parts, hs):
        vg_ref[r, :] = _dot(h, w_ref[:, vg_lo:vg_lo + G_WIDTH]).astype(BF16)
    for r, h in zip(parts, hs):
        rg_ref[r, :] = _dot(h, wrg_ref[...])
    for src, dst in zip(cast_in, cast_out):
        dst[...] = src[...].astype(BF16)


def _inproj(x2, g, w, wrg, wglr, wgu, bg, later):
    n = x2.shape[0]
    tm = INPROJ_TILE
    steps = n // tm
    const = lambda a: pl.BlockSpec(a.shape, lambda i: (0, 0), pipeline_mode=pl.Buffered(1))
    row = lambda width: pl.BlockSpec((tm, width), lambda i: (i, 0))
    assert all(a.shape[0] % (steps * GATE_ROWS) == 0 for a in later)
    part = [pl.BlockSpec((a.shape[0] // steps, a.shape[1]), lambda i: (i, 0)) for a in later]
    return pl.pallas_call(
        _inproj_kernel,
        grid=(steps,),
        in_specs=[row(D_MODEL)] + [const(a) for a in (g, w, wrg, wglr, wgu, bg)] + part,
        out_specs=[row(2 * A_WIDTH), pl.BlockSpec((A_WIDTH, tm), lambda i: (0, i)),
                   row(G_KEY_WIDTH), row(G_KEY_WIDTH), row(G_KEY_WIDTH), row(G_WIDTH), row(G_WIDTH)] + part,
        out_shape=[jax.ShapeDtypeStruct((n, 2 * A_WIDTH), BF16),
                   jax.ShapeDtypeStruct((A_WIDTH, n), BF16),
                   jax.ShapeDtypeStruct((n, G_KEY_WIDTH), F32),
                   jax.ShapeDtypeStruct((n, G_KEY_WIDTH), F32),
                   jax.ShapeDtypeStruct((n, G_KEY_WIDTH), F32),
                   jax.ShapeDtypeStruct((n, G_WIDTH), BF16),
                   jax.ShapeDtypeStruct((n, G_WIDTH), F32)] + [jax.ShapeDtypeStruct(a.shape, BF16) for a in later],
        compiler_params=pltpu.CompilerParams(dimension_semantics=("parallel",), vmem_limit_bytes=VMEM_LIMIT),
        name="inproj",
    )(x2, g, w, wrg, wglr, wgu, bg, *later)


def _moba_kernel(c31_ref, q_ref, k_ref, vt_ref, bias_ref, o_ref, khead, s_scr, *, seq):
    step = pl.program_id(1)
    nblk = seq // MOBA_BLOCK
    half = LANES // 2
    lane2 = lax.broadcasted_iota(jnp.int32, (seq, LANES), 1)
    lane_q = lax.broadcasted_iota(jnp.int32, (MOBA_BLOCK, LANES), 1)
    rowi = lax.broadcasted_iota(jnp.int32, (GATE_ROWS, MOBA_BLOCK), 0)
    blk = lambda n: slice(n * MOBA_BLOCK, (n + 1) * MOBA_BLOCK)
    pair_lanes = lambda pp: slice(pp * LANES, (pp + 1) * LANES)

    def build(pp):
        k = k_ref[:, pair_lanes(pp)]
        khead[2 * pp] = k * jnp.where(lane2 < half, 1.0, 0.0).astype(BF16)
        khead[2 * pp + 1] = k * jnp.where(lane2 >= half, 1.0, 0.0).astype(BF16)
        n_i = lax.broadcasted_iota(jnp.int32, (GATE_ROWS, seq), 0)
        t_i = lax.broadcasted_iota(jnp.int32, (GATE_ROWS, seq), 1)
        avg = jnp.where(t_i // MOBA_BLOCK == n_i, 1.0 / MOBA_BLOCK, 0.0).astype(BF16)
        km = _dot(avg, k)
        kmh = km.astype(BF16)
        return kmh, (km - kmh.astype(F32)).astype(BF16)

    def prepare(pp, kmeans, qi, h, slot):
        qm = q_ref[blk(qi), pair_lanes(pp)] * jnp.where((lane_q < half) if h == 0 else (lane_q >= half), 1.0,
                                                        0.0).astype(BF16)
        gate = _nt(kmeans[0], qm) + _nt(kmeans[1], qm) if qi > MOBA_TOPK else None
        head = 2 * pp + h
        return dict(pp=pp, qi=qi, h=h, head=head, qm=qm, gate=gate, sbuf=s_scr.at[slot], near=None, far=None,
                    acc=None, c31=c31_ref[2 * MOBA_PAIRS * step + head] * LOG2E)

    def penalty(u, n):
        return 0.0 if u["pen"] is None else u["pen"][n:n + 1, :]

    def score_matmul(u):
        u["s_all"] = _nt(khead[u["head"], 0:(u["qi"] + 1) * MOBA_BLOCK, :], u["qm"])

    def score_reduce(u):
        qi, head = u["qi"], u["head"]
        u["pen"] = None
        if u["gate"] is not None:
            g = u["gate"]
            cnt = jnp.zeros((GATE_ROWS, MOBA_BLOCK), F32)
            for m in range(qi):
                gm = g[m:m + 1, :]
                cnt = cnt + jnp.where((gm > g) | ((gm == g) & (m < rowi)), 1.0, 0.0)
            u["pen"] = jnp.where((rowi < qi) & (cnt < MOBA_TOPK), 0.0, NEG)
        for n in range(qi + 1):
            s = u["s_all"][blk(n)]
            if n == qi:
                s = s + bias_ref[head, MOBA_BLOCK:2 * MOBA_BLOCK, :]
            elif n == qi - 1:
                s = s + bias_ref[head, 0:MOBA_BLOCK, :]
            u["sbuf"][blk(n), :] = s
            t = jnp.max(s.reshape(MOBA_BLOCK // SUBLANES, SUBLANES, MOBA_BLOCK), axis=0)
            if n < qi:
                t = t + penalty(u, n)
            key = "far" if n < qi - 1 else "near"
            u[key] = t if u[key] is None else jnp.maximum(u[key], t)
        mx = u["near"] if u["far"] is None else jnp.maximum(u["near"], u["far"] + u["c31"])
        u["m_near"] = jnp.max(mx, axis=0, keepdims=True)
        u["m_far"] = u["m_near"] - u["c31"]

    def prob_exp(u):
        qi = u["qi"]
        shift = lambda n: (u["m_far"] if n < qi - 1 else u["m_near"]) - (penalty(u, n) if n < qi else 0.0)
        u["pt"] = jnp.concatenate([jnp.exp2(u["sbuf"][blk(n), :] - shift(n)).astype(BF16) for n in range(qi + 1)],
                                  axis=0)

    def prob_matmul(u):
        qi, h = u["qi"], u["h"]
        width = (qi + 1) * MOBA_BLOCK
        r0 = u["head"] * half
        vrows = vt_ref[r0:r0 + half, 0:width]
        ones_rows = jnp.ones((half, width), BF16)
        vt = jnp.concatenate([vrows, ones_rows] if h == 0 else [ones_rows, vrows], axis=0)
        u["acc"] = _dot(vt, u["pt"])

    outs = {}

    def finish(u):
        acc, h = u["acc"], u["h"]
        num = acc[half * h:half * (h + 1)]
        den = acc[half * (1 - h):half * (1 - h) + 1]
        outs[h] = num / den
        if h == 1:
            o_ref[blk(u["qi"]), pair_lanes(u["pp"])] = jnp.concatenate([outs[0], outs[1]], axis=0).T.astype(BF16)

    group = 2
    stages = [(pp, [(qi, h) for qi in range(g * group, (g + 1) * group) for h in range(2)])
              for pp in range(MOBA_PAIRS) for g in range(nblk // group)]
    prev, kmeans = [], {}
    for g, (pp, members) in enumerate(stages + [(None, [])]):
        if members and pp not in kmeans:
            kmeans[pp] = build(pp)
        cur = [prepare(pp, kmeans[pp], qi, h, (g % 2) * 2 * group + 2 * (qi % group) + h) for qi, h in members]
        for u in prev:
            prob_exp(u)
        for u in cur:
            score_matmul(u)
        for u in prev:
            prob_matmul(u)
        for u in cur:
            score_reduce(u)
        for u in prev:
            finish(u)
        prev = cur


def _moba(qk, vt, bias, c31, batch, seq):
    n = qk.shape[0]
    width = MOBA_PAIRS * LANES
    nstep = A_WIDTH // width
    kern = functools.partial(_moba_kernel, seq=seq)
    return pl.pallas_call(
        kern,
        grid=(batch, nstep),
        in_specs=[pl.BlockSpec(memory_space=pltpu.SMEM),
                  pl.BlockSpec((seq, width), lambda b, p: (b, p)),
                  pl.BlockSpec((seq, width), lambda b, p: (b, nstep + p)),
                  pl.BlockSpec((width, seq), lambda b, p: (p, b)),
                  pl.BlockSpec((2 * MOBA_PAIRS, 2 * MOBA_BLOCK, MOBA_BLOCK), lambda b, p: (p, 0, 0))],
        out_specs=pl.BlockSpec((seq, width), lambda b, p: (b, p)),
        out_shape=jax.ShapeDtypeStruct((n, A_WIDTH), BF16),
        scratch_shapes=[pltpu.VMEM((2 * MOBA_PAIRS, seq, LANES), BF16),
                        pltpu.VMEM((8, seq, MOBA_BLOCK), F32)],
        compiler_params=pltpu.CompilerParams(dimension_semantics=("parallel", "arbitrary"),
                                             vmem_limit_bytes=VMEM_LIMIT),
        name="moba",
    )(c31, qk, qk, vt, bias)


_GLA_LEVELS = int(math.log2(GLA_CHUNK))


_GLA_BIG_LEVELS = _GLA_LEVELS - int(math.log2(SUBLANES))


def _gla_weights():
    c = GLA_CHUNK
    i = np.arange(c)[:, None]
    j = np.arange(c)[None, :]
    mats = [(j <= i)]
    for lvl in range(_GLA_BIG_LEVELS, _GLA_LEVELS):
        s = (c // 2) >> lvl
        ref = (i // (2 * s)) * (2 * s) + s - 1
        mats.append(np.where(i > ref, (j > ref) & (j <= i), (j > i) & (j <= ref)))
    return np.concatenate(mats, axis=0).astype(np.float32)


def _gla_level_map():
    c = GLA_CHUNK
    i = np.arange(c)[:, None]
    j = np.arange(c)[None, :]
    top_bit = np.floor(np.log2(np.maximum(i ^ j, 1))).astype(np.int64)
    lvl = np.where(j < i, _GLA_LEVELS - 1 - top_bit, np.where(j == i, _GLA_LEVELS, _GLA_LEVELS + 1))
    return np.concatenate([lvl, lvl], axis=1).astype(np.int32)


def _gla_kernel(q_ref, k_ref, la_ref, v_ref, rg_ref, w_ref, lmap_ref, gn_ref, o_ref, st_ref):
    c = GLA_CHUNK
    half = LANES // 2
    npair = G_HEADS // 2

    @pl.when(pl.program_id(1) == 0)
    def _init():
        st_ref[...] = jnp.zeros(st_ref.shape, F32)

    w = w_ref[...]
    lmap = lmap_ref[...]
    lane = lax.broadcasted_iota(jnp.int32, (1, LANES), 1)
    lane_c = lax.broadcasted_iota(jnp.int32, (c, LANES), 1)
    lm = [jnp.where(lane_c < half, 1.0, 0.0).astype(BF16), jnp.where(lane_c >= half, 1.0, 0.0).astype(BF16)]
    row_small = lax.broadcasted_iota(jnp.int32, (c, G_KEY_WIDTH), 0)

    def level_factors(q, k, b, d_all, lvl):
        s = (c // 2) >> lvl
        if s < SUBLANES:
            j = lvl - _GLA_BIG_LEVELS
            e = jnp.exp2(d_all[c * (j + 1):c * (j + 2)])
            odd = (row_small // s) % 2 == 1
            return jnp.where(odd, q * e, 0.0), jnp.where(odd, 0.0, k * e)
        zeros = jnp.zeros((s, q.shape[1]), F32)
        qp, kp = [], []
        for blk in range(c // s):
            rows = slice(blk * s, (blk + 1) * s)
            ref = (blk // 2) * 2 * s + s - 1
            if blk % 2 == 1:
                qp.append(q[rows] * jnp.exp2(b[rows] - b[ref:ref + 1]))
                kp.append(zeros)
            else:
                qp.append(zeros)
                kp.append(k[rows] * jnp.exp2(b[ref:ref + 1] - b[rows]))
        return jnp.concatenate(qp, axis=0), jnp.concatenate(kp, axis=0)

    nch = GLA_STEP // c
    rows_of = [slice(ch * c, (ch + 1) * c) for ch in range(nch)]
    pair_lanes = [slice(LANES * p, LANES * (p + 1)) for p in range(npair)]

    def decays(ch):
        la = la_ref[rows_of[ch], :] * LOG2E
        hi = la.astype(BF16)
        lo = (la - hi.astype(F32)).astype(BF16)
        return _dot(w, hi) + _dot(w, lo)

    def factors(ch, d_all):
        q = q_ref[rows_of[ch], :]
        k = k_ref[rows_of[ch], :]
        b = d_all[0:c]
        blast = b[c - 1:c, :]
        qts, kts = [], []
        for lvl in range(_GLA_LEVELS):
            qt, kt = level_factors(q, k, b, d_all, lvl)
            qts.append(qt.astype(BF16))
            kts.append(kt.astype(BF16))
        qts.append(q.astype(BF16))
        kts.append(k.astype(BF16))
        return dict(ch=ch, qts=qts, kts=kts, qe=(q * jnp.exp2(b)).astype(BF16),
                    khat=(k * jnp.exp2(blast - b)).astype(BF16), dec=jnp.exp2(blast),
                    a=[jnp.zeros((c, 2 * c), F32)] * npair)

    def level_scores(units):
        for lvl in range(_GLA_LEVELS + 1):
            mask = lmap == lvl
            s = (c // 2) >> lvl if lvl < _GLA_BIG_LEVELS else c
            blocks = [slice(i * s, (i + 1) * s) for i in range(c // s)]
            live = [i for i in range(len(blocks)) if i % 2 == 1 or s == c]
            for u in units:
                for p, sl in enumerate(pair_lanes):
                    kp = u["kts"][lvl][:, sl]
                    ql = u["qts"][lvl]
                    pm = _nt(jnp.concatenate([ql[blocks[i], sl] for i in live], axis=0),
                             jnp.concatenate([kp * lm[0], kp * lm[1]], axis=0))
                    a = u["a"][p]
                    pieces = [a[r] for r in blocks]
                    for n, i in enumerate(live):
                        pieces[i] = jnp.where(mask[blocks[i]], pm[n * s:(n + 1) * s], pieces[i])
                    u["a"] = [jnp.concatenate(pieces, axis=0) if j == p else v for j, v in enumerate(u["a"])]

    def stateless_matmuls(units):
        for u in units:
            vhs = [v_ref[rows_of[u["ch"]], LANES * h:LANES * (h + 1)] for h in range(G_HEADS)]
            u["intra"] = [_dot(u["a"][h // 2][:, (h % 2) * c:(h % 2 + 1) * c].astype(BF16), vhs[h])
                          for h in range(G_HEADS)]
            u["inc"] = [_tn(vhs[h], u["khat"][:, pair_lanes[h // 2]]) for h in range(G_HEADS)]

    def state_chain(units, states):
        for u in units:
            rows = rows_of[u["ch"]]
            for p, sl in enumerate(pair_lanes):
                stb = states[p].astype(BF16)
                for hh in range(2):
                    h = 2 * p + hh
                    o = u["intra"][h] + _nt(u["qe"][:, sl] * lm[hh], stb)
                    y = _rms(o, gn_ref[...])
                    rg = rg_ref[rows, LANES * h:LANES * (h + 1)]
                    y = y * (rg / (1.0 + jnp.exp(-rg)))
                    o_ref[rows, LANES * h:LANES * (h + 1)] = y.astype(BF16)
                states[p] = states[p] * u["dec"][:, sl] + jnp.where(lane < half, u["inc"][2 * p], u["inc"][2 * p + 1])

    d_alls = [decays(ch) for ch in range(nch)]
    chunks = [factors(ch, d_alls[ch]) for ch in range(nch)]
    level_scores(chunks)
    stateless_matmuls(chunks)
    states = [st_ref[p] for p in range(npair)]
    state_chain(chunks, states)
    for p in range(npair):
        st_ref[p] = states[p]


def _gla(qg, kg, la, vg, rg, gn, batch, seq):
    n = qg.shape[0]
    nchunk = seq // GLA_STEP
    w = jnp.asarray(_gla_weights(), BF16)
    lmap = jnp.asarray(_gla_level_map())
    row = lambda width: pl.BlockSpec((GLA_STEP, width), lambda b, c: (b * nchunk + c, 0))
    const = functools.partial(pl.BlockSpec, pipeline_mode=pl.Buffered(1))
    return pl.pallas_call(
        _gla_kernel,
        grid=(batch, nchunk),
        in_specs=[row(G_KEY_WIDTH), row(G_KEY_WIDTH), row(G_KEY_WIDTH), row(G_WIDTH), row(G_WIDTH),
                  const(w.shape, lambda b, c: (0, 0)),
                  const(lmap.shape, lambda b, c: (0, 0)),
                  const((1, G_HEAD_V), lambda b, c: (0, 0))],
        out_specs=row(G_WIDTH),
        out_shape=jax.ShapeDtypeStruct((n, G_WIDTH), BF16),
        scratch_shapes=[pltpu.VMEM((G_HEADS // 2, G_HEAD_V, LANES), F32)],
        compiler_params=pltpu.CompilerParams(dimension_semantics=("parallel", "arbitrary"),
                                             vmem_limit_bytes=VMEM_LIMIT),
        name="gla",
    )(qg, kg, la, vg, rg, w, lmap, gn)


def _memkv_kernel(m_ref, g_ref, w_ref, k_ref, v_ref):
    nb, mlen, _ = m_ref.shape
    h = _rms(m_ref[...].reshape(nb * mlen, D_MODEL), g_ref[...]).astype(BF16)
    k_ref[...] = _dot(h, w_ref[:, 0:D_MODEL]).astype(BF16).reshape(nb, mlen, D_MODEL)
    v_ref[...] = _dot(h, w_ref[:, D_MODEL:2 * D_MODEL]).astype(BF16).reshape(nb, mlen, D_MODEL)


def _memkv(mem, g, w):
    batch, mlen, _ = mem.shape
    nb = 2 if batch % 2 == 0 else 1
    const = functools.partial(pl.BlockSpec, pipeline_mode=pl.Buffered(1))
    blk = pl.BlockSpec((nb, mlen, D_MODEL), lambda b: (b, 0, 0))
    return pl.pallas_call(
        _memkv_kernel,
        grid=(batch // nb,),
        in_specs=[blk, const((1, D_MODEL), lambda b: (0, 0)), const((D_MODEL, 2 * D_MODEL), lambda b: (0, 0))],
        out_specs=[blk, blk],
        out_shape=[jax.ShapeDtypeStruct(mem.shape, BF16)] * 2,
        compiler_params=pltpu.CompilerParams(dimension_semantics=("parallel",), vmem_limit_bytes=VMEM_LIMIT),
        name="memkv",
    )(mem, g, w)


def _post_kernel(x_ref, oa_ref, og_ref, wo_ref, gx_ref, wq_ref, kx_ref, vx_ref, wxo_ref, o_ref):
    tm = x_ref.shape[0]
    halves = [slice(i * tm // POST_PARTS, (i + 1) * tm // POST_PARTS) for i in range(POST_PARTS)]
    x1 = [x_ref[r, :] + _dot(oa_ref[r, :], wo_ref[0:A_WIDTH, :]) + _dot(og_ref[r, :], wo_ref[A_WIDTH:D_MODEL, :])
          for r in halves]
    h = [_rms(v, gx_ref[...]).astype(BF16) for v in x1]
    qx = [(_dot(v, wq_ref[...]) * (LOG2E * X_HEAD_DIM ** -0.5)).astype(BF16) for v in h]
    heads = [slice(X_HEAD_DIM * hd, X_HEAD_DIM * (hd + 1)) for hd in range(X_HEADS)]
    scores = [[_nt(q[:, sl], kx_ref[0, :, sl]) for q in qx] for sl in heads]
    outs = [[] for _ in halves]
    for sl, s in zip(heads, scores):
        pexp = [jnp.exp2(v - jnp.max(v, axis=1, keepdims=True)) for v in s]
        pn = [(v / jnp.sum(v, axis=1, keepdims=True)).astype(BF16) for v in pexp]
        for i, v in enumerate(pn):
            outs[i].append(_dot(v, vx_ref[0, :, sl]).astype(BF16))
    for i, r in enumerate(halves):
        o_ref[r, :] = x1[i] + _dot(jnp.concatenate(outs[i], axis=1), wxo_ref[...])


def _post(x2, oa, og, wo, gx, wq, kx, vx, wxo, seq):
    n = x2.shape[0]
    tm = POST_TILE
    per_b = seq // tm
    mlen = kx.shape[1]
    const = functools.partial(pl.BlockSpec, pipeline_mode=pl.Buffered(1))
    row = lambda width: pl.BlockSpec((tm, width), lambda i: (i, 0))
    mem = pl.BlockSpec((1, mlen, D_MODEL), lambda i: (i // per_b, 0, 0))
    sq = const((D_MODEL, D_MODEL), lambda i: (0, 0))
    return pl.pallas_call(
        _post_kernel,
        grid=(n // tm,),
        in_specs=[row(D_MODEL), row(A_WIDTH), row(G_WIDTH), sq, const((1, D_MODEL), lambda i: (0, 0)), sq, mem, mem, sq],
        out_specs=row(D_MODEL),
        out_shape=jax.ShapeDtypeStruct((n, D_MODEL), F32),
        compiler_params=pltpu.CompilerParams(dimension_semantics=("parallel",), vmem_limit_bytes=VMEM_LIMIT),
        name="post",
    )(x2, oa, og, wo, gx, wq, kx, vx, wxo)


def _mlp_kernel(x_ref, g_ref, wu_ref, wd_ref, gf_ref, o_ref, *, final_norm):
    tm = x_ref.shape[0]
    parts = [slice(i * tm // MLP_PARTS, (i + 1) * tm // MLP_PARTS) for i in range(MLP_PARTS)]
    acc = [x_ref[r, :] for r in parts]
    h = [_rms(x, g_ref[...]).astype(BF16) for x in acc]
    for c in range(D_FF // D_MODEL):
        sl = slice(D_MODEL * c, D_MODEL * (c + 1))
        u = [jnp.square(jnp.maximum(_dot(v, wu_ref[:, sl]), 0.0)).astype(BF16) for v in h]
        acc = [a + _dot(v, wd_ref[sl, :]) for a, v in zip(acc, u)]
    for r, a in zip(parts, acc):
        o_ref[r, :] = _rms(a, gf_ref[...]) if final_norm else a


def _mlp(x2, g, wu, wd, gf, final_norm):
    n = x2.shape[0]
    tm = MLP_TILE
    const = functools.partial(pl.BlockSpec, pipeline_mode=pl.Buffered(1))
    row = pl.BlockSpec((tm, D_MODEL), lambda i: (i, 0))
    vec = const((1, D_MODEL), lambda i: (0, 0))
    return pl.pallas_call(
        functools.partial(_mlp_kernel, final_norm=final_norm),
        grid=(n // tm,),
        in_specs=[row, vec, const((D_MODEL, D_FF), lambda i: (0, 0)), const((D_FF, D_MODEL), lambda i: (0, 0)), vec],
        out_specs=row,
        out_shape=jax.ShapeDtypeStruct((n, D_MODEL), F32),
        compiler_params=pltpu.CompilerParams(dimension_semantics=("parallel",), vmem_limit_bytes=VMEM_LIMIT),
        name="mlp",
    )(x2, g, wu, wd, gf)


def kernel(x, mem, rp_table, norm_mix, w_in, w_gate_up, b_gate, g_norm, w_out, norm_xattn, norm_mem, w_xq, w_xkv,
           w_xo, norm_mlp, w_up, w_down, norm_final):
    batch, seq, _ = x.shape
    depth = w_in.shape[0]
    assert seq % max(INPROJ_TILE, POST_TILE, MLP_TILE, GLA_STEP) == 0 and seq // MOBA_BLOCK <= 8
    x2 = x.reshape(batch * seq, D_MODEL)
    bias = _bias_tiles(rp_table)
    c31 = rp_table[RP_BUCKETS - 1]
    glr_lo = 3 * A_WIDTH + 2 * G_KEY_WIDTH + G_WIDTH
    rg_lo = glr_lo + G_GATE_RANK
    pad_rank = LANES - G_GATE_RANK
    for l in range(depth):
        col = lambda lo, hi: w_in[l, :, lo:hi].astype(BF16)
        wglr = jnp.pad(col(glr_lo, rg_lo), ((0, 0), (0, pad_rank)))
        wgu = jnp.pad(w_gate_up[l].astype(BF16), ((0, pad_rank), (0, 0)))
        later = (w_out[l], w_xq[l], w_xkv[l], w_xo[l], w_up[l], w_down[l])
        qk, vt, qg, kg, la, vg, rg, wo, wxq, wxkv, wxo, wup, wdown = _inproj(
            x2, norm_mix[l][None], w_in[l].astype(BF16), col(rg_lo, rg_lo + G_WIDTH), wglr, wgu, b_gate[l][None], later)
        oa = _moba(qk, vt, bias, c31, batch, seq)
        og = _gla(qg, kg, la, vg, rg, g_norm[l][None], batch, seq)
        kx, vx = _memkv(mem, norm_mem[l][None], wxkv)
        x2 = _post(x2, oa, og, wo, norm_xattn[l][None], wxq, kx, vx, wxo, seq)
        last = l == depth - 1
        x2 = _mlp(x2, norm_mlp[l][None], wup, wdown, norm_final[None], last)
    return x2.reshape(batch, seq, D_MODEL)
```

```python
import functools
import math

import numpy as np
import jax
import jax.numpy as jnp
from jax import lax
from jax.experimental import pallas as pl
from jax.experimental.pallas import tpu as pltpu

F32 = jnp.float32
BF16 = jnp.bfloat16

D_MODEL = 1024
A_HEADS = 8
A_HEAD_DIM = 64
A_WIDTH = A_HEADS * A_HEAD_DIM
MOBA_BLOCK = 256
MOBA_TOPK = 3
G_HEADS = 4
G_WIDTH = D_MODEL - A_WIDTH
G_HEAD_V = G_WIDTH // G_HEADS
G_KEY_WIDTH = G_WIDTH // 2
G_HEAD_K = G_KEY_WIDTH // G_HEADS
G_GATE_RANK = 16
G_GATE_NORM = 16.0
X_HEADS = 4
X_HEAD_DIM = D_MODEL // X_HEADS
D_FF = 4 * D_MODEL
RP_BUCKETS = 32
RP_MAX_DIST = 128
EPS = 1e-6

LANES = 128
SUBLANES = 8
GATE_ROWS = 16
NEG = -1e30
MOBA_PAIRS = 2
GLA_CHUNK = 128
GLA_STEP = 1024
LOG2E = float(np.log2(np.e))
INPROJ_TILE = 1024
INPROJ_PARTS = 2
MLP_TILE = 1024
MLP_PARTS = 2
POST_TILE = 1024
POST_PARTS = 2
VMEM_LIMIT = 56 * 1024 * 1024

def _nt(a, b):
    return lax.dot_general(a, b, (((1,), (1,)), ((), ())), preferred_element_type=F32)


def _tn(a, b):
    return lax.dot_general(a, b, (((0,), (0,)), ((), ())), preferred_element_type=F32)


def _dot(a, b):
    return jnp.dot(a, b, preferred_element_type=F32)


def _rms(x, g):
    return x * lax.rsqrt(jnp.mean(x * x, axis=-1, keepdims=True) + EPS) * g


def _bucket_thresholds():
    max_exact = RP_BUCKETS // 2
    d = np.arange(1, 4 * RP_MAX_DIST)
    val = (np.log(d.astype(np.float32) / np.float32(max_exact)) / np.float32(math.log(RP_MAX_DIST / max_exact))
           * np.float32(RP_BUCKETS - max_exact))
    bucket = np.minimum(max_exact + val.astype(np.int32), RP_BUCKETS - 1)
    return [int(d[(d >= max_exact) & (bucket >= max_exact + k)][0]) for k in range(1, RP_BUCKETS - max_exact)]


_THRESHOLDS = _bucket_thresholds()


def _bias_kernel(tab_ref, out_ref):
    h = pl.program_id(0)
    rows, width = 2 * MOBA_BLOCK, 3 * MOBA_BLOCK
    d = lax.broadcasted_iota(jnp.int32, (8, width), 1)
    max_exact = RP_BUCKETS // 2
    bucket = jnp.where(d < max_exact, d, max_exact)
    for t in _THRESHOLDS:
        bucket = bucket + jnp.where(d >= t, 1, 0)
    prof = jnp.full((8, width), NEG, F32)
    for b in range(RP_BUCKETS):
        prof = jnp.where(bucket == b, tab_ref[b, h] * LOG2E, prof)
    prof = jnp.where(d < rows, prof, NEG)
    tile = pltpu.roll(jnp.broadcast_to(prof[0:1], (rows, width)), 0, 1, stride=1, stride_axis=0)
    out_ref[0] = tile[:, MOBA_BLOCK:2 * MOBA_BLOCK]


def _bias_tiles(rp_table):
    return pl.pallas_call(
        _bias_kernel,
        grid=(A_HEADS,),
        in_specs=[pl.BlockSpec(memory_space=pltpu.SMEM)],
        out_specs=pl.BlockSpec((1, 2 * MOBA_BLOCK, MOBA_BLOCK), lambda h: (h, 0, 0)),
        out_shape=jax.ShapeDtypeStruct((A_HEADS, 2 * MOBA_BLOCK, MOBA_BLOCK), F32),
        name="bias",
    )(rp_table)


def _inproj_kernel(x_ref, g_ref, w_ref, wrg_ref, wglr_ref, wgu_ref, bg_ref, *rest):
    ncast = (len(rest) - 7) // 2
    cast_in, cast_out = rest[:ncast], rest[ncast + 7:]
    qk_ref, vt_ref, qg_ref, kg_ref, la_ref, vg_ref, rg_ref = rest[ncast:ncast + 7]
    tm = x_ref.shape[0]
    parts = [slice(i * tm // INPROJ_PARTS, (i + 1) * tm // INPROJ_PARTS) for i in range(INPROJ_PARTS)]
    hs = [_rms(x_ref[r, :], g_ref[...]).astype(BF16) for r in parts]
    va_lo = 2 * A_WIDTH
    qg_lo = va_lo + A_WIDTH
    kg_lo = qg_lo + G_KEY_WIDTH
    vg_lo = kg_lo + G_KEY_WIDTH

    glrs = [_dot(h, wglr_ref[...]).astype(BF16) for h in hs]
    for r, h in zip(parts, hs):
        qk_ref[r, 0:A_WIDTH] = (_dot(h, w_ref[:, 0:A_WIDTH]) * (LOG2E * A_HEAD_DIM ** -0.5)).astype(BF16)
    for r, h in zip(parts, hs):
        qk_ref[r, A_WIDTH:va_lo] = _dot(h, w_ref[:, A_WIDTH:va_lo]).astype(BF16)
    for r, glr in zip(parts, glrs):
        z = _dot(glr, wgu_ref[...]) + bg_ref[...]
        log_sig = jnp.minimum(z, 0.0) - jnp.log(1.0 + jnp.exp(-jnp.abs(z)))
        la_ref[r, :] = log_sig * (1.0 / G_GATE_NORM)
    for r, h in zip(parts, hs):
        vt_ref[:, r] = _dot(h, w_ref[:, va_lo:qg_lo]).astype(BF16).T
    for r, h in zip(parts, hs):
        qg_ref[r, :] = _dot(h, w_ref[:, qg_lo:kg_lo]) * (G_HEAD_K ** -0.5)
        kg_ref[r, :] = _dot(h, w_ref[:, kg_lo:vg_lo])
    for r, h in zip(parts, hs):
        vg_ref[r, :] = _dot(h, w_ref[:, vg_lo:vg_lo + G_WIDTH]).astype(BF16)
    for r, h in zip(parts, hs):
        rg_ref[r, :] = _dot(h, wrg_ref[...])
    for src, dst in zip(cast_in, cast_out):
        dst[...] = src[...].astype(BF16)


def _inproj(x2, g, w, wrg, wglr, wgu, bg, later):
    n = x2.shape[0]
    tm = INPROJ_TILE
    steps = n // tm
    const = lambda a: pl.BlockSpec(a.shape, lambda i: (0, 0), pipeline_mode=pl.Buffered(1))
    row = lambda width: pl.BlockSpec((tm, width), lambda i: (i, 0))
    assert all(a.shape[0] % (steps * GATE_ROWS) == 0 for a in later)
    part = [pl.BlockSpec((a.shape[0] // steps, a.shape[1]), lambda i: (i, 0)) for a in later]
    return pl.pallas_call(
        _inproj_kernel,
        grid=(steps,),
        in_specs=[row(D_MODEL)] + [const(a) for a in (g, w, wrg, wglr, wgu, bg)] + part,
        out_specs=[row(2 * A_WIDTH), pl.BlockSpec((A_WIDTH, tm), lambda i: (0, i)),
                   row(G_KEY_WIDTH), row(G_KEY_WIDTH), row(G_KEY_WIDTH), row(G_WIDTH), row(G_WIDTH)] + part,
        out_shape=[jax.ShapeDtypeStruct((n, 2 * A_WIDTH), BF16),
                   jax.ShapeDtypeStruct((A_WIDTH, n), BF16),
                   jax.ShapeDtypeStruct((n, G_KEY_WIDTH), F32),
                   jax.ShapeDtypeStruct((n, G_KEY_WIDTH), F32),
                   jax.ShapeDtypeStruct((n, G_KEY_WIDTH), F32),
                   jax.ShapeDtypeStruct((n, G_WIDTH), BF16),
                   jax.ShapeDtypeStruct((n, G_WIDTH), F32)] + [jax.ShapeDtypeStruct(a.shape, BF16) for a in later],
        compiler_params=pltpu.CompilerParams(dimension_semantics=("parallel",), vmem_limit_bytes=VMEM_LIMIT),
        name="inproj",
    )(x2, g, w, wrg, wglr, wgu, bg, *later)


def _moba_kernel(c31_ref, q_ref, k_ref, vt_ref, bias_ref, o_ref, khead, s_scr, *, seq):
    step = pl.program_id(1)
    nblk = seq // MOBA_BLOCK
    half = LANES // 2
    lane2 = lax.broadcasted_iota(jnp.int32, (seq, LANES), 1)
    lane_q = lax.broadcasted_iota(jnp.int32, (MOBA_BLOCK, LANES), 1)
    rowi = lax.broadcasted_iota(jnp.int32, (GATE_ROWS, MOBA_BLOCK), 0)
    blk = lambda n: slice(n * MOBA_BLOCK, (n + 1) * MOBA_BLOCK)
    pair_lanes = lambda pp: slice(pp * LANES, (pp + 1) * LANES)

    def build(pp):
        k = k_ref[:, pair_lanes(pp)]
        khead[2 * pp] = k * jnp.where(lane2 < half, 1.0, 0.0).astype(BF16)
        khead[2 * pp + 1] = k * jnp.where(lane2 >= half, 1.0, 0.0).astype(BF16)
        n_i = lax.broadcasted_iota(jnp.int32, (GATE_ROWS, seq), 0)
        t_i = lax.broadcasted_iota(jnp.int32, (GATE_ROWS, seq), 1)
        avg = jnp.where(t_i // MOBA_BLOCK == n_i, 1.0 / MOBA_BLOCK, 0.0).astype(BF16)
        km = _dot(avg, k)
        kmh = km.astype(BF16)
        return kmh, (km - kmh.astype(F32)).astype(BF16)

    def prepare(pp, kmeans, qi, h, slot):
        qm = q_ref[blk(qi), pair_lanes(pp)] * jnp.where((lane_q < half) if h == 0 else (lane_q >= half), 1.0,
                                                        0.0).astype(BF16)
        gate = _nt(kmeans[0], qm) + _nt(kmeans[1], qm) if qi > MOBA_TOPK else None
        head = 2 * pp + h
        return dict(pp=pp, qi=qi, h=h, head=head, qm=qm, gate=gate, sbuf=s_scr.at[slot], near=None, far=None,
                    acc=None, c31=c31_ref[2 * MOBA_PAIRS * step + head] * LOG2E)

    def penalty(u, n):
        return 0.0 if u["pen"] is None else u["pen"][n:n + 1, :]

    def score_matmul(u):
        u["s_all"] = _nt(khead[u["head"], 0:(u["qi"] + 1) * MOBA_BLOCK, :], u["qm"])

    def score_reduce(u):
        qi, head = u["qi"], u["head"]
        u["pen"] = None
        if u["gate"] is not None:
            g = u["gate"]
            cnt = jnp.zeros((GATE_ROWS, MOBA_BLOCK), F32)
            for m in range(qi):
                gm = g[m:m + 1, :]
                cnt = cnt + jnp.where((gm > g) | ((gm == g) & (m < rowi)), 1.0, 0.0)
            u["pen"] = jnp.where((rowi < qi) & (cnt < MOBA_TOPK), 0.0, NEG)
        for n in range(qi + 1):
            s = u["s_all"][blk(n)]
            if n == qi:
                s = s + bias_ref[head, MOBA_BLOCK:2 * MOBA_BLOCK, :]
            elif n == qi - 1:
                s = s + bias_ref[head, 0:MOBA_BLOCK, :]
            u["sbuf"][blk(n), :] = s
            t = jnp.max(s.reshape(MOBA_BLOCK // SUBLANES, SUBLANES, MOBA_BLOCK), axis=0)
            if n < qi:
                t = t + penalty(u, n)
            key = "far" if n < qi - 1 else "near"
            u[key] = t if u[key] is None else jnp.maximum(u[key], t)
        mx = u["near"] if u["far"] is None else jnp.maximum(u["near"], u["far"] + u["c31"])
        u["m_near"] = jnp.max(mx, axis=0, keepdims=True)
        u["m_far"] = u["m_near"] - u["c31"]

    def prob_exp(u):
        qi = u["qi"]
        shift = lambda n: (u["m_far"] if n < qi - 1 else u["m_near"]) - (penalty(u, n) if n < qi else 0.0)
        u["pt"] = jnp.concatenate([jnp.exp2(u["sbuf"][blk(n), :] - shift(n)).astype(BF16) for n in range(qi + 1)],
                                  axis=0)

    def prob_matmul(u):
        qi, h = u["qi"], u["h"]
        width = (qi + 1) * MOBA_BLOCK
        r0 = u["head"] * half
        vrows = vt_ref[r0:r0 + half, 0:width]
        ones_rows = jnp.ones((half, width), BF16)
        vt = jnp.concatenate([vrows, ones_rows] if h == 0 else [ones_rows, vrows], axis=0)
        u["acc"] = _dot(vt, u["pt"])

    outs = {}

    def finish(u):
        acc, h = u["acc"], u["h"]
        num = acc[half * h:half * (h + 1)]
        den = acc[half * (1 - h):half * (1 - h) + 1]
        outs[h] = num / den
        if h == 1:
            o_ref[blk(u["qi"]), pair_lanes(u["pp"])] = jnp.concatenate([outs[0], outs[1]], axis=0).T.astype(BF16)

    group = 2
    stages = [(pp, [(qi, h) for qi in range(g * group, (g + 1) * group) for h in range(2)])
              for pp in range(MOBA_PAIRS) for g in range(nblk // group)]
    prev, kmeans = [], {}
    for g, (pp, members) in enumerate(stages + [(None, [])]):
        if members and pp not in kmeans:
            kmeans[pp] = build(pp)
        cur = [prepare(pp, kmeans[pp], qi, h, (g % 2) * 2 * group + 2 * (qi % group) + h) for qi, h in members]
        for u in prev:
            prob_exp(u)
        for u in cur:
            score_matmul(u)
        for u in prev:
            prob_matmul(u)
        for u in cur:
            score_reduce(u)
        for u in prev:
            finish(u)
        prev = cur


def _moba(qk, vt, bias, c31, batch, seq):
    n = qk.shape[0]
    width = MOBA_PAIRS * LANES
    nstep = A_WIDTH // width
    kern = functools.partial(_moba_kernel, seq=seq)
    return pl.pallas_call(
        kern,
        grid=(batch, nstep),
        in_specs=[pl.BlockSpec(memory_space=pltpu.SMEM),
                  pl.BlockSpec((seq, width), lambda b, p: (b, p)),
                  pl.BlockSpec((seq, width), lambda b, p: (b, nstep + p)),
                  pl.BlockSpec((width, seq), lambda b, p: (p, b)),
                  pl.BlockSpec((2 * MOBA_PAIRS, 2 * MOBA_BLOCK, MOBA_BLOCK), lambda b, p: (p, 0, 0))],
        out_specs=pl.BlockSpec((seq, width), lambda b, p: (b, p)),
        out_shape=jax.ShapeDtypeStruct((n, A_WIDTH), BF16),
        scratch_shapes=[pltpu.VMEM((2 * MOBA_PAIRS, seq, LANES), BF16),
                        pltpu.VMEM((8, seq, MOBA_BLOCK), F32)],
        compiler_params=pltpu.CompilerParams(dimension_semantics=("parallel", "arbitrary"),
                                             vmem_limit_bytes=VMEM_LIMIT),
        name="moba",
    )(c31, qk, qk, vt, bias)


_GLA_LEVELS = int(math.log2(GLA_CHUNK))


_GLA_BIG_LEVELS = _GLA_LEVELS - int(math.log2(SUBLANES))


def _gla_weights():
    c = GLA_CHUNK
    i = np.arange(c)[:, None]
    j = np.arange(c)[None, :]
    mats = [(j <= i)]
    for lvl in range(_GLA_BIG_LEVELS, _GLA_LEVELS):
        s = (c // 2) >> lvl
        ref = (i // (2 * s)) * (2 * s) + s - 1
        mats.append(np.where(i > ref, (j > ref) & (j <= i), (j > i) & (j <= ref)))
    return np.concatenate(mats, axis=0).astype(np.float32)


def _gla_level_map():
    c = GLA_CHUNK
    i = np.arange(c)[:, None]
    j = np.arange(c)[None, :]
    top_bit = np.floor(np.log2(np.maximum(i ^ j, 1))).astype(np.int64)
    lvl = np.where(j < i, _GLA_LEVELS - 1 - top_bit, np.where(j == i, _GLA_LEVELS, _GLA_LEVELS + 1))
    return np.concatenate([lvl, lvl], axis=1).astype(np.int32)


def _gla_kernel(q_ref, k_ref, la_ref, v_ref, rg_ref, w_ref, lmap_ref, gn_ref, o_ref, st_ref):
    c = GLA_CHUNK
    half = LANES // 2
    npair = G_HEADS // 2

    @pl.when(pl.program_id(1) == 0)
    def _init():
        st_ref[...] = jnp.zeros(st_ref.shape, F32)

    w = w_ref[...]
    lmap = lmap_ref[...]
    lane = lax.broadcasted_iota(jnp.int32, (1, LANES), 1)
    lane_c = lax.broadcasted_iota(jnp.int32, (c, LANES), 1)
    lm = [jnp.where(lane_c < half, 1.0, 0.0).astype(BF16), jnp.where(lane_c >= half, 1.0, 0.0).astype(BF16)]
    row_small = lax.broadcasted_iota(jnp.int32, (c, G_KEY_WIDTH), 0)

    def level_factors(q, k, b, d_all, lvl):
        s = (c // 2) >> lvl
        if s < SUBLANES:
            j = lvl - _GLA_BIG_LEVELS
            e = jnp.exp2(d_all[c * (j + 1):c * (j + 2)])
            odd = (row_small // s) % 2 == 1
            return jnp.where(odd, q * e, 0.0), jnp.where(odd, 0.0, k * e)
        zeros = jnp.zeros((s, q.shape[1]), F32)
        qp, kp = [], []
        for blk in range(c // s):
            rows = slice(blk * s, (blk + 1) * s)
            ref = (blk // 2) * 2 * s + s - 1
            if blk % 2 == 1:
                qp.append(q[rows] * jnp.exp2(b[rows] - b[ref:ref + 1]))
                kp.append(zeros)
            else:
                qp.append(zeros)
                kp.append(k[rows] * jnp.exp2(b[ref:ref + 1] - b[rows]))
        return jnp.concatenate(qp, axis=0), jnp.concatenate(kp, axis=0)

    nch = GLA_STEP // c
    rows_of = [slice(ch * c, (ch + 1) * c) for ch in range(nch)]
    pair_lanes = [slice(LANES * p, LANES * (p + 1)) for p in range(npair)]

    def decays(ch):
        la = la_ref[rows_of[ch], :] * LOG2E
        hi = la.astype(BF16)
        lo = (la - hi.astype(F32)).astype(BF16)
        return _dot(w, hi) + _dot(w, lo)

    def factors(ch, d_all):
        q = q_ref[rows_of[ch], :]
        k = k_ref[rows_of[ch], :]
        b = d_all[0:c]
        blast = b[c - 1:c, :]
        qts, kts = [], []
        for lvl in range(_GLA_LEVELS):
            qt, kt = level_factors(q, k, b, d_all, lvl)
            qts.append(qt.astype(BF16))
            kts.append(kt.astype(BF16))
        qts.append(q.astype(BF16))
        kts.append(k.astype(BF16))
        return dict(ch=ch, qts=qts, kts=kts, qe=(q * jnp.exp2(b)).astype(BF16),
                    khat=(k * jnp.exp2(blast - b)).astype(BF16), dec=jnp.exp2(blast),
                    a=[jnp.zeros((c, 2 * c), F32)] * npair)

    def level_scores(units):
        for lvl in range(_GLA_LEVELS + 1):
            mask = lmap == lvl
            s = (c // 2) >> lvl if lvl < _GLA_BIG_LEVELS else c
            blocks = [slice(i * s, (i + 1) * s) for i in range(c // s)]
            live = [i for i in range(len(blocks)) if i % 2 == 1 or s == c]
            for u in units:
                for p, sl in enumerate(pair_lanes):
                    kp = u["kts"][lvl][:, sl]
                    ql = u["qts"][lvl]
                    pm = _nt(jnp.concatenate([ql[blocks[i], sl] for i in live], axis=0),
                             jnp.concatenate([kp * lm[0], kp * lm[1]], axis=0))
                    a = u["a"][p]
                    pieces = [a[r] for r in blocks]
                    for n, i in enumerate(live):
                        pieces[i] = jnp.where(mask[blocks[i]], pm[n * s:(n + 1) * s], pieces[i])
                    u["a"] = [jnp.concatenate(pieces, axis=0) if j == p else v for j, v in enumerate(u["a"])]

    def stateless_matmuls(units):
        for u in units:
            vhs = [v_ref[rows_of[u["ch"]], LANES * h:LANES * (h + 1)] for h in range(G_HEADS)]
            u["intra"] = [_dot(u["a"][h // 2][:, (h % 2) * c:(h % 2 + 1) * c].astype(BF16), vhs[h])
                          for h in range(G_HEADS)]
            u["inc"] = [_tn(vhs[h], u["khat"][:, pair_lanes[h // 2]]) for h in range(G_HEADS)]

    def state_step(u, states):
        u["o"] = []
        for p, sl in enumerate(pair_lanes):
            stb = states[p].astype(BF16)
            for hh in range(2):
                u["o"].append(u["intra"][2 * p + hh] + _nt(u["qe"][:, sl] * lm[hh], stb))
            states[p] = states[p] * u["dec"][:, sl] + jnp.where(lane < half, u["inc"][2 * p], u["inc"][2 * p + 1])

    def epilogue(u):
        rows = rows_of[u["ch"]]
        for h, o in enumerate(u["o"]):
            rg = rg_ref[rows, LANES * h:LANES * (h + 1)]
            y = _rms(o, gn_ref[...]) * (rg / (1.0 + jnp.exp(-rg)))
            o_ref[rows, LANES * h:LANES * (h + 1)] = y.astype(BF16)

    d_alls = [decays(ch) for ch in range(nch)]
    chunks = [factors(ch, d_alls[ch]) for ch in range(nch)]
    level_scores(chunks)
    states = [st_ref[p] for p in range(npair)]
    stateless_matmuls(chunks[:1])
    for i, u in enumerate(chunks):
        state_step(u, states)
        stateless_matmuls(chunks[i + 1:i + 2])
        epilogue(u)
    for p in range(npair):
        st_ref[p] = states[p]


def _gla(qg, kg, la, vg, rg, gn, batch, seq):
    n = qg.shape[0]
    nchunk = seq // GLA_STEP
    w = jnp.asarray(_gla_weights(), BF16)
    lmap = jnp.asarray(_gla_level_map())
    row = lambda width: pl.BlockSpec((GLA_STEP, width), lambda b, c: (b * nchunk + c, 0))
    const = functools.partial(pl.BlockSpec, pipeline_mode=pl.Buffered(1))
    return pl.pallas_call(
        _gla_kernel,
        grid=(batch, nchunk),
        in_specs=[row(G_KEY_WIDTH), row(G_KEY_WIDTH), row(G_KEY_WIDTH), row(G_WIDTH), row(G_WIDTH),
                  const(w.shape, lambda b, c: (0, 0)),
                  const(lmap.shape, lambda b, c: (0, 0)),
                  const((1, G_HEAD_V), lambda b, c: (0, 0))],
        out_specs=row(G_WIDTH),
        out_shape=jax.ShapeDtypeStruct((n, G_WIDTH), BF16),
        scratch_shapes=[pltpu.VMEM((G_HEADS // 2, G_HEAD_V, LANES), F32)],
        compiler_params=pltpu.CompilerParams(dimension_semantics=("parallel", "arbitrary"),
                                             vmem_limit_bytes=VMEM_LIMIT),
        name="gla",
    )(qg, kg, la, vg, rg, w, lmap, gn)


def _memkv_kernel(m_ref, g_ref, w_ref, k_ref, v_ref):
    nb, mlen, _ = m_ref.shape
    h = _rms(m_ref[...].reshape(nb * mlen, D_MODEL), g_ref[...]).astype(BF16)
    k_ref[...] = _dot(h, w_ref[:, 0:D_MODEL]).astype(BF16).reshape(nb, mlen, D_MODEL)
    v_ref[...] = _dot(h, w_ref[:, D_MODEL:2 * D_MODEL]).astype(BF16).reshape(nb, mlen, D_MODEL)


def _memkv(mem, g, w):
    batch, mlen, _ = mem.shape
    nb = 2 if batch % 2 == 0 else 1
    const = functools.partial(pl.BlockSpec, pipeline_mode=pl.Buffered(1))
    blk = pl.BlockSpec((nb, mlen, D_MODEL), lambda b: (b, 0, 0))
    return pl.pallas_call(
        _memkv_kernel,
        grid=(batch // nb,),
        in_specs=[blk, const((1, D_MODEL), lambda b: (0, 0)), const((D_MODEL, 2 * D_MODEL), lambda b: (0, 0))],
        out_specs=[blk, blk],
        out_shape=[jax.ShapeDtypeStruct(mem.shape, BF16)] * 2,
        compiler_params=pltpu.CompilerParams(dimension_semantics=("parallel",), vmem_limit_bytes=VMEM_LIMIT),
        name="memkv",
    )(mem, g, w)


def _post_kernel(x_ref, oa_ref, og_ref, wo_ref, gx_ref, wq_ref, kx_ref, vx_ref, wxo_ref, o_ref):
    tm = x_ref.shape[0]
    parts = [slice(i * tm // POST_PARTS, (i + 1) * tm // POST_PARTS) for i in range(POST_PARTS)]
    heads = [slice(X_HEAD_DIM * hd, X_HEAD_DIM * (hd + 1)) for hd in range(X_HEADS)]

    def out_proj(r):
        return x_ref[r, :] + _dot(oa_ref[r, :], wo_ref[0:A_WIDTH, :]) + _dot(og_ref[r, :], wo_ref[A_WIDTH:D_MODEL, :])

    def scores(x1):
        h = _rms(x1, gx_ref[...]).astype(BF16)
        qx = (_dot(h, wq_ref[...]) * (LOG2E * X_HEAD_DIM ** -0.5)).astype(BF16)
        return [_nt(qx[:, sl], kx_ref[0, :, sl]) for sl in heads]

    def attend(r, x1, s_heads):
        outs = []
        for sl, s in zip(heads, s_heads):
            pexp = jnp.exp2(s - jnp.max(s, axis=1, keepdims=True))
            pn = (pexp / jnp.sum(pexp, axis=1, keepdims=True)).astype(BF16)
            outs.append(_dot(pn, vx_ref[0, :, sl]).astype(BF16))
        o_ref[r, :] = x1 + _dot(jnp.concatenate(outs, axis=1), wxo_ref[...])

    x1s = [out_proj(r) for r in parts]
    s_parts = [scores(x1) for x1 in x1s]
    for r, x1, s_heads in zip(parts, x1s, s_parts):
        attend(r, x1, s_heads)


def _post(x2, oa, og, wo, gx, wq, kx, vx, wxo, seq):
    n = x2.shape[0]
    tm = POST_TILE
    per_b = seq // tm
    mlen = kx.shape[1]
    const = functools.partial(pl.BlockSpec, pipeline_mode=pl.Buffered(1))
    row = lambda width: pl.BlockSpec((tm, width), lambda i: (i, 0))
    mem = pl.BlockSpec((1, mlen, D_MODEL), lambda i: (i // per_b, 0, 0))
    sq = const((D_MODEL, D_MODEL), lambda i: (0, 0))
    return pl.pallas_call(
        _post_kernel,
        grid=(n // tm,),
        in_specs=[row(D_MODEL), row(A_WIDTH), row(G_WIDTH), sq, const((1, D_MODEL), lambda i: (0, 0)), sq, mem, mem, sq],
        out_specs=row(D_MODEL),
        out_shape=jax.ShapeDtypeStruct((n, D_MODEL), F32),
        compiler_params=pltpu.CompilerParams(dimension_semantics=("parallel",), vmem_limit_bytes=VMEM_LIMIT),
        name="post",
    )(x2, oa, og, wo, gx, wq, kx, vx, wxo)


def _mlp_kernel(x_ref, g_ref, wu_ref, wd_ref, gf_ref, o_ref, *, final_norm):
    tm = x_ref.shape[0]
    parts = [slice(i * tm // MLP_PARTS, (i + 1) * tm // MLP_PARTS) for i in range(MLP_PARTS)]
    acc = [x_ref[r, :] for r in parts]
    h = [_rms(x, g_ref[...]).astype(BF16) for x in acc]
    for c in range(D_FF // D_MODEL):
        sl = slice(D_MODEL * c, D_MODEL * (c + 1))
        u = [jnp.square(jnp.maximum(_dot(v, wu_ref[:, sl]), 0.0)).astype(BF16) for v in h]
        acc = [a + _dot(v, wd_ref[sl, :]) for a, v in zip(acc, u)]
    for r, a in zip(parts, acc):
        o_ref[r, :] = _rms(a, gf_ref[...]) if final_norm else a


def _mlp(x2, g, wu, wd, gf, final_norm):
    n = x2.shape[0]
    tm = MLP_TILE
    const = functools.partial(pl.BlockSpec, pipeline_mode=pl.Buffered(1))
    row = pl.BlockSpec((tm, D_MODEL), lambda i: (i, 0))
    vec = const((1, D_MODEL), lambda i: (0, 0))
    return pl.pallas_call(
        functools.partial(_mlp_kernel, final_norm=final_norm),
        grid=(n // tm,),
        in_specs=[row, vec, const((D_MODEL, D_FF), lambda i: (0, 0)), const((D_FF, D_MODEL), lambda i: (0, 0)), vec],
        out_specs=row,
        out_shape=jax.ShapeDtypeStruct((n, D_MODEL), F32),
        compiler_params=pltpu.CompilerParams(dimension_semantics=("parallel",), vmem_limit_bytes=VMEM_LIMIT),
        name="mlp",
    )(x2, g, wu, wd, gf)


def kernel(x, mem, rp_table, norm_mix, w_in, w_gate_up, b_gate, g_norm, w_out, norm_xattn, norm_mem, w_xq, w_xkv,
           w_xo, norm_mlp, w_up, w_down, norm_final):
    batch, seq, _ = x.shape
    depth = w_in.shape[0]
    assert seq % max(INPROJ_TILE, POST_TILE, MLP_TILE, GLA_STEP) == 0 and seq // MOBA_BLOCK <= 8
    x2 = x.reshape(batch * seq, D_MODEL)
    bias = _bias_tiles(rp_table)
    c31 = rp_table[RP_BUCKETS - 1]
    glr_lo = 3 * A_WIDTH + 2 * G_KEY_WIDTH + G_WIDTH
    rg_lo = glr_lo + G_GATE_RANK
    pad_rank = LANES - G_GATE_RANK
    for l in range(depth):
        col = lambda lo, hi: w_in[l, :, lo:hi].astype(BF16)
        wglr = jnp.pad(col(glr_lo, rg_lo), ((0, 0), (0, pad_rank)))
        wgu = jnp.pad(w_gate_up[l].astype(BF16), ((0, pad_rank), (0, 0)))
        later = (w_out[l], w_xq[l], w_xkv[l], w_xo[l], w_up[l], w_down[l])
        qk, vt, qg, kg, la, vg, rg, wo, wxq, wxkv, wxo, wup, wdown = _inproj(
            x2, norm_mix[l][None], w_in[l].astype(BF16), col(rg_lo, rg_lo + G_WIDTH), wglr, wgu, b_gate[l][None], later)
        oa = _moba(qk, vt, bias, c31, batch, seq)
        og = _gla(qg, kg, la, vg, rg, g_norm[l][None], batch, seq)
        kx, vx = _memkv(mem, norm_mem[l][None], wxkv)
        x2 = _post(x2, oa, og, wo, norm_xattn[l][None], wxq, kx, vx, wxo, seq)
        last = l == depth - 1
        x2 = _mlp(x2, norm_mlp[l][None], wup, wdown, norm_final[None], last)
    return x2.reshape(batch, seq, D_MODEL)
```

```python
import functools
import math

import numpy as np
import jax
import jax.numpy as jnp
from jax import lax
from jax.experimental import pallas as pl
from jax.experimental.pallas import tpu as pltpu

F32 = jnp.float32
BF16 = jnp.bfloat16

D_MODEL = 1024
A_HEADS = 8
A_HEAD_DIM = 64
A_WIDTH = A_HEADS * A_HEAD_DIM
MOBA_BLOCK = 256
MOBA_TOPK = 3
G_HEADS = 4
G_WIDTH = D_MODEL - A_WIDTH
G_HEAD_V = G_WIDTH // G_HEADS
G_KEY_WIDTH = G_WIDTH // 2
G_HEAD_K = G_KEY_WIDTH // G_HEADS
G_GATE_RANK = 16
G_GATE_NORM = 16.0
X_HEADS = 4
X_HEAD_DIM = D_MODEL // X_HEADS
D_FF = 4 * D_MODEL
RP_BUCKETS = 32
RP_MAX_DIST = 128
EPS = 1e-6

LANES = 128
SUBLANES = 8
GATE_ROWS = 16
NEG = -1e30
MOBA_PAIRS = 2
MOBA_GROUP = 2
GLA_CHUNK = 128
GLA_STEP = 1024
LOG2E = float(np.log2(np.e))
INPROJ_TILE = 1024
INPROJ_PARTS = 2
MLP_TILE = 1024
MLP_PARTS = 2
POST_TILE = 1024
POST_PARTS = 2
VMEM_LIMIT = 56 * 1024 * 1024

def _nt(a, b):
    return lax.dot_general(a, b, (((1,), (1,)), ((), ())), preferred_element_type=F32)


def _tn(a, b):
    return lax.dot_general(a, b, (((0,), (0,)), ((), ())), preferred_element_type=F32)


def _dot(a, b):
    return jnp.dot(a, b, preferred_element_type=F32)


def _rms(x, g):
    return x * lax.rsqrt(jnp.mean(x * x, axis=-1, keepdims=True) + EPS) * g


def _bucket_thresholds():
    max_exact = RP_BUCKETS // 2
    d = np.arange(1, 4 * RP_MAX_DIST)
    val = (np.log(d.astype(np.float32) / np.float32(max_exact)) / np.float32(math.log(RP_MAX_DIST / max_exact))
           * np.float32(RP_BUCKETS - max_exact))
    bucket = np.minimum(max_exact + val.astype(np.int32), RP_BUCKETS - 1)
    return [int(d[(d >= max_exact) & (bucket >= max_exact + k)][0]) for k in range(1, RP_BUCKETS - max_exact)]


_THRESHOLDS = _bucket_thresholds()


def _bias_kernel(tab_ref, out_ref):
    h = pl.program_id(0)
    rows, width = 2 * MOBA_BLOCK, 3 * MOBA_BLOCK
    d = lax.broadcasted_iota(jnp.int32, (SUBLANES, width), 1)
    max_exact = RP_BUCKETS // 2
    bucket = jnp.where(d < max_exact, d, max_exact)
    for t in _THRESHOLDS:
        bucket = bucket + jnp.where(d >= t, 1, 0)
    prof = jnp.full((SUBLANES, width), NEG, F32)
    for b in range(RP_BUCKETS):
        prof = jnp.where(bucket == b, tab_ref[b, h] * LOG2E, prof)
    prof = jnp.where(d < rows, prof, NEG)
    tile = pltpu.roll(jnp.broadcast_to(prof[0:1], (rows, width)), 0, 1, stride=1, stride_axis=0)
    out_ref[0] = tile[:, MOBA_BLOCK:2 * MOBA_BLOCK]


def _bias_tiles(rp_table):
    return pl.pallas_call(
        _bias_kernel,
        grid=(A_HEADS,),
        in_specs=[pl.BlockSpec(memory_space=pltpu.SMEM)],
        out_specs=pl.BlockSpec((1, 2 * MOBA_BLOCK, MOBA_BLOCK), lambda h: (h, 0, 0)),
        out_shape=jax.ShapeDtypeStruct((A_HEADS, 2 * MOBA_BLOCK, MOBA_BLOCK), F32),
        name="bias",
    )(rp_table)


def _inproj_kernel(x_ref, g_ref, w_ref, wrg_ref, wglr_ref, wgu_ref, bg_ref, *rest):
    ncast = (len(rest) - 7) // 2
    cast_in, cast_out = rest[:ncast], rest[ncast + 7:]
    qk_ref, vt_ref, qg_ref, kg_ref, la_ref, vg_ref, rg_ref = rest[ncast:ncast + 7]
    tm = x_ref.shape[0]
    parts = [slice(i * tm // INPROJ_PARTS, (i + 1) * tm // INPROJ_PARTS) for i in range(INPROJ_PARTS)]
    hs = [_rms(x_ref[r, :], g_ref[...]).astype(BF16) for r in parts]
    va_lo = 2 * A_WIDTH
    qg_lo = va_lo + A_WIDTH
    kg_lo = qg_lo + G_KEY_WIDTH
    vg_lo = kg_lo + G_KEY_WIDTH

    glrs = [_dot(h, wglr_ref[...]).astype(BF16) for h in hs]
    for r, h in zip(parts, hs):
        qk_ref[r, 0:A_WIDTH] = (_dot(h, w_ref[:, 0:A_WIDTH]) * (LOG2E * A_HEAD_DIM ** -0.5)).astype(BF16)
    for r, h in zip(parts, hs):
        qk_ref[r, A_WIDTH:va_lo] = _dot(h, w_ref[:, A_WIDTH:va_lo]).astype(BF16)
    for r, glr in zip(parts, glrs):
        z = _dot(glr, wgu_ref[...]) + bg_ref[...]
        log_sig = jnp.minimum(z, 0.0) - jnp.log(1.0 + jnp.exp(-jnp.abs(z)))
        la_ref[r, :] = log_sig * (1.0 / G_GATE_NORM)
    for r, h in zip(parts, hs):
        vt_ref[:, r] = _dot(h, w_ref[:, va_lo:qg_lo]).astype(BF16).T
    for r, h in zip(parts, hs):
        qg_ref[r, :] = _dot(h, w_ref[:, qg_lo:kg_lo]) * (G_HEAD_K ** -0.5)
        kg_ref[r, :] = _dot(h, w_ref[:, kg_lo:vg_lo])
    for r, h in zip(parts, hs):
        vg_ref[r, :] = _dot(h, w_ref[:, vg_lo:vg_lo + G_WIDTH]).astype(BF16)
    for r, h in zip(parts, hs):
        rg_ref[r, :] = _dot(h, wrg_ref[...])
    for src, dst in zip(cast_in, cast_out):
        dst[...] = src[...].astype(BF16)


def _inproj(x2, g, w, wrg, wglr, wgu, bg, later):
    n = x2.shape[0]
    tm = INPROJ_TILE
    steps = n // tm
    const = lambda a: pl.BlockSpec(a.shape, lambda i: (0, 0), pipeline_mode=pl.Buffered(1))
    row = lambda width: pl.BlockSpec((tm, width), lambda i: (i, 0))
    assert all(a.shape[0] % (steps * GATE_ROWS) == 0 for a in later)
    part = [pl.BlockSpec((a.shape[0] // steps, a.shape[1]), lambda i: (i, 0)) for a in later]
    return pl.pallas_call(
        _inproj_kernel,
        grid=(steps,),
        in_specs=[row(D_MODEL)] + [const(a) for a in (g, w, wrg, wglr, wgu, bg)] + part,
        out_specs=[row(2 * A_WIDTH), pl.BlockSpec((A_WIDTH, tm), lambda i: (0, i)),
                   row(G_KEY_WIDTH), row(G_KEY_WIDTH), row(G_KEY_WIDTH), row(G_WIDTH), row(G_WIDTH)] + part,
        out_shape=[jax.ShapeDtypeStruct((n, 2 * A_WIDTH), BF16),
                   jax.ShapeDtypeStruct((A_WIDTH, n), BF16),
                   jax.ShapeDtypeStruct((n, G_KEY_WIDTH), F32),
                   jax.ShapeDtypeStruct((n, G_KEY_WIDTH), F32),
                   jax.ShapeDtypeStruct((n, G_KEY_WIDTH), F32),
                   jax.ShapeDtypeStruct((n, G_WIDTH), BF16),
                   jax.ShapeDtypeStruct((n, G_WIDTH), F32)] + [jax.ShapeDtypeStruct(a.shape, BF16) for a in later],
        compiler_params=pltpu.CompilerParams(dimension_semantics=("parallel",), vmem_limit_bytes=VMEM_LIMIT),
        name="inproj",
    )(x2, g, w, wrg, wglr, wgu, bg, *later)


def _moba_kernel(c31_ref, q_ref, k_ref, vt_ref, bias_ref, o_ref, khead, s_scr, *, seq):
    step = pl.program_id(1)
    nblk = seq // MOBA_BLOCK
    half = LANES // 2
    lane2 = lax.broadcasted_iota(jnp.int32, (seq, LANES), 1)
    lane_q = lax.broadcasted_iota(jnp.int32, (MOBA_BLOCK, LANES), 1)
    rowi = lax.broadcasted_iota(jnp.int32, (GATE_ROWS, MOBA_BLOCK), 0)
    blk = lambda n: slice(n * MOBA_BLOCK, (n + 1) * MOBA_BLOCK)
    pair_lanes = lambda pp: slice(pp * LANES, (pp + 1) * LANES)

    def build(pp):
        k = k_ref[:, pair_lanes(pp)]
        khead[2 * pp] = k * jnp.where(lane2 < half, 1.0, 0.0).astype(BF16)
        khead[2 * pp + 1] = k * jnp.where(lane2 >= half, 1.0, 0.0).astype(BF16)
        n_i = lax.broadcasted_iota(jnp.int32, (GATE_ROWS, seq), 0)
        t_i = lax.broadcasted_iota(jnp.int32, (GATE_ROWS, seq), 1)
        avg = jnp.where(t_i // MOBA_BLOCK == n_i, 1.0 / MOBA_BLOCK, 0.0).astype(BF16)
        km = _dot(avg, k)
        kmh = km.astype(BF16)
        return kmh, (km - kmh.astype(F32)).astype(BF16)

    def prepare(pp, kmeans, qi, h, slot):
        qm = q_ref[blk(qi), pair_lanes(pp)] * jnp.where((lane_q < half) if h == 0 else (lane_q >= half), 1.0,
                                                        0.0).astype(BF16)
        gate = _nt(kmeans[0], qm) + _nt(kmeans[1], qm) if qi > MOBA_TOPK else None
        head = 2 * pp + h
        return dict(pp=pp, qi=qi, h=h, head=head, qm=qm, gate=gate, sbuf=s_scr.at[slot], near=None, far=None,
                    acc=None, c31=c31_ref[2 * MOBA_PAIRS * step + head] * LOG2E)

    def penalty(u, n):
        return 0.0 if u["pen"] is None else u["pen"][n:n + 1, :]

    def score_matmul(u):
        u["s_all"] = _nt(khead[u["head"], 0:(u["qi"] + 1) * MOBA_BLOCK, :], u["qm"])

    def score_reduce(u):
        qi, head = u["qi"], u["head"]
        u["pen"] = None
        if u["gate"] is not None:
            g = u["gate"]
            cnt = jnp.zeros((GATE_ROWS, MOBA_BLOCK), F32)
            for m in range(qi):
                gm = g[m:m + 1, :]
                cnt = cnt + jnp.where((gm > g) | ((gm == g) & (m < rowi)), 1.0, 0.0)
            u["pen"] = jnp.where((rowi < qi) & (cnt < MOBA_TOPK), 0.0, NEG)
        for n in range(qi + 1):
            s = u["s_all"][blk(n)]
            if n == qi:
                s = s + bias_ref[head, MOBA_BLOCK:2 * MOBA_BLOCK, :]
            elif n == qi - 1:
                s = s + bias_ref[head, 0:MOBA_BLOCK, :]
            u["sbuf"][blk(n), :] = s
            t = jnp.max(s.reshape(MOBA_BLOCK // SUBLANES, SUBLANES, MOBA_BLOCK), axis=0)
            if n < qi:
                t = t + penalty(u, n)
            key = "far" if n < qi - 1 else "near"
            u[key] = t if u[key] is None else jnp.maximum(u[key], t)
        mx = u["near"] if u["far"] is None else jnp.maximum(u["near"], u["far"] + u["c31"])
        u["m_near"] = jnp.max(mx, axis=0, keepdims=True)
        u["m_far"] = u["m_near"] - u["c31"]

    def prob_exp(u):
        qi = u["qi"]
        shift = lambda n: (u["m_far"] if n < qi - 1 else u["m_near"]) - (penalty(u, n) if n < qi else 0.0)
        u["pt"] = jnp.concatenate([jnp.exp2(u["sbuf"][blk(n), :] - shift(n)).astype(BF16) for n in range(qi + 1)],
                                  axis=0)

    def prob_matmul(u):
        qi, h = u["qi"], u["h"]
        width = (qi + 1) * MOBA_BLOCK
        r0 = u["head"] * half
        vrows = vt_ref[r0:r0 + half, 0:width]
        ones_rows = jnp.ones((half, width), BF16)
        vt = jnp.concatenate([vrows, ones_rows] if h == 0 else [ones_rows, vrows], axis=0)
        u["acc"] = _dot(vt, u["pt"])

    outs = {}

    def finish(u):
        acc, h = u["acc"], u["h"]
        num = acc[half * h:half * (h + 1)]
        den = acc[half * (1 - h):half * (1 - h) + 1]
        outs[h] = num / den
        if h == 1:
            o_ref[blk(u["qi"]), pair_lanes(u["pp"])] = jnp.concatenate([outs[0], outs[1]], axis=0).T.astype(BF16)

    group = MOBA_GROUP
    stages = [(pp, [(qi, h) for qi in range(g * group, (g + 1) * group) for h in range(2)])
              for pp in range(MOBA_PAIRS) for g in range(nblk // group)]
    prev, kmeans = [], {}
    for g, (pp, members) in enumerate(stages + [(None, [])]):
        if members and pp not in kmeans:
            kmeans[pp] = build(pp)
        cur = [prepare(pp, kmeans[pp], qi, h, (g % 2) * 2 * group + 2 * (qi % group) + h) for qi, h in members]
        for u in prev:
            prob_exp(u)
        for u in cur:
            score_matmul(u)
        for u in prev:
            prob_matmul(u)
        for u in cur:
            score_reduce(u)
        for u in prev:
            finish(u)
        prev = cur


def _moba(qk, vt, bias, c31, batch, seq):
    n = qk.shape[0]
    width = MOBA_PAIRS * LANES
    nstep = A_WIDTH // width
    kern = functools.partial(_moba_kernel, seq=seq)
    return pl.pallas_call(
        kern,
        grid=(batch, nstep),
        in_specs=[pl.BlockSpec(memory_space=pltpu.SMEM),
                  pl.BlockSpec((seq, width), lambda b, p: (b, p)),
                  pl.BlockSpec((seq, width), lambda b, p: (b, nstep + p)),
                  pl.BlockSpec((width, seq), lambda b, p: (p, b)),
                  pl.BlockSpec((2 * MOBA_PAIRS, 2 * MOBA_BLOCK, MOBA_BLOCK), lambda b, p: (p, 0, 0))],
        out_specs=pl.BlockSpec((seq, width), lambda b, p: (b, p)),
        out_shape=jax.ShapeDtypeStruct((n, A_WIDTH), BF16),
        scratch_shapes=[pltpu.VMEM((2 * MOBA_PAIRS, seq, LANES), BF16),
                        pltpu.VMEM((2 * 2 * MOBA_GROUP, seq, MOBA_BLOCK), F32)],
        compiler_params=pltpu.CompilerParams(dimension_semantics=("parallel", "arbitrary"),
                                             vmem_limit_bytes=VMEM_LIMIT),
        name="moba",
    )(c31, qk, qk, vt, bias)


_GLA_LEVELS = int(math.log2(GLA_CHUNK))


_GLA_BIG_LEVELS = _GLA_LEVELS - int(math.log2(SUBLANES))


def _gla_weights():
    c = GLA_CHUNK
    i = np.arange(c)[:, None]
    j = np.arange(c)[None, :]
    mats = [(j <= i)]
    for lvl in range(_GLA_BIG_LEVELS, _GLA_LEVELS):
        s = (c // 2) >> lvl
        ref = (i // (2 * s)) * (2 * s) + s - 1
        mats.append(np.where(i > ref, (j > ref) & (j <= i), (j > i) & (j <= ref)))
    return np.concatenate(mats, axis=0).astype(np.float32)


def _gla_level_map():
    c = GLA_CHUNK
    i = np.arange(c)[:, None]
    j = np.arange(c)[None, :]
    top_bit = np.floor(np.log2(np.maximum(i ^ j, 1))).astype(np.int64)
    lvl = np.where(j < i, _GLA_LEVELS - 1 - top_bit, np.where(j == i, _GLA_LEVELS, _GLA_LEVELS + 1))
    return np.concatenate([lvl, lvl], axis=1).astype(np.int32)


def _gla_kernel(q_ref, k_ref, la_ref, v_ref, rg_ref, w_ref, lmap_ref, gn_ref, o_ref, st_ref):
    c = GLA_CHUNK
    half = LANES // 2
    npair = G_HEADS // 2

    @pl.when(pl.program_id(1) == 0)
    def _init():
        st_ref[...] = jnp.zeros(st_ref.shape, F32)

    w = w_ref[...]
    lmap = lmap_ref[...]
    lane = lax.broadcasted_iota(jnp.int32, (1, LANES), 1)
    lane_c = lax.broadcasted_iota(jnp.int32, (c, LANES), 1)
    lm = [jnp.where(lane_c < half, 1.0, 0.0).astype(BF16), jnp.where(lane_c >= half, 1.0, 0.0).astype(BF16)]
    row_small = lax.broadcasted_iota(jnp.int32, (c, G_KEY_WIDTH), 0)

    def level_factors(q, k, b, d_all, lvl):
        s = (c // 2) >> lvl
        if s < SUBLANES:
            j = lvl - _GLA_BIG_LEVELS
            e = jnp.exp2(d_all[c * (j + 1):c * (j + 2)])
            odd = (row_small // s) % 2 == 1
            return jnp.where(odd, q * e, 0.0), jnp.where(odd, 0.0, k * e)
        zeros = jnp.zeros((s, q.shape[1]), F32)
        qp, kp = [], []
        for blk in range(c // s):
            rows = slice(blk * s, (blk + 1) * s)
            ref = (blk // 2) * 2 * s + s - 1
            if blk % 2 == 1:
                qp.append(q[rows] * jnp.exp2(b[rows] - b[ref:ref + 1]))
                kp.append(zeros)
            else:
                qp.append(zeros)
                kp.append(k[rows] * jnp.exp2(b[ref:ref + 1] - b[rows]))
        return jnp.concatenate(qp, axis=0), jnp.concatenate(kp, axis=0)

    nch = GLA_STEP // c
    rows_of = [slice(ch * c, (ch + 1) * c) for ch in range(nch)]
    pair_lanes = [slice(LANES * p, LANES * (p + 1)) for p in range(npair)]

    def decays(ch):
        la = la_ref[rows_of[ch], :] * LOG2E
        hi = la.astype(BF16)
        lo = (la - hi.astype(F32)).astype(BF16)
        return _dot(w, hi) + _dot(w, lo)

    def factors(ch, d_all):
        q = q_ref[rows_of[ch], :]
        k = k_ref[rows_of[ch], :]
        b = d_all[0:c]
        blast = b[c - 1:c, :]
        qts, kts = [], []
        for lvl in range(_GLA_LEVELS):
            qt, kt = level_factors(q, k, b, d_all, lvl)
            qts.append(qt.astype(BF16))
            kts.append(kt.astype(BF16))
        qts.append(q.astype(BF16))
        kts.append(k.astype(BF16))
        return dict(ch=ch, qts=qts, kts=kts, qe=(q * jnp.exp2(b)).astype(BF16),
                    khat=(k * jnp.exp2(blast - b)).astype(BF16), dec=jnp.exp2(blast),
                    a=[jnp.zeros((c, 2 * c), F32)] * npair)

    def level_scores(units):
        for lvl in range(_GLA_LEVELS + 1):
            mask = lmap == lvl
            s = (c // 2) >> lvl if lvl < _GLA_BIG_LEVELS else c
            blocks = [slice(i * s, (i + 1) * s) for i in range(c // s)]
            live = [i for i in range(len(blocks)) if i % 2 == 1 or s == c]
            for u in units:
                for p, sl in enumerate(pair_lanes):
                    kp = u["kts"][lvl][:, sl]
                    ql = u["qts"][lvl]
                    pm = _nt(jnp.concatenate([ql[blocks[i], sl] for i in live], axis=0),
                             jnp.concatenate([kp * lm[0], kp * lm[1]], axis=0))
                    a = u["a"][p]
                    pieces = [a[r] for r in blocks]
                    for n, i in enumerate(live):
                        pieces[i] = jnp.where(mask[blocks[i]], pm[n * s:(n + 1) * s], pieces[i])
                    u["a"] = [jnp.concatenate(pieces, axis=0) if j == p else v for j, v in enumerate(u["a"])]

    def stateless_matmuls(units):
        for u in units:
            vhs = [v_ref[rows_of[u["ch"]], LANES * h:LANES * (h + 1)] for h in range(G_HEADS)]
            u["intra"] = [_dot(u["a"][h // 2][:, (h % 2) * c:(h % 2 + 1) * c].astype(BF16), vhs[h])
                          for h in range(G_HEADS)]
            u["inc"] = [_tn(vhs[h], u["khat"][:, pair_lanes[h // 2]]) for h in range(G_HEADS)]

    def state_chain(units, states):
        for u in units:
            rows = rows_of[u["ch"]]
            for p, sl in enumerate(pair_lanes):
                stb = states[p].astype(BF16)
                for hh in range(2):
                    h = 2 * p + hh
                    o = u["intra"][h] + _nt(u["qe"][:, sl] * lm[hh], stb)
                    y = _rms(o, gn_ref[...])
                    rg = rg_ref[rows, LANES * h:LANES * (h + 1)]
                    y = y * (rg / (1.0 + jnp.exp(-rg)))
                    o_ref[rows, LANES * h:LANES * (h + 1)] = y.astype(BF16)
                states[p] = states[p] * u["dec"][:, sl] + jnp.where(lane < half, u["inc"][2 * p], u["inc"][2 * p + 1])

    d_alls = [decays(ch) for ch in range(nch)]
    chunks = [factors(ch, d_alls[ch]) for ch in range(nch)]
    level_scores(chunks)
    stateless_matmuls(chunks)
    states = [st_ref[p] for p in range(npair)]
    state_chain(chunks, states)
    for p in range(npair):
        st_ref[p] = states[p]


def _gla(qg, kg, la, vg, rg, gn, batch, seq):
    n = qg.shape[0]
    nchunk = seq // GLA_STEP
    w = jnp.asarray(_gla_weights(), BF16)
    lmap = jnp.asarray(_gla_level_map())
    row = lambda width: pl.BlockSpec((GLA_STEP, width), lambda b, c: (b * nchunk + c, 0))
    const = functools.partial(pl.BlockSpec, pipeline_mode=pl.Buffered(1))
    return pl.pallas_call(
        _gla_kernel,
        grid=(batch, nchunk),
        in_specs=[row(G_KEY_WIDTH), row(G_KEY_WIDTH), row(G_KEY_WIDTH), row(G_WIDTH), row(G_WIDTH),
                  const(w.shape, lambda b, c: (0, 0)),
                  const(lmap.shape, lambda b, c: (0, 0)),
                  const((1, G_HEAD_V), lambda b, c: (0, 0))],
        out_specs=row(G_WIDTH),
        out_shape=jax.ShapeDtypeStruct((n, G_WIDTH), BF16),
        scratch_shapes=[pltpu.VMEM((G_HEADS // 2, G_HEAD_V, LANES), F32)],
        compiler_params=pltpu.CompilerParams(dimension_semantics=("parallel", "arbitrary"),
                                             vmem_limit_bytes=VMEM_LIMIT),
        name="gla",
    )(qg, kg, la, vg, rg, w, lmap, gn)


def _memkv_kernel(m_ref, g_ref, w_ref, k_ref, v_ref):
    nb, mlen, _ = m_ref.shape
    h = _rms(m_ref[...].reshape(nb * mlen, D_MODEL), g_ref[...]).astype(BF16)
    k_ref[...] = _dot(h, w_ref[:, 0:D_MODEL]).astype(BF16).reshape(nb, mlen, D_MODEL)
    v_ref[...] = _dot(h, w_ref[:, D_MODEL:2 * D_MODEL]).astype(BF16).reshape(nb, mlen, D_MODEL)


def _memkv(mem, g, w):
    batch, mlen, _ = mem.shape
    nb = 2 if batch % 2 == 0 else 1
    const = functools.partial(pl.BlockSpec, pipeline_mode=pl.Buffered(1))
    blk = pl.BlockSpec((nb, mlen, D_MODEL), lambda b: (b, 0, 0))
    return pl.pallas_call(
        _memkv_kernel,
        grid=(batch // nb,),
        in_specs=[blk, const((1, D_MODEL), lambda b: (0, 0)), const((D_MODEL, 2 * D_MODEL), lambda b: (0, 0))],
        out_specs=[blk, blk],
        out_shape=[jax.ShapeDtypeStruct(mem.shape, BF16)] * 2,
        compiler_params=pltpu.CompilerParams(dimension_semantics=("parallel",), vmem_limit_bytes=VMEM_LIMIT),
        name="memkv",
    )(mem, g, w)


def _post_kernel(x_ref, oa_ref, og_ref, wo_ref, gx_ref, wq_ref, kx_ref, vx_ref, wxo_ref, o_ref):
    tm = x_ref.shape[0]
    parts = [slice(i * tm // POST_PARTS, (i + 1) * tm // POST_PARTS) for i in range(POST_PARTS)]
    heads = [slice(X_HEAD_DIM * hd, X_HEAD_DIM * (hd + 1)) for hd in range(X_HEADS)]

    def out_proj(r):
        return x_ref[r, :] + _dot(oa_ref[r, :], wo_ref[0:A_WIDTH, :]) + _dot(og_ref[r, :], wo_ref[A_WIDTH:D_MODEL, :])

    def scores(x1):
        h = _rms(x1, gx_ref[...]).astype(BF16)
        qx = (_dot(h, wq_ref[...]) * (LOG2E * X_HEAD_DIM ** -0.5)).astype(BF16)
        return [_nt(qx[:, sl], kx_ref[0, :, sl]) for sl in heads]

    def attend(r, x1, s_heads):
        outs = []
        for sl, s in zip(heads, s_heads):
            pexp = jnp.exp2(s - jnp.max(s, axis=1, keepdims=True))
            pn = (pexp / jnp.sum(pexp, axis=1, keepdims=True)).astype(BF16)
            outs.append(_dot(pn, vx_ref[0, :, sl]).astype(BF16))
        o_ref[r, :] = x1 + _dot(jnp.concatenate(outs, axis=1), wxo_ref[...])

    x1s = [out_proj(r) for r in parts]
    s_parts = [scores(x1) for x1 in x1s]
    for r, x1, s_heads in zip(parts, x1s, s_parts):
        attend(r, x1, s_heads)


def _post(x2, oa, og, wo, gx, wq, kx, vx, wxo, seq):
    n = x2.shape[0]
    tm = POST_TILE
    per_b = seq // tm
    mlen = kx.shape[1]
    const = functools.partial(pl.BlockSpec, pipeline_mode=pl.Buffered(1))
    row = lambda width: pl.BlockSpec((tm, width), lambda i: (i, 0))
    mem = pl.BlockSpec((1, mlen, D_MODEL), lambda i: (i // per_b, 0, 0))
    sq = const((D_MODEL, D_MODEL), lambda i: (0, 0))
    return pl.pallas_call(
        _post_kernel,
        grid=(n // tm,),
        in_specs=[row(D_MODEL), row(A_WIDTH), row(G_WIDTH), sq, const((1, D_MODEL), lambda i: (0, 0)), sq, mem, mem, sq],
        out_specs=row(D_MODEL),
        out_shape=jax.ShapeDtypeStruct((n, D_MODEL), F32),
        compiler_params=pltpu.CompilerParams(dimension_semantics=("parallel",), vmem_limit_bytes=VMEM_LIMIT),
        name="post",
    )(x2, oa, og, wo, gx, wq, kx, vx, wxo)


def _mlp_kernel(x_ref, g_ref, wu_ref, wd_ref, gf_ref, o_ref, *, final_norm):
    tm = x_ref.shape[0]
    parts = [slice(i * tm // MLP_PARTS, (i + 1) * tm // MLP_PARTS) for i in range(MLP_PARTS)]
    acc = [x_ref[r, :] for r in parts]
    h = [_rms(x, g_ref[...]).astype(BF16) for x in acc]
    for c in range(D_FF // D_MODEL):
        sl = slice(D_MODEL * c, D_MODEL * (c + 1))
        u = [jnp.square(jnp.maximum(_dot(v, wu_ref[:, sl]), 0.0)).astype(BF16) for v in h]
        acc = [a + _dot(v, wd_ref[sl, :]) for a, v in zip(acc, u)]
    for r, a in zip(parts, acc):
        o_ref[r, :] = _rms(a, gf_ref[...]) if final_norm else a


def _mlp(x2, g, wu, wd, gf, final_norm):
    n = x2.shape[0]
    tm = MLP_TILE
    const = functools.partial(pl.BlockSpec, pipeline_mode=pl.Buffered(1))
    row = pl.BlockSpec((tm, D_MODEL), lambda i: (i, 0))
    vec = const((1, D_MODEL), lambda i: (0, 0))
    return pl.pallas_call(
        functools.partial(_mlp_kernel, final_norm=final_norm),
        grid=(n // tm,),
        in_specs=[row, vec, const((D_MODEL, D_FF), lambda i: (0, 0)), const((D_FF, D_MODEL), lambda i: (0, 0)), vec],
        out_specs=row,
        out_shape=jax.ShapeDtypeStruct((n, D_MODEL), F32),
        compiler_params=pltpu.CompilerParams(dimension_semantics=("parallel",), vmem_limit_bytes=VMEM_LIMIT),
        name="mlp",
    )(x2, g, wu, wd, gf)


def kernel(x, mem, rp_table, norm_mix, w_in, w_gate_up, b_gate, g_norm, w_out, norm_xattn, norm_mem, w_xq, w_xkv,
           w_xo, norm_mlp, w_up, w_down, norm_final):
    batch, seq, _ = x.shape
    depth = w_in.shape[0]
    assert seq % max(INPROJ_TILE, POST_TILE, MLP_TILE, GLA_STEP) == 0 and seq // MOBA_BLOCK <= 8
    x2 = x.reshape(batch * seq, D_MODEL)
    bias = _bias_tiles(rp_table)
    c31 = rp_table[RP_BUCKETS - 1]
    glr_lo = 3 * A_WIDTH + 2 * G_KEY_WIDTH + G_WIDTH
    rg_lo = glr_lo + G_GATE_RANK
    pad_rank = LANES - G_GATE_RANK
    for l in range(depth):
        col = lambda lo, hi: w_in[l, :, lo:hi].astype(BF16)
        wglr = jnp.pad(col(glr_lo, rg_lo), ((0, 0), (0, pad_rank)))
        wgu = jnp.pad(w_gate_up[l].astype(BF16), ((0, pad_rank), (0, 0)))
        later = (w_out[l], w_xq[l], w_xkv[l], w_xo[l], w_up[l], w_down[l])
        qk, vt, qg, kg, la, vg, rg, wo, wxq, wxkv, wxo, wup, wdown = _inproj(
            x2, norm_mix[l][None], w_in[l].astype(BF16), col(rg_lo, rg_lo + G_WIDTH), wglr, wgu, b_gate[l][None], later)
        oa = _moba(qk, vt, bias, c31, batch, seq)
        og = _gla(qg, kg, la, vg, rg, g_norm[l][None], batch, seq)
        kx, vx = _memkv(mem, norm_mem[l][None], wxkv)
        x2 = _post(x2, oa, og, wo, norm_xattn[l][None], wxq, kx, vx, wxo, seq)
        last = l == depth - 1
        x2 = _mlp(x2, norm_mlp[l][None], wup, wdown, norm_final[None], last)
    return x2.reshape(batch, seq, D_MODEL)
```

```python
import functools
import math

import numpy as np
import jax
import jax.numpy as jnp
from jax import lax
from jax.experimental import pallas as pl
from jax.experimental.pallas import tpu as pltpu

F32 = jnp.float32
BF16 = jnp.bfloat16

D_MODEL = 1024
A_HEADS = 8
A_HEAD_DIM = 64
A_WIDTH = A_HEADS * A_HEAD_DIM
MOBA_BLOCK = 256
MOBA_TOPK = 3
G_HEADS = 4
G_WIDTH = D_MODEL - A_WIDTH
G_HEAD_V = G_WIDTH // G_HEADS
G_KEY_WIDTH = G_WIDTH // 2
G_HEAD_K = G_KEY_WIDTH // G_HEADS
G_GATE_RANK = 16
G_GATE_NORM = 16.0
X_HEADS = 4
X_HEAD_DIM = D_MODEL // X_HEADS
D_FF = 4 * D_MODEL
RP_BUCKETS = 32
RP_MAX_DIST = 128
EPS = 1e-6

LANES = 128
SUBLANES = 8
GATE_ROWS = 16
NEG = -1e30
MOBA_PAIRS = 2
MOBA_GROUP = 2
GLA_CHUNK = 128
GLA_STEP = 1024
LOG2E = float(np.log2(np.e))
INPROJ_TILE = 1024
INPROJ_PARTS = 2
MLP_TILE = 1024
MLP_PARTS = 2
POST_TILE = 1024
POST_PARTS = 2
VMEM_LIMIT = 56 * 1024 * 1024

def _nt(a, b):
    return lax.dot_general(a, b, (((1,), (1,)), ((), ())), preferred_element_type=F32)


def _tn(a, b):
    return lax.dot_general(a, b, (((0,), (0,)), ((), ())), preferred_element_type=F32)


def _dot(a, b):
    return jnp.dot(a, b, preferred_element_type=F32)


def _rms(x, g):
    return x * lax.rsqrt(jnp.mean(x * x, axis=-1, keepdims=True) + EPS) * g


def _bucket_thresholds():
    max_exact = RP_BUCKETS // 2
    d = np.arange(1, 4 * RP_MAX_DIST)
    val = (np.log(d.astype(np.float32) / np.float32(max_exact)) / np.float32(math.log(RP_MAX_DIST / max_exact))
           * np.float32(RP_BUCKETS - max_exact))
    bucket = np.minimum(max_exact + val.astype(np.int32), RP_BUCKETS - 1)
    return [int(d[(d >= max_exact) & (bucket >= max_exact + k)][0]) for k in range(1, RP_BUCKETS - max_exact)]


_THRESHOLDS = _bucket_thresholds()


def _bias_kernel(tab_ref, out_ref):
    h = pl.program_id(0)
    rows, width = 2 * MOBA_BLOCK, 3 * MOBA_BLOCK
    d = lax.broadcasted_iota(jnp.int32, (SUBLANES, width), 1)
    max_exact = RP_BUCKETS // 2
    bucket = jnp.where(d < max_exact, d, max_exact)
    for t in _THRESHOLDS:
        bucket = bucket + jnp.where(d >= t, 1, 0)
    prof = jnp.full((SUBLANES, width), NEG, F32)
    for b in range(RP_BUCKETS):
        prof = jnp.where(bucket == b, tab_ref[b, h] * LOG2E, prof)
    prof = jnp.where(d < rows, prof, NEG)
    tile = pltpu.roll(jnp.broadcast_to(prof[0:1], (rows, width)), 0, 1, stride=1, stride_axis=0)
    out_ref[0] = tile[:, MOBA_BLOCK:2 * MOBA_BLOCK]


def _bias_tiles(rp_table):
    return pl.pallas_call(
        _bias_kernel,
        grid=(A_HEADS,),
        in_specs=[pl.BlockSpec(memory_space=pltpu.SMEM)],
        out_specs=pl.BlockSpec((1, 2 * MOBA_BLOCK, MOBA_BLOCK), lambda h: (h, 0, 0)),
        out_shape=jax.ShapeDtypeStruct((A_HEADS, 2 * MOBA_BLOCK, MOBA_BLOCK), F32),
        name="bias",
    )(rp_table)


def _inproj_kernel(x_ref, g_ref, w_ref, wrg_ref, wglr_ref, wgu_ref, bg_ref, *rest):
    ncast = (len(rest) - 7) // 2
    cast_in, cast_out = rest[:ncast], rest[ncast + 7:]
    qk_ref, vt_ref, qg_ref, kg_ref, la_ref, vg_ref, rg_ref = rest[ncast:ncast + 7]
    tm = x_ref.shape[0]
    parts = [slice(i * tm // INPROJ_PARTS, (i + 1) * tm // INPROJ_PARTS) for i in range(INPROJ_PARTS)]
    hs = [_rms(x_ref[r, :], g_ref[...]).astype(BF16) for r in parts]
    va_lo = 2 * A_WIDTH
    qg_lo = va_lo + A_WIDTH
    kg_lo = qg_lo + G_KEY_WIDTH
    vg_lo = kg_lo + G_KEY_WIDTH

    glrs = [_dot(h, wglr_ref[...]).astype(BF16) for h in hs]
    for r, h in zip(parts, hs):
        qk_ref[r, 0:A_WIDTH] = (_dot(h, w_ref[:, 0:A_WIDTH]) * (LOG2E * A_HEAD_DIM ** -0.5)).astype(BF16)
    for r, h in zip(parts, hs):
        qk_ref[r, A_WIDTH:va_lo] = _dot(h, w_ref[:, A_WIDTH:va_lo]).astype(BF16)
    for r, glr in zip(parts, glrs):
        z = _dot(glr, wgu_ref[...]) + bg_ref[...]
        log_sig = jnp.minimum(z, 0.0) - jnp.log(1.0 + jnp.exp(-jnp.abs(z)))
        la_ref[r, :] = log_sig * (1.0 / G_GATE_NORM)
    for r, h in zip(parts, hs):
        vt_ref[:, r] = _dot(h, w_ref[:, va_lo:qg_lo]).astype(BF16).T
    for r, h in zip(parts, hs):
        qg_ref[r, :] = _dot(h, w_ref[:, qg_lo:kg_lo]) * (G_HEAD_K ** -0.5)
        kg_ref[r, :] = _dot(h, w_ref[:, kg_lo:vg_lo])
    for r, h in zip(parts, hs):
        vg_ref[r, :] = _dot(h, w_ref[:, vg_lo:vg_lo + G_WIDTH]).astype(BF16)
    for r, h in zip(parts, hs):
        rg_ref[r, :] = _dot(h, wrg_ref[...])
    for src, dst in zip(cast_in, cast_out):
        dst[...] = src[...].astype(BF16)


def _inproj(x2, g, w, wrg, wglr, wgu, bg, later):
    n = x2.shape[0]
    tm = INPROJ_TILE
    steps = n // tm
    const = lambda a: pl.BlockSpec(a.shape, lambda i: (0, 0), pipeline_mode=pl.Buffered(1))
    row = lambda width: pl.BlockSpec((tm, width), lambda i: (i, 0))
    assert all(a.shape[0] % (steps * GATE_ROWS) == 0 for a in later)
    part = [pl.BlockSpec((a.shape[0] // steps, a.shape[1]), lambda i: (i, 0)) for a in later]
    return pl.pallas_call(
        _inproj_kernel,
        grid=(steps,),
        in_specs=[row(D_MODEL)] + [const(a) for a in (g, w, wrg, wglr, wgu, bg)] + part,
        out_specs=[row(2 * A_WIDTH), pl.BlockSpec((A_WIDTH, tm), lambda i: (0, i)),
                   row(G_KEY_WIDTH), row(G_KEY_WIDTH), row(G_KEY_WIDTH), row(G_WIDTH), row(G_WIDTH)] + part,
        out_shape=[jax.ShapeDtypeStruct((n, 2 * A_WIDTH), BF16),
                   jax.ShapeDtypeStruct((A_WIDTH, n), BF16),
                   jax.ShapeDtypeStruct((n, G_KEY_WIDTH), F32),
                   jax.ShapeDtypeStruct((n, G_KEY_WIDTH), F32),
                   jax.ShapeDtypeStruct((n, G_KEY_WIDTH), F32),
                   jax.ShapeDtypeStruct((n, G_WIDTH), BF16),
                   jax.ShapeDtypeStruct((n, G_WIDTH), F32)] + [jax.ShapeDtypeStruct(a.shape, BF16) for a in later],
        compiler_params=pltpu.CompilerParams(dimension_semantics=("parallel",), vmem_limit_bytes=VMEM_LIMIT),
        name="inproj",
    )(x2, g, w, wrg, wglr, wgu, bg, *later)


def _moba_kernel(c31_ref, q_ref, k_ref, vt_ref, bias_ref, o_ref, khead, s_scr, *, seq):
    step = pl.program_id(1)
    nblk = seq // MOBA_BLOCK
    half = LANES // 2
    lane2 = lax.broadcasted_iota(jnp.int32, (seq, LANES), 1)
    lane_q = lax.broadcasted_iota(jnp.int32, (MOBA_BLOCK, LANES), 1)
    rowi = lax.broadcasted_iota(jnp.int32, (GATE_ROWS, MOBA_BLOCK), 0)
    blk = lambda n: slice(n * MOBA_BLOCK, (n + 1) * MOBA_BLOCK)
    pair_lanes = lambda pp: slice(pp * LANES, (pp + 1) * LANES)

    def build(pp):
        k = k_ref[:, pair_lanes(pp)]
        khead[2 * pp] = k * jnp.where(lane2 < half, 1.0, 0.0).astype(BF16)
        khead[2 * pp + 1] = k * jnp.where(lane2 >= half, 1.0, 0.0).astype(BF16)
        n_i = lax.broadcasted_iota(jnp.int32, (GATE_ROWS, seq), 0)
        t_i = lax.broadcasted_iota(jnp.int32, (GATE_ROWS, seq), 1)
        avg = jnp.where(t_i // MOBA_BLOCK == n_i, 1.0 / MOBA_BLOCK, 0.0).astype(BF16)
        km = _dot(avg, k)
        kmh = km.astype(BF16)
        return kmh, (km - kmh.astype(F32)).astype(BF16)

    def prepare(pp, kmeans, qi, h, slot):
        qm = q_ref[blk(qi), pair_lanes(pp)] * jnp.where((lane_q < half) if h == 0 else (lane_q >= half), 1.0,
                                                        0.0).astype(BF16)
        gate = _nt(kmeans[0], qm) + _nt(kmeans[1], qm) if qi > MOBA_TOPK else None
        head = 2 * pp + h
        return dict(pp=pp, qi=qi, h=h, head=head, qm=qm, gate=gate, sbuf=s_scr.at[slot], near=None, far=None,
                    acc=None, c31=c31_ref[2 * MOBA_PAIRS * step + head] * LOG2E)

    def penalty(u, n):
        return 0.0 if u["pen"] is None else u["pen"][n:n + 1, :]

    def score_matmul(u):
        u["s_all"] = _nt(khead[u["head"], 0:(u["qi"] + 1) * MOBA_BLOCK, :], u["qm"])

    def score_reduce(u):
        qi, head = u["qi"], u["head"]
        u["pen"] = None
        if u["gate"] is not None:
            g = u["gate"]
            cnt = jnp.zeros((GATE_ROWS, MOBA_BLOCK), F32)
            for m in range(qi):
                gm = g[m:m + 1, :]
                cnt = cnt + jnp.where((gm > g) | ((gm == g) & (m < rowi)), 1.0, 0.0)
            u["pen"] = jnp.where((rowi < qi) & (cnt < MOBA_TOPK), 0.0, NEG)
        for n in range(qi + 1):
            s = u["s_all"][blk(n)]
            if n == qi:
                s = s + bias_ref[head, MOBA_BLOCK:2 * MOBA_BLOCK, :]
            elif n == qi - 1:
                s = s + bias_ref[head, 0:MOBA_BLOCK, :]
            u["sbuf"][blk(n), :] = s
            t = jnp.max(s.reshape(MOBA_BLOCK // SUBLANES, SUBLANES, MOBA_BLOCK), axis=0)
            if n < qi:
                t = t + penalty(u, n)
            key = "far" if n < qi - 1 else "near"
            u[key] = t if u[key] is None else jnp.maximum(u[key], t)
        mx = u["near"] if u["far"] is None else jnp.maximum(u["near"], u["far"] + u["c31"])
        u["m_near"] = jnp.max(mx, axis=0, keepdims=True)
        u["m_far"] = u["m_near"] - u["c31"]

    def prob_exp(u):
        qi = u["qi"]
        shift = lambda n: (u["m_far"] if n < qi - 1 else u["m_near"]) - (penalty(u, n) if n < qi else 0.0)
        u["pt"] = jnp.concatenate([jnp.exp2(u["sbuf"][blk(n), :] - shift(n)).astype(BF16) for n in range(qi + 1)],
                                  axis=0)

    def prob_matmul(u):
        qi, h = u["qi"], u["h"]
        width = (qi + 1) * MOBA_BLOCK
        r0 = u["head"] * half
        vrows = vt_ref[r0:r0 + half, 0:width]
        ones_rows = jnp.ones((half, width), BF16)
        vt = jnp.concatenate([vrows, ones_rows] if h == 0 else [ones_rows, vrows], axis=0)
        u["acc"] = _dot(vt, u["pt"])

    outs = {}

    def finish(u):
        acc, h = u["acc"], u["h"]
        num = acc[half * h:half * (h + 1)]
        den = acc[half * (1 - h):half * (1 - h) + 1]
        outs[h] = num / den
        if h == 1:
            o_ref[blk(u["qi"]), pair_lanes(u["pp"])] = jnp.concatenate([outs[0], outs[1]], axis=0).T.astype(BF16)

    group = MOBA_GROUP
    stages = [(pp, [(qi, h) for qi in range(g * group, (g + 1) * group) for h in range(2)])
              for pp in range(MOBA_PAIRS) for g in range(nblk // group)]
    prev, kmeans = [], {}
    for g, (pp, members) in enumerate(stages + [(None, [])]):
        if members and pp not in kmeans:
            kmeans[pp] = build(pp)
        cur = [prepare(pp, kmeans[pp], qi, h, (g % 2) * 2 * group + 2 * (qi % group) + h) for qi, h in members]
        for u in prev:
            prob_exp(u)
        for u in cur:
            score_matmul(u)
        for u in prev:
            prob_matmul(u)
        for u in cur:
            score_reduce(u)
        for u in prev:
            finish(u)
        prev = cur


def _moba(qk, vt, bias, c31, batch, seq):
    n = qk.shape[0]
    width = MOBA_PAIRS * LANES
    nstep = A_WIDTH // width
    kern = functools.partial(_moba_kernel, seq=seq)
    return pl.pallas_call(
        kern,
        grid=(batch, nstep),
        in_specs=[pl.BlockSpec(memory_space=pltpu.SMEM),
                  pl.BlockSpec((seq, width), lambda b, p: (b, p)),
                  pl.BlockSpec((seq, width), lambda b, p: (b, nstep + p)),
                  pl.BlockSpec((width, seq), lambda b, p: (p, b)),
                  pl.BlockSpec((2 * MOBA_PAIRS, 2 * MOBA_BLOCK, MOBA_BLOCK), lambda b, p: (p, 0, 0))],
        out_specs=pl.BlockSpec((seq, width), lambda b, p: (b, p)),
        out_shape=jax.ShapeDtypeStruct((n, A_WIDTH), BF16),
        scratch_shapes=[pltpu.VMEM((2 * MOBA_PAIRS, seq, LANES), BF16),
                        pltpu.VMEM((2 * 2 * MOBA_GROUP, seq, MOBA_BLOCK), F32)],
        compiler_params=pltpu.CompilerParams(dimension_semantics=("parallel", "arbitrary"),
                                             vmem_limit_bytes=VMEM_LIMIT),
        name="moba",
    )(c31, qk, qk, vt, bias)


_GLA_LEVELS = int(math.log2(GLA_CHUNK))


_GLA_BIG_LEVELS = _GLA_LEVELS - int(math.log2(SUBLANES))


def _gla_weights():
    c = GLA_CHUNK
    i = np.arange(c)[:, None]
    j = np.arange(c)[None, :]
    mats = [(j <= i)]
    for lvl in range(_GLA_BIG_LEVELS, _GLA_LEVELS):
        s = (c // 2) >> lvl
        ref = (i // (2 * s)) * (2 * s) + s - 1
        mats.append(np.where(i > ref, (j > ref) & (j <= i), (j > i) & (j <= ref)))
    return np.concatenate(mats, axis=0).astype(np.float32)


def _gla_level_map():
    c = GLA_CHUNK
    i = np.arange(c)[:, None]
    j = np.arange(c)[None, :]
    top_bit = np.floor(np.log2(np.maximum(i ^ j, 1))).astype(np.int64)
    lvl = np.where(j < i, _GLA_LEVELS - 1 - top_bit, np.where(j == i, _GLA_LEVELS, _GLA_LEVELS + 1))
    return np.concatenate([lvl, lvl], axis=1).astype(np.int32)


def _gla_kernel(q_ref, k_ref, la_ref, v_ref, w_ref, lmap_ref, o_ref, st_ref):
    c = GLA_CHUNK
    half = LANES // 2
    npair = G_HEADS // 2

    @pl.when(pl.program_id(1) == 0)
    def _init():
        st_ref[...] = jnp.zeros(st_ref.shape, F32)

    w = w_ref[...]
    lmap = lmap_ref[...]
    lane = lax.broadcasted_iota(jnp.int32, (1, LANES), 1)
    lane_c = lax.broadcasted_iota(jnp.int32, (c, LANES), 1)
    lm = [jnp.where(lane_c < half, 1.0, 0.0).astype(BF16), jnp.where(lane_c >= half, 1.0, 0.0).astype(BF16)]
    row_small = lax.broadcasted_iota(jnp.int32, (c, G_KEY_WIDTH), 0)

    def level_factors(q, k, b, d_all, lvl):
        s = (c // 2) >> lvl
        if s < SUBLANES:
            j = lvl - _GLA_BIG_LEVELS
            e = jnp.exp2(d_all[c * (j + 1):c * (j + 2)])
            odd = (row_small // s) % 2 == 1
            return jnp.where(odd, q * e, 0.0), jnp.where(odd, 0.0, k * e)
        zeros = jnp.zeros((s, q.shape[1]), F32)
        qp, kp = [], []
        for blk in range(c // s):
            rows = slice(blk * s, (blk + 1) * s)
            ref = (blk // 2) * 2 * s + s - 1
            if blk % 2 == 1:
                qp.append(q[rows] * jnp.exp2(b[rows] - b[ref:ref + 1]))
                kp.append(zeros)
            else:
                qp.append(zeros)
                kp.append(k[rows] * jnp.exp2(b[ref:ref + 1] - b[rows]))
        return jnp.concatenate(qp, axis=0), jnp.concatenate(kp, axis=0)

    nch = GLA_STEP // c
    rows_of = [slice(ch * c, (ch + 1) * c) for ch in range(nch)]
    pair_lanes = [slice(LANES * p, LANES * (p + 1)) for p in range(npair)]

    def decays(ch):
        la = la_ref[rows_of[ch], :] * LOG2E
        hi = la.astype(BF16)
        lo = (la - hi.astype(F32)).astype(BF16)
        return _dot(w, hi) + _dot(w, lo)

    def factors(ch, d_all):
        q = q_ref[rows_of[ch], :]
        k = k_ref[rows_of[ch], :]
        b = d_all[0:c]
        blast = b[c - 1:c, :]
        qts, kts = [], []
        for lvl in range(_GLA_LEVELS):
            qt, kt = level_factors(q, k, b, d_all, lvl)
            qts.append(qt.astype(BF16))
            kts.append(kt.astype(BF16))
        qts.append(q.astype(BF16))
        kts.append(k.astype(BF16))
        return dict(ch=ch, qts=qts, kts=kts, qe=(q * jnp.exp2(b)).astype(BF16),
                    khat=(k * jnp.exp2(blast - b)).astype(BF16), dec=jnp.exp2(blast),
                    a=[jnp.zeros((c, 2 * c), F32)] * npair)

    def level_scores(units):
        for lvl in range(_GLA_LEVELS + 1):
            mask = lmap == lvl
            s = (c // 2) >> lvl if lvl < _GLA_BIG_LEVELS else c
            blocks = [slice(i * s, (i + 1) * s) for i in range(c // s)]
            live = [i for i in range(len(blocks)) if i % 2 == 1 or s == c]
            for u in units:
                for p, sl in enumerate(pair_lanes):
                    kp = u["kts"][lvl][:, sl]
                    ql = u["qts"][lvl]
                    pm = _nt(jnp.concatenate([ql[blocks[i], sl] for i in live], axis=0),
                             jnp.concatenate([kp * lm[0], kp * lm[1]], axis=0))
                    a = u["a"][p]
                    pieces = [a[r] for r in blocks]
                    for n, i in enumerate(live):
                        pieces[i] = jnp.where(mask[blocks[i]], pm[n * s:(n + 1) * s], pieces[i])
                    u["a"] = [jnp.concatenate(pieces, axis=0) if j == p else v for j, v in enumerate(u["a"])]

    def stateless_matmuls(units):
        for u in units:
            vhs = [v_ref[rows_of[u["ch"]], LANES * h:LANES * (h + 1)] for h in range(G_HEADS)]
            u["intra"] = [_dot(u["a"][h // 2][:, (h % 2) * c:(h % 2 + 1) * c].astype(BF16), vhs[h])
                          for h in range(G_HEADS)]
            u["inc"] = [_tn(vhs[h], u["khat"][:, pair_lanes[h // 2]]) for h in range(G_HEADS)]

    def state_chain(units, states):
        for u in units:
            rows = rows_of[u["ch"]]
            for p, sl in enumerate(pair_lanes):
                stb = states[p].astype(BF16)
                for hh in range(2):
                    h = 2 * p + hh
                    o_ref[rows, LANES * h:LANES * (h + 1)] = u["intra"][h] + _nt(u["qe"][:, sl] * lm[hh], stb)
                states[p] = states[p] * u["dec"][:, sl] + jnp.where(lane < half, u["inc"][2 * p], u["inc"][2 * p + 1])

    d_alls = [decays(ch) for ch in range(nch)]
    chunks = [factors(ch, d_alls[ch]) for ch in range(nch)]
    level_scores(chunks)
    stateless_matmuls(chunks)
    states = [st_ref[p] for p in range(npair)]
    state_chain(chunks, states)
    for p in range(npair):
        st_ref[p] = states[p]


def _gla(qg, kg, la, vg, batch, seq):
    n = qg.shape[0]
    nchunk = seq // GLA_STEP
    w = jnp.asarray(_gla_weights(), BF16)
    lmap = jnp.asarray(_gla_level_map())
    row = lambda width: pl.BlockSpec((GLA_STEP, width), lambda b, c: (b * nchunk + c, 0))
    const = functools.partial(pl.BlockSpec, pipeline_mode=pl.Buffered(1))
    return pl.pallas_call(
        _gla_kernel,
        grid=(batch, nchunk),
        in_specs=[row(G_KEY_WIDTH), row(G_KEY_WIDTH), row(G_KEY_WIDTH), row(G_WIDTH),
                  const(w.shape, lambda b, c: (0, 0)),
                  const(lmap.shape, lambda b, c: (0, 0))],
        out_specs=row(G_WIDTH),
        out_shape=jax.ShapeDtypeStruct((n, G_WIDTH), F32),
        scratch_shapes=[pltpu.VMEM((G_HEADS // 2, G_HEAD_V, LANES), F32)],
        compiler_params=pltpu.CompilerParams(dimension_semantics=("parallel", "arbitrary"),
                                             vmem_limit_bytes=VMEM_LIMIT),
        name="gla",
    )(qg, kg, la, vg, w, lmap)


def _memkv_kernel(m_ref, g_ref, w_ref, k_ref, v_ref):
    nb, mlen, _ = m_ref.shape
    h = _rms(m_ref[...].reshape(nb * mlen, D_MODEL), g_ref[...]).astype(BF16)
    k_ref[...] = _dot(h, w_ref[:, 0:D_MODEL]).astype(BF16).reshape(nb, mlen, D_MODEL)
    v_ref[...] = _dot(h, w_ref[:, D_MODEL:2 * D_MODEL]).astype(BF16).reshape(nb, mlen, D_MODEL)


def _memkv(mem, g, w):
    batch, mlen, _ = mem.shape
    nb = 2 if batch % 2 == 0 else 1
    const = functools.partial(pl.BlockSpec, pipeline_mode=pl.Buffered(1))
    blk = pl.BlockSpec((nb, mlen, D_MODEL), lambda b: (b, 0, 0))
    return pl.pallas_call(
        _memkv_kernel,
        grid=(batch // nb,),
        in_specs=[blk, const((1, D_MODEL), lambda b: (0, 0)), const((D_MODEL, 2 * D_MODEL), lambda b: (0, 0))],
        out_specs=[blk, blk],
        out_shape=[jax.ShapeDtypeStruct(mem.shape, BF16)] * 2,
        compiler_params=pltpu.CompilerParams(dimension_semantics=("parallel",), vmem_limit_bytes=VMEM_LIMIT),
        name="memkv",
    )(mem, g, w)


def _post_kernel(x_ref, oa_ref, og_ref, rg_ref, gn_ref, wo_ref, gx_ref, wq_ref, kx_ref, vx_ref, wxo_ref, o_ref):
    tm = x_ref.shape[0]
    parts = [slice(i * tm // POST_PARTS, (i + 1) * tm // POST_PARTS) for i in range(POST_PARTS)]
    heads = [slice(X_HEAD_DIM * hd, X_HEAD_DIM * (hd + 1)) for hd in range(X_HEADS)]

    def out_proj(r):
        gla = []
        for h in range(G_HEADS):
            sl = slice(G_HEAD_V * h, G_HEAD_V * (h + 1))
            rg = rg_ref[r, sl]
            gla.append((_rms(og_ref[r, sl], gn_ref[...]) * (rg / (1.0 + jnp.exp(-rg)))).astype(BF16))
        return (x_ref[r, :] + _dot(oa_ref[r, :], wo_ref[0:A_WIDTH, :])
                + _dot(jnp.concatenate(gla, axis=1), wo_ref[A_WIDTH:D_MODEL, :]))

    def scores(x1):
        h = _rms(x1, gx_ref[...]).astype(BF16)
        qx = (_dot(h, wq_ref[...]) * (LOG2E * X_HEAD_DIM ** -0.5)).astype(BF16)
        return [_nt(qx[:, sl], kx_ref[0, :, sl]) for sl in heads]

    def attend(r, x1, s_heads):
        outs = []
        for sl, s in zip(heads, s_heads):
            pexp = jnp.exp2(s - jnp.max(s, axis=1, keepdims=True))
            pn = (pexp / jnp.sum(pexp, axis=1, keepdims=True)).astype(BF16)
            outs.append(_dot(pn, vx_ref[0, :, sl]).astype(BF16))
        o_ref[r, :] = x1 + _dot(jnp.concatenate(outs, axis=1), wxo_ref[...])

    x1s = [out_proj(r) for r in parts]
    s_parts = [scores(x1) for x1 in x1s]
    for r, x1, s_heads in zip(parts, x1s, s_parts):
        attend(r, x1, s_heads)


def _post(x2, oa, og, rg, gn, wo, gx, wq, kx, vx, wxo, seq):
    n = x2.shape[0]
    tm = POST_TILE
    per_b = seq // tm
    mlen = kx.shape[1]
    const = functools.partial(pl.BlockSpec, pipeline_mode=pl.Buffered(1))
    row = lambda width: pl.BlockSpec((tm, width), lambda i: (i, 0))
    mem = pl.BlockSpec((1, mlen, D_MODEL), lambda i: (i // per_b, 0, 0))
    sq = const((D_MODEL, D_MODEL), lambda i: (0, 0))
    return pl.pallas_call(
        _post_kernel,
        grid=(n // tm,),
        in_specs=[row(D_MODEL), row(A_WIDTH), row(G_WIDTH), row(G_WIDTH), const((1, G_HEAD_V), lambda i: (0, 0)), sq,
                  const((1, D_MODEL), lambda i: (0, 0)), sq, mem, mem, sq],
        out_specs=row(D_MODEL),
        out_shape=jax.ShapeDtypeStruct((n, D_MODEL), F32),
        compiler_params=pltpu.CompilerParams(dimension_semantics=("parallel",), vmem_limit_bytes=VMEM_LIMIT),
        name="post",
    )(x2, oa, og, rg, gn, wo, gx, wq, kx, vx, wxo)


def _mlp_kernel(x_ref, g_ref, wu_ref, wd_ref, gf_ref, o_ref, *, final_norm):
    tm = x_ref.shape[0]
    parts = [slice(i * tm // MLP_PARTS, (i + 1) * tm // MLP_PARTS) for i in range(MLP_PARTS)]
    acc = [x_ref[r, :] for r in parts]
    h = [_rms(x, g_ref[...]).astype(BF16) for x in acc]
    for c in range(D_FF // D_MODEL):
        sl = slice(D_MODEL * c, D_MODEL * (c + 1))
        u = [jnp.square(jnp.maximum(_dot(v, wu_ref[:, sl]), 0.0)).astype(BF16) for v in h]
        acc = [a + _dot(v, wd_ref[sl, :]) for a, v in zip(acc, u)]
    for r, a in zip(parts, acc):
        o_ref[r, :] = _rms(a, gf_ref[...]) if final_norm else a


def _mlp(x2, g, wu, wd, gf, final_norm):
    n = x2.shape[0]
    tm = MLP_TILE
    const = functools.partial(pl.BlockSpec, pipeline_mode=pl.Buffered(1))
    row = pl.BlockSpec((tm, D_MODEL), lambda i: (i, 0))
    vec = const((1, D_MODEL), lambda i: (0, 0))
    return pl.pallas_call(
        functools.partial(_mlp_kernel, final_norm=final_norm),
        grid=(n // tm,),
        in_specs=[row, vec, const((D_MODEL, D_FF), lambda i: (0, 0)), const((D_FF, D_MODEL), lambda i: (0, 0)), vec],
        out_specs=row,
        out_shape=jax.ShapeDtypeStruct((n, D_MODEL), F32),
        compiler_params=pltpu.CompilerParams(dimension_semantics=("parallel",), vmem_limit_bytes=VMEM_LIMIT),
        name="mlp",
    )(x2, g, wu, wd, gf)


def kernel(x, mem, rp_table, norm_mix, w_in, w_gate_up, b_gate, g_norm, w_out, norm_xattn, norm_mem, w_xq, w_xkv,
           w_xo, norm_mlp, w_up, w_down, norm_final):
    batch, seq, _ = x.shape
    depth = w_in.shape[0]
    assert seq % max(INPROJ_TILE, POST_TILE, MLP_TILE, GLA_STEP) == 0 and seq // MOBA_BLOCK <= 8
    x2 = x.reshape(batch * seq, D_MODEL)
    bias = _bias_tiles(rp_table)
    c31 = rp_table[RP_BUCKETS - 1]
    glr_lo = 3 * A_WIDTH + 2 * G_KEY_WIDTH + G_WIDTH
    rg_lo = glr_lo + G_GATE_RANK
    pad_rank = LANES - G_GATE_RANK
    for l in range(depth):
        col = lambda lo, hi: w_in[l, :, lo:hi].astype(BF16)
        wglr = jnp.pad(col(glr_lo, rg_lo), ((0, 0), (0, pad_rank)))
        wgu = jnp.pad(w_gate_up[l].astype(BF16), ((0, pad_rank), (0, 0)))
        later = (w_out[l], w_xq[l], w_xkv[l], w_xo[l], w_up[l], w_down[l])
        qk, vt, qg, kg, la, vg, rg, wo, wxq, wxkv, wxo, wup, wdown = _inproj(
            x2, norm_mix[l][None], w_in[l].astype(BF16), col(rg_lo, rg_lo + G_WIDTH), wglr, wgu, b_gate[l][None], later)
        oa = _moba(qk, vt, bias, c31, batch, seq)
        og = _gla(qg, kg, la, vg, batch, seq)
        kx, vx = _memkv(mem, norm_mem[l][None], wxkv)
        x2 = _post(x2, oa, og, rg, g_norm[l][None], wo, norm_xattn[l][None], wxq, kx, vx, wxo, seq)
        last = l == depth - 1
        x2 = _mlp(x2, norm_mlp[l][None], wup, wdown, norm_final[None], last)
    return x2.reshape(batch, seq, D_MODEL)
```

```python
import functools
import math

import numpy as np
import jax
import jax.numpy as jnp
from jax import lax
from jax.experimental import pallas as pl
from jax.experimental.pallas import tpu as pltpu

F32 = jnp.float32
BF16 = jnp.bfloat16

D_MODEL = 1024
A_HEADS = 8
A_HEAD_DIM = 64
A_WIDTH = A_HEADS * A_HEAD_DIM
MOBA_BLOCK = 256
MOBA_TOPK = 3
G_HEADS = 4
G_WIDTH = D_MODEL - A_WIDTH
G_HEAD_V = G_WIDTH // G_HEADS
G_KEY_WIDTH = G_WIDTH // 2
G_HEAD_K = G_KEY_WIDTH // G_HEADS
G_GATE_RANK = 16
G_GATE_NORM = 16.0
X_HEADS = 4
X_HEAD_DIM = D_MODEL // X_HEADS
D_FF = 4 * D_MODEL
RP_BUCKETS = 32
RP_MAX_DIST = 128
EPS = 1e-6

LANES = 128
SUBLANES = 8
GATE_ROWS = 16
NEG = -1e30
MOBA_PAIRS = 2
MOBA_GROUP = 2
GLA_CHUNK = 128
GLA_STEP = 1024
LOG2E = float(np.log2(np.e))
INPROJ_TILE = 1024
INPROJ_PARTS = 2
MLP_TILE = 1024
MLP_PARTS = 2
POST_TILE = 1024
POST_PARTS = 2
VMEM_LIMIT = 56 * 1024 * 1024

def _nt(a, b):
    return lax.dot_general(a, b, (((1,), (1,)), ((), ())), preferred_element_type=F32)


def _tn(a, b):
    return lax.dot_general(a, b, (((0,), (0,)), ((), ())), preferred_element_type=F32)


def _dot(a, b):
    return jnp.dot(a, b, preferred_element_type=F32)


def _rms(x, g):
    return x * lax.rsqrt(jnp.mean(x * x, axis=-1, keepdims=True) + EPS) * g


def _bucket_thresholds():
    max_exact = RP_BUCKETS // 2
    d = np.arange(1, 4 * RP_MAX_DIST)
    val = (np.log(d.astype(np.float32) / np.float32(max_exact)) / np.float32(math.log(RP_MAX_DIST / max_exact))
           * np.float32(RP_BUCKETS - max_exact))
    bucket = np.minimum(max_exact + val.astype(np.int32), RP_BUCKETS - 1)
    return [int(d[(d >= max_exact) & (bucket >= max_exact + k)][0]) for k in range(1, RP_BUCKETS - max_exact)]


_THRESHOLDS = _bucket_thresholds()


def _bias_kernel(tab_ref, out_ref):
    h = pl.program_id(0)
    rows, width = 2 * MOBA_BLOCK, 3 * MOBA_BLOCK
    d = lax.broadcasted_iota(jnp.int32, (SUBLANES, width), 1)
    max_exact = RP_BUCKETS // 2
    bucket = jnp.where(d < max_exact, d, max_exact)
    for t in _THRESHOLDS:
        bucket = bucket + jnp.where(d >= t, 1, 0)
    prof = jnp.full((SUBLANES, width), NEG, F32)
    for b in range(RP_BUCKETS):
        prof = jnp.where(bucket == b, tab_ref[b, h] * LOG2E, prof)
    prof = jnp.where(d < rows, prof, NEG)
    tile = pltpu.roll(jnp.broadcast_to(prof[0:1], (rows, width)), 0, 1, stride=1, stride_axis=0)
    out_ref[0] = tile[:, MOBA_BLOCK:2 * MOBA_BLOCK]


def _bias_tiles(rp_table):
    return pl.pallas_call(
        _bias_kernel,
        grid=(A_HEADS,),
        in_specs=[pl.BlockSpec(memory_space=pltpu.SMEM)],
        out_specs=pl.BlockSpec((1, 2 * MOBA_BLOCK, MOBA_BLOCK), lambda h: (h, 0, 0)),
        out_shape=jax.ShapeDtypeStruct((A_HEADS, 2 * MOBA_BLOCK, MOBA_BLOCK), F32),
        name="bias",
    )(rp_table)


def _inproj_kernel(x_ref, g_ref, w_ref, wrg_ref, wglr_ref, wgu_ref, bg_ref, *rest):
    ncast = (len(rest) - 7) // 2
    cast_in, cast_out = rest[:ncast], rest[ncast + 7:]
    qk_ref, vt_ref, qg_ref, kg_ref, la_ref, vg_ref, rg_ref = rest[ncast:ncast + 7]
    tm = x_ref.shape[0]
    parts = [slice(i * tm // INPROJ_PARTS, (i + 1) * tm // INPROJ_PARTS) for i in range(INPROJ_PARTS)]
    hs = [_rms(x_ref[r, :], g_ref[...]).astype(BF16) for r in parts]
    va_lo = 2 * A_WIDTH
    qg_lo = va_lo + A_WIDTH
    kg_lo = qg_lo + G_KEY_WIDTH
    vg_lo = kg_lo + G_KEY_WIDTH

    glrs = [_dot(h, wglr_ref[...]).astype(BF16) for h in hs]
    for r, h in zip(parts, hs):
        qk_ref[r, 0:A_WIDTH] = (_dot(h, w_ref[:, 0:A_WIDTH]) * (LOG2E * A_HEAD_DIM ** -0.5)).astype(BF16)
    for r, h in zip(parts, hs):
        qk_ref[r, A_WIDTH:va_lo] = _dot(h, w_ref[:, A_WIDTH:va_lo]).astype(BF16)
    for r, glr in zip(parts, glrs):
        z = _dot(glr, wgu_ref[...]) + bg_ref[...]
        log_sig = jnp.minimum(z, 0.0) - jnp.log(1.0 + jnp.exp(-jnp.abs(z)))
        la_ref[r, :] = log_sig * (1.0 / G_GATE_NORM)
    for r, h in zip(parts, hs):
        vt_ref[:, r] = _dot(h, w_ref[:, va_lo:qg_lo]).astype(BF16).T
    for r, h in zip(parts, hs):
        qg_ref[r, :] = _dot(h, w_ref[:, qg_lo:kg_lo]) * (G_HEAD_K ** -0.5)
        kg_ref[r, :] = _dot(h, w_ref[:, kg_lo:vg_lo])
    for r, h in zip(parts, hs):
        vg_ref[r, :] = _dot(h, w_ref[:, vg_lo:vg_lo + G_WIDTH]).astype(BF16)
    for r, h in zip(parts, hs):
        rg_ref[r, :] = _dot(h, wrg_ref[...])
    for src, dst in zip(cast_in, cast_out):
        dst[...] = src[...].astype(BF16)


def _inproj(x2, g, w, wrg, wglr, wgu, bg, later):
    n = x2.shape[0]
    tm = INPROJ_TILE
    steps = n // tm
    const = lambda a: pl.BlockSpec(a.shape, lambda i: (0, 0), pipeline_mode=pl.Buffered(1))
    row = lambda width: pl.BlockSpec((tm, width), lambda i: (i, 0))
    assert all(a.shape[0] % (steps * GATE_ROWS) == 0 for a in later)
    part = [pl.BlockSpec((a.shape[0] // steps, a.shape[1]), lambda i: (i, 0)) for a in later]
    return pl.pallas_call(
        _inproj_kernel,
        grid=(steps,),
        in_specs=[row(D_MODEL)] + [const(a) for a in (g, w, wrg, wglr, wgu, bg)] + part,
        out_specs=[row(2 * A_WIDTH), pl.BlockSpec((A_WIDTH, tm), lambda i: (0, i)),
                   row(G_KEY_WIDTH), row(G_KEY_WIDTH), row(G_KEY_WIDTH), row(G_WIDTH), row(G_WIDTH)] + part,
        out_shape=[jax.ShapeDtypeStruct((n, 2 * A_WIDTH), BF16),
                   jax.ShapeDtypeStruct((A_WIDTH, n), BF16),
                   jax.ShapeDtypeStruct((n, G_KEY_WIDTH), F32),
                   jax.ShapeDtypeStruct((n, G_KEY_WIDTH), F32),
                   jax.ShapeDtypeStruct((n, G_KEY_WIDTH), F32),
                   jax.ShapeDtypeStruct((n, G_WIDTH), BF16),
                   jax.ShapeDtypeStruct((n, G_WIDTH), F32)] + [jax.ShapeDtypeStruct(a.shape, BF16) for a in later],
        compiler_params=pltpu.CompilerParams(dimension_semantics=("parallel",), vmem_limit_bytes=VMEM_LIMIT),
        name="inproj",
    )(x2, g, w, wrg, wglr, wgu, bg, *later)


def _moba_kernel(c31_ref, q_ref, k_ref, vt_ref, bias_ref, o_ref, khead, s_scr, *, seq):
    step = pl.program_id(1)
    nblk = seq // MOBA_BLOCK
    half = LANES // 2
    lane2 = lax.broadcasted_iota(jnp.int32, (seq, LANES), 1)
    lane_q = lax.broadcasted_iota(jnp.int32, (MOBA_BLOCK, LANES), 1)
    rowi = lax.broadcasted_iota(jnp.int32, (GATE_ROWS, MOBA_BLOCK), 0)
    blk = lambda n: slice(n * MOBA_BLOCK, (n + 1) * MOBA_BLOCK)
    pair_lanes = lambda pp: slice(pp * LANES, (pp + 1) * LANES)

    def build(pp):
        k = k_ref[:, pair_lanes(pp)]
        khead[2 * pp] = k * jnp.where(lane2 < half, 1.0, 0.0).astype(BF16)
        khead[2 * pp + 1] = k * jnp.where(lane2 >= half, 1.0, 0.0).astype(BF16)
        n_i = lax.broadcasted_iota(jnp.int32, (GATE_ROWS, seq), 0)
        t_i = lax.broadcasted_iota(jnp.int32, (GATE_ROWS, seq), 1)
        avg = jnp.where(t_i // MOBA_BLOCK == n_i, 1.0 / MOBA_BLOCK, 0.0).astype(BF16)
        km = _dot(avg, k)
        kmh = km.astype(BF16)
        return kmh, (km - kmh.astype(F32)).astype(BF16)

    def prepare(pp, kmeans, qi, h, slot):
        qm = q_ref[blk(qi), pair_lanes(pp)] * jnp.where((lane_q < half) if h == 0 else (lane_q >= half), 1.0,
                                                        0.0).astype(BF16)
        head = 2 * pp + h
        return dict(pp=pp, qi=qi, h=h, head=head, qm=qm, kmeans=kmeans if qi > MOBA_TOPK else None,
                    sbuf=s_scr.at[slot], near=None, far=None, acc=None,
                    c31=c31_ref[2 * MOBA_PAIRS * step + head] * LOG2E)

    def penalty(u, n):
        return 0.0 if u["pen"] is None else u["pen"][n:n + 1, :]

    def score_matmul(u):
        width = (u["qi"] + 1) * MOBA_BLOCK
        keys = khead[u["head"], 0:width, :]
        if u["kmeans"] is None:
            u["s_all"], u["gate"] = _nt(keys, u["qm"]), None
        else:
            s = _nt(jnp.concatenate([keys, u["kmeans"][0], u["kmeans"][1]], axis=0), u["qm"])
            u["s_all"] = s[0:width]
            u["gate"] = s[width:width + GATE_ROWS] + s[width + GATE_ROWS:width + 2 * GATE_ROWS]

    def score_reduce(u):
        qi, head = u["qi"], u["head"]
        u["pen"] = None
        if u["gate"] is not None:
            g = u["gate"]
            cnt = jnp.zeros((GATE_ROWS, MOBA_BLOCK), F32)
            for m in range(qi):
                gm = g[m:m + 1, :]
                cnt = cnt + jnp.where((gm > g) | ((gm == g) & (m < rowi)), 1.0, 0.0)
            u["pen"] = jnp.where((rowi < qi) & (cnt < MOBA_TOPK), 0.0, NEG)
        for n in range(qi + 1):
            s = u["s_all"][blk(n)]
            if n == qi:
                s = s + bias_ref[head, MOBA_BLOCK:2 * MOBA_BLOCK, :]
            elif n == qi - 1:
                s = s + bias_ref[head, 0:MOBA_BLOCK, :]
            u["sbuf"][blk(n), :] = s
            t = jnp.max(s.reshape(MOBA_BLOCK // SUBLANES, SUBLANES, MOBA_BLOCK), axis=0)
            if n < qi:
                t = t + penalty(u, n)
            key = "far" if n < qi - 1 else "near"
            u[key] = t if u[key] is None else jnp.maximum(u[key], t)
        mx = u["near"] if u["far"] is None else jnp.maximum(u["near"], u["far"] + u["c31"])
        u["m_near"] = jnp.max(mx, axis=0, keepdims=True)
        u["m_far"] = u["m_near"] - u["c31"]

    def prob_exp(u):
        qi = u["qi"]
        shift = lambda n: (u["m_far"] if n < qi - 1 else u["m_near"]) - (penalty(u, n) if n < qi else 0.0)
        u["pt"] = jnp.concatenate([jnp.exp2(u["sbuf"][blk(n), :] - shift(n)).astype(BF16) for n in range(qi + 1)],
                                  axis=0)

    def prob_matmul(u):
        qi, h = u["qi"], u["h"]
        width = (qi + 1) * MOBA_BLOCK
        r0 = u["head"] * half
        vrows = vt_ref[r0:r0 + half, 0:width]
        ones_rows = jnp.ones((half, width), BF16)
        vt = jnp.concatenate([vrows, ones_rows] if h == 0 else [ones_rows, vrows], axis=0)
        u["acc"] = _dot(vt, u["pt"])

    outs = {}

    def finish(u):
        acc, h = u["acc"], u["h"]
        num = acc[half * h:half * (h + 1)]
        den = acc[half * (1 - h):half * (1 - h) + 1]
        outs[h] = num / den
        if h == 1:
            o_ref[blk(u["qi"]), pair_lanes(u["pp"])] = jnp.concatenate([outs[0], outs[1]], axis=0).T.astype(BF16)

    group = MOBA_GROUP
    stages = [(pp, [(qi, h) for qi in range(g * group, (g + 1) * group) for h in range(2)])
              for pp in range(MOBA_PAIRS) for g in range(nblk // group)]
    prev, kmeans = [], {}
    for g, (pp, members) in enumerate(stages + [(None, [])]):
        if members and pp not in kmeans:
            kmeans[pp] = build(pp)
        cur = [prepare(pp, kmeans[pp], qi, h, (g % 2) * 2 * group + 2 * (qi % group) + h) for qi, h in members]
        for u in prev:
            prob_exp(u)
        for u in cur:
            score_matmul(u)
        for u in prev:
            prob_matmul(u)
        for u in cur:
            score_reduce(u)
        for u in prev:
            finish(u)
        prev = cur


def _moba(qk, vt, bias, c31, batch, seq):
    n = qk.shape[0]
    width = MOBA_PAIRS * LANES
    nstep = A_WIDTH // width
    kern = functools.partial(_moba_kernel, seq=seq)
    return pl.pallas_call(
        kern,
        grid=(batch, nstep),
        in_specs=[pl.BlockSpec(memory_space=pltpu.SMEM),
                  pl.BlockSpec((seq, width), lambda b, p: (b, p)),
                  pl.BlockSpec((seq, width), lambda b, p: (b, nstep + p)),
                  pl.BlockSpec((width, seq), lambda b, p: (p, b)),
                  pl.BlockSpec((2 * MOBA_PAIRS, 2 * MOBA_BLOCK, MOBA_BLOCK), lambda b, p: (p, 0, 0))],
        out_specs=pl.BlockSpec((seq, width), lambda b, p: (b, p)),
        out_shape=jax.ShapeDtypeStruct((n, A_WIDTH), BF16),
        scratch_shapes=[pltpu.VMEM((2 * MOBA_PAIRS, seq, LANES), BF16),
                        pltpu.VMEM((2 * 2 * MOBA_GROUP, seq, MOBA_BLOCK), F32)],
        compiler_params=pltpu.CompilerParams(dimension_semantics=("parallel", "arbitrary"),
                                             vmem_limit_bytes=VMEM_LIMIT),
        name="moba",
    )(c31, qk, qk, vt, bias)


_GLA_LEVELS = int(math.log2(GLA_CHUNK))


_GLA_BIG_LEVELS = _GLA_LEVELS - int(math.log2(SUBLANES))


def _gla_weights():
    c = GLA_CHUNK
    i = np.arange(c)[:, None]
    j = np.arange(c)[None, :]
    mats = [(j <= i)]
    for lvl in range(_GLA_BIG_LEVELS, _GLA_LEVELS):
        s = (c // 2) >> lvl
        ref = (i // (2 * s)) * (2 * s) + s - 1
        mats.append(np.where(i > ref, (j > ref) & (j <= i), (j > i) & (j <= ref)))
    return np.concatenate(mats, axis=0).astype(np.float32)


def _gla_level_map():
    c = GLA_CHUNK
    i = np.arange(c)[:, None]
    j = np.arange(c)[None, :]
    top_bit = np.floor(np.log2(np.maximum(i ^ j, 1))).astype(np.int64)
    lvl = np.where(j < i, _GLA_LEVELS - 1 - top_bit, np.where(j == i, _GLA_LEVELS, _GLA_LEVELS + 1))
    return np.concatenate([lvl, lvl], axis=1).astype(np.int32)


def _gla_kernel(q_ref, k_ref, la_ref, v_ref, rg_ref, w_ref, lmap_ref, gn_ref, o_ref, st_ref):
    c = GLA_CHUNK
    half = LANES // 2
    npair = G_HEADS // 2

    @pl.when(pl.program_id(1) == 0)
    def _init():
        st_ref[...] = jnp.zeros(st_ref.shape, F32)

    w = w_ref[...]
    lmap = lmap_ref[...]
    lane = lax.broadcasted_iota(jnp.int32, (1, LANES), 1)
    lane_c = lax.broadcasted_iota(jnp.int32, (c, LANES), 1)
    lm = [jnp.where(lane_c < half, 1.0, 0.0).astype(BF16), jnp.where(lane_c >= half, 1.0, 0.0).astype(BF16)]
    row_small = lax.broadcasted_iota(jnp.int32, (c, G_KEY_WIDTH), 0)

    def level_factors(q, k, b, d_all, lvl):
        s = (c // 2) >> lvl
        if s < SUBLANES:
            j = lvl - _GLA_BIG_LEVELS
            e = jnp.exp2(d_all[c * (j + 1):c * (j + 2)])
            odd = (row_small // s) % 2 == 1
            return jnp.where(odd, q * e, 0.0), jnp.where(odd, 0.0, k * e)
        zeros = jnp.zeros((s, q.shape[1]), F32)
        qp, kp = [], []
        for blk in range(c // s):
            rows = slice(blk * s, (blk + 1) * s)
            ref = (blk // 2) * 2 * s + s - 1
            if blk % 2 == 1:
                qp.append(q[rows] * jnp.exp2(b[rows] - b[ref:ref + 1]))
                kp.append(zeros)
            else:
                qp.append(zeros)
                kp.append(k[rows] * jnp.exp2(b[ref:ref + 1] - b[rows]))
        return jnp.concatenate(qp, axis=0), jnp.concatenate(kp, axis=0)

    nch = GLA_STEP // c
    rows_of = [slice(ch * c, (ch + 1) * c) for ch in range(nch)]
    pair_lanes = [slice(LANES * p, LANES * (p + 1)) for p in range(npair)]

    def decays(ch):
        la = la_ref[rows_of[ch], :] * LOG2E
        hi = la.astype(BF16)
        lo = (la - hi.astype(F32)).astype(BF16)
        return _dot(w, hi) + _dot(w, lo)

    def factors(ch, d_all):
        q = q_ref[rows_of[ch], :]
        k = k_ref[rows_of[ch], :]
        b = d_all[0:c]
        blast = b[c - 1:c, :]
        qts, kts = [], []
        for lvl in range(_GLA_LEVELS):
            qt, kt = level_factors(q, k, b, d_all, lvl)
            qts.append(qt.astype(BF16))
            kts.append(kt.astype(BF16))
        qts.append(q.astype(BF16))
        kts.append(k.astype(BF16))
        return dict(ch=ch, qts=qts, kts=kts, qe=(q * jnp.exp2(b)).astype(BF16),
                    khat=(k * jnp.exp2(blast - b)).astype(BF16), dec=jnp.exp2(blast),
                    a=[jnp.zeros((c, 2 * c), F32)] * npair)

    def level_scores(units):
        for lvl in range(_GLA_LEVELS + 1):
            mask = lmap == lvl
            s = (c // 2) >> lvl if lvl < _GLA_BIG_LEVELS else c
            blocks = [slice(i * s, (i + 1) * s) for i in range(c // s)]
            live = [i for i in range(len(blocks)) if i % 2 == 1 or s == c]
            for u in units:
                for p, sl in enumerate(pair_lanes):
                    kp = u["kts"][lvl][:, sl]
                    ql = u["qts"][lvl]
                    pm = _nt(jnp.concatenate([ql[blocks[i], sl] for i in live], axis=0),
                             jnp.concatenate([kp * lm[0], kp * lm[1]], axis=0))
                    a = u["a"][p]
                    pieces = [a[r] for r in blocks]
                    for n, i in enumerate(live):
                        pieces[i] = jnp.where(mask[blocks[i]], pm[n * s:(n + 1) * s], pieces[i])
                    u["a"] = [jnp.concatenate(pieces, axis=0) if j == p else v for j, v in enumerate(u["a"])]

    def stateless_matmuls(units):
        for u in units:
            vhs = [v_ref[rows_of[u["ch"]], LANES * h:LANES * (h + 1)] for h in range(G_HEADS)]
            u["intra"] = [_dot(u["a"][h // 2][:, (h % 2) * c:(h % 2 + 1) * c].astype(BF16), vhs[h])
                          for h in range(G_HEADS)]
            u["inc"] = [_tn(vhs[h], u["khat"][:, pair_lanes[h // 2]]) for h in range(G_HEADS)]

    def state_chain(units, states):
        for u in units:
            rows = rows_of[u["ch"]]
            for p, sl in enumerate(pair_lanes):
                stb = states[p].astype(BF16)
                for hh in range(2):
                    h = 2 * p + hh
                    o = u["intra"][h] + _nt(u["qe"][:, sl] * lm[hh], stb)
                    y = _rms(o, gn_ref[...])
                    rg = rg_ref[rows, LANES * h:LANES * (h + 1)]
                    y = y * (rg / (1.0 + jnp.exp(-rg)))
                    o_ref[rows, LANES * h:LANES * (h + 1)] = y.astype(BF16)
                states[p] = states[p] * u["dec"][:, sl] + jnp.where(lane < half, u["inc"][2 * p], u["inc"][2 * p + 1])

    d_alls = [decays(ch) for ch in range(nch)]
    chunks = [factors(ch, d_alls[ch]) for ch in range(nch)]
    level_scores(chunks)
    stateless_matmuls(chunks)
    states = [st_ref[p] for p in range(npair)]
    state_chain(chunks, states)
    for p in range(npair):
        st_ref[p] = states[p]


def _gla(qg, kg, la, vg, rg, gn, batch, seq):
    n = qg.shape[0]
    nchunk = seq // GLA_STEP
    w = jnp.asarray(_gla_weights(), BF16)
    lmap = jnp.asarray(_gla_level_map())
    row = lambda width: pl.BlockSpec((GLA_STEP, width), lambda b, c: (b * nchunk + c, 0))
    const = functools.partial(pl.BlockSpec, pipeline_mode=pl.Buffered(1))
    return pl.pallas_call(
        _gla_kernel,
        grid=(batch, nchunk),
        in_specs=[row(G_KEY_WIDTH), row(G_KEY_WIDTH), row(G_KEY_WIDTH), row(G_WIDTH), row(G_WIDTH),
                  const(w.shape, lambda b, c: (0, 0)),
                  const(lmap.shape, lambda b, c: (0, 0)),
                  const((1, G_HEAD_V), lambda b, c: (0, 0))],
        out_specs=row(G_WIDTH),
        out_shape=jax.ShapeDtypeStruct((n, G_WIDTH), BF16),
        scratch_shapes=[pltpu.VMEM((G_HEADS // 2, G_HEAD_V, LANES), F32)],
        compiler_params=pltpu.CompilerParams(dimension_semantics=("parallel", "arbitrary"),
                                             vmem_limit_bytes=VMEM_LIMIT),
        name="gla",
    )(qg, kg, la, vg, rg, w, lmap, gn)


def _memkv_kernel(m_ref, g_ref, w_ref, k_ref, v_ref):
    nb, mlen, _ = m_ref.shape
    h = _rms(m_ref[...].reshape(nb * mlen, D_MODEL), g_ref[...]).astype(BF16)
    k_ref[...] = _dot(h, w_ref[:, 0:D_MODEL]).astype(BF16).reshape(nb, mlen, D_MODEL)
    v_ref[...] = _dot(h, w_ref[:, D_MODEL:2 * D_MODEL]).astype(BF16).reshape(nb, mlen, D_MODEL)


def _memkv(mem, g, w):
    batch, mlen, _ = mem.shape
    nb = 2 if batch % 2 == 0 else 1
    const = functools.partial(pl.BlockSpec, pipeline_mode=pl.Buffered(1))
    blk = pl.BlockSpec((nb, mlen, D_MODEL), lambda b: (b, 0, 0))
    return pl.pallas_call(
        _memkv_kernel,
        grid=(batch // nb,),
        in_specs=[blk, const((1, D_MODEL), lambda b: (0, 0)), const((D_MODEL, 2 * D_MODEL), lambda b: (0, 0))],
        out_specs=[blk, blk],
        out_shape=[jax.ShapeDtypeStruct(mem.shape, BF16)] * 2,
        compiler_params=pltpu.CompilerParams(dimension_semantics=("parallel",), vmem_limit_bytes=VMEM_LIMIT),
        name="memkv",
    )(mem, g, w)


def _post_kernel(x_ref, oa_ref, og_ref, wo_ref, gx_ref, wq_ref, kx_ref, vx_ref, wxo_ref, o_ref):
    tm = x_ref.shape[0]
    parts = [slice(i * tm // POST_PARTS, (i + 1) * tm // POST_PARTS) for i in range(POST_PARTS)]
    heads = [slice(X_HEAD_DIM * hd, X_HEAD_DIM * (hd + 1)) for hd in range(X_HEADS)]

    def out_proj(r):
        return x_ref[r, :] + _dot(oa_ref[r, :], wo_ref[0:A_WIDTH, :]) + _dot(og_ref[r, :], wo_ref[A_WIDTH:D_MODEL, :])

    def scores(x1):
        h = _rms(x1, gx_ref[...]).astype(BF16)
        qx = (_dot(h, wq_ref[...]) * (LOG2E * X_HEAD_DIM ** -0.5)).astype(BF16)
        return [_nt(qx[:, sl], kx_ref[0, :, sl]) for sl in heads]

    def attend(r, x1, s_heads):
        outs = []
        for sl, s in zip(heads, s_heads):
            pexp = jnp.exp2(s - jnp.max(s, axis=1, keepdims=True))
            pn = (pexp / jnp.sum(pexp, axis=1, keepdims=True)).astype(BF16)
            outs.append(_dot(pn, vx_ref[0, :, sl]).astype(BF16))
        o_ref[r, :] = x1 + _dot(jnp.concatenate(outs, axis=1), wxo_ref[...])

    x1s = [out_proj(r) for r in parts]
    s_parts = [scores(x1) for x1 in x1s]
    for r, x1, s_heads in zip(parts, x1s, s_parts):
        attend(r, x1, s_heads)


def _post(x2, oa, og, wo, gx, wq, kx, vx, wxo, seq):
    n = x2.shape[0]
    tm = POST_TILE
    per_b = seq // tm
    mlen = kx.shape[1]
    const = functools.partial(pl.BlockSpec, pipeline_mode=pl.Buffered(1))
    row = lambda width: pl.BlockSpec((tm, width), lambda i: (i, 0))
    mem = pl.BlockSpec((1, mlen, D_MODEL), lambda i: (i // per_b, 0, 0))
    sq = const((D_MODEL, D_MODEL), lambda i: (0, 0))
    return pl.pallas_call(
        _post_kernel,
        grid=(n // tm,),
        in_specs=[row(D_MODEL), row(A_WIDTH), row(G_WIDTH), sq, const((1, D_MODEL), lambda i: (0, 0)), sq, mem, mem, sq],
        out_specs=row(D_MODEL),
        out_shape=jax.ShapeDtypeStruct((n, D_MODEL), F32),
        compiler_params=pltpu.CompilerParams(dimension_semantics=("parallel",), vmem_limit_bytes=VMEM_LIMIT),
        name="post",
    )(x2, oa, og, wo, gx, wq, kx, vx, wxo)


def _mlp_kernel(x_ref, g_ref, wu_ref, wd_ref, gf_ref, o_ref, *, final_norm):
    tm = x_ref.shape[0]
    parts = [slice(i * tm // MLP_PARTS, (i + 1) * tm // MLP_PARTS) for i in range(MLP_PARTS)]
    acc = [x_ref[r, :] for r in parts]
    h = [_rms(x, g_ref[...]).astype(BF16) for x in acc]
    for c in range(D_FF // D_MODEL):
        sl = slice(D_MODEL * c, D_MODEL * (c + 1))
        u = [jnp.square(jnp.maximum(_dot(v, wu_ref[:, sl]), 0.0)).astype(BF16) for v in h]
        acc = [a + _dot(v, wd_ref[sl, :]) for a, v in zip(acc, u)]
    for r, a in zip(parts, acc):
        o_ref[r, :] = _rms(a, gf_ref[...]) if final_norm else a


def _mlp(x2, g, wu, wd, gf, final_norm):
    n = x2.shape[0]
    tm = MLP_TILE
    const = functools.partial(pl.BlockSpec, pipeline_mode=pl.Buffered(1))
    row = pl.BlockSpec((tm, D_MODEL), lambda i: (i, 0))
    vec = const((1, D_MODEL), lambda i: (0, 0))
    return pl.pallas_call(
        functools.partial(_mlp_kernel, final_norm=final_norm),
        grid=(n // tm,),
        in_specs=[row, vec, const((D_MODEL, D_FF), lambda i: (0, 0)), const((D_FF, D_MODEL), lambda i: (0, 0)), vec],
        out_specs=row,
        out_shape=jax.ShapeDtypeStruct((n, D_MODEL), F32),
        compiler_params=pltpu.CompilerParams(dimension_semantics=("parallel",), vmem_limit_bytes=VMEM_LIMIT),
        name="mlp",
    )(x2, g, wu, wd, gf)


def kernel(x, mem, rp_table, norm_mix, w_in, w_gate_up, b_gate, g_norm, w_out, norm_xattn, norm_mem, w_xq, w_xkv,
           w_xo, norm_mlp, w_up, w_down, norm_final):
    batch, seq, _ = x.shape
    depth = w_in.shape[0]
    assert seq % max(INPROJ_TILE, POST_TILE, MLP_TILE, GLA_STEP) == 0 and seq // MOBA_BLOCK <= 8
    x2 = x.reshape(batch * seq, D_MODEL)
    bias = _bias_tiles(rp_table)
    c31 = rp_table[RP_BUCKETS - 1]
    glr_lo = 3 * A_WIDTH + 2 * G_KEY_WIDTH + G_WIDTH
    rg_lo = glr_lo + G_GATE_RANK
    pad_rank = LANES - G_GATE_RANK
    for l in range(depth):
        col = lambda lo, hi: w_in[l, :, lo:hi].astype(BF16)
        wglr = jnp.pad(col(glr_lo, rg_lo), ((0, 0), (0, pad_rank)))
        wgu = jnp.pad(w_gate_up[l].astype(BF16), ((0, pad_rank), (0, 0)))
        later = (w_out[l], w_xq[l], w_xkv[l], w_xo[l], w_up[l], w_down[l])
        qk, vt, qg, kg, la, vg, rg, wo, wxq, wxkv, wxo, wup, wdown = _inproj(
            x2, norm_mix[l][None], w_in[l].astype(BF16), col(rg_lo, rg_lo + G_WIDTH), wglr, wgu, b_gate[l][None], later)
        oa = _moba(qk, vt, bias, c31, batch, seq)
        og = _gla(qg, kg, la, vg, rg, g_norm[l][None], batch, seq)
        kx, vx = _memkv(mem, norm_mem[l][None], wxkv)
        x2 = _post(x2, oa, og, wo, norm_xattn[l][None], wxq, kx, vx, wxo, seq)
        last = l == depth - 1
        x2 = _mlp(x2, norm_mlp[l][None], wup, wdown, norm_final[None], last)
    return x2.reshape(batch, seq, D_MODEL)
```

```python
import functools
import math

import numpy as np
import jax
import jax.numpy as jnp
from jax import lax
from jax.experimental import pallas as pl
from jax.experimental.pallas import tpu as pltpu

F32 = jnp.float32
BF16 = jnp.bfloat16

D_MODEL = 1024
A_HEADS = 8
A_HEAD_DIM = 64
A_WIDTH = A_HEADS * A_HEAD_DIM
MOBA_BLOCK = 256
MOBA_TOPK = 3
G_HEADS = 4
G_WIDTH = D_MODEL - A_WIDTH
G_HEAD_V = G_WIDTH // G_HEADS
G_KEY_WIDTH = G_WIDTH // 2
G_HEAD_K = G_KEY_WIDTH // G_HEADS
G_GATE_RANK = 16
G_GATE_NORM = 16.0
X_HEADS = 4
X_HEAD_DIM = D_MODEL // X_HEADS
D_FF = 4 * D_MODEL
RP_BUCKETS = 32
RP_MAX_DIST = 128
EPS = 1e-6

LANES = 128
SUBLANES = 8
GATE_ROWS = 16
NEG = -1e30
MOBA_PAIRS = 2
MOBA_GROUP = 2
GLA_CHUNK = 128
GLA_STEP = 2048
LOG2E = float(np.log2(np.e))
INPROJ_TILE = 1024
INPROJ_PARTS = 2
MLP_TILE = 1024
MLP_PARTS = 2
POST_TILE = 1024
POST_PARTS = 2
VMEM_LIMIT = 56 * 1024 * 1024

def _nt(a, b):
    return lax.dot_general(a, b, (((1,), (1,)), ((), ())), preferred_element_type=F32)


def _tn(a, b):
    return lax.dot_general(a, b, (((0,), (0,)), ((), ())), preferred_element_type=F32)


def _dot(a, b):
    return jnp.dot(a, b, preferred_element_type=F32)


def _rms(x, g):
    return x * lax.rsqrt(jnp.mean(x * x, axis=-1, keepdims=True) + EPS) * g


def _bucket_thresholds():
    max_exact = RP_BUCKETS // 2
    d = np.arange(1, 4 * RP_MAX_DIST)
    val = (np.log(d.astype(np.float32) / np.float32(max_exact)) / np.float32(math.log(RP_MAX_DIST / max_exact))
           * np.float32(RP_BUCKETS - max_exact))
    bucket = np.minimum(max_exact + val.astype(np.int32), RP_BUCKETS - 1)
    return [int(d[(d >= max_exact) & (bucket >= max_exact + k)][0]) for k in range(1, RP_BUCKETS - max_exact)]


_THRESHOLDS = _bucket_thresholds()


def _bias_kernel(tab_ref, out_ref):
    h = pl.program_id(0)
    rows, width = 2 * MOBA_BLOCK, 3 * MOBA_BLOCK
    d = lax.broadcasted_iota(jnp.int32, (SUBLANES, width), 1)
    max_exact = RP_BUCKETS // 2
    bucket = jnp.where(d < max_exact, d, max_exact)
    for t in _THRESHOLDS:
        bucket = bucket + jnp.where(d >= t, 1, 0)
    prof = jnp.full((SUBLANES, width), NEG, F32)
    for b in range(RP_BUCKETS):
        prof = jnp.where(bucket == b, tab_ref[b, h] * LOG2E, prof)
    prof = jnp.where(d < rows, prof, NEG)
    tile = pltpu.roll(jnp.broadcast_to(prof[0:1], (rows, width)), 0, 1, stride=1, stride_axis=0)
    out_ref[0] = tile[:, MOBA_BLOCK:2 * MOBA_BLOCK]


def _bias_tiles(rp_table):
    return pl.pallas_call(
        _bias_kernel,
        grid=(A_HEADS,),
        in_specs=[pl.BlockSpec(memory_space=pltpu.SMEM)],
        out_specs=pl.BlockSpec((1, 2 * MOBA_BLOCK, MOBA_BLOCK), lambda h: (h, 0, 0)),
        out_shape=jax.ShapeDtypeStruct((A_HEADS, 2 * MOBA_BLOCK, MOBA_BLOCK), F32),
        name="bias",
    )(rp_table)


def _inproj_kernel(x_ref, g_ref, w_ref, wrg_ref, wglr_ref, wgu_ref, bg_ref, *rest):
    ncast = (len(rest) - 7) // 2
    cast_in, cast_out = rest[:ncast], rest[ncast + 7:]
    qk_ref, vt_ref, qg_ref, kg_ref, la_ref, vg_ref, rg_ref = rest[ncast:ncast + 7]
    tm = x_ref.shape[0]
    parts = [slice(i * tm // INPROJ_PARTS, (i + 1) * tm // INPROJ_PARTS) for i in range(INPROJ_PARTS)]
    hs = [_rms(x_ref[r, :], g_ref[...]).astype(BF16) for r in parts]
    va_lo = 2 * A_WIDTH
    qg_lo = va_lo + A_WIDTH
    kg_lo = qg_lo + G_KEY_WIDTH
    vg_lo = kg_lo + G_KEY_WIDTH

    glrs = [_dot(h, wglr_ref[...]).astype(BF16) for h in hs]
    for r, h in zip(parts, hs):
        qk_ref[r, 0:A_WIDTH] = (_dot(h, w_ref[:, 0:A_WIDTH]) * (LOG2E * A_HEAD_DIM ** -0.5)).astype(BF16)
    for r, h in zip(parts, hs):
        qk_ref[r, A_WIDTH:va_lo] = _dot(h, w_ref[:, A_WIDTH:va_lo]).astype(BF16)
    for r, glr in zip(parts, glrs):
        z = _dot(glr, wgu_ref[...]) + bg_ref[...]
        log_sig = jnp.minimum(z, 0.0) - jnp.log(1.0 + jnp.exp(-jnp.abs(z)))
        la_ref[r, :] = log_sig * (1.0 / G_GATE_NORM)
    for r, h in zip(parts, hs):
        vt_ref[:, r] = _dot(h, w_ref[:, va_lo:qg_lo]).astype(BF16).T
    for r, h in zip(parts, hs):
        qg_ref[r, :] = _dot(h, w_ref[:, qg_lo:kg_lo]) * (G_HEAD_K ** -0.5)
        kg_ref[r, :] = _dot(h, w_ref[:, kg_lo:vg_lo])
    for r, h in zip(parts, hs):
        vg_ref[r, :] = _dot(h, w_ref[:, vg_lo:vg_lo + G_WIDTH]).astype(BF16)
    for r, h in zip(parts, hs):
        rg_ref[r, :] = _dot(h, wrg_ref[...])
    for src, dst in zip(cast_in, cast_out):
        dst[...] = src[...].astype(BF16)


def _inproj(x2, g, w, wrg, wglr, wgu, bg, later):
    n = x2.shape[0]
    tm = INPROJ_TILE
    steps = n // tm
    const = lambda a: pl.BlockSpec(a.shape, lambda i: (0, 0), pipeline_mode=pl.Buffered(1))
    row = lambda width: pl.BlockSpec((tm, width), lambda i: (i, 0))
    assert all(a.shape[0] % (steps * GATE_ROWS) == 0 for a in later)
    part = [pl.BlockSpec((a.shape[0] // steps, a.shape[1]), lambda i: (i, 0)) for a in later]
    return pl.pallas_call(
        _inproj_kernel,
        grid=(steps,),
        in_specs=[row(D_MODEL)] + [const(a) for a in (g, w, wrg, wglr, wgu, bg)] + part,
        out_specs=[row(2 * A_WIDTH), pl.BlockSpec((A_WIDTH, tm), lambda i: (0, i)),
                   row(G_KEY_WIDTH), row(G_KEY_WIDTH), row(G_KEY_WIDTH), row(G_WIDTH), row(G_WIDTH)] + part,
        out_shape=[jax.ShapeDtypeStruct((n, 2 * A_WIDTH), BF16),
                   jax.ShapeDtypeStruct((A_WIDTH, n), BF16),
                   jax.ShapeDtypeStruct((n, G_KEY_WIDTH), F32),
                   jax.ShapeDtypeStruct((n, G_KEY_WIDTH), F32),
                   jax.ShapeDtypeStruct((n, G_KEY_WIDTH), F32),
                   jax.ShapeDtypeStruct((n, G_WIDTH), BF16),
                   jax.ShapeDtypeStruct((n, G_WIDTH), F32)] + [jax.ShapeDtypeStruct(a.shape, BF16) for a in later],
        compiler_params=pltpu.CompilerParams(dimension_semantics=("parallel",), vmem_limit_bytes=VMEM_LIMIT),
        name="inproj",
    )(x2, g, w, wrg, wglr, wgu, bg, *later)


def _moba_kernel(c31_ref, q_ref, k_ref, vt_ref, bias_ref, o_ref, khead, s_scr, *, seq):
    step = pl.program_id(1)
    nblk = seq // MOBA_BLOCK
    half = LANES // 2
    lane2 = lax.broadcasted_iota(jnp.int32, (seq, LANES), 1)
    lane_q = lax.broadcasted_iota(jnp.int32, (MOBA_BLOCK, LANES), 1)
    rowi = lax.broadcasted_iota(jnp.int32, (GATE_ROWS, MOBA_BLOCK), 0)
    blk = lambda n: slice(n * MOBA_BLOCK, (n + 1) * MOBA_BLOCK)
    pair_lanes = lambda pp: slice(pp * LANES, (pp + 1) * LANES)

    def build(pp):
        k = k_ref[:, pair_lanes(pp)]
        khead[2 * pp] = k * jnp.where(lane2 < half, 1.0, 0.0).astype(BF16)
        khead[2 * pp + 1] = k * jnp.where(lane2 >= half, 1.0, 0.0).astype(BF16)
        n_i = lax.broadcasted_iota(jnp.int32, (GATE_ROWS, seq), 0)
        t_i = lax.broadcasted_iota(jnp.int32, (GATE_ROWS, seq), 1)
        avg = jnp.where(t_i // MOBA_BLOCK == n_i, 1.0 / MOBA_BLOCK, 0.0).astype(BF16)
        km = _dot(avg, k)
        kmh = km.astype(BF16)
        return kmh, (km - kmh.astype(F32)).astype(BF16)

    def prepare(pp, kmeans, qi, h, slot):
        qm = q_ref[blk(qi), pair_lanes(pp)] * jnp.where((lane_q < half) if h == 0 else (lane_q >= half), 1.0,
                                                        0.0).astype(BF16)
        head = 2 * pp + h
        return dict(pp=pp, qi=qi, h=h, head=head, qm=qm, kmeans=kmeans if qi > MOBA_TOPK else None,
                    sbuf=s_scr.at[slot], near=None, far=None, acc=None,
                    c31=c31_ref[2 * MOBA_PAIRS * step + head] * LOG2E)

    def penalty(u, n):
        return 0.0 if u["pen"] is None else u["pen"][n:n + 1, :]

    def score_matmul(u):
        width = (u["qi"] + 1) * MOBA_BLOCK
        keys = khead[u["head"], 0:width, :]
        if u["kmeans"] is None:
            u["s_all"], u["gate"] = _nt(keys, u["qm"]), None
        else:
            s = _nt(jnp.concatenate([keys, u["kmeans"][0], u["kmeans"][1]], axis=0), u["qm"])
            u["s_all"] = s[0:width]
            u["gate"] = s[width:width + GATE_ROWS] + s[width + GATE_ROWS:width + 2 * GATE_ROWS]

    def score_reduce(u):
        qi, head = u["qi"], u["head"]
        u["pen"] = None
        if u["gate"] is not None:
            g = u["gate"]
            cnt = jnp.zeros((GATE_ROWS, MOBA_BLOCK), F32)
            for m in range(qi):
                gm = g[m:m + 1, :]
                cnt = cnt + jnp.where((gm > g) | ((gm == g) & (m < rowi)), 1.0, 0.0)
            u["pen"] = jnp.where((rowi < qi) & (cnt < MOBA_TOPK), 0.0, NEG)
        for n in range(qi + 1):
            s = u["s_all"][blk(n)]
            if n == qi:
                s = s + bias_ref[head, MOBA_BLOCK:2 * MOBA_BLOCK, :]
            elif n == qi - 1:
                s = s + bias_ref[head, 0:MOBA_BLOCK, :]
            u["sbuf"][blk(n), :] = s
            t = jnp.max(s.reshape(MOBA_BLOCK // SUBLANES, SUBLANES, MOBA_BLOCK), axis=0)
            if n < qi:
                t = t + penalty(u, n)
            key = "far" if n < qi - 1 else "near"
            u[key] = t if u[key] is None else jnp.maximum(u[key], t)
        mx = u["near"] if u["far"] is None else jnp.maximum(u["near"], u["far"] + u["c31"])
        u["m_near"] = jnp.max(mx, axis=0, keepdims=True)
        u["m_far"] = u["m_near"] - u["c31"]

    def prob_exp(u):
        qi = u["qi"]
        shift = lambda n: (u["m_far"] if n < qi - 1 else u["m_near"]) - (penalty(u, n) if n < qi else 0.0)
        u["pt"] = jnp.concatenate([jnp.exp2(u["sbuf"][blk(n), :] - shift(n)).astype(BF16) for n in range(qi + 1)],
                                  axis=0)

    def prob_matmul(u):
        qi, h = u["qi"], u["h"]
        width = (qi + 1) * MOBA_BLOCK
        r0 = u["head"] * half
        vrows = vt_ref[r0:r0 + half, 0:width]
        ones_rows = jnp.ones((half, width), BF16)
        vt = jnp.concatenate([vrows, ones_rows] if h == 0 else [ones_rows, vrows], axis=0)
        u["acc"] = _dot(vt, u["pt"])

    outs = {}

    def finish(u):
        acc, h = u["acc"], u["h"]
        num = acc[half * h:half * (h + 1)]
        den = acc[half * (1 - h):half * (1 - h) + 1]
        outs[h] = num / den
        if h == 1:
            o_ref[blk(u["qi"]), pair_lanes(u["pp"])] = jnp.concatenate([outs[0], outs[1]], axis=0).T.astype(BF16)

    group = MOBA_GROUP
    stages = [(pp, [(qi, h) for qi in range(g * group, (g + 1) * group) for h in range(2)])
              for pp in range(MOBA_PAIRS) for g in range(nblk // group)]
    prev, kmeans = [], {}
    for g, (pp, members) in enumerate(stages + [(None, [])]):
        if members and pp not in kmeans:
            kmeans[pp] = build(pp)
        cur = [prepare(pp, kmeans[pp], qi, h, (g % 2) * 2 * group + 2 * (qi % group) + h) for qi, h in members]
        for u in prev:
            prob_exp(u)
        for u in cur:
            score_matmul(u)
        for u in prev:
            prob_matmul(u)
        for u in cur:
            score_reduce(u)
        for u in prev:
            finish(u)
        prev = cur


def _moba(qk, vt, bias, c31, batch, seq):
    n = qk.shape[0]
    width = MOBA_PAIRS * LANES
    nstep = A_WIDTH // width
    kern = functools.partial(_moba_kernel, seq=seq)
    return pl.pallas_call(
        kern,
        grid=(batch, nstep),
        in_specs=[pl.BlockSpec(memory_space=pltpu.SMEM),
                  pl.BlockSpec((seq, width), lambda b, p: (b, p)),
                  pl.BlockSpec((seq, width), lambda b, p: (b, nstep + p)),
                  pl.BlockSpec((width, seq), lambda b, p: (p, b)),
                  pl.BlockSpec((2 * MOBA_PAIRS, 2 * MOBA_BLOCK, MOBA_BLOCK), lambda b, p: (p, 0, 0))],
        out_specs=pl.BlockSpec((seq, width), lambda b, p: (b, p)),
        out_shape=jax.ShapeDtypeStruct((n, A_WIDTH), BF16),
        scratch_shapes=[pltpu.VMEM((2 * MOBA_PAIRS, seq, LANES), BF16),
                        pltpu.VMEM((2 * 2 * MOBA_GROUP, seq, MOBA_BLOCK), F32)],
        compiler_params=pltpu.CompilerParams(dimension_semantics=("parallel", "arbitrary"),
                                             vmem_limit_bytes=VMEM_LIMIT),
        name="moba",
    )(c31, qk, qk, vt, bias)


_GLA_LEVELS = int(math.log2(GLA_CHUNK))


_GLA_BIG_LEVELS = _GLA_LEVELS - int(math.log2(SUBLANES))


def _gla_weights():
    c = GLA_CHUNK
    i = np.arange(c)[:, None]
    j = np.arange(c)[None, :]
    mats = [(j <= i)]
    for lvl in range(_GLA_BIG_LEVELS, _GLA_LEVELS):
        s = (c // 2) >> lvl
        ref = (i // (2 * s)) * (2 * s) + s - 1
        mats.append(np.where(i > ref, (j > ref) & (j <= i), (j > i) & (j <= ref)))
    return np.concatenate(mats, axis=0).astype(np.float32)


def _gla_level_map():
    c = GLA_CHUNK
    i = np.arange(c)[:, None]
    j = np.arange(c)[None, :]
    top_bit = np.floor(np.log2(np.maximum(i ^ j, 1))).astype(np.int64)
    lvl = np.where(j < i, _GLA_LEVELS - 1 - top_bit, np.where(j == i, _GLA_LEVELS, _GLA_LEVELS + 1))
    return np.concatenate([lvl, lvl], axis=1).astype(np.int32)


def _gla_kernel(q_ref, k_ref, la_ref, v_ref, rg_ref, w_ref, lmap_ref, gn_ref, o_ref, st_ref):
    c = GLA_CHUNK
    half = LANES // 2
    npair = G_HEADS // 2

    @pl.when(pl.program_id(1) == 0)
    def _init():
        st_ref[...] = jnp.zeros(st_ref.shape, F32)

    w = w_ref[...]
    lmap = lmap_ref[...]
    lane = lax.broadcasted_iota(jnp.int32, (1, LANES), 1)
    lane_c = lax.broadcasted_iota(jnp.int32, (c, LANES), 1)
    lm = [jnp.where(lane_c < half, 1.0, 0.0).astype(BF16), jnp.where(lane_c >= half, 1.0, 0.0).astype(BF16)]
    row_small = lax.broadcasted_iota(jnp.int32, (c, G_KEY_WIDTH), 0)

    def level_factors(q, k, b, d_all, lvl):
        s = (c // 2) >> lvl
        if s < SUBLANES:
            j = lvl - _GLA_BIG_LEVELS
            e = jnp.exp2(d_all[c * (j + 1):c * (j + 2)])
            odd = (row_small // s) % 2 == 1
            return jnp.where(odd, q * e, 0.0), jnp.where(odd, 0.0, k * e)
        zeros = jnp.zeros((s, q.shape[1]), F32)
        qp, kp = [], []
        for blk in range(c // s):
            rows = slice(blk * s, (blk + 1) * s)
            ref = (blk // 2) * 2 * s + s - 1
            if blk % 2 == 1:
                qp.append(q[rows] * jnp.exp2(b[rows] - b[ref:ref + 1]))
                kp.append(zeros)
            else:
                qp.append(zeros)
                kp.append(k[rows] * jnp.exp2(b[ref:ref + 1] - b[rows]))
        return jnp.concatenate(qp, axis=0), jnp.concatenate(kp, axis=0)

    nch = GLA_STEP // c
    rows_of = [slice(ch * c, (ch + 1) * c) for ch in range(nch)]
    pair_lanes = [slice(LANES * p, LANES * (p + 1)) for p in range(npair)]

    def decays(ch):
        la = la_ref[rows_of[ch], :] * LOG2E
        hi = la.astype(BF16)
        lo = (la - hi.astype(F32)).astype(BF16)
        return _dot(w, hi) + _dot(w, lo)

    def factors(ch, d_all):
        q = q_ref[rows_of[ch], :]
        k = k_ref[rows_of[ch], :]
        b = d_all[0:c]
        blast = b[c - 1:c, :]
        qts, kts = [], []
        for lvl in range(_GLA_LEVELS):
            qt, kt = level_factors(q, k, b, d_all, lvl)
            qts.append(qt.astype(BF16))
            kts.append(kt.astype(BF16))
        qts.append(q.astype(BF16))
        kts.append(k.astype(BF16))
        return dict(ch=ch, qts=qts, kts=kts, qe=(q * jnp.exp2(b)).astype(BF16),
                    khat=(k * jnp.exp2(blast - b)).astype(BF16), dec=jnp.exp2(blast),
                    a=[jnp.zeros((c, 2 * c), F32)] * npair)

    def level_scores(units):
        for lvl in range(_GLA_LEVELS + 1):
            mask = lmap == lvl
            s = (c // 2) >> lvl if lvl < _GLA_BIG_LEVELS else c
            blocks = [slice(i * s, (i + 1) * s) for i in range(c // s)]
            live = [i for i in range(len(blocks)) if i % 2 == 1 or s == c]
            for u in units:
                for p, sl in enumerate(pair_lanes):
                    kp = u["kts"][lvl][:, sl]
                    ql = u["qts"][lvl]
                    pm = _nt(jnp.concatenate([ql[blocks[i], sl] for i in live], axis=0),
                             jnp.concatenate([kp * lm[0], kp * lm[1]], axis=0))
                    a = u["a"][p]
                    pieces = [a[r] for r in blocks]
                    for n, i in enumerate(live):
                        pieces[i] = jnp.where(mask[blocks[i]], pm[n * s:(n + 1) * s], pieces[i])
                    u["a"] = [jnp.concatenate(pieces, axis=0) if j == p else v for j, v in enumerate(u["a"])]

    def stateless_matmuls(units):
        for u in units:
            vhs = [v_ref[rows_of[u["ch"]], LANES * h:LANES * (h + 1)] for h in range(G_HEADS)]
            u["intra"] = [_dot(u["a"][h // 2][:, (h % 2) * c:(h % 2 + 1) * c].astype(BF16), vhs[h])
                          for h in range(G_HEADS)]
            u["inc"] = [_tn(vhs[h], u["khat"][:, pair_lanes[h // 2]]) for h in range(G_HEADS)]

    def state_chain(units, states):
        for u in units:
            rows = rows_of[u["ch"]]
            for p, sl in enumerate(pair_lanes):
                stb = states[p].astype(BF16)
                for hh in range(2):
                    h = 2 * p + hh
                    o = u["intra"][h] + _nt(u["qe"][:, sl] * lm[hh], stb)
                    y = _rms(o, gn_ref[...])
                    rg = rg_ref[rows, LANES * h:LANES * (h + 1)]
                    y = y * (rg / (1.0 + jnp.exp(-rg)))
                    o_ref[rows, LANES * h:LANES * (h + 1)] = y.astype(BF16)
                states[p] = states[p] * u["dec"][:, sl] + jnp.where(lane < half, u["inc"][2 * p], u["inc"][2 * p + 1])

    d_alls = [decays(ch) for ch in range(nch)]
    chunks = [factors(ch, d_alls[ch]) for ch in range(nch)]
    level_scores(chunks)
    stateless_matmuls(chunks)
    states = [st_ref[p] for p in range(npair)]
    state_chain(chunks, states)
    for p in range(npair):
        st_ref[p] = states[p]


def _gla(qg, kg, la, vg, rg, gn, batch, seq):
    n = qg.shape[0]
    nchunk = seq // GLA_STEP
    w = jnp.asarray(_gla_weights(), BF16)
    lmap = jnp.asarray(_gla_level_map())
    row = lambda width: pl.BlockSpec((GLA_STEP, width), lambda b, c: (b * nchunk + c, 0))
    const = functools.partial(pl.BlockSpec, pipeline_mode=pl.Buffered(1))
    return pl.pallas_call(
        _gla_kernel,
        grid=(batch, nchunk),
        in_specs=[row(G_KEY_WIDTH), row(G_KEY_WIDTH), row(G_KEY_WIDTH), row(G_WIDTH), row(G_WIDTH),
                  const(w.shape, lambda b, c: (0, 0)),
                  const(lmap.shape, lambda b, c: (0, 0)),
                  const((1, G_HEAD_V), lambda b, c: (0, 0))],
        out_specs=row(G_WIDTH),
        out_shape=jax.ShapeDtypeStruct((n, G_WIDTH), BF16),
        scratch_shapes=[pltpu.VMEM((G_HEADS // 2, G_HEAD_V, LANES), F32)],
        compiler_params=pltpu.CompilerParams(dimension_semantics=("parallel", "arbitrary"),
                                             vmem_limit_bytes=VMEM_LIMIT),
        name="gla",
    )(qg, kg, la, vg, rg, w, lmap, gn)


def _memkv_kernel(m_ref, g_ref, w_ref, k_ref, v_ref):
    nb, mlen, _ = m_ref.shape
    h = _rms(m_ref[...].reshape(nb * mlen, D_MODEL), g_ref[...]).astype(BF16)
    k_ref[...] = _dot(h, w_ref[:, 0:D_MODEL]).astype(BF16).reshape(nb, mlen, D_MODEL)
    v_ref[...] = _dot(h, w_ref[:, D_MODEL:2 * D_MODEL]).astype(BF16).reshape(nb, mlen, D_MODEL)


def _memkv(mem, g, w):
    batch, mlen, _ = mem.shape
    nb = 2 if batch % 2 == 0 else 1
    const = functools.partial(pl.BlockSpec, pipeline_mode=pl.Buffered(1))
    blk = pl.BlockSpec((nb, mlen, D_MODEL), lambda b: (b, 0, 0))
    return pl.pallas_call(
        _memkv_kernel,
        grid=(batch // nb,),
        in_specs=[blk, const((1, D_MODEL), lambda b: (0, 0)), const((D_MODEL, 2 * D_MODEL), lambda b: (0, 0))],
        out_specs=[blk, blk],
        out_shape=[jax.ShapeDtypeStruct(mem.shape, BF16)] * 2,
        compiler_params=pltpu.CompilerParams(dimension_semantics=("parallel",), vmem_limit_bytes=VMEM_LIMIT),
        name="memkv",
    )(mem, g, w)


def _post_kernel(x_ref, oa_ref, og_ref, wo_ref, gx_ref, wq_ref, kx_ref, vx_ref, wxo_ref, o_ref):
    tm = x_ref.shape[0]
    parts = [slice(i * tm // POST_PARTS, (i + 1) * tm // POST_PARTS) for i in range(POST_PARTS)]
    heads = [slice(X_HEAD_DIM * hd, X_HEAD_DIM * (hd + 1)) for hd in range(X_HEADS)]

    def out_proj(r):
        return x_ref[r, :] + _dot(oa_ref[r, :], wo_ref[0:A_WIDTH, :]) + _dot(og_ref[r, :], wo_ref[A_WIDTH:D_MODEL, :])

    def scores(x1):
        h = _rms(x1, gx_ref[...]).astype(BF16)
        qx = (_dot(h, wq_ref[...]) * (LOG2E * X_HEAD_DIM ** -0.5)).astype(BF16)
        return [_nt(qx[:, sl], kx_ref[0, :, sl]) for sl in heads]

    def attend(r, x1, s_heads):
        outs = []
        for sl, s in zip(heads, s_heads):
            pexp = jnp.exp2(s - jnp.max(s, axis=1, keepdims=True))
            pn = (pexp / jnp.sum(pexp, axis=1, keepdims=True)).astype(BF16)
            outs.append(_dot(pn, vx_ref[0, :, sl]).astype(BF16))
        o_ref[r, :] = x1 + _dot(jnp.concatenate(outs, axis=1), wxo_ref[...])

    x1s = [out_proj(r) for r in parts]
    s_parts = [scores(x1) for x1 in x1s]
    for r, x1, s_heads in zip(parts, x1s, s_parts):
        attend(r, x1, s_heads)


def _post(x2, oa, og, wo, gx, wq, kx, vx, wxo, seq):
    n = x2.shape[0]
    tm = POST_TILE
    per_b = seq // tm
    mlen = kx.shape[1]
    const = functools.partial(pl.BlockSpec, pipeline_mode=pl.Buffered(1))
    row = lambda width: pl.BlockSpec((tm, width), lambda i: (i, 0))
    mem = pl.BlockSpec((1, mlen, D_MODEL), lambda i: (i // per_b, 0, 0))
    sq = const((D_MODEL, D_MODEL), lambda i: (0, 0))
    return pl.pallas_call(
        _post_kernel,
        grid=(n // tm,),
        in_specs=[row(D_MODEL), row(A_WIDTH), row(G_WIDTH), sq, const((1, D_MODEL), lambda i: (0, 0)), sq, mem, mem, sq],
        out_specs=row(D_MODEL),
        out_shape=jax.ShapeDtypeStruct((n, D_MODEL), F32),
        compiler_params=pltpu.CompilerParams(dimension_semantics=("parallel",), vmem_limit_bytes=VMEM_LIMIT),
        name="post",
    )(x2, oa, og, wo, gx, wq, kx, vx, wxo)


def _mlp_kernel(x_ref, g_ref, wu_ref, wd_ref, gf_ref, o_ref, *, final_norm):
    tm = x_ref.shape[0]
    parts = [slice(i * tm // MLP_PARTS, (i + 1) * tm // MLP_PARTS) for i in range(MLP_PARTS)]
    acc = [x_ref[r, :] for r in parts]
    h = [_rms(x, g_ref[...]).astype(BF16) for x in acc]
    for c in range(D_FF // D_MODEL):
        sl = slice(D_MODEL * c, D_MODEL * (c + 1))
        u = [jnp.square(jnp.maximum(_dot(v, wu_ref[:, sl]), 0.0)).astype(BF16) for v in h]
        acc = [a + _dot(v, wd_ref[sl, :]) for a, v in zip(acc, u)]
    for r, a in zip(parts, acc):
        o_ref[r, :] = _rms(a, gf_ref[...]) if final_norm else a


def _mlp(x2, g, wu, wd, gf, final_norm):
    n = x2.shape[0]
    tm = MLP_TILE
    const = functools.partial(pl.BlockSpec, pipeline_mode=pl.Buffered(1))
    row = pl.BlockSpec((tm, D_MODEL), lambda i: (i, 0))
    vec = const((1, D_MODEL), lambda i: (0, 0))
    return pl.pallas_call(
        functools.partial(_mlp_kernel, final_norm=final_norm),
        grid=(n // tm,),
        in_specs=[row, vec, const((D_MODEL, D_FF), lambda i: (0, 0)), const((D_FF, D_MODEL), lambda i: (0, 0)), vec],
        out_specs=row,
        out_shape=jax.ShapeDtypeStruct((n, D_MODEL), F32),
        compiler_params=pltpu.CompilerParams(dimension_semantics=("parallel",), vmem_limit_bytes=VMEM_LIMIT),
        name="mlp",
    )(x2, g, wu, wd, gf)


def kernel(x, mem, rp_table, norm_mix, w_in, w_gate_up, b_gate, g_norm, w_out, norm_xattn, norm_mem, w_xq, w_xkv,
           w_xo, norm_mlp, w_up, w_down, norm_final):
    batch, seq, _ = x.shape
    depth = w_in.shape[0]
    assert seq % max(INPROJ_TILE, POST_TILE, MLP_TILE, GLA_STEP) == 0 and seq // MOBA_BLOCK <= 8
    x2 = x.reshape(batch * seq, D_MODEL)
    bias = _bias_tiles(rp_table)
    c31 = rp_table[RP_BUCKETS - 1]
    glr_lo = 3 * A_WIDTH + 2 * G_KEY_WIDTH + G_WIDTH
    rg_lo = glr_lo + G_GATE_RANK
    pad_rank = LANES - G_GATE_RANK
    for l in range(depth):
        col = lambda lo, hi: w_in[l, :, lo:hi].astype(BF16)
        wglr = jnp.pad(col(glr_lo, rg_lo), ((0, 0), (0, pad_rank)))
        wgu = jnp.pad(w_gate_up[l].astype(BF16), ((0, pad_rank), (0, 0)))
        later = (w_out[l], w_xq[l], w_xkv[l], w_xo[l], w_up[l], w_down[l])
        qk, vt, qg, kg, la, vg, rg, wo, wxq, wxkv, wxo, wup, wdown = _inproj(
            x2, norm_mix[l][None], w_in[l].astype(BF16), col(rg_lo, rg_lo + G_WIDTH), wglr, wgu, b_gate[l][None], later)
        oa = _moba(qk, vt, bias, c31, batch, seq)
        og = _gla(qg, kg, la, vg, rg, g_norm[l][None], batch, seq)
        kx, vx = _memkv(mem, norm_mem[l][None], wxkv)
        x2 = _post(x2, oa, og, wo, norm_xattn[l][None], wxq, kx, vx, wxo, seq)
        last = l == depth - 1
        x2 = _mlp(x2, norm_mlp[l][None], wup, wdown, norm_final[None], last)
    return x2.reshape(batch, seq, D_MODEL)
```

```python
import functools
import math

import numpy as np
import jax
import jax.numpy as jnp
from jax import lax
from jax.experimental import pallas as pl
from jax.experimental.pallas import tpu as pltpu

F32 = jnp.float32
BF16 = jnp.bfloat16

D_MODEL = 1024
A_HEADS = 8
A_HEAD_DIM = 64
A_WIDTH = A_HEADS * A_HEAD_DIM
MOBA_BLOCK = 256
MOBA_TOPK = 3
G_HEADS = 4
G_WIDTH = D_MODEL - A_WIDTH
G_HEAD_V = G_WIDTH // G_HEADS
G_KEY_WIDTH = G_WIDTH // 2
G_HEAD_K = G_KEY_WIDTH // G_HEADS
G_GATE_RANK = 16
G_GATE_NORM = 16.0
X_HEADS = 4
X_HEAD_DIM = D_MODEL // X_HEADS
D_FF = 4 * D_MODEL
RP_BUCKETS = 32
RP_MAX_DIST = 128
EPS = 1e-6

LANES = 128
SUBLANES = 8
GATE_ROWS = 16
NEG = -1e30
MOBA_PAIRS = 2
MOBA_GROUP = 2
GLA_CHUNK = 128
GLA_STEP = 2048
LOG2E = float(np.log2(np.e))
INPROJ_TILE = 1024
INPROJ_PARTS = 2
POST_TILE = 1024
POST_PARTS = 2
VMEM_LIMIT = 62 * 1024 * 1024

def _nt(a, b):
    return lax.dot_general(a, b, (((1,), (1,)), ((), ())), preferred_element_type=F32)


def _tn(a, b):
    return lax.dot_general(a, b, (((0,), (0,)), ((), ())), preferred_element_type=F32)


def _dot(a, b):
    return jnp.dot(a, b, preferred_element_type=F32)


def _rms(x, g):
    return x * lax.rsqrt(jnp.mean(x * x, axis=-1, keepdims=True) + EPS) * g


def _bucket_thresholds():
    max_exact = RP_BUCKETS // 2
    d = np.arange(1, 4 * RP_MAX_DIST)
    val = (np.log(d.astype(np.float32) / np.float32(max_exact)) / np.float32(math.log(RP_MAX_DIST / max_exact))
           * np.float32(RP_BUCKETS - max_exact))
    bucket = np.minimum(max_exact + val.astype(np.int32), RP_BUCKETS - 1)
    return [int(d[(d >= max_exact) & (bucket >= max_exact + k)][0]) for k in range(1, RP_BUCKETS - max_exact)]


_THRESHOLDS = _bucket_thresholds()


def _bias_kernel(tab_ref, out_ref):
    h = pl.program_id(0)
    rows, width = 2 * MOBA_BLOCK, 3 * MOBA_BLOCK
    d = lax.broadcasted_iota(jnp.int32, (SUBLANES, width), 1)
    max_exact = RP_BUCKETS // 2
    bucket = jnp.where(d < max_exact, d, max_exact)
    for t in _THRESHOLDS:
        bucket = bucket + jnp.where(d >= t, 1, 0)
    prof = jnp.full((SUBLANES, width), NEG, F32)
    for b in range(RP_BUCKETS):
        prof = jnp.where(bucket == b, tab_ref[b, h] * LOG2E, prof)
    prof = jnp.where(d < rows, prof, NEG)
    tile = pltpu.roll(jnp.broadcast_to(prof[0:1], (rows, width)), 0, 1, stride=1, stride_axis=0)
    out_ref[0] = tile[:, MOBA_BLOCK:2 * MOBA_BLOCK]


def _bias_tiles(rp_table):
    return pl.pallas_call(
        _bias_kernel,
        grid=(A_HEADS,),
        in_specs=[pl.BlockSpec(memory_space=pltpu.SMEM)],
        out_specs=pl.BlockSpec((1, 2 * MOBA_BLOCK, MOBA_BLOCK), lambda h: (h, 0, 0)),
        out_shape=jax.ShapeDtypeStruct((A_HEADS, 2 * MOBA_BLOCK, MOBA_BLOCK), F32),
        name="bias",
    )(rp_table)


def _inproj_kernel(x_ref, g_ref, w_ref, wrg_ref, wglr_ref, wgu_ref, bg_ref, *rest):
    ncast = (len(rest) - 7) // 2
    cast_in, cast_out = rest[:ncast], rest[ncast + 7:]
    qk_ref, vt_ref, qg_ref, kg_ref, la_ref, vg_ref, rg_ref = rest[ncast:ncast + 7]
    tm = x_ref.shape[0]
    parts = [slice(i * tm // INPROJ_PARTS, (i + 1) * tm // INPROJ_PARTS) for i in range(INPROJ_PARTS)]
    hs = [_rms(x_ref[r, :], g_ref[...]).astype(BF16) for r in parts]
    va_lo = 2 * A_WIDTH
    qg_lo = va_lo + A_WIDTH
    kg_lo = qg_lo + G_KEY_WIDTH
    vg_lo = kg_lo + G_KEY_WIDTH

    glrs = [_dot(h, wglr_ref[...]).astype(BF16) for h in hs]
    for r, h in zip(parts, hs):
        qk_ref[r, 0:A_WIDTH] = (_dot(h, w_ref[:, 0:A_WIDTH]) * (LOG2E * A_HEAD_DIM ** -0.5)).astype(BF16)
    for r, h in zip(parts, hs):
        qk_ref[r, A_WIDTH:va_lo] = _dot(h, w_ref[:, A_WIDTH:va_lo]).astype(BF16)
    for r, glr in zip(parts, glrs):
        z = _dot(glr, wgu_ref[...]) + bg_ref[...]
        log_sig = jnp.minimum(z, 0.0) - jnp.log(1.0 + jnp.exp(-jnp.abs(z)))
        la_ref[r, :] = log_sig * (1.0 / G_GATE_NORM)
    for r, h in zip(parts, hs):
        vt_ref[:, r] = _dot(h, w_ref[:, va_lo:qg_lo]).astype(BF16).T
    for r, h in zip(parts, hs):
        qg_ref[r, :] = _dot(h, w_ref[:, qg_lo:kg_lo]) * (G_HEAD_K ** -0.5)
        kg_ref[r, :] = _dot(h, w_ref[:, kg_lo:vg_lo])
    for r, h in zip(parts, hs):
        vg_ref[r, :] = _dot(h, w_ref[:, vg_lo:vg_lo + G_WIDTH]).astype(BF16)
    for r, h in zip(parts, hs):
        rg_ref[r, :] = _dot(h, wrg_ref[...])
    for src, dst in zip(cast_in, cast_out):
        dst[...] = src[...].astype(BF16)


def _inproj(x2, g, w, wrg, wglr, wgu, bg, later):
    n = x2.shape[0]
    tm = INPROJ_TILE
    steps = n // tm
    const = lambda a: pl.BlockSpec(a.shape, lambda i: (0, 0), pipeline_mode=pl.Buffered(1))
    row = lambda width: pl.BlockSpec((tm, width), lambda i: (i, 0))
    assert all(a.shape[0] % (steps * GATE_ROWS) == 0 for a in later)
    part = [pl.BlockSpec((a.shape[0] // steps, a.shape[1]), lambda i: (i, 0)) for a in later]
    return pl.pallas_call(
        _inproj_kernel,
        grid=(steps,),
        in_specs=[row(D_MODEL)] + [const(a) for a in (g, w, wrg, wglr, wgu, bg)] + part,
        out_specs=[row(2 * A_WIDTH), pl.BlockSpec((A_WIDTH, tm), lambda i: (0, i)),
                   row(G_KEY_WIDTH), row(G_KEY_WIDTH), row(G_KEY_WIDTH), row(G_WIDTH), row(G_WIDTH)] + part,
        out_shape=[jax.ShapeDtypeStruct((n, 2 * A_WIDTH), BF16),
                   jax.ShapeDtypeStruct((A_WIDTH, n), BF16),
                   jax.ShapeDtypeStruct((n, G_KEY_WIDTH), F32),
                   jax.ShapeDtypeStruct((n, G_KEY_WIDTH), F32),
                   jax.ShapeDtypeStruct((n, G_KEY_WIDTH), F32),
                   jax.ShapeDtypeStruct((n, G_WIDTH), BF16),
                   jax.ShapeDtypeStruct((n, G_WIDTH), F32)] + [jax.ShapeDtypeStruct(a.shape, BF16) for a in later],
        compiler_params=pltpu.CompilerParams(dimension_semantics=("parallel",), vmem_limit_bytes=VMEM_LIMIT),
        name="inproj",
    )(x2, g, w, wrg, wglr, wgu, bg, *later)


def _moba_kernel(c31_ref, q_ref, k_ref, vt_ref, bias_ref, o_ref, khead, s_scr, *, seq):
    step = pl.program_id(1)
    nblk = seq // MOBA_BLOCK
    half = LANES // 2
    lane2 = lax.broadcasted_iota(jnp.int32, (seq, LANES), 1)
    lane_q = lax.broadcasted_iota(jnp.int32, (MOBA_BLOCK, LANES), 1)
    rowi = lax.broadcasted_iota(jnp.int32, (GATE_ROWS, MOBA_BLOCK), 0)
    blk = lambda n: slice(n * MOBA_BLOCK, (n + 1) * MOBA_BLOCK)
    pair_lanes = lambda pp: slice(pp * LANES, (pp + 1) * LANES)

    def build(pp):
        k = k_ref[:, pair_lanes(pp)]
        khead[2 * pp] = k * jnp.where(lane2 < half, 1.0, 0.0).astype(BF16)
        khead[2 * pp + 1] = k * jnp.where(lane2 >= half, 1.0, 0.0).astype(BF16)
        n_i = lax.broadcasted_iota(jnp.int32, (GATE_ROWS, seq), 0)
        t_i = lax.broadcasted_iota(jnp.int32, (GATE_ROWS, seq), 1)
        avg = jnp.where(t_i // MOBA_BLOCK == n_i, 1.0 / MOBA_BLOCK, 0.0).astype(BF16)
        km = _dot(avg, k)
        kmh = km.astype(BF16)
        return kmh, (km - kmh.astype(F32)).astype(BF16)

    def prepare(pp, kmeans, qi, h, slot):
        qm = q_ref[blk(qi), pair_lanes(pp)] * jnp.where((lane_q < half) if h == 0 else (lane_q >= half), 1.0,
                                                        0.0).astype(BF16)
        head = 2 * pp + h
        return dict(pp=pp, qi=qi, h=h, head=head, qm=qm, kmeans=kmeans if qi > MOBA_TOPK else None,
                    sbuf=s_scr.at[slot], near=None, far=None, acc=None,
                    c31=c31_ref[2 * MOBA_PAIRS * step + head] * LOG2E)

    def penalty(u, n):
        return 0.0 if u["pen"] is None else u["pen"][n:n + 1, :]

    def score_matmul(u):
        width = (u["qi"] + 1) * MOBA_BLOCK
        keys = khead[u["head"], 0:width, :]
        if u["kmeans"] is None:
            u["s_all"], u["gate"] = _nt(keys, u["qm"]), None
        else:
            s = _nt(jnp.concatenate([keys, u["kmeans"][0], u["kmeans"][1]], axis=0), u["qm"])
            u["s_all"] = s[0:width]
            u["gate"] = s[width:width + GATE_ROWS] + s[width + GATE_ROWS:width + 2 * GATE_ROWS]

    def score_reduce(u):
        qi, head = u["qi"], u["head"]
        u["pen"] = None
        if u["gate"] is not None:
            g = u["gate"]
            cnt = jnp.zeros((GATE_ROWS, MOBA_BLOCK), F32)
            for m in range(qi):
                gm = g[m:m + 1, :]
                cnt = cnt + jnp.where((gm > g) | ((gm == g) & (m < rowi)), 1.0, 0.0)
            u["pen"] = jnp.where((rowi < qi) & (cnt < MOBA_TOPK), 0.0, NEG)
        for n in range(qi + 1):
            s = u["s_all"][blk(n)]
            if n == qi:
                s = s + bias_ref[head, MOBA_BLOCK:2 * MOBA_BLOCK, :]
            elif n == qi - 1:
                s = s + bias_ref[head, 0:MOBA_BLOCK, :]
            u["sbuf"][blk(n), :] = s
            t = jnp.max(s.reshape(MOBA_BLOCK // SUBLANES, SUBLANES, MOBA_BLOCK), axis=0)
            if n < qi:
                t = t + penalty(u, n)
            key = "far" if n < qi - 1 else "near"
            u[key] = t if u[key] is None else jnp.maximum(u[key], t)
        mx = u["near"] if u["far"] is None else jnp.maximum(u["near"], u["far"] + u["c31"])
        u["m_near"] = jnp.max(mx, axis=0, keepdims=True)
        u["m_far"] = u["m_near"] - u["c31"]

    def prob_exp(u):
        qi = u["qi"]
        shift = lambda n: (u["m_far"] if n < qi - 1 else u["m_near"]) - (penalty(u, n) if n < qi else 0.0)
        u["pt"] = jnp.concatenate([jnp.exp2(u["sbuf"][blk(n), :] - shift(n)).astype(BF16) for n in range(qi + 1)],
                                  axis=0)

    def prob_matmul(u):
        qi, h = u["qi"], u["h"]
        width = (qi + 1) * MOBA_BLOCK
        r0 = u["head"] * half
        vrows = vt_ref[r0:r0 + half, 0:width]
        ones_rows = jnp.ones((half, width), BF16)
        vt = jnp.concatenate([vrows, ones_rows] if h == 0 else [ones_rows, vrows], axis=0)
        u["acc"] = _dot(vt, u["pt"])

    outs = {}

    def finish(u):
        acc, h = u["acc"], u["h"]
        num = acc[half * h:half * (h + 1)]
        den = acc[half * (1 - h):half * (1 - h) + 1]
        outs[h] = num / den
        if h == 1:
            o_ref[blk(u["qi"]), pair_lanes(u["pp"])] = jnp.concatenate([outs[0], outs[1]], axis=0).T.astype(BF16)

    group = MOBA_GROUP
    stages = [(pp, [(qi, h) for qi in range(g * group, (g + 1) * group) for h in range(2)])
              for pp in range(MOBA_PAIRS) for g in range(nblk // group)]
    prev, kmeans = [], {}
    for g, (pp, members) in enumerate(stages + [(None, [])]):
        if members and pp not in kmeans:
            kmeans[pp] = build(pp)
        cur = [prepare(pp, kmeans[pp], qi, h, (g % 2) * 2 * group + 2 * (qi % group) + h) for qi, h in members]
        for u in prev:
            prob_exp(u)
        for u in cur:
            score_matmul(u)
        for u in prev:
            prob_matmul(u)
        for u in cur:
            score_reduce(u)
        for u in prev:
            finish(u)
        prev = cur


def _moba(qk, vt, bias, c31, batch, seq):
    n = qk.shape[0]
    width = MOBA_PAIRS * LANES
    nstep = A_WIDTH // width
    kern = functools.partial(_moba_kernel, seq=seq)
    return pl.pallas_call(
        kern,
        grid=(batch, nstep),
        in_specs=[pl.BlockSpec(memory_space=pltpu.SMEM),
                  pl.BlockSpec((seq, width), lambda b, p: (b, p)),
                  pl.BlockSpec((seq, width), lambda b, p: (b, nstep + p)),
                  pl.BlockSpec((width, seq), lambda b, p: (p, b)),
                  pl.BlockSpec((2 * MOBA_PAIRS, 2 * MOBA_BLOCK, MOBA_BLOCK), lambda b, p: (p, 0, 0))],
        out_specs=pl.BlockSpec((seq, width), lambda b, p: (b, p)),
        out_shape=jax.ShapeDtypeStruct((n, A_WIDTH), BF16),
        scratch_shapes=[pltpu.VMEM((2 * MOBA_PAIRS, seq, LANES), BF16),
                        pltpu.VMEM((2 * 2 * MOBA_GROUP, seq, MOBA_BLOCK), F32)],
        compiler_params=pltpu.CompilerParams(dimension_semantics=("parallel", "arbitrary"),
                                             vmem_limit_bytes=VMEM_LIMIT),
        name="moba",
    )(c31, qk, qk, vt, bias)


_GLA_LEVELS = int(math.log2(GLA_CHUNK))


_GLA_BIG_LEVELS = _GLA_LEVELS - int(math.log2(SUBLANES))


def _gla_weights():
    c = GLA_CHUNK
    i = np.arange(c)[:, None]
    j = np.arange(c)[None, :]
    mats = [(j <= i)]
    for lvl in range(_GLA_BIG_LEVELS, _GLA_LEVELS):
        s = (c // 2) >> lvl
        ref = (i // (2 * s)) * (2 * s) + s - 1
        mats.append(np.where(i > ref, (j > ref) & (j <= i), (j > i) & (j <= ref)))
    return np.concatenate(mats, axis=0).astype(np.float32)


def _gla_level_map():
    c = GLA_CHUNK
    i = np.arange(c)[:, None]
    j = np.arange(c)[None, :]
    top_bit = np.floor(np.log2(np.maximum(i ^ j, 1))).astype(np.int64)
    lvl = np.where(j < i, _GLA_LEVELS - 1 - top_bit, np.where(j == i, _GLA_LEVELS, _GLA_LEVELS + 1))
    return np.concatenate([lvl, lvl], axis=1).astype(np.int32)


def _gla_kernel(q_ref, k_ref, la_ref, v_ref, rg_ref, w_ref, lmap_ref, gn_ref, o_ref, st_ref):
    c = GLA_CHUNK
    half = LANES // 2
    npair = G_HEADS // 2

    @pl.when(pl.program_id(1) == 0)
    def _init():
        st_ref[...] = jnp.zeros(st_ref.shape, F32)

    w = w_ref[...]
    lmap = lmap_ref[...]
    lane = lax.broadcasted_iota(jnp.int32, (1, LANES), 1)
    lane_c = lax.broadcasted_iota(jnp.int32, (c, LANES), 1)
    lm = [jnp.where(lane_c < half, 1.0, 0.0).astype(BF16), jnp.where(lane_c >= half, 1.0, 0.0).astype(BF16)]
    row_small = lax.broadcasted_iota(jnp.int32, (c, G_KEY_WIDTH), 0)

    def level_factors(q, k, b, d_all, lvl):
        s = (c // 2) >> lvl
        if s < SUBLANES:
            j = lvl - _GLA_BIG_LEVELS
            e = jnp.exp2(d_all[c * (j + 1):c * (j + 2)])
            odd = (row_small // s) % 2 == 1
            return jnp.where(odd, q * e, 0.0), jnp.where(odd, 0.0, k * e)
        zeros = jnp.zeros((s, q.shape[1]), F32)
        qp, kp = [], []
        for blk in range(c // s):
            rows = slice(blk * s, (blk + 1) * s)
            ref = (blk // 2) * 2 * s + s - 1
            if blk % 2 == 1:
                qp.append(q[rows] * jnp.exp2(b[rows] - b[ref:ref + 1]))
                kp.append(zeros)
            else:
                qp.append(zeros)
                kp.append(k[rows] * jnp.exp2(b[ref:ref + 1] - b[rows]))
        return jnp.concatenate(qp, axis=0), jnp.concatenate(kp, axis=0)

    nch = GLA_STEP // c
    rows_of = [slice(ch * c, (ch + 1) * c) for ch in range(nch)]
    pair_lanes = [slice(LANES * p, LANES * (p + 1)) for p in range(npair)]

    def decays(ch):
        la = la_ref[rows_of[ch], :] * LOG2E
        hi = la.astype(BF16)
        lo = (la - hi.astype(F32)).astype(BF16)
        return _dot(w, hi) + _dot(w, lo)

    def factors(ch, d_all):
        q = q_ref[rows_of[ch], :]
        k = k_ref[rows_of[ch], :]
        b = d_all[0:c]
        blast = b[c - 1:c, :]
        qts, kts = [], []
        for lvl in range(_GLA_LEVELS):
            qt, kt = level_factors(q, k, b, d_all, lvl)
            qts.append(qt.astype(BF16))
            kts.append(kt.astype(BF16))
        qts.append(q.astype(BF16))
        kts.append(k.astype(BF16))
        return dict(ch=ch, qts=qts, kts=kts, qe=(q * jnp.exp2(b)).astype(BF16),
                    khat=(k * jnp.exp2(blast - b)).astype(BF16), dec=jnp.exp2(blast),
                    a=[jnp.zeros((c, 2 * c), F32)] * npair)

    def level_scores(units):
        for lvl in range(_GLA_LEVELS + 1):
            mask = lmap == lvl
            s = (c // 2) >> lvl if lvl < _GLA_BIG_LEVELS else c
            blocks = [slice(i * s, (i + 1) * s) for i in range(c // s)]
            live = [i for i in range(len(blocks)) if i % 2 == 1 or s == c]
            for u in units:
                for p, sl in enumerate(pair_lanes):
                    kp = u["kts"][lvl][:, sl]
                    ql = u["qts"][lvl]
                    pm = _nt(jnp.concatenate([ql[blocks[i], sl] for i in live], axis=0),
                             jnp.concatenate([kp * lm[0], kp * lm[1]], axis=0))
                    a = u["a"][p]
                    pieces = [a[r] for r in blocks]
                    for n, i in enumerate(live):
                        pieces[i] = jnp.where(mask[blocks[i]], pm[n * s:(n + 1) * s], pieces[i])
                    u["a"] = [jnp.concatenate(pieces, axis=0) if j == p else v for j, v in enumerate(u["a"])]

    def stateless_matmuls(units):
        for u in units:
            vhs = [v_ref[rows_of[u["ch"]], LANES * h:LANES * (h + 1)] for h in range(G_HEADS)]
            u["intra"] = [_dot(u["a"][h // 2][:, (h % 2) * c:(h % 2 + 1) * c].astype(BF16), vhs[h])
                          for h in range(G_HEADS)]
            u["inc"] = [_tn(vhs[h], u["khat"][:, pair_lanes[h // 2]]) for h in range(G_HEADS)]

    def state_chain(units, states):
        for u in units:
            rows = rows_of[u["ch"]]
            for p, sl in enumerate(pair_lanes):
                stb = states[p].astype(BF16)
                for hh in range(2):
                    h = 2 * p + hh
                    o = u["intra"][h] + _nt(u["qe"][:, sl] * lm[hh], stb)
                    y = _rms(o, gn_ref[...])
                    rg = rg_ref[rows, LANES * h:LANES * (h + 1)]
                    y = y * (rg / (1.0 + jnp.exp(-rg)))
                    o_ref[rows, LANES * h:LANES * (h + 1)] = y.astype(BF16)
                states[p] = states[p] * u["dec"][:, sl] + jnp.where(lane < half, u["inc"][2 * p], u["inc"][2 * p + 1])

    d_alls = [decays(ch) for ch in range(nch)]
    chunks = [factors(ch, d_alls[ch]) for ch in range(nch)]
    level_scores(chunks)
    stateless_matmuls(chunks)
    states = [st_ref[p] for p in range(npair)]
    state_chain(chunks, states)
    for p in range(npair):
        st_ref[p] = states[p]


def _gla(qg, kg, la, vg, rg, gn, batch, seq):
    n = qg.shape[0]
    nchunk = seq // GLA_STEP
    w = jnp.asarray(_gla_weights(), BF16)
    lmap = jnp.asarray(_gla_level_map())
    row = lambda width: pl.BlockSpec((GLA_STEP, width), lambda b, c: (b * nchunk + c, 0))
    const = functools.partial(pl.BlockSpec, pipeline_mode=pl.Buffered(1))
    return pl.pallas_call(
        _gla_kernel,
        grid=(batch, nchunk),
        in_specs=[row(G_KEY_WIDTH), row(G_KEY_WIDTH), row(G_KEY_WIDTH), row(G_WIDTH), row(G_WIDTH),
                  const(w.shape, lambda b, c: (0, 0)),
                  const(lmap.shape, lambda b, c: (0, 0)),
                  const((1, G_HEAD_V), lambda b, c: (0, 0))],
        out_specs=row(G_WIDTH),
        out_shape=jax.ShapeDtypeStruct((n, G_WIDTH), BF16),
        scratch_shapes=[pltpu.VMEM((G_HEADS // 2, G_HEAD_V, LANES), F32)],
        compiler_params=pltpu.CompilerParams(dimension_semantics=("parallel", "arbitrary"),
                                             vmem_limit_bytes=VMEM_LIMIT),
        name="gla",
    )(qg, kg, la, vg, rg, w, lmap, gn)


def _memkv_kernel(m_ref, g_ref, w_ref, k_ref, v_ref):
    nb, mlen, _ = m_ref.shape
    h = _rms(m_ref[...].reshape(nb * mlen, D_MODEL), g_ref[...]).astype(BF16)
    k_ref[...] = _dot(h, w_ref[:, 0:D_MODEL]).astype(BF16).reshape(nb, mlen, D_MODEL)
    v_ref[...] = _dot(h, w_ref[:, D_MODEL:2 * D_MODEL]).astype(BF16).reshape(nb, mlen, D_MODEL)


def _memkv(mem, g, w):
    batch, mlen, _ = mem.shape
    nb = 2 if batch % 2 == 0 else 1
    const = functools.partial(pl.BlockSpec, pipeline_mode=pl.Buffered(1))
    blk = pl.BlockSpec((nb, mlen, D_MODEL), lambda b: (b, 0, 0))
    return pl.pallas_call(
        _memkv_kernel,
        grid=(batch // nb,),
        in_specs=[blk, const((1, D_MODEL), lambda b: (0, 0)), const((D_MODEL, 2 * D_MODEL), lambda b: (0, 0))],
        out_specs=[blk, blk],
        out_shape=[jax.ShapeDtypeStruct(mem.shape, BF16)] * 2,
        compiler_params=pltpu.CompilerParams(dimension_semantics=("parallel",), vmem_limit_bytes=VMEM_LIMIT),
        name="memkv",
    )(mem, g, w)


def _post_kernel(x_ref, oa_ref, og_ref, wo_ref, gx_ref, wq_ref, kx_ref, vx_ref, wxo_ref, g_ref, wu_ref, wd_ref,
                 gf_ref, o_ref, *, final_norm):
    tm = x_ref.shape[0]
    parts = [slice(i * tm // POST_PARTS, (i + 1) * tm // POST_PARTS) for i in range(POST_PARTS)]
    heads = [slice(X_HEAD_DIM * hd, X_HEAD_DIM * (hd + 1)) for hd in range(X_HEADS)]

    def out_proj(r):
        return x_ref[r, :] + _dot(oa_ref[r, :], wo_ref[0:A_WIDTH, :]) + _dot(og_ref[r, :], wo_ref[A_WIDTH:D_MODEL, :])

    def scores(x1):
        h = _rms(x1, gx_ref[...]).astype(BF16)
        qx = (_dot(h, wq_ref[...]) * (LOG2E * X_HEAD_DIM ** -0.5)).astype(BF16)
        return [_nt(qx[:, sl], kx_ref[0, :, sl]) for sl in heads]

    def attend(r, x1, s_heads):
        outs = []
        for sl, s in zip(heads, s_heads):
            pexp = jnp.exp2(s - jnp.max(s, axis=1, keepdims=True))
            pn = (pexp / jnp.sum(pexp, axis=1, keepdims=True)).astype(BF16)
            outs.append(_dot(pn, vx_ref[0, :, sl]).astype(BF16))
        return x1 + _dot(jnp.concatenate(outs, axis=1), wxo_ref[...])

    x1s = [out_proj(r) for r in parts]
    s_parts = [scores(x1) for x1 in x1s]
    acc = [attend(r, x1, s_heads) for r, x1, s_heads in zip(parts, x1s, s_parts)]
    h = [_rms(x, g_ref[...]).astype(BF16) for x in acc]
    for c in range(D_FF // D_MODEL):
        sl = slice(D_MODEL * c, D_MODEL * (c + 1))
        u = [jnp.square(jnp.maximum(_dot(v, wu_ref[:, sl]), 0.0)).astype(BF16) for v in h]
        acc = [a + _dot(v, wd_ref[sl, :]) for a, v in zip(acc, u)]
    for r, a in zip(parts, acc):
        o_ref[r, :] = _rms(a, gf_ref[...]) if final_norm else a


def _post(x2, oa, og, wo, gx, wq, kx, vx, wxo, g, wu, wd, gf, final_norm, seq):
    n = x2.shape[0]
    tm = POST_TILE
    per_b = seq // tm
    mlen = kx.shape[1]
    const = functools.partial(pl.BlockSpec, pipeline_mode=pl.Buffered(1))
    row = lambda width: pl.BlockSpec((tm, width), lambda i: (i, 0))
    mem = pl.BlockSpec((1, mlen, D_MODEL), lambda i: (i // per_b, 0, 0))
    sq = const((D_MODEL, D_MODEL), lambda i: (0, 0))
    vec = const((1, D_MODEL), lambda i: (0, 0))
    return pl.pallas_call(
        functools.partial(_post_kernel, final_norm=final_norm),
        grid=(n // tm,),
        in_specs=[row(D_MODEL), row(A_WIDTH), row(G_WIDTH), sq, vec, sq, mem, mem, sq,
                  vec, const((D_MODEL, D_FF), lambda i: (0, 0)), const((D_FF, D_MODEL), lambda i: (0, 0)), vec],
        out_specs=row(D_MODEL),
        out_shape=jax.ShapeDtypeStruct((n, D_MODEL), F32),
        compiler_params=pltpu.CompilerParams(dimension_semantics=("parallel",), vmem_limit_bytes=VMEM_LIMIT),
        name="post",
    )(x2, oa, og, wo, gx, wq, kx, vx, wxo, g, wu, wd, gf)


def kernel(x, mem, rp_table, norm_mix, w_in, w_gate_up, b_gate, g_norm, w_out, norm_xattn, norm_mem, w_xq, w_xkv,
           w_xo, norm_mlp, w_up, w_down, norm_final):
    batch, seq, _ = x.shape
    depth = w_in.shape[0]
    assert seq % max(INPROJ_TILE, POST_TILE, GLA_STEP) == 0 and seq // MOBA_BLOCK <= 8
    x2 = x.reshape(batch * seq, D_MODEL)
    bias = _bias_tiles(rp_table)
    c31 = rp_table[RP_BUCKETS - 1]
    glr_lo = 3 * A_WIDTH + 2 * G_KEY_WIDTH + G_WIDTH
    rg_lo = glr_lo + G_GATE_RANK
    pad_rank = LANES - G_GATE_RANK
    for l in range(depth):
        col = lambda lo, hi: w_in[l, :, lo:hi].astype(BF16)
        wglr = jnp.pad(col(glr_lo, rg_lo), ((0, 0), (0, pad_rank)))
        wgu = jnp.pad(w_gate_up[l].astype(BF16), ((0, pad_rank), (0, 0)))
        later = (w_out[l], w_xq[l], w_xkv[l], w_xo[l], w_up[l], w_down[l])
        qk, vt, qg, kg, la, vg, rg, wo, wxq, wxkv, wxo, wup, wdown = _inproj(
            x2, norm_mix[l][None], w_in[l].astype(BF16), col(rg_lo, rg_lo + G_WIDTH), wglr, wgu, b_gate[l][None], later)
        oa = _moba(qk, vt, bias, c31, batch, seq)
        og = _gla(qg, kg, la, vg, rg, g_norm[l][None], batch, seq)
        kx, vx = _memkv(mem, norm_mem[l][None], wxkv)
        x2 = _post(x2, oa, og, wo, norm_xattn[l][None], wxq, kx, vx, wxo, norm_mlp[l][None], wup, wdown,
                   norm_final[None], l == depth - 1, seq)
    return x2.reshape(batch, seq, D_MODEL)
```

```python
import functools
import math

import numpy as np
import jax
import jax.numpy as jnp
from jax import lax
from jax.experimental import pallas as pl
from jax.experimental.pallas import tpu as pltpu

F32 = jnp.float32
BF16 = jnp.bfloat16

D_MODEL = 1024
A_HEADS = 8
A_HEAD_DIM = 64
A_WIDTH = A_HEADS * A_HEAD_DIM
MOBA_BLOCK = 256
MOBA_TOPK = 3
G_HEADS = 4
G_WIDTH = D_MODEL - A_WIDTH
G_HEAD_V = G_WIDTH // G_HEADS
G_KEY_WIDTH = G_WIDTH // 2
G_HEAD_K = G_KEY_WIDTH // G_HEADS
G_GATE_RANK = 16
G_GATE_NORM = 16.0
X_HEADS = 4
X_HEAD_DIM = D_MODEL // X_HEADS
D_FF = 4 * D_MODEL
RP_BUCKETS = 32
RP_MAX_DIST = 128
EPS = 1e-6

LANES = 128
SUBLANES = 8
GATE_ROWS = 16
NEG = -1e30
MOBA_PAIRS = 2
MOBA_GROUP = 2
GLA_CHUNK = 128
GLA_STEP = 2048
LOG2E = float(np.log2(np.e))
INPROJ_TILE = 1024
INPROJ_PARTS = 2
POST_TILE = 1024
POST_PARTS = 2
VMEM_LIMIT = 62 * 1024 * 1024

def _nt(a, b):
    return lax.dot_general(a, b, (((1,), (1,)), ((), ())), preferred_element_type=F32)


def _tn(a, b):
    return lax.dot_general(a, b, (((0,), (0,)), ((), ())), preferred_element_type=F32)


def _dot(a, b):
    return jnp.dot(a, b, preferred_element_type=F32)


def _rms(x, g):
    return x * lax.rsqrt(jnp.mean(x * x, axis=-1, keepdims=True) + EPS) * g


def _bucket_thresholds():
    max_exact = RP_BUCKETS // 2
    d = np.arange(1, 4 * RP_MAX_DIST)
    val = (np.log(d.astype(np.float32) / np.float32(max_exact)) / np.float32(math.log(RP_MAX_DIST / max_exact))
           * np.float32(RP_BUCKETS - max_exact))
    bucket = np.minimum(max_exact + val.astype(np.int32), RP_BUCKETS - 1)
    return [int(d[(d >= max_exact) & (bucket >= max_exact + k)][0]) for k in range(1, RP_BUCKETS - max_exact)]


_THRESHOLDS = _bucket_thresholds()


def _bias_kernel(tab_ref, out_ref):
    h = pl.program_id(0)
    rows, width = 2 * MOBA_BLOCK, 3 * MOBA_BLOCK
    d = lax.broadcasted_iota(jnp.int32, (SUBLANES, width), 1)
    max_exact = RP_BUCKETS // 2
    bucket = jnp.where(d < max_exact, d, max_exact)
    for t in _THRESHOLDS:
        bucket = bucket + jnp.where(d >= t, 1, 0)
    prof = jnp.full((SUBLANES, width), NEG, F32)
    for b in range(RP_BUCKETS):
        prof = jnp.where(bucket == b, tab_ref[b, h] * LOG2E, prof)
    prof = jnp.where(d < rows, prof, NEG)
    tile = pltpu.roll(jnp.broadcast_to(prof[0:1], (rows, width)), 0, 1, stride=1, stride_axis=0)
    out_ref[0] = tile[:, MOBA_BLOCK:2 * MOBA_BLOCK]


def _bias_tiles(rp_table):
    return pl.pallas_call(
        _bias_kernel,
        grid=(A_HEADS,),
        in_specs=[pl.BlockSpec(memory_space=pltpu.SMEM)],
        out_specs=pl.BlockSpec((1, 2 * MOBA_BLOCK, MOBA_BLOCK), lambda h: (h, 0, 0)),
        out_shape=jax.ShapeDtypeStruct((A_HEADS, 2 * MOBA_BLOCK, MOBA_BLOCK), F32),
        name="bias",
    )(rp_table)


def _inproj_kernel(x_ref, g_ref, wf_ref, wrg_ref, wglr_ref, wgu_ref, bg_ref, *rest):
    ncast = (len(rest) - 8) // 2
    cast_in, cast_out, w_ref = rest[:ncast], rest[ncast + 7:-1], rest[-1]
    qk_ref, vt_ref, qg_ref, kg_ref, la_ref, vg_ref, rg_ref = rest[ncast:ncast + 7]

    @pl.when(pl.program_id(0) == 0)
    def _cast_weights():
        w_ref[...] = wf_ref[:, 0:w_ref.shape[1]].astype(BF16)

    tm = x_ref.shape[0]
    parts = [slice(i * tm // INPROJ_PARTS, (i + 1) * tm // INPROJ_PARTS) for i in range(INPROJ_PARTS)]
    hs = [_rms(x_ref[r, :], g_ref[...]).astype(BF16) for r in parts]
    va_lo = 2 * A_WIDTH
    qg_lo = va_lo + A_WIDTH
    kg_lo = qg_lo + G_KEY_WIDTH
    vg_lo = kg_lo + G_KEY_WIDTH

    glrs = [_dot(h, wglr_ref[...]).astype(BF16) for h in hs]
    for r, h in zip(parts, hs):
        qk_ref[r, 0:A_WIDTH] = (_dot(h, w_ref[:, 0:A_WIDTH]) * (LOG2E * A_HEAD_DIM ** -0.5)).astype(BF16)
    for r, h in zip(parts, hs):
        qk_ref[r, A_WIDTH:va_lo] = _dot(h, w_ref[:, A_WIDTH:va_lo]).astype(BF16)
    for r, glr in zip(parts, glrs):
        z = _dot(glr, wgu_ref[...]) + bg_ref[...]
        log_sig = jnp.minimum(z, 0.0) - jnp.log(1.0 + jnp.exp(-jnp.abs(z)))
        la_ref[r, :] = log_sig * (1.0 / G_GATE_NORM)
    for r, h in zip(parts, hs):
        vt_ref[:, r] = _dot(h, w_ref[:, va_lo:qg_lo]).astype(BF16).T
    for r, h in zip(parts, hs):
        qg_ref[r, :] = _dot(h, w_ref[:, qg_lo:kg_lo]) * (G_HEAD_K ** -0.5)
        kg_ref[r, :] = _dot(h, w_ref[:, kg_lo:vg_lo])
    for r, h in zip(parts, hs):
        vg_ref[r, :] = _dot(h, w_ref[:, vg_lo:vg_lo + G_WIDTH]).astype(BF16)
    for r, h in zip(parts, hs):
        rg_ref[r, :] = _dot(h, wrg_ref[...])
    for src, dst in zip(cast_in, cast_out):
        dst[...] = src[...].astype(BF16)


def _inproj(x2, g, w, wrg, wglr, wgu, bg, later):
    n = x2.shape[0]
    tm = INPROJ_TILE
    steps = n // tm
    const = lambda a: pl.BlockSpec(a.shape, lambda i: (0, 0), pipeline_mode=pl.Buffered(1))
    row = lambda width: pl.BlockSpec((tm, width), lambda i: (i, 0))
    assert all(a.shape[0] % (steps * GATE_ROWS) == 0 for a in later)
    part = [pl.BlockSpec((a.shape[0] // steps, a.shape[1]), lambda i: (i, 0)) for a in later]
    return pl.pallas_call(
        _inproj_kernel,
        grid=(steps,),
        in_specs=[row(D_MODEL)] + [const(a) for a in (g, w, wrg, wglr, wgu, bg)] + part,
        out_specs=[row(2 * A_WIDTH), pl.BlockSpec((A_WIDTH, tm), lambda i: (0, i)),
                   row(G_KEY_WIDTH), row(G_KEY_WIDTH), row(G_KEY_WIDTH), row(G_WIDTH), row(G_WIDTH)] + part,
        out_shape=[jax.ShapeDtypeStruct((n, 2 * A_WIDTH), BF16),
                   jax.ShapeDtypeStruct((A_WIDTH, n), BF16),
                   jax.ShapeDtypeStruct((n, G_KEY_WIDTH), F32),
                   jax.ShapeDtypeStruct((n, G_KEY_WIDTH), F32),
                   jax.ShapeDtypeStruct((n, G_KEY_WIDTH), F32),
                   jax.ShapeDtypeStruct((n, G_WIDTH), BF16),
                   jax.ShapeDtypeStruct((n, G_WIDTH), F32)] + [jax.ShapeDtypeStruct(a.shape, BF16) for a in later],
        scratch_shapes=[pltpu.VMEM((D_MODEL, 3 * A_WIDTH + 2 * G_KEY_WIDTH + G_WIDTH), BF16)],
        compiler_params=pltpu.CompilerParams(dimension_semantics=("arbitrary",), vmem_limit_bytes=VMEM_LIMIT),
        name="inproj",
    )(x2, g, w, wrg, wglr, wgu, bg, *later)


def _moba_kernel(c31_ref, q_ref, k_ref, vt_ref, bias_ref, o_ref, khead, s_scr, *, seq):
    step = pl.program_id(1)
    nblk = seq // MOBA_BLOCK
    half = LANES // 2
    lane2 = lax.broadcasted_iota(jnp.int32, (seq, LANES), 1)
    lane_q = lax.broadcasted_iota(jnp.int32, (MOBA_BLOCK, LANES), 1)
    rowi = lax.broadcasted_iota(jnp.int32, (GATE_ROWS, MOBA_BLOCK), 0)
    blk = lambda n: slice(n * MOBA_BLOCK, (n + 1) * MOBA_BLOCK)
    pair_lanes = lambda pp: slice(pp * LANES, (pp + 1) * LANES)

    def build(pp):
        k = k_ref[:, pair_lanes(pp)]
        khead[2 * pp] = k * jnp.where(lane2 < half, 1.0, 0.0).astype(BF16)
        khead[2 * pp + 1] = k * jnp.where(lane2 >= half, 1.0, 0.0).astype(BF16)
        n_i = lax.broadcasted_iota(jnp.int32, (GATE_ROWS, seq), 0)
        t_i = lax.broadcasted_iota(jnp.int32, (GATE_ROWS, seq), 1)
        avg = jnp.where(t_i // MOBA_BLOCK == n_i, 1.0 / MOBA_BLOCK, 0.0).astype(BF16)
        km = _dot(avg, k)
        kmh = km.astype(BF16)
        return kmh, (km - kmh.astype(F32)).astype(BF16)

    def prepare(pp, kmeans, qi, h, slot):
        qm = q_ref[blk(qi), pair_lanes(pp)] * jnp.where((lane_q < half) if h == 0 else (lane_q >= half), 1.0,
                                                        0.0).astype(BF16)
        head = 2 * pp + h
        return dict(pp=pp, qi=qi, h=h, head=head, qm=qm, kmeans=kmeans if qi > MOBA_TOPK else None,
                    sbuf=s_scr.at[slot], near=None, far=None, acc=None,
                    c31=c31_ref[2 * MOBA_PAIRS * step + head] * LOG2E)

    def penalty(u, n):
        return 0.0 if u["pen"] is None else u["pen"][n:n + 1, :]

    def score_matmul(u):
        width = (u["qi"] + 1) * MOBA_BLOCK
        keys = khead[u["head"], 0:width, :]
        if u["kmeans"] is None:
            u["s_all"], u["gate"] = _nt(keys, u["qm"]), None
        else:
            s = _nt(jnp.concatenate([keys, u["kmeans"][0], u["kmeans"][1]], axis=0), u["qm"])
            u["s_all"] = s[0:width]
            u["gate"] = s[width:width + GATE_ROWS] + s[width + GATE_ROWS:width + 2 * GATE_ROWS]

    def score_reduce(u):
        qi, head = u["qi"], u["head"]
        u["pen"] = None
        if u["gate"] is not None:
            g = u["gate"]
            cnt = jnp.zeros((GATE_ROWS, MOBA_BLOCK), F32)
            for m in range(qi):
                gm = g[m:m + 1, :]
                cnt = cnt + jnp.where((gm > g) | ((gm == g) & (m < rowi)), 1.0, 0.0)
            u["pen"] = jnp.where((rowi < qi) & (cnt < MOBA_TOPK), 0.0, NEG)
        for n in range(qi + 1):
            s = u["s_all"][blk(n)]
            if n == qi:
                s = s + bias_ref[head, MOBA_BLOCK:2 * MOBA_BLOCK, :]
            elif n == qi - 1:
                s = s + bias_ref[head, 0:MOBA_BLOCK, :]
            u["sbuf"][blk(n), :] = s
            t = jnp.max(s.reshape(MOBA_BLOCK // SUBLANES, SUBLANES, MOBA_BLOCK), axis=0)
            if n < qi:
                t = t + penalty(u, n)
            key = "far" if n < qi - 1 else "near"
            u[key] = t if u[key] is None else jnp.maximum(u[key], t)
        mx = u["near"] if u["far"] is None else jnp.maximum(u["near"], u["far"] + u["c31"])
        u["m_near"] = jnp.max(mx, axis=0, keepdims=True)
        u["m_far"] = u["m_near"] - u["c31"]

    def prob_exp(u):
        qi = u["qi"]
        shift = lambda n: (u["m_far"] if n < qi - 1 else u["m_near"]) - (penalty(u, n) if n < qi else 0.0)
        u["pt"] = jnp.concatenate([jnp.exp2(u["sbuf"][blk(n), :] - shift(n)).astype(BF16) for n in range(qi + 1)],
                                  axis=0)

    def prob_matmul(u):
        qi, h = u["qi"], u["h"]
        width = (qi + 1) * MOBA_BLOCK
        r0 = u["head"] * half
        vrows = vt_ref[r0:r0 + half, 0:width]
        ones_rows = jnp.ones((half, width), BF16)
        vt = jnp.concatenate([vrows, ones_rows] if h == 0 else [ones_rows, vrows], axis=0)
        u["acc"] = _dot(vt, u["pt"])

    outs = {}

    def finish(u):
        acc, h = u["acc"], u["h"]
        num = acc[half * h:half * (h + 1)]
        den = acc[half * (1 - h):half * (1 - h) + 1]
        outs[h] = num / den
        if h == 1:
            o_ref[blk(u["qi"]), pair_lanes(u["pp"])] = jnp.concatenate([outs[0], outs[1]], axis=0).T.astype(BF16)

    group = MOBA_GROUP
    stages = [(pp, [(qi, h) for qi in range(g * group, (g + 1) * group) for h in range(2)])
              for pp in range(MOBA_PAIRS) for g in range(nblk // group)]
    prev, kmeans = [], {}
    for g, (pp, members) in enumerate(stages + [(None, [])]):
        if members and pp not in kmeans:
            kmeans[pp] = build(pp)
        cur = [prepare(pp, kmeans[pp], qi, h, (g % 2) * 2 * group + 2 * (qi % group) + h) for qi, h in members]
        for u in prev:
            prob_exp(u)
        for u in cur:
            score_matmul(u)
        for u in prev:
            prob_matmul(u)
        for u in cur:
            score_reduce(u)
        for u in prev:
            finish(u)
        prev = cur


def _moba(qk, vt, bias, c31, batch, seq):
    n = qk.shape[0]
    width = MOBA_PAIRS * LANES
    nstep = A_WIDTH // width
    kern = functools.partial(_moba_kernel, seq=seq)
    return pl.pallas_call(
        kern,
        grid=(batch, nstep),
        in_specs=[pl.BlockSpec(memory_space=pltpu.SMEM),
                  pl.BlockSpec((seq, width), lambda b, p: (b, p)),
                  pl.BlockSpec((seq, width), lambda b, p: (b, nstep + p)),
                  pl.BlockSpec((width, seq), lambda b, p: (p, b)),
                  pl.BlockSpec((2 * MOBA_PAIRS, 2 * MOBA_BLOCK, MOBA_BLOCK), lambda b, p: (p, 0, 0))],
        out_specs=pl.BlockSpec((seq, width), lambda b, p: (b, p)),
        out_shape=jax.ShapeDtypeStruct((n, A_WIDTH), BF16),
        scratch_shapes=[pltpu.VMEM((2 * MOBA_PAIRS, seq, LANES), BF16),
                        pltpu.VMEM((2 * 2 * MOBA_GROUP, seq, MOBA_BLOCK), F32)],
        compiler_params=pltpu.CompilerParams(dimension_semantics=("parallel", "arbitrary"),
                                             vmem_limit_bytes=VMEM_LIMIT),
        name="moba",
    )(c31, qk, qk, vt, bias)


_GLA_LEVELS = int(math.log2(GLA_CHUNK))


_GLA_BIG_LEVELS = _GLA_LEVELS - int(math.log2(SUBLANES))


def _gla_weights():
    c = GLA_CHUNK
    i = np.arange(c)[:, None]
    j = np.arange(c)[None, :]
    mats = [(j <= i)]
    for lvl in range(_GLA_BIG_LEVELS, _GLA_LEVELS):
        s = (c // 2) >> lvl
        ref = (i // (2 * s)) * (2 * s) + s - 1
        mats.append(np.where(i > ref, (j > ref) & (j <= i), (j > i) & (j <= ref)))
    return np.concatenate(mats, axis=0).astype(np.float32)


def _gla_level_map():
    c = GLA_CHUNK
    i = np.arange(c)[:, None]
    j = np.arange(c)[None, :]
    top_bit = np.floor(np.log2(np.maximum(i ^ j, 1))).astype(np.int64)
    lvl = np.where(j < i, _GLA_LEVELS - 1 - top_bit, np.where(j == i, _GLA_LEVELS, _GLA_LEVELS + 1))
    return np.concatenate([lvl, lvl], axis=1).astype(np.int32)


def _gla_kernel(q_ref, k_ref, la_ref, v_ref, rg_ref, w_ref, lmap_ref, gn_ref, o_ref, st_ref):
    c = GLA_CHUNK
    half = LANES // 2
    npair = G_HEADS // 2

    @pl.when(pl.program_id(1) == 0)
    def _init():
        st_ref[...] = jnp.zeros(st_ref.shape, F32)

    w = w_ref[...]
    lmap = lmap_ref[...]
    lane = lax.broadcasted_iota(jnp.int32, (1, LANES), 1)
    lane_c = lax.broadcasted_iota(jnp.int32, (c, LANES), 1)
    lm = [jnp.where(lane_c < half, 1.0, 0.0).astype(BF16), jnp.where(lane_c >= half, 1.0, 0.0).astype(BF16)]
    row_small = lax.broadcasted_iota(jnp.int32, (c, G_KEY_WIDTH), 0)

    def level_factors(q, k, b, d_all, lvl):
        s = (c // 2) >> lvl
        if s < SUBLANES:
            j = lvl - _GLA_BIG_LEVELS
            e = jnp.exp2(d_all[c * (j + 1):c * (j + 2)])
            odd = (row_small // s) % 2 == 1
            return jnp.where(odd, q * e, 0.0), jnp.where(odd, 0.0, k * e)
        zeros = jnp.zeros((s, q.shape[1]), F32)
        qp, kp = [], []
        for blk in range(c // s):
            rows = slice(blk * s, (blk + 1) * s)
            ref = (blk // 2) * 2 * s + s - 1
            if blk % 2 == 1:
                qp.append(q[rows] * jnp.exp2(b[rows] - b[ref:ref + 1]))
                kp.append(zeros)
            else:
                qp.append(zeros)
                kp.append(k[rows] * jnp.exp2(b[ref:ref + 1] - b[rows]))
        return jnp.concatenate(qp, axis=0), jnp.concatenate(kp, axis=0)

    nch = GLA_STEP // c
    rows_of = [slice(ch * c, (ch + 1) * c) for ch in range(nch)]
    pair_lanes = [slice(LANES * p, LANES * (p + 1)) for p in range(npair)]

    def decays(ch):
        la = la_ref[rows_of[ch], :] * LOG2E
        hi = la.astype(BF16)
        lo = (la - hi.astype(F32)).astype(BF16)
        return _dot(w, hi) + _dot(w, lo)

    def factors(ch, d_all):
        q = q_ref[rows_of[ch], :]
        k = k_ref[rows_of[ch], :]
        b = d_all[0:c]
        blast = b[c - 1:c, :]
        qts, kts = [], []
        for lvl in range(_GLA_LEVELS):
            qt, kt = level_factors(q, k, b, d_all, lvl)
            qts.append(qt.astype(BF16))
            kts.append(kt.astype(BF16))
        qts.append(q.astype(BF16))
        kts.append(k.astype(BF16))
        return dict(ch=ch, qts=qts, kts=kts, qe=(q * jnp.exp2(b)).astype(BF16),
                    khat=(k * jnp.exp2(blast - b)).astype(BF16), dec=jnp.exp2(blast),
                    a=[jnp.zeros((c, 2 * c), F32)] * npair)

    def level_scores(units):
        for lvl in range(_GLA_LEVELS + 1):
            mask = lmap == lvl
            s = (c // 2) >> lvl if lvl < _GLA_BIG_LEVELS else c
            blocks = [slice(i * s, (i + 1) * s) for i in range(c // s)]
            live = [i for i in range(len(blocks)) if i % 2 == 1 or s == c]
            for u in units:
                for p, sl in enumerate(pair_lanes):
                    kp = u["kts"][lvl][:, sl]
                    ql = u["qts"][lvl]
                    pm = _nt(jnp.concatenate([ql[blocks[i], sl] for i in live], axis=0),
                             jnp.concatenate([kp * lm[0], kp * lm[1]], axis=0))
                    a = u["a"][p]
                    pieces = [a[r] for r in blocks]
                    for n, i in enumerate(live):
                        pieces[i] = jnp.where(mask[blocks[i]], pm[n * s:(n + 1) * s], pieces[i])
                    u["a"] = [jnp.concatenate(pieces, axis=0) if j == p else v for j, v in enumerate(u["a"])]

    def stateless_matmuls(units):
        for u in units:
            vhs = [v_ref[rows_of[u["ch"]], LANES * h:LANES * (h + 1)] for h in range(G_HEADS)]
            u["intra"] = [_dot(u["a"][h // 2][:, (h % 2) * c:(h % 2 + 1) * c].astype(BF16), vhs[h])
                          for h in range(G_HEADS)]
            u["inc"] = [_tn(vhs[h], u["khat"][:, pair_lanes[h // 2]]) for h in range(G_HEADS)]

    def state_chain(units, states):
        for u in units:
            rows = rows_of[u["ch"]]
            for p, sl in enumerate(pair_lanes):
                stb = states[p].astype(BF16)
                for hh in range(2):
                    h = 2 * p + hh
                    o = u["intra"][h] + _nt(u["qe"][:, sl] * lm[hh], stb)
                    y = _rms(o, gn_ref[...])
                    rg = rg_ref[rows, LANES * h:LANES * (h + 1)]
                    y = y * (rg / (1.0 + jnp.exp(-rg)))
                    o_ref[rows, LANES * h:LANES * (h + 1)] = y.astype(BF16)
                states[p] = states[p] * u["dec"][:, sl] + jnp.where(lane < half, u["inc"][2 * p], u["inc"][2 * p + 1])

    d_alls = [decays(ch) for ch in range(nch)]
    chunks = [factors(ch, d_alls[ch]) for ch in range(nch)]
    level_scores(chunks)
    stateless_matmuls(chunks)
    states = [st_ref[p] for p in range(npair)]
    state_chain(chunks, states)
    for p in range(npair):
        st_ref[p] = states[p]


def _gla(qg, kg, la, vg, rg, gn, batch, seq):
    n = qg.shape[0]
    nchunk = seq // GLA_STEP
    w = jnp.asarray(_gla_weights(), BF16)
    lmap = jnp.asarray(_gla_level_map())
    row = lambda width: pl.BlockSpec((GLA_STEP, width), lambda b, c: (b * nchunk + c, 0))
    const = functools.partial(pl.BlockSpec, pipeline_mode=pl.Buffered(1))
    return pl.pallas_call(
        _gla_kernel,
        grid=(batch, nchunk),
        in_specs=[row(G_KEY_WIDTH), row(G_KEY_WIDTH), row(G_KEY_WIDTH), row(G_WIDTH), row(G_WIDTH),
                  const(w.shape, lambda b, c: (0, 0)),
                  const(lmap.shape, lambda b, c: (0, 0)),
                  const((1, G_HEAD_V), lambda b, c: (0, 0))],
        out_specs=row(G_WIDTH),
        out_shape=jax.ShapeDtypeStruct((n, G_WIDTH), BF16),
        scratch_shapes=[pltpu.VMEM((G_HEADS // 2, G_HEAD_V, LANES), F32)],
        compiler_params=pltpu.CompilerParams(dimension_semantics=("parallel", "arbitrary"),
                                             vmem_limit_bytes=VMEM_LIMIT),
        name="gla",
    )(qg, kg, la, vg, rg, w, lmap, gn)


def _memkv_kernel(m_ref, g_ref, w_ref, k_ref, v_ref):
    nb, mlen, _ = m_ref.shape
    h = _rms(m_ref[...].reshape(nb * mlen, D_MODEL), g_ref[...]).astype(BF16)
    k_ref[...] = _dot(h, w_ref[:, 0:D_MODEL]).astype(BF16).reshape(nb, mlen, D_MODEL)
    v_ref[...] = _dot(h, w_ref[:, D_MODEL:2 * D_MODEL]).astype(BF16).reshape(nb, mlen, D_MODEL)


def _memkv(mem, g, w):
    batch, mlen, _ = mem.shape
    nb = 2 if batch % 2 == 0 else 1
    const = functools.partial(pl.BlockSpec, pipeline_mode=pl.Buffered(1))
    blk = pl.BlockSpec((nb, mlen, D_MODEL), lambda b: (b, 0, 0))
    return pl.pallas_call(
        _memkv_kernel,
        grid=(batch // nb,),
        in_specs=[blk, const((1, D_MODEL), lambda b: (0, 0)), const((D_MODEL, 2 * D_MODEL), lambda b: (0, 0))],
        out_specs=[blk, blk],
        out_shape=[jax.ShapeDtypeStruct(mem.shape, BF16)] * 2,
        compiler_params=pltpu.CompilerParams(dimension_semantics=("parallel",), vmem_limit_bytes=VMEM_LIMIT),
        name="memkv",
    )(mem, g, w)


def _post_kernel(x_ref, oa_ref, og_ref, wo_ref, gx_ref, wq_ref, kx_ref, vx_ref, wxo_ref, g_ref, wu_ref, wd_ref,
                 gf_ref, o_ref, *, final_norm):
    tm = x_ref.shape[0]
    parts = [slice(i * tm // POST_PARTS, (i + 1) * tm // POST_PARTS) for i in range(POST_PARTS)]
    heads = [slice(X_HEAD_DIM * hd, X_HEAD_DIM * (hd + 1)) for hd in range(X_HEADS)]

    def out_proj(r):
        return x_ref[r, :] + _dot(oa_ref[r, :], wo_ref[0:A_WIDTH, :]) + _dot(og_ref[r, :], wo_ref[A_WIDTH:D_MODEL, :])

    def scores(x1):
        h = _rms(x1, gx_ref[...]).astype(BF16)
        qx = (_dot(h, wq_ref[...]) * (LOG2E * X_HEAD_DIM ** -0.5)).astype(BF16)
        return [_nt(qx[:, sl], kx_ref[0, :, sl]) for sl in heads]

    def attend(r, x1, s_heads):
        outs = []
        for sl, s in zip(heads, s_heads):
            pexp = jnp.exp2(s - jnp.max(s, axis=1, keepdims=True))
            pn = (pexp / jnp.sum(pexp, axis=1, keepdims=True)).astype(BF16)
            outs.append(_dot(pn, vx_ref[0, :, sl]).astype(BF16))
        return x1 + _dot(jnp.concatenate(outs, axis=1), wxo_ref[...])

    x1s = [out_proj(r) for r in parts]
    s_parts = [scores(x1) for x1 in x1s]
    acc = [attend(r, x1, s_heads) for r, x1, s_heads in zip(parts, x1s, s_parts)]
    h = [_rms(x, g_ref[...]).astype(BF16) for x in acc]
    for c in range(D_FF // D_MODEL):
        sl = slice(D_MODEL * c, D_MODEL * (c + 1))
        u = [jnp.square(jnp.maximum(_dot(v, wu_ref[:, sl]), 0.0)).astype(BF16) for v in h]
        acc = [a + _dot(v, wd_ref[sl, :]) for a, v in zip(acc, u)]
    for r, a in zip(parts, acc):
        o_ref[r, :] = _rms(a, gf_ref[...]) if final_norm else a


def _post(x2, oa, og, wo, gx, wq, kx, vx, wxo, g, wu, wd, gf, final_norm, seq):
    n = x2.shape[0]
    tm = POST_TILE
    per_b = seq // tm
    mlen = kx.shape[1]
    const = functools.partial(pl.BlockSpec, pipeline_mode=pl.Buffered(1))
    row = lambda width: pl.BlockSpec((tm, width), lambda i: (i, 0))
    mem = pl.BlockSpec((1, mlen, D_MODEL), lambda i: (i // per_b, 0, 0))
    sq = const((D_MODEL, D_MODEL), lambda i: (0, 0))
    vec = const((1, D_MODEL), lambda i: (0, 0))
    return pl.pallas_call(
        functools.partial(_post_kernel, final_norm=final_norm),
        grid=(n // tm,),
        in_specs=[row(D_MODEL), row(A_WIDTH), row(G_WIDTH), sq, vec, sq, mem, mem, sq,
                  vec, const((D_MODEL, D_FF), lambda i: (0, 0)), const((D_FF, D_MODEL), lambda i: (0, 0)), vec],
        out_specs=row(D_MODEL),
        out_shape=jax.ShapeDtypeStruct((n, D_MODEL), F32),
        compiler_params=pltpu.CompilerParams(dimension_semantics=("parallel",), vmem_limit_bytes=VMEM_LIMIT),
        name="post",
    )(x2, oa, og, wo, gx, wq, kx, vx, wxo, g, wu, wd, gf)


def kernel(x, mem, rp_table, norm_mix, w_in, w_gate_up, b_gate, g_norm, w_out, norm_xattn, norm_mem, w_xq, w_xkv,
           w_xo, norm_mlp, w_up, w_down, norm_final):
    batch, seq, _ = x.shape
    depth = w_in.shape[0]
    assert seq % max(INPROJ_TILE, POST_TILE, GLA_STEP) == 0 and seq // MOBA_BLOCK <= 8
    x2 = x.reshape(batch * seq, D_MODEL)
    bias = _bias_tiles(rp_table)
    c31 = rp_table[RP_BUCKETS - 1]
    glr_lo = 3 * A_WIDTH + 2 * G_KEY_WIDTH + G_WIDTH
    rg_lo = glr_lo + G_GATE_RANK
    pad_rank = LANES - G_GATE_RANK
    for l in range(depth):
        col = lambda lo, hi: w_in[l, :, lo:hi].astype(BF16)
        wglr = jnp.pad(col(glr_lo, rg_lo), ((0, 0), (0, pad_rank)))
        wgu = jnp.pad(w_gate_up[l].astype(BF16), ((0, pad_rank), (0, 0)))
        later = (w_out[l], w_xq[l], w_xkv[l], w_xo[l], w_up[l], w_down[l])
        qk, vt, qg, kg, la, vg, rg, wo, wxq, wxkv, wxo, wup, wdown = _inproj(
            x2, norm_mix[l][None], w_in[l], col(rg_lo, rg_lo + G_WIDTH), wglr, wgu, b_gate[l][None], later)
        oa = _moba(qk, vt, bias, c31, batch, seq)
        og = _gla(qg, kg, la, vg, rg, g_norm[l][None], batch, seq)
        kx, vx = _memkv(mem, norm_mem[l][None], wxkv)
        x2 = _post(x2, oa, og, wo, norm_xattn[l][None], wxq, kx, vx, wxo, norm_mlp[l][None], wup, wdown,
                   norm_final[None], l == depth - 1, seq)
    return x2.reshape(batch, seq, D_MODEL)
```
